```python
import jax, jax.numpy as jnp
from jax import lax
import numpy as np

D_MODEL = 1024
BATCH = 16
SEQ = 256
DEPTH = 2
DEC_BATCH = 8
DEC_SEQ = 2048
PAST_LEN = 256

GRID_W = 64
HEAD_DIM = 64
A_HEADS = 4
A_WIDTH = A_HEADS * HEAD_DIM
A_CHUNK = 16
B_Q_HEADS = 6
B_KV_HEADS = 2
B_Q_WIDTH = B_Q_HEADS * HEAD_DIM
B_KV_WIDTH = B_KV_HEADS * HEAD_DIM
C_Q_HEADS = 6
C_KV_HEADS = 2
C_Q_WIDTH = C_Q_HEADS * HEAD_DIM
C_KV_WIDTH = C_KV_HEADS * HEAD_DIM
WINDOW = 128
Q_BLOCK = 128
ROPE_THETA = 10000.0
MIX_WIDTH = A_WIDTH + B_Q_WIDTH + C_Q_WIDTH
IN_WIDTH = 5 * A_WIDTH + B_Q_WIDTH + 2 * B_KV_WIDTH + C_Q_WIDTH + 2 * C_KV_WIDTH
N_EXPERTS = 16
N_GROUPS = 4
EXPERTS_PER_GROUP = N_EXPERTS // N_GROUPS
TOP_K = 2
D_EXPERT = 256
EPS = 1e-6
MASK_VALUE = -1e30
TINY = 1e-30

kernel_name = 'hybrid_diffusion_prefix_step'


def rms_norm(x, g):
    xf = x.astype(jnp.float32)
    y = xf * lax.rsqrt(jnp.mean(xf * xf, axis=-1, keepdims=True) + EPS)
    return (y * g.astype(jnp.float32)).astype(x.dtype)


def to_heads(a):
    return a.reshape(*a.shape[:-1], a.shape[-1] // HEAD_DIM, HEAD_DIM)


def split_proj(z):
    sizes = (A_WIDTH,) * 5 + (B_Q_WIDTH, B_KV_WIDTH, B_KV_WIDTH, C_Q_WIDTH, C_KV_WIDTH, C_KV_WIDTH)
    cuts = [int(v) for v in np.cumsum(sizes)[:-1]]
    return jnp.split(z, cuts, axis=-1)


def rope_2d(x):
    T = x.shape[1]
    rows = T // GRID_W
    row = jnp.broadcast_to(jnp.arange(rows)[:, None], (rows, GRID_W)).reshape(-1)
    col = jnp.broadcast_to(jnp.arange(GRID_W)[None, :], (rows, GRID_W)).reshape(-1)
    pos = jnp.stack([row, col], 0).astype(jnp.float32)
    n_freq = HEAD_DIM // 4
    freqs = ROPE_THETA ** (-jnp.arange(n_freq, dtype=jnp.float32) / n_freq)
    ang = jnp.transpose(pos[:, :, None] * freqs, (1, 0, 2))[None, :, None]
    cos, sin = jnp.cos(ang), jnp.sin(ang)
    xf = x.astype(jnp.float32).reshape(*x.shape[:-1], 2, 2, n_freq)
    x1, x2 = xf[..., 0, :], xf[..., 1, :]
    out = jnp.stack([x1 * cos - x2 * sin, x1 * sin + x2 * cos], axis=-2)
    return out.reshape(x.shape).astype(x.dtype)


def blocked_attention(q, k, v, sink=None):
    B, Tq, Hq, D = q.shape
    Hkv = k.shape[2]
    G = Hq // Hkv
    nb = Tq // Q_BLOCK
    qb = q.reshape(B, nb, Q_BLOCK, Hkv, G, D).transpose(1, 0, 2, 3, 4, 5)
    scale = D ** -0.5

    def one_block(qi):
        s = jnp.einsum('bqkgd,bskd->bkgqs', qi, k).astype(jnp.float32) * scale
        if sink is not None:
            sk = jnp.broadcast_to(sink.astype(jnp.float32).reshape(Hkv, G, 1, 1), s.shape[:-1] + (1,))
            p = jax.nn.softmax(jnp.concatenate([s, sk], axis=-1), axis=-1)[..., :-1]
        else:
            p = jax.nn.softmax(s, axis=-1)
        return jnp.einsum('bkgqs,bskd->bqkgd', p.astype(v.dtype), v)

    o = lax.map(one_block, qb)
    return o.transpose(1, 0, 2, 3, 4, 5).reshape(B, Tq, Hq * D)


def banded_attention(q, k, v, k_ctx, v_ctx, sink):
    B, T, Hq, D = q.shape
    Hkv = k.shape[2]
    G = Hq // Hkv
    L = k_ctx.shape[1]
    nb = T // Q_BLOCK
    scale = D ** -0.5

    def neighbours(a):
        ap = jnp.pad(a, ((0, 0), (Q_BLOCK, Q_BLOCK), (0, 0), (0, 0))).reshape(B, nb + 2, Q_BLOCK, Hkv, D)
        return jnp.concatenate([ap[:, :-2], ap[:, 1:-1], ap[:, 2:]], axis=2)

    kb, vb = neighbours(k), neighbours(v)
    qb = q.reshape(B, nb, Q_BLOCK, Hkv, G, D)
    s_loc = jnp.einsum('bnqkgd,bnskd->bnkgqs', qb, kb).astype(jnp.float32) * scale
    qpos = jnp.arange(nb)[:, None] * Q_BLOCK + jnp.arange(Q_BLOCK)[None, :]
    kpos = (jnp.arange(nb)[:, None] - 1) * Q_BLOCK + jnp.arange(3 * Q_BLOCK)[None, :]
    valid = ((kpos[:, None, :] >= 0) & (kpos[:, None, :] < T)
             & (jnp.abs(qpos[:, :, None] - kpos[:, None, :]) <= WINDOW))
    s_loc = jnp.where(valid[None, :, None, None], s_loc, MASK_VALUE)
    s_ctx = jnp.einsum('bnqkgd,bskd->bnkgqs', qb, k_ctx).astype(jnp.float32) * scale
    sk = jnp.broadcast_to(sink.astype(jnp.float32).reshape(1, 1, Hkv, G, 1, 1), s_loc.shape[:-1] + (1,))
    p = jax.nn.softmax(jnp.concatenate([s_loc, s_ctx, sk], axis=-1), axis=-1).astype(v.dtype)
    o = (jnp.einsum('bnkgqs,bnskd->bnqkgd', p[..., :3 * Q_BLOCK], vb)
         + jnp.einsum('bnkgqs,bskd->bnqkgd', p[..., 3 * Q_BLOCK:3 * Q_BLOCK + L], v_ctx))
    return o.reshape(B, T, Hq * D)


def hgrn_lower_bounds(lb_logits):
    p = jax.nn.softmax(lb_logits.astype(jnp.float32), axis=1)
    return jnp.cumsum(p, axis=1) - p[:, :1]


def hgrn_scan(q, k, v, log_f, s0):
    B, T, H, K = q.shape
    n = T // A_CHUNK

    def chunks(a):
        return a.reshape(B, n, A_CHUNK, H, a.shape[-1]).transpose(1, 0, 3, 2, 4)

    qc, kc, vc, gc = chunks(q), chunks(k), chunks(v), chunks(log_f)
    b = jnp.cumsum(gc, axis=3)
    causal = jnp.tril(jnp.ones((A_CHUNK, A_CHUNK), dtype=bool))[:, :, None]
    diff = b[..., :, None, :] - b[..., None, :, :]
    decay = jnp.where(causal, jnp.exp(jnp.where(causal, diff, 0.0)), 0.0)
    scores = jnp.einsum('nbhtk,nbhsk,nbhtsk->nbhts', qc, kc, decay)
    o_intra = jnp.einsum('nbhts,nbhsv->nbhtv', scores, vc)
    q_in = qc * jnp.exp(b)
    k_out = kc * jnp.exp(b[..., -1:, :] - b)
    d_chunk = jnp.exp(b[..., -1, :])

    def step(S, xs):
        qi, ki, vi, di = xs
        o = jnp.einsum('bhtk,bhkv->bhtv', qi, S)
        S = di[..., None] * S + jnp.einsum('bhsk,bhsv->bhkv', ki, vi)
        return S, o

    s_fin, o_inter = lax.scan(step, s0, (q_in, k_out, vc, d_chunk))
    o = (o_intra + o_inter).transpose(1, 0, 3, 2, 4).reshape(B, T, H, v.shape[-1])
    return o, s_fin


def hgrn_mixer(qa, ia, fza, fzb, ga, lb, norm_g, s0):
    f32 = jnp.float32
    hd = lambda a: to_heads(a.astype(f32))
    q = jax.nn.silu(hd(qa))
    v = hd(ia)

    def gates(z, lb_d):
        lb_d = lb_d.reshape(A_HEADS, HEAD_DIM)
        f = lb_d + (1.0 - lb_d) * jax.nn.sigmoid(z)
        log_f = jnp.log(jnp.maximum(f, TINY))
        k = (1.0 - lb_d) * jax.nn.sigmoid(-z)
        return log_f, k

    lf_f, k_f = gates(hd(fza), lb[0])
    lf_b, k_b = gates(hd(fzb), lb[1])
    rev = lambda a: a[:, ::-1]
    o_f, s_f = hgrn_scan(q, k_f, v, lf_f, s0[:, 0])
    o_b, s_b = hgrn_scan(rev(q), rev(k_b), rev(v), rev(lf_b), s0[:, 1])
    o = rms_norm(o_f + rev(o_b), norm_g) * jax.nn.sigmoid(hd(ga))
    return o.reshape(*qa.shape[:-1], A_WIDTH).astype(qa.dtype), jnp.stack([s_f, s_b], axis=1)


def moe(h, router_w, router_b, w_gate_up, w_down):
    B, T, D = h.shape
    hf = h.reshape(B * T, D)
    scores = jax.nn.sigmoid((hf @ router_w).astype(jnp.float32))
    sel = scores + router_b.astype(jnp.float32)
    group_score = lax.top_k(sel.reshape(-1, N_GROUPS, EXPERTS_PER_GROUP), 2)[0].sum(-1)
    gid = jnp.argmax(group_score, axis=-1)
    in_group = (jnp.arange(N_EXPERTS) // EXPERTS_PER_GROUP)[None, :] == gid[:, None]
    _, idx = lax.top_k(jnp.where(in_group, sel, MASK_VALUE), TOP_K)
    w = jnp.take_along_axis(scores, idx, axis=-1)
    w = w / jnp.sum(w, axis=-1, keepdims=True)
    combine = jnp.sum(jax.nn.one_hot(idx, N_EXPERTS, dtype=jnp.float32) * w[..., None], axis=1)
    gu = jnp.einsum('nd,edf->nef', hf, w_gate_up)
    g, u = jnp.split(gu, 2, axis=-1)
    act = jax.nn.silu(g) * u * combine[..., None].astype(h.dtype)
    return jnp.einsum('nef,efd->nd', act, w_down).reshape(B, T, D)


def trunk_layer(x, cond, lb, cached, mod_w, mod_b, norm1_g, norm2_g, w_in, w_out, a_norm_g,
                q_norm_g, k_norm_g, sink, router_w, router_b, w_gate_up, w_down):
    mod = jax.nn.silu(cond) @ mod_w + mod_b
    sh1, sc1, g1, sh2, sc2, g2 = jnp.split(mod[:, None, :], 6, axis=-1)
    h = rms_norm(x, norm1_g) * (1.0 + sc1) + sh1
    qa, ia, fza, fzb, ga, qg, kg, vg, qw, kw, vw = split_proj(h @ w_in)
    qg = rms_norm(to_heads(qg), q_norm_g)
    kg = rms_norm(to_heads(kg), k_norm_g)
    vg, qw, kw, vw = to_heads(vg), to_heads(qw), to_heads(kw), to_heads(vw)
    if cached is None:
        s0 = jnp.zeros((x.shape[0], 2, A_HEADS, HEAD_DIM, HEAD_DIM), jnp.float32)
        o_a, s_a = hgrn_mixer(qa, ia, fza, fzb, ga, lb, a_norm_g, s0)
        o_g = blocked_attention(qg, kg, vg)
        o_w = blocked_attention(qw, kw, vw, sink)
        new = (kg, vg, kw, vw, s_a.astype(x.dtype))
    else:
        ck_g, cv_g, ck_w, cv_w, s_ctx = cached
        o_a, _ = hgrn_mixer(qa, ia, fza, fzb, ga, lb, a_norm_g, s_ctx.astype(jnp.float32))
        o_g = blocked_attention(rope_2d(qg), jnp.concatenate([rope_2d(kg), ck_g], axis=1),
                                jnp.concatenate([vg, cv_g], axis=1))
        o_w = banded_attention(rope_2d(qw), rope_2d(kw), vw, ck_w, cv_w, sink)
        new = None
    x = x + g1 * (jnp.concatenate([o_a, o_g, o_w], axis=-1) @ w_out)
    h = rms_norm(x, norm2_g) * (1.0 + sc2) + sh2
    x = x + g2 * moe(h, router_w, router_b, w_gate_up, w_down)
    return x, new


def setup_inputs(seed: int = 0) -> dict:
    key = jax.random.key(seed)
    ks = jax.random.split(key, 25)
    nrm = lambda k, shape, s=1.0: jax.random.normal(k, shape, jnp.float32) * s
    return {
        'x_prompt': nrm(ks[0], (BATCH, SEQ, D_MODEL)),
        'x_sample': nrm(ks[1], (DEC_BATCH, DEC_SEQ, D_MODEL)),
        'cache_glob_k': nrm(ks[2], (DEC_BATCH, DEPTH, PAST_LEN, B_KV_HEADS, HEAD_DIM)),
        'cache_glob_v': nrm(ks[3], (DEC_BATCH, DEPTH, PAST_LEN, B_KV_HEADS, HEAD_DIM)),
        'cache_win_k': nrm(ks[4], (DEC_BATCH, DEPTH, PAST_LEN, C_KV_HEADS, HEAD_DIM)),
        'cache_win_v': nrm(ks[5], (DEC_BATCH, DEPTH, PAST_LEN, C_KV_HEADS, HEAD_DIM)),
        'state_hgrn': nrm(ks[6], (DEC_BATCH, DEPTH, 2, A_HEADS, HEAD_DIM, HEAD_DIM), 0.5),
        'c': nrm(ks[7], (DEC_BATCH, D_MODEL)),
        'c_ctx': nrm(ks[8], (D_MODEL,)),
        'mod_w': nrm(ks[9], (DEPTH, D_MODEL, 6 * D_MODEL), 0.5 * D_MODEL ** -0.5),
        'mod_b': nrm(ks[10], (DEPTH, 6 * D_MODEL), 0.02),
        'norm1_g': 1.0 + nrm(ks[11], (DEPTH, D_MODEL), 0.02),
        'norm2_g': 1.0 + nrm(ks[12], (DEPTH, D_MODEL), 0.02),
        'w_in': nrm(ks[13], (DEPTH, D_MODEL, IN_WIDTH), D_MODEL ** -0.5),
        'w_out': nrm(ks[14], (DEPTH, MIX_WIDTH, D_MODEL), MIX_WIDTH ** -0.5),
        'hgrn_lb': nrm(ks[15], (2, DEPTH, A_WIDTH)),
        'hgrn_norm_g': 1.0 + nrm(ks[16], (DEPTH, HEAD_DIM), 0.02),
        'q_norm_g': 1.0 + nrm(ks[17], (DEPTH, HEAD_DIM), 0.02),
        'k_norm_g': 1.0 + nrm(ks[18], (DEPTH, HEAD_DIM), 0.02),
        'win_sink': nrm(ks[19], (DEPTH, C_Q_HEADS), 0.5),
        'router_w': nrm(ks[20], (D_MODEL, N_EXPERTS), D_MODEL ** -0.5),
        'router_b': nrm(ks[21], (N_EXPERTS,), 0.01),
        'w_gate_up': nrm(ks[22], (DEPTH, N_EXPERTS, D_MODEL, 2 * D_EXPERT), D_MODEL ** -0.5),
        'w_down': nrm(ks[23], (DEPTH, N_EXPERTS, D_EXPERT, D_MODEL), D_EXPERT ** -0.5),
        'final_g': 1.0 + nrm(ks[24], (D_MODEL,), 0.02),
    }


def reference(x_prompt, x_sample, cache_glob_k, cache_glob_v, cache_win_k, cache_win_v, state_hgrn,
              c, c_ctx, mod_w, mod_b, norm1_g, norm2_g, w_in, w_out, hgrn_lb, hgrn_norm_g,
              q_norm_g, k_norm_g, win_sink, router_w, router_b, w_gate_up, w_down, final_g):
    lbs = hgrn_lower_bounds(hgrn_lb)
    xp, xs = x_prompt, x_sample
    gk, gv, wk, wv, sa = [], [], [], [], []
    for l in range(DEPTH):
        params = (mod_w[l], mod_b[l], norm1_g[l], norm2_g[l], w_in[l], w_out[l], hgrn_norm_g[l],
                  q_norm_g[l], k_norm_g[l], win_sink[l], router_w, router_b, w_gate_up[l], w_down[l])
        xp, ctx_out = trunk_layer(xp, c_ctx[None, :], lbs[:, l], None, *params)
        gk.append(ctx_out[0]); gv.append(ctx_out[1]); wk.append(ctx_out[2]); wv.append(ctx_out[3]); sa.append(ctx_out[4])
        cached = (cache_glob_k[:, l], cache_glob_v[:, l], cache_win_k[:, l], cache_win_v[:, l], state_hgrn[:, l])
        xs, _ = trunk_layer(xs, c, lbs[:, l], cached, *params)
    y_prompt = rms_norm(xp, final_g)
    y_sample = rms_norm(xs, final_g)
    new_glob_k = jnp.stack(gk, axis=1)
    new_glob_v = jnp.stack(gv, axis=1)
    new_win_k = jnp.stack(wk, axis=1)
    new_win_v = jnp.stack(wv, axis=1)
    new_state_hgrn = jnp.stack(sa, axis=1)
    return (y_prompt, y_sample, new_glob_k, new_glob_v, new_win_k, new_win_v, new_state_hgrn)
```

```python
import functools

import numpy as np
import jax
import jax.numpy as jnp
from jax import lax
from jax.experimental import pallas as pl
from jax.experimental.pallas import tpu as pltpu

F32 = jnp.float32
BF16 = jnp.bfloat16

D_MODEL = 1024
HEAD_DIM = 64
GRID_W = 64
A_HEADS = 4
A_WIDTH = A_HEADS * HEAD_DIM
B_Q_HEADS = 6
B_KV_HEADS = 2
C_Q_HEADS = 6
C_KV_HEADS = 2
Q_WIDTH = B_Q_HEADS * HEAD_DIM
KV_WIDTH = B_KV_HEADS * HEAD_DIM
GQA_GROUP = B_Q_HEADS // B_KV_HEADS
WINDOW = 128
ROPE_THETA = 10000.0
IN_WIDTH = 5 * A_WIDTH + 2 * (Q_WIDTH + 2 * KV_WIDTH)
N_EXPERTS = 16
N_GROUPS = 4
EXPERTS_PER_GROUP = N_EXPERTS // N_GROUPS
D_EXPERT = 256
EPS = 1e-6
MASK_VALUE = -1e30
TINY = 1e-30

LANES = 128
SUB = 16
CHUNK = 128
TOKEN_TILE = 512
MOE_TILE = 1024
COND_ROWS = 16
VMEM_LIMIT = 56 * 1024 * 1024

_NT = (((1,), (1,)), ((), ()))
_TN = (((0,), (0,)), ((), ()))


def _cparams(*sem):
    return pltpu.CompilerParams(dimension_semantics=sem, vmem_limit_bytes=VMEM_LIMIT)


def _dot(a, b):
    return jnp.dot(a, b, preferred_element_type=F32)


def _split2(x):
    h = x.astype(BF16)
    return h, (x - h.astype(F32)).astype(BF16)


def _split3(x):
    h = x.astype(BF16)
    r = x - h.astype(F32)
    m = r.astype(BF16)
    l = (r - m.astype(F32)).astype(BF16)
    return h, m, l


def _dot_sel(c, x):
    h, m, l = _split3(x)
    return _dot(c, h) + _dot(c, m) + _dot(c, l)


def _mm(a, b, precise, dims=None):
    if dims is None:
        dims = (((a.ndim - 1,), (0,)), ((), ()))
    dg = lambda x, y: lax.dot_general(x, y, dims, preferred_element_type=F32)
    if not precise:
        return dg(a.astype(BF16), b.astype(BF16))
    ah, al = _split2(a)
    bh, bl = _split2(b)
    return dg(ah, bh) + dg(ah, bl) + dg(al, bh)


def _mm_sel(a, sel, precise):
    if not precise:
        return _dot(a.astype(BF16), sel)
    ah, al = _split2(a)
    return _dot(ah, sel) + _dot(al, sel)


def _sigmoid(x):
    return jax.nn.sigmoid(x)


def _silu(x):
    return x * jax.nn.sigmoid(x)


def _mod_kernel(cond_ref, w_ref, b_ref, o_ref):
    o_ref[...] = _mm(_silu(cond_ref[...]), w_ref[...], True) + b_ref[...]


def _modulation(cond, mod_w, mod_b):
    depth, d, width = mod_w.shape
    tn = 1536
    return pl.pallas_call(
        _mod_kernel,
        grid=(depth, width // tn),
        in_specs=[
            pl.BlockSpec((COND_ROWS, d), lambda l, j: (0, 0)),
            pl.BlockSpec((None, d, tn), lambda l, j: (l, 0, j)),
            pl.BlockSpec((None, 1, tn), lambda l, j: (l, 0, j)),
        ],
        out_specs=pl.BlockSpec((None, COND_ROWS, tn), lambda l, j: (l, 0, j)),
        out_shape=jax.ShapeDtypeStruct((depth, COND_ROWS, width), F32),
        compiler_params=_cparams("parallel", "parallel"),
        name="modulation",
    )(cond, mod_w, mod_b.reshape(depth, 1, width))


def _head_rms(xb, g2, ones_blk):
    ss = _mm_sel(xb * xb, ones_blk, True)
    return xb * lax.rsqrt(ss * (1.0 / HEAD_DIM) + EPS) * g2


def _rope(yb, cos, sin_signed, first16):
    partner = jnp.where(first16, pltpu.roll(yb, LANES - 16, 1), pltpu.roll(yb, 16, 1))
    return yb * cos + partner * sin_signed


def _q_attention_layout(blocks, lane_half):
    out = []
    for j in range(2 * len(blocks)):
        src = blocks[j // 2]
        dst_half = j // GQA_GROUP
        if j % 2 != dst_half:
            src = pltpu.roll(src, HEAD_DIM, 1)
        out.append(jnp.where(lane_half == dst_half, src, 0.0))
    return jnp.concatenate(out, axis=1)


def _inproj_kernel(*refs, precise_ctx, lat_tiles):
    if precise_ctx:
        (x_ref, mod_ref, g1_ref, w_ref, wlo_ref, cos_ref, sin_ref, qg_ref, kg_ref, ones_ref,
         za_ref, qga_ref, kvg_ref, qwa_ref, kvw_ref, kv32_ref, qg32_ref, qw32_ref, z_ref) = refs
    else:
        (x_ref, mod_ref, g1_ref, w_ref, cos_ref, sin_ref, qg_ref, kg_ref, ones_ref,
         za_ref, qga_ref, kvg_ref, qwa_ref, kvw_ref, kv32_ref, qg32_ref, qw32_ref, z_ref) = refs
        wlo_ref = None
    x = x_ref[...]
    y = x * lax.rsqrt(jnp.mean(x * x, axis=-1, keepdims=True) + EPS) * g1_ref[...]
    mod = mod_ref[...]
    h = y * (1.0 + mod[:, D_MODEL:2 * D_MODEL]) + mod[:, 0:D_MODEL]

    def one_pass():
        z_ref[...] = _dot(h.astype(BF16), w_ref[...])

    def three_pass():
        hh, hl = _split2(h)
        z_ref[...] = _dot(hh, w_ref[...]) + _dot(hl, w_ref[...]) + _dot(hh, wlo_ref[...])

    if precise_ctx:
        is_lat = pl.program_id(0) < lat_tiles
        pl.when(is_lat)(one_pass)
        pl.when(jnp.logical_not(is_lat))(three_pass)
    else:
        one_pass()
    z = z_ref[...]
    za_ref[...] = z[:, :5 * A_WIDTH]

    cos = cos_ref[...]
    sin = sin_ref[...]
    ones_blk = ones_ref[...]
    lane = lax.broadcasted_iota(jnp.int32, (x.shape[0], LANES), 1)
    first16 = (lane % 32) < 16
    lane_half = lane // HEAD_DIM
    scale = HEAD_DIM ** -0.5

    o = 5 * A_WIDTH
    qg = [z[:, o + LANES * b:o + LANES * (b + 1)] for b in range(3)]
    o += Q_WIDTH
    kg = z[:, o:o + KV_WIDTH]
    vg = z[:, o + KV_WIDTH:o + 2 * KV_WIDTH]
    o += 2 * KV_WIDTH
    qw = [z[:, o + LANES * b:o + LANES * (b + 1)] for b in range(3)]
    o += Q_WIDTH
    kw = z[:, o:o + KV_WIDTH]
    vw = z[:, o + KV_WIDTH:o + 2 * KV_WIDTH]

    qg = [_rope(_head_rms(b, qg_ref[...], ones_blk), cos, sin, first16) * scale for b in qg]
    kg = _rope(_head_rms(kg, kg_ref[...], ones_blk), cos, sin, first16)
    qw = [_rope(b, cos, sin, first16) * scale for b in qw]
    kw = _rope(kw, cos, sin, first16)

    qg_att = _q_attention_layout(qg, lane_half)
    qw_att = _q_attention_layout(qw, lane_half)
    qga_ref[...] = qg_att.astype(BF16)
    qwa_ref[...] = qw_att.astype(BF16)
    kvg_ref[...] = jnp.concatenate([kg, vg], axis=1).astype(BF16)
    kvw_ref[...] = jnp.concatenate([kw, vw], axis=1).astype(BF16)
    kv32_ref[...] = jnp.concatenate([kg, vg, kw, vw], axis=1)
    qg32_ref[...] = qg_att
    qw32_ref[...] = qw_att


def _in_projection(x, mod, g1, w_hi, w_lo, cos_t, sin_t, qn_g, kn_g, ones_blk, dims):
    n = x.shape[0]
    tm = TOKEN_TILE
    n_tiles = n // tm
    lat_tiles = dims["n_lat"] // tm
    ctx_tiles = n_tiles - lat_tiles
    tiles_per_seq = dims["t_lat"] // tm
    ctx_cond = dims["ctx_cond"]
    precise_ctx = w_lo is not None

    def cond_idx(i):
        return jnp.where(i < lat_tiles, i // tiles_per_seq, ctx_cond)

    def rope_idx(i):
        return jnp.where(i < lat_tiles, 1 + i % tiles_per_seq, 0)

    def ctx_idx(i):
        return jnp.where(i < lat_tiles, ctx_tiles, i - lat_tiles)

    row = lambda w: pl.BlockSpec((tm, w), lambda i: (i, 0))
    ctx_row = lambda w: pl.BlockSpec((tm, w), lambda i: (ctx_idx(i), 0))
    const = lambda a: pl.BlockSpec(a.shape, lambda i: (0,) * a.ndim)
    weights = [w_hi, w_lo] if precise_ctx else [w_hi]
    ctx_rows = (ctx_tiles + 1) * tm
    return pl.pallas_call(
        functools.partial(_inproj_kernel, precise_ctx=precise_ctx, lat_tiles=lat_tiles),
        grid=(n_tiles,),
        in_specs=[
            row(D_MODEL),
            pl.BlockSpec((None, 1, 6 * D_MODEL), lambda i: (cond_idx(i), 0, 0)),
            const(g1)] + [const(w) for w in weights] + [
            pl.BlockSpec((tm, LANES), lambda i: (rope_idx(i), 0)),
            pl.BlockSpec((tm, LANES), lambda i: (rope_idx(i), 0)),
            const(qn_g), const(kn_g), const(ones_blk),
        ],
        out_specs=[
            row(5 * A_WIDTH), row(2 * Q_WIDTH), row(2 * KV_WIDTH), row(2 * Q_WIDTH), row(2 * KV_WIDTH),
            ctx_row(4 * KV_WIDTH), ctx_row(2 * Q_WIDTH), ctx_row(2 * Q_WIDTH),
        ],
        out_shape=[
            jax.ShapeDtypeStruct((n, 5 * A_WIDTH), F32),
            jax.ShapeDtypeStruct((n, 2 * Q_WIDTH), BF16),
            jax.ShapeDtypeStruct((n, 2 * KV_WIDTH), BF16),
            jax.ShapeDtypeStruct((n, 2 * Q_WIDTH), BF16),
            jax.ShapeDtypeStruct((n, 2 * KV_WIDTH), BF16),
            jax.ShapeDtypeStruct((ctx_rows, 4 * KV_WIDTH), F32),
            jax.ShapeDtypeStruct((ctx_rows, 2 * Q_WIDTH), F32),
            jax.ShapeDtypeStruct((ctx_rows, 2 * Q_WIDTH), F32),
        ],
        scratch_shapes=[pltpu.VMEM((tm, IN_WIDTH), F32)],
        compiler_params=_cparams("arbitrary"),
        name="in_projection_precise_ctx" if precise_ctx else "in_projection",
    )(x, mod, g1, *weights, cos_t, sin_t, qn_g, kn_g, ones_blk)


def _hgrn_chunk(qa, v, fz, lb, st, cum, tot_sel, e_sel, x_sel, reverse, precise):
    q = _silu(qa)
    f = lb + (1.0 - lb) * _sigmoid(fz)
    logf = jnp.log(jnp.maximum(f, TINY))
    kk = (1.0 - lb) * _sigmoid(-fz)

    b = _dot_sel(cum, logf)
    tot = _dot_sel(tot_sel, logf)
    q_in = q * jnp.exp(b)
    k_out = kk * jnp.exp(tot - b)
    d_sub = jnp.exp(tot)

    n_sub = CHUNK // SUB
    row = lax.broadcasted_iota(jnp.int32, (CHUNK, A_WIDTH), 0)
    t_local = row % SUB
    sub_id = row // SUB

    b3 = b.reshape(n_sub, SUB, A_WIDTH)
    k3 = kk.reshape(n_sub, SUB, A_WIDTH)
    pieces = []
    for s in range(SUB):
        bs = jnp.broadcast_to(b3[:, s:s + 1, :], (n_sub, SUB, A_WIDTH)).reshape(CHUNK, A_WIDTH)
        ks = jnp.broadcast_to(k3[:, s:s + 1, :], (n_sub, SUB, A_WIDTH)).reshape(CHUNK, A_WIDTH)
        valid = (t_local <= s) if reverse else (t_local >= s)
        decay = jnp.where(valid, jnp.exp(jnp.where(valid, b - bs, 0.0)), 0.0)
        piece = q * ks * decay
        pieces.append(piece if precise else piece.astype(BF16))
    w_cat = jnp.concatenate(pieces, axis=1)
    s_local = _mm_sel(w_cat, e_sel, precise)
    col = lax.broadcasted_iota(jnp.int32, (CHUNK, A_HEADS * CHUNK), 1)
    rowp = lax.broadcasted_iota(jnp.int32, (CHUNK, A_HEADS * CHUNK), 0)
    same_sub = (col % CHUNK) // SUB == rowp // SUB
    p = jnp.where(same_sub, _mm_sel(s_local, x_sel, precise), 0.0)
    vrow = lax.broadcasted_iota(jnp.int32, (A_HEADS * CHUNK, A_WIDTH), 0) // CHUNK
    vcol = lax.broadcasted_iota(jnp.int32, (A_HEADS * CHUNK, A_WIDTH), 1) // HEAD_DIM
    v_bd = jnp.where(vrow == vcol, jnp.concatenate([v] * A_HEADS, axis=0), 0.0)
    o_intra = _mm(p, v_bd, precise)

    hr = lax.broadcasted_iota(jnp.int32, (A_WIDTH, A_WIDTH), 0) // HEAD_DIM
    hc = lax.broadcasted_iota(jnp.int32, (A_WIDTH, A_WIDTH), 1) // HEAD_DIM
    same_head = hr == hc
    v_op = v if precise else v.astype(BF16)
    q_op = q_in if precise else q_in.astype(BF16)
    o_inter = [None] * n_sub
    order = range(n_sub - 1, -1, -1) if reverse else range(n_sub)
    for j in order:
        o_inter[j] = _mm(q_op[SUB * j:SUB * (j + 1)], st, precise, _NT)
        k_j = jnp.where(sub_id == j, k_out, 0.0)
        u_t = _mm(v_op, k_j, precise, _TN)
        st = d_sub[SUB * j:SUB * j + 1, :] * st + jnp.where(same_head, u_t, 0.0)
    return o_intra + jnp.concatenate(o_inter, axis=0), st


def _hgrn_kernel(rowblk_ref, first_ref, last_ref, s0idx_ref, isctx_ref, soutidx_ref,
                 qa_ref, ia_ref, fz_ref, lb_ref, s0_ref, cum_ref, tot_ref, e_ref, x_ref,
                 o_ref, sout_ref, st_ref, *, reverse, precise_ctx):
    del rowblk_ref, s0idx_ref, soutidx_ref
    step = pl.program_id(0)
    is_ctx = isctx_ref[step] == 1

    @pl.when(first_ref[step] == 1)
    def _():
        st_ref[...] = s0_ref[...] * jnp.where(is_ctx, 0.0, 1.0)

    def run(precise):
        o, st = _hgrn_chunk(qa_ref[...], ia_ref[...], fz_ref[...], lb_ref[...], st_ref[...],
                            cum_ref[...], tot_ref[...], e_ref[...], x_ref[...], reverse, precise)
        o_ref[...] = o
        st_ref[...] = st

    if precise_ctx:
        pl.when(is_ctx)(functools.partial(run, True))
        pl.when(jnp.logical_not(is_ctx))(functools.partial(run, False))
    else:
        run(False)

    @pl.when(last_ref[step] == 1)
    def _():
        sout_ref[...] = st_ref[...]


def _hgrn_tables(dims, reverse):
    rowblk, first, last, s0idx, isctx, sout = [], [], [], [], [], []
    n_ctx_seq = dims["n_ctx_seq"]
    for kind, n_seq, t in (("lat", dims["n_lat_seq"], dims["t_lat"]), ("ctx", n_ctx_seq, dims["t_ctx"])):
        base = 0 if kind == "lat" else dims["n_lat"] // CHUNK
        nc = t // CHUNK
        for s in range(n_seq):
            for c in range(nc):
                cc = nc - 1 - c if reverse else c
                rowblk.append(base + s * nc + cc)
                first.append(int(c == 0))
                last.append(int(c == nc - 1))
                s0idx.append(s if kind == "lat" else 0)
                isctx.append(int(kind == "ctx"))
                sout.append(n_ctx_seq if kind == "lat" else s)
    return [jnp.asarray(np.asarray(a, np.int32)) for a in (rowblk, first, last, s0idx, isctx, sout)]


def _hgrn_consts(reverse):
    r = np.arange(CHUNK)
    same = (r[:, None] // SUB) == (r[None, :] // SUB)
    tri = (r[None, :] >= r[:, None]) if reverse else (r[None, :] <= r[:, None])
    cum = (same & tri).astype(np.float32)
    tot = same.astype(np.float32)
    e = np.zeros((SUB, A_HEADS, HEAD_DIM, A_HEADS, SUB), np.float32)
    for s in range(SUB):
        for h in range(A_HEADS):
            e[s, h, :, h, s] = 1.0
    e = e.reshape(SUB * A_WIDTH, A_HEADS * SUB)
    x = np.zeros((A_HEADS, SUB, A_HEADS, CHUNK), np.float32)
    for h in range(A_HEADS):
        for s in range(CHUNK):
            x[h, s % SUB, h, s] = 1.0
    x = x.reshape(A_HEADS * SUB, A_HEADS * CHUNK)
    return [jnp.asarray(a, BF16) for a in (cum, tot, e, x)]


def _hgrn_scan(za, lb_row, s0_t, dims, reverse, precise_ctx):
    n = za.shape[0]
    tables = _hgrn_tables(dims, reverse)
    n_steps = tables[0].shape[0]
    consts = _hgrn_consts(reverse)
    fz_block = 3 if reverse else 2
    const = lambda a: pl.BlockSpec(a.shape, lambda i, *_: (0,) * a.ndim)
    grid_spec = pltpu.PrefetchScalarGridSpec(
        num_scalar_prefetch=6,
        grid=(n_steps,),
        in_specs=[
            pl.BlockSpec((CHUNK, A_WIDTH), lambda i, rb, *_: (rb[i], 0)),
            pl.BlockSpec((CHUNK, A_WIDTH), lambda i, rb, *_: (rb[i], 1)),
            pl.BlockSpec((CHUNK, A_WIDTH), lambda i, rb, *_: (rb[i], fz_block)),
            const(lb_row),
            pl.BlockSpec((None, A_WIDTH, A_WIDTH), lambda i, rb, fi, la, s0, *_: (s0[i], 0, 0)),
        ] + [const(c) for c in consts],
        out_specs=[
            pl.BlockSpec((CHUNK, A_WIDTH), lambda i, rb, *_: (rb[i], 0)),
            pl.BlockSpec((None, A_WIDTH, A_WIDTH), lambda i, rb, fi, la, s0, ic, so: (so[i], 0, 0)),
        ],
        scratch_shapes=[pltpu.VMEM((A_WIDTH, A_WIDTH), F32)],
    )
    name = ("hgrn_bwd" if reverse else "hgrn_fwd") + ("_precise_ctx" if precise_ctx else "")
    o, s_fin = pl.pallas_call(
        functools.partial(_hgrn_kernel, reverse=reverse, precise_ctx=precise_ctx),
        grid_spec=grid_spec,
        out_shape=[
            jax.ShapeDtypeStruct((n, A_WIDTH), F32),
            jax.ShapeDtypeStruct((dims["n_ctx_seq"] + 1, A_WIDTH, A_WIDTH), F32),
        ],
        compiler_params=_cparams("arbitrary"),
        name=name,
    )(*tables, za, za, za, lb_row, s0_t, *consts)
    return o, s_fin


def _attend(q3, segments, sink_col, precise=False):
    scores = []
    for k, _, valid in segments:
        s = _mm(q3, k, precise, _NT)
        if valid is not None:
            s = jnp.where(valid, s, MASK_VALUE)
        scores.append(s)
    m = scores[0].max(axis=-1, keepdims=True)
    for s in scores[1:]:
        m = jnp.maximum(m, s.max(axis=-1, keepdims=True))
    if sink_col is not None:
        m = jnp.maximum(m, sink_col)
    denom = jnp.exp(sink_col - m) if sink_col is not None else 0.0
    acc = 0.0
    for s, (_, v, _) in zip(scores, segments):
        p = jnp.exp(s - m)
        denom = denom + p.sum(axis=-1, keepdims=True)
        acc = acc + _mm(p, v, precise)
    return acc / denom


def _stack_heads(q, kvh):
    base = kvh * GQA_GROUP
    return jnp.concatenate([q[:, LANES * (base + g):LANES * (base + g + 1)] for g in range(GQA_GROUP)], axis=0)


def _merge_heads(res, tq):
    lane = lax.broadcasted_iota(jnp.int32, (tq, LANES), 1)
    heads = []
    for j in range(B_Q_HEADS):
        kvh, g = divmod(j, GQA_GROUP)
        o = res[kvh][g * tq:(g + 1) * tq]
        if kvh != j % 2:
            o = pltpu.roll(o, HEAD_DIM, 1)
        heads.append(o)
    blocks = [jnp.where(lane < HEAD_DIM, heads[2 * b], heads[2 * b + 1]) for b in range(B_Q_HEADS // 2)]
    return jnp.concatenate(blocks, axis=1)


def _sink_column(sink_ref, kvh, tq):
    return jnp.concatenate(
        [jnp.full((tq, 1), sink_ref[kvh * GQA_GROUP + g], F32) for g in range(GQA_GROUP)], axis=0)


def _global_attn_kernel(q_ref, kv_ref, ck_ref, cv_ref, o_ref, *, tq):
    q = q_ref[...]
    kv = kv_ref[...]
    k, v = kv[:, :KV_WIDTH], kv[:, KV_WIDTH:]
    ck, cv = ck_ref[...], cv_ref[...]
    res = [_attend(_stack_heads(q, kvh), [(k, v, None), (ck, cv, None)], None) for kvh in range(B_KV_HEADS)]
    o_ref[...] = _merge_heads(res, tq).astype(BF16)


def _global_attention(q_att, kv, ck, cv, dims):
    t, n_seq = dims["t_lat"], dims["n_lat_seq"]
    tq = 256
    nq = t // tq
    return pl.pallas_call(
        functools.partial(_global_attn_kernel, tq=tq),
        grid=(n_seq, nq),
        in_specs=[
            pl.BlockSpec((tq, 2 * Q_WIDTH), lambda b, i: (b * nq + i, 0)),
            pl.BlockSpec((t, 2 * KV_WIDTH), lambda b, i: (b, 0)),
            pl.BlockSpec((None,) + ck.shape[1:], lambda b, i: (b, 0, 0)),
            pl.BlockSpec((None,) + cv.shape[1:], lambda b, i: (b, 0, 0)),
        ],
        out_specs=pl.BlockSpec((tq, Q_WIDTH), lambda b, i: (b * nq + i, 0)),
        out_shape=jax.ShapeDtypeStruct((n_seq * t, Q_WIDTH), BF16),
        compiler_params=_cparams("parallel", "arbitrary"),
        name="global_attention",
    )(q_att, kv, ck, cv)


def _window_attn_kernel(sink_ref, q_ref, kv_ref, ck_ref, cv_ref, o_ref, *, tq, t):
    j = pl.program_id(1)
    span = 3 * tq
    start = pl.multiple_of(jnp.clip((j - 1) * tq, 0, t - span), tq)
    q = q_ref[...]
    kv = kv_ref[pl.ds(start, span), :]
    k, v = kv[:, :KV_WIDTH], kv[:, KV_WIDTH:]
    ck, cv = ck_ref[...], cv_ref[...]
    rows = lax.broadcasted_iota(jnp.int32, (GQA_GROUP * tq, span), 0)
    cols = lax.broadcasted_iota(jnp.int32, (GQA_GROUP * tq, span), 1)
    valid = jnp.abs(j * tq + rows % tq - (start + cols)) <= WINDOW
    res = [_attend(_stack_heads(q, kvh), [(k, v, valid), (ck, cv, None)], _sink_column(sink_ref, kvh, tq))
           for kvh in range(C_KV_HEADS)]
    o_ref[...] = _merge_heads(res, tq).astype(BF16)


def _window_attention(sink, q_att, kv, ck, cv, dims):
    t, n_seq = dims["t_lat"], dims["n_lat_seq"]
    tq = WINDOW
    nq = t // tq
    return pl.pallas_call(
        functools.partial(_window_attn_kernel, tq=tq, t=t),
        grid=(n_seq, nq),
        in_specs=[
            pl.BlockSpec(memory_space=pltpu.SMEM),
            pl.BlockSpec((tq, 2 * Q_WIDTH), lambda b, i: (b * nq + i, 0)),
            pl.BlockSpec((t, 2 * KV_WIDTH), lambda b, i: (b, 0)),
            pl.BlockSpec((None,) + ck.shape[1:], lambda b, i: (b, 0, 0)),
            pl.BlockSpec((None,) + cv.shape[1:], lambda b, i: (b, 0, 0)),
        ],
        out_specs=pl.BlockSpec((tq, Q_WIDTH), lambda b, i: (b * nq + i, 0)),
        out_shape=jax.ShapeDtypeStruct((n_seq * t, Q_WIDTH), BF16),
        compiler_params=_cparams("parallel", "arbitrary"),
        name="window_attention",
    )(sink, q_att, kv, ck, cv)


def _context_attn_kernel(sink_ref, qg_ref, qw_ref, kv_ref, og_ref, ow_ref, *, t, precise):
    kv = kv_ref[...]
    for q_ref, o_ref, off, use_sink in ((qg_ref, og_ref, 0, False), (qw_ref, ow_ref, 2 * KV_WIDTH, True)):
        q = q_ref[...]
        k, v = kv[:, off:off + KV_WIDTH], kv[:, off + KV_WIDTH:off + 2 * KV_WIDTH]
        res = [_attend(_stack_heads(q, kvh), [(k, v, None)],
                       _sink_column(sink_ref, kvh, t) if use_sink else None, precise)
               for kvh in range(B_KV_HEADS)]
        o_ref[...] = _merge_heads(res, t)


def _context_attention(sink, qg32, qw32, kv32, dims, precise):
    t, n_seq = dims["t_ctx"], dims["n_ctx_seq"]
    seq = lambda w: pl.BlockSpec((t, w), lambda b: (b, 0))
    return pl.pallas_call(
        functools.partial(_context_attn_kernel, t=t, precise=precise),
        grid=(n_seq,),
        in_specs=[pl.BlockSpec(memory_space=pltpu.SMEM), seq(2 * Q_WIDTH), seq(2 * Q_WIDTH), seq(4 * KV_WIDTH)],
        out_specs=[seq(Q_WIDTH), seq(Q_WIDTH)],
        out_shape=[jax.ShapeDtypeStruct((n_seq * t, Q_WIDTH), F32)] * 2,
        compiler_params=_cparams("parallel"),
        name="context_attention_precise" if precise else "context_attention",
    )(sink, qg32, qw32, kv32)


def _route(scores_t, sel_t):
    s = [scores_t[e:e + 1, :] for e in range(N_EXPERTS)]
    z = [sel_t[e:e + 1, :] for e in range(N_EXPERTS)]
    gs = []
    for g in range(N_GROUPS):
        m = z[g * EXPERTS_PER_GROUP:(g + 1) * EXPERTS_PER_GROUP]
        best = None
        for a in range(EXPERTS_PER_GROUP):
            for b in range(a + 1, EXPERTS_PER_GROUP):
                pair = m[a] + m[b]
                best = pair if best is None else jnp.maximum(best, pair)
        gs.append(best)
    combine = []
    for g in range(N_GROUPS):
        chosen_g = None
        for g2 in range(N_GROUPS):
            if g2 == g:
                continue
            c = (gs[g] > gs[g2]) if g2 < g else (gs[g] >= gs[g2])
            chosen_g = c if chosen_g is None else jnp.logical_and(chosen_g, c)
        base = g * EXPERTS_PER_GROUP
        picked = []
        for a in range(EXPERTS_PER_GROUP):
            rank = 0.0
            for b in range(EXPERTS_PER_GROUP):
                if b == a:
                    continue
                ahead = (z[base + b] >= z[base + a]) if b < a else (z[base + b] > z[base + a])
                rank = rank + ahead.astype(F32)
            picked.append(jnp.where(jnp.logical_and(chosen_g, rank < 2.0), s[base + a], 0.0))
        denom = picked[0] + picked[1] + picked[2] + picked[3]
        denom = jnp.where(chosen_g, denom, 1.0)
        combine.extend(pk / denom for pk in picked)
    return jnp.concatenate(combine, axis=0)


def _outproj_kernel(*refs, lat_tiles, precise_ctx):
    if precise_ctx:
        (of_ref, ob_ref, ga_ref, ogl_ref, ogc_ref, owl_ref, owc_ref, x_ref, mod_ref, ag_ref, ones_ref,
         w_ref, wlo_ref, g2_ref, rw_ref, rb_ref, x1_ref, h2_ref, comb_ref, mix_ref) = refs
    else:
        (of_ref, ob_ref, ga_ref, ogl_ref, ogc_ref, owl_ref, owc_ref, x_ref, mod_ref, ag_ref, ones_ref,
         w_ref, g2_ref, rw_ref, rb_ref, x1_ref, h2_ref, comb_ref, mix_ref) = refs
        wlo_ref = None
    is_lat = pl.program_id(0) < lat_tiles
    o = of_ref[...] + ob_ref[...]
    oa = _head_rms(o, ag_ref[...], ones_ref[...]) * _sigmoid(ga_ref[...])
    parts = ((0, A_WIDTH), (A_WIDTH, A_WIDTH + Q_WIDTH), (A_WIDTH + Q_WIDTH, D_MODEL))

    def mix(operands, precise):
        acc = 0.0
        for a, (r0, r1) in zip(operands, parts):
            if precise:
                ah, al = _split2(a)
                acc = acc + _dot(ah, w_ref[r0:r1, :]) + _dot(al, w_ref[r0:r1, :]) + _dot(ah, wlo_ref[r0:r1, :])
            else:
                acc = acc + _dot(a.astype(BF16), w_ref[r0:r1, :])
        mix_ref[...] = acc

    pl.when(is_lat)(lambda: mix((oa, ogl_ref[...], owl_ref[...]), False))
    pl.when(jnp.logical_not(is_lat))(lambda: mix((oa, ogc_ref[...], owc_ref[...]), precise_ctx))

    mod = mod_ref[...]
    gate1 = mod[:, 2 * D_MODEL:3 * D_MODEL]
    sh2 = mod[:, 3 * D_MODEL:4 * D_MODEL]
    sc2 = mod[:, 4 * D_MODEL:5 * D_MODEL]
    x1 = x_ref[...] + gate1 * mix_ref[...]
    x1_ref[...] = x1
    y = x1 * lax.rsqrt(jnp.mean(x1 * x1, axis=-1, keepdims=True) + EPS) * g2_ref[...]
    h2 = y * (1.0 + sc2) + sh2
    h2_ref[...] = h2.astype(BF16)
    logits_t = _mm(rw_ref[...], h2, True, _NT)
    scores_t = _sigmoid(logits_t)
    comb_ref[...] = _route(scores_t, scores_t + rb_ref[...])


def _out_projection(o_f, o_b, za, og_lat, og_ctx, ow_lat, ow_ctx, x, mod, an_g, ones_blk, w_hi, w_lo, g2,
                    rw_t, rb, dims):
    n = x.shape[0]
    tm = TOKEN_TILE
    n_tiles = n // tm
    lat_tiles = dims["n_lat"] // tm
    tiles_per_seq = dims["t_lat"] // tm
    ctx_cond = dims["ctx_cond"]
    precise_ctx = w_lo is not None
    cond_idx = lambda i: jnp.where(i < lat_tiles, i // tiles_per_seq, ctx_cond)
    row = lambda w: pl.BlockSpec((tm, w), lambda i: (i, 0))
    lat = lambda w: pl.BlockSpec((tm, w), lambda i: (jnp.minimum(i, lat_tiles - 1), 0))
    ctx = lambda w: pl.BlockSpec((tm, w), lambda i: (jnp.maximum(i - lat_tiles, 0), 0))
    const = lambda a: pl.BlockSpec(a.shape, lambda i: (0,) * a.ndim)
    weights = [w_hi, w_lo] if precise_ctx else [w_hi]
    return pl.pallas_call(
        functools.partial(_outproj_kernel, lat_tiles=lat_tiles, precise_ctx=precise_ctx),
        grid=(n_tiles,),
        in_specs=[
            row(A_WIDTH), row(A_WIDTH),
            pl.BlockSpec((tm, A_WIDTH), lambda i: (i, 4)),
            lat(Q_WIDTH), ctx(Q_WIDTH), lat(Q_WIDTH), ctx(Q_WIDTH),
            row(D_MODEL),
            pl.BlockSpec((None, 1, 6 * D_MODEL), lambda i: (cond_idx(i), 0, 0)),
            const(an_g), const(ones_blk)] + [const(w) for w in weights] + [
            const(g2), const(rw_t), const(rb),
        ],
        out_specs=[row(D_MODEL), row(D_MODEL), pl.BlockSpec((N_EXPERTS, tm), lambda i: (0, i))],
        out_shape=[
            jax.ShapeDtypeStruct((n, D_MODEL), F32),
            jax.ShapeDtypeStruct((n, D_MODEL), BF16),
            jax.ShapeDtypeStruct((N_EXPERTS, n), F32),
        ],
        scratch_shapes=[pltpu.VMEM((tm, D_MODEL), F32)],
        compiler_params=_cparams("arbitrary"),
        name="out_projection_precise_ctx" if precise_ctx else "out_projection",
    )(o_f, o_b, za, og_lat, og_ctx, ow_lat, ow_ctx, x, mod, an_g, ones_blk, *weights, g2, rw_t, rb)


def _moe_kernel(h_ref, comb_ref, x1_ref, mod_ref, wgu_ref, wd_ref, fg_ref, o_ref, acc_ref, *, final):
    e = pl.program_id(1)

    @pl.when(e == 0)
    def _():
        acc_ref[...] = jnp.zeros_like(acc_ref)

    gu = _dot(h_ref[...], wgu_ref[...])
    g, u = gu[:, :D_EXPERT], gu[:, D_EXPERT:]
    comb = comb_ref[...]
    lane = lax.broadcasted_iota(jnp.int32, comb.shape, 1)
    ce = jnp.sum(jnp.where(lane == e, comb, 0.0), axis=1, keepdims=True)
    act = (_silu(g) * u * ce).astype(BF16)
    acc_ref[...] += _dot(act, wd_ref[...])

    @pl.when(e == N_EXPERTS - 1)
    def _():
        x2 = x1_ref[...] + mod_ref[...][:, 5 * D_MODEL:] * acc_ref[...]
        if final:
            x2 = x2 * lax.rsqrt(jnp.mean(x2 * x2, axis=-1, keepdims=True) + EPS) * fg_ref[...]
        o_ref[...] = x2


def _moe(h2, comb, x1, mod, w_gu, w_d, final_g, dims, final):
    n = x1.shape[0]
    tm = MOE_TILE
    lat_tiles = dims["n_lat"] // tm
    tiles_per_seq = dims["t_lat"] // tm
    ctx_cond = dims["ctx_cond"]
    cond_idx = lambda i: jnp.where(i < lat_tiles, i // tiles_per_seq, ctx_cond)
    return pl.pallas_call(
        functools.partial(_moe_kernel, final=final),
        grid=(n // tm, N_EXPERTS),
        in_specs=[
            pl.BlockSpec((tm, D_MODEL), lambda i, e: (i, 0)),
            pl.BlockSpec((tm, N_EXPERTS), lambda i, e: (i, 0)),
            pl.BlockSpec((tm, D_MODEL), lambda i, e: (i, 0)),
            pl.BlockSpec((None, 1, 6 * D_MODEL), lambda i, e: (cond_idx(i), 0, 0)),
            pl.BlockSpec((None, D_MODEL, 2 * D_EXPERT), lambda i, e: (e, 0, 0)),
            pl.BlockSpec((None, D_EXPERT, D_MODEL), lambda i, e: (e, 0, 0)),
            pl.BlockSpec((1, D_MODEL), lambda i, e: (0, 0)),
        ],
        out_specs=pl.BlockSpec((tm, D_MODEL), lambda i, e: (i, 0)),
        out_shape=jax.ShapeDtypeStruct((n, D_MODEL), F32),
        scratch_shapes=[pltpu.VMEM((tm, D_MODEL), F32)],
        compiler_params=_cparams("parallel", "arbitrary"),
        name="moe_final" if final else "moe",
    )(h2, comb, x1, mod, w_gu, w_d, final_g)


def _rope_tables(t, tile):
    n_freq = HEAD_DIM // 4
    pos = jnp.arange(t)
    freqs = ROPE_THETA ** (-jnp.arange(n_freq, dtype=F32) / n_freq)
    ang_row = (pos // GRID_W).astype(F32)[:, None] * freqs
    ang_col = (pos % GRID_W).astype(F32)[:, None] * freqs
    cos_h = jnp.concatenate([jnp.cos(ang_row)] * 2 + [jnp.cos(ang_col)] * 2, axis=1)
    sin_h = jnp.concatenate([-jnp.sin(ang_row), jnp.sin(ang_row), -jnp.sin(ang_col), jnp.sin(ang_col)], axis=1)
    cos_t = jnp.concatenate([jnp.ones((tile, LANES), F32), jnp.concatenate([cos_h, cos_h], axis=1)], axis=0)
    sin_t = jnp.concatenate([jnp.zeros((tile, LANES), F32), jnp.concatenate([sin_h, sin_h], axis=1)], axis=0)
    return cos_t, sin_t


def _head_ones(width):
    r = np.arange(width) // HEAD_DIM
    return jnp.asarray((r[:, None] == r[None, :]).astype(np.float32), BF16)


def _split_kernel(w_ref, hi_ref, lo_ref):
    hi, lo = _split2(w_ref[...])
    hi_ref[...] = hi
    lo_ref[...] = lo


def _hi_lo(w, want_lo):
    if not want_lo:
        return w.astype(BF16), None
    rows, width = w.shape
    tr = 256
    spec = pl.BlockSpec((tr, width), lambda i: (i, 0))
    return pl.pallas_call(
        _split_kernel,
        grid=(rows // tr,),
        in_specs=[spec],
        out_specs=[spec, spec],
        out_shape=[jax.ShapeDtypeStruct(w.shape, BF16)] * 2,
        compiler_params=_cparams("parallel"),
        name="split_weight",
    )(w)


def _state_to_block_diag_t(s):
    b = s.shape[0]
    st = jnp.swapaxes(s, -1, -2)
    eye = jnp.eye(A_HEADS, dtype=s.dtype)
    return jnp.einsum("bhvk,hg->bhvgk", st, eye).reshape(b, A_WIDTH, A_WIDTH)


def _block_diag_t_to_state(st):
    b = st.shape[0]
    s5 = st.reshape(b, A_HEADS, HEAD_DIM, A_HEADS, HEAD_DIM)
    diag = jnp.stack([s5[:, h, :, h, :] for h in range(A_HEADS)], axis=1)
    return jnp.swapaxes(diag, -1, -2)


def kernel(x_prompt, x_sample, cache_glob_k, cache_glob_v, cache_win_k, cache_win_v, state_hgrn, c, c_ctx,
           mod_w, mod_b, norm1_g, norm2_g, w_in, w_out, hgrn_lb, hgrn_norm_g, q_norm_g, k_norm_g, win_sink,
           router_w, router_b, w_gate_up, w_down, final_g):
    n_ctx_seq, t_ctx, d = x_prompt.shape
    n_lat_seq, t_lat, _ = x_sample.shape
    depth = mod_w.shape[0]
    past = cache_glob_k.shape[2]
    n_lat, n_ctx = n_lat_seq * t_lat, n_ctx_seq * t_ctx
    dims = dict(n_ctx_seq=n_ctx_seq, t_ctx=t_ctx, n_lat_seq=n_lat_seq, t_lat=t_lat,
                n_lat=n_lat, n_ctx=n_ctx, ctx_cond=n_lat_seq)
    assert d == D_MODEL and n_lat_seq < COND_ROWS
    assert n_lat % MOE_TILE == 0 and n_ctx % MOE_TILE == 0 and t_lat % MOE_TILE == 0
    assert t_ctx % CHUNK == 0 and t_lat >= 3 * WINDOW

    x = jnp.concatenate([x_sample.reshape(n_lat, d), x_prompt.reshape(n_ctx, d)], axis=0)
    cond = jnp.concatenate([c, c_ctx[None, :], jnp.zeros((COND_ROWS - n_lat_seq - 1, d), F32)], axis=0)
    mod = _modulation(cond, mod_w, mod_b).reshape(depth, COND_ROWS, 1, 6 * d)

    p_lb = jax.nn.softmax(hgrn_lb.astype(F32), axis=1)
    lbs = jnp.cumsum(p_lb, axis=1) - p_lb[:, :1]

    cos_t, sin_t = _rope_tables(t_lat, TOKEN_TILE)
    ones128, ones256 = _head_ones(LANES), _head_ones(A_WIDTH)
    tile2 = lambda g, reps: jnp.tile(g, reps)[None, :]
    rw_t = router_w.T
    rb = router_b[:, None]
    fg = final_g[None, :]

    new_kv, new_state = [], []
    for l in range(depth):
        precise = l < depth - 1
        win_hi, win_lo = _hi_lo(w_in[l], precise)
        za, qg_att, kvg, qw_att, kvw, kv32, qg32, qw32 = _in_projection(
            x, mod[l], norm1_g[l][None, :], win_hi, win_lo, cos_t, sin_t,
            tile2(q_norm_g[l], 2), tile2(k_norm_g[l], 2), ones128, dims)

        s0 = state_hgrn[:, l].astype(F32)
        o_f, s_f = _hgrn_scan(za, lbs[0, l][None, :], _state_to_block_diag_t(s0[:, 0]), dims, False, precise)
        o_b, s_b = _hgrn_scan(za, lbs[1, l][None, :], _state_to_block_diag_t(s0[:, 1]), dims, True, precise)

        cast_cache = lambda a: a[:, l].reshape(n_lat_seq, past, KV_WIDTH).astype(BF16)
        og_lat = _global_attention(qg_att, kvg, cast_cache(cache_glob_k), cast_cache(cache_glob_v), dims)
        ow_lat = _window_attention(win_sink[l], qw_att, kvw, cast_cache(cache_win_k),
                                   cast_cache(cache_win_v), dims)
        og_ctx, ow_ctx = _context_attention(win_sink[l], qg32, qw32, kv32, dims, precise)

        wout_hi, wout_lo = _hi_lo(w_out[l], precise)
        x1, h2, comb_t = _out_projection(
            o_f, o_b, za, og_lat, og_ctx, ow_lat, ow_ctx, x, mod[l], tile2(hgrn_norm_g[l], A_HEADS), ones256,
            wout_hi, wout_lo, norm2_g[l][None, :], rw_t, rb, dims)

        x = _moe(h2, comb_t.T, x1, mod[l], w_gate_up[l].astype(BF16), w_down[l].astype(BF16), fg, dims,
                 final=(l == depth - 1))

        new_kv.append(kv32[:n_ctx].reshape(n_ctx_seq, t_ctx, 4, B_KV_HEADS, HEAD_DIM))
        new_state.append(jnp.stack([_block_diag_t_to_state(s_f[:n_ctx_seq]),
                                    _block_diag_t_to_state(s_b[:n_ctx_seq])], axis=1))

    y_sample = x[:n_lat].reshape(n_lat_seq, t_lat, d)
    y_prompt = x[n_lat:].reshape(n_ctx_seq, t_ctx, d)
    kv = jnp.stack(new_kv, axis=1)
    return (y_prompt, y_sample, kv[:, :, :, 0], kv[:, :, :, 1], kv[:, :, :, 2], kv[:, :, :, 3],
            jnp.stack(new_state, axis=1).astype(x_prompt.dtype))
```

```python
import functools

import numpy as np
import jax
import jax.numpy as jnp
from jax import lax
from jax.experimental import pallas as pl
from jax.experimental.pallas import tpu as pltpu

F32 = jnp.float32
BF16 = jnp.bfloat16

D_MODEL = 1024
HEAD_DIM = 64
GRID_W = 64
A_HEADS = 4
A_WIDTH = A_HEADS * HEAD_DIM
B_Q_HEADS = 6
B_KV_HEADS = 2
C_Q_HEADS = 6
C_KV_HEADS = 2
Q_WIDTH = B_Q_HEADS * HEAD_DIM
KV_WIDTH = B_KV_HEADS * HEAD_DIM
GQA_GROUP = B_Q_HEADS // B_KV_HEADS
WINDOW = 128
ROPE_THETA = 10000.0
IN_WIDTH = 5 * A_WIDTH + 2 * (Q_WIDTH + 2 * KV_WIDTH)
N_EXPERTS = 16
N_GROUPS = 4
EXPERTS_PER_GROUP = N_EXPERTS // N_GROUPS
D_EXPERT = 256
EPS = 1e-6
MASK_VALUE = -1e30
TINY = 1e-30

LANES = 128
SUB = 16
CHUNK = 128
TOKEN_TILE = 512
MOE_TILE = 1024
COND_ROWS = 16
VMEM_LIMIT = 56 * 1024 * 1024

_NT = (((1,), (1,)), ((), ()))
_TN = (((0,), (0,)), ((), ()))


def _cparams(*sem):
    return pltpu.CompilerParams(dimension_semantics=sem, vmem_limit_bytes=VMEM_LIMIT)


def _dot(a, b):
    return jnp.dot(a, b, preferred_element_type=F32)


def _split2(x):
    h = x.astype(BF16)
    return h, (x - h.astype(F32)).astype(BF16)


def _split3(x):
    h = x.astype(BF16)
    r = x - h.astype(F32)
    m = r.astype(BF16)
    l = (r - m.astype(F32)).astype(BF16)
    return h, m, l


def _dot_sel(c, x):
    h, m, l = _split3(x)
    return _dot(c, h) + _dot(c, m) + _dot(c, l)


def _mm(a, b, precise, dims=None):
    if dims is None:
        dims = (((a.ndim - 1,), (0,)), ((), ()))
    dg = lambda x, y: lax.dot_general(x, y, dims, preferred_element_type=F32)
    if not precise:
        return dg(a.astype(BF16), b.astype(BF16))
    ah, al = _split2(a)
    bh, bl = _split2(b)
    return dg(ah, bh) + dg(ah, bl) + dg(al, bh)


def _mm_sel(a, sel, precise):
    if not precise:
        return _dot(a.astype(BF16), sel)
    ah, al = _split2(a)
    return _dot(ah, sel) + _dot(al, sel)


def _sigmoid(x):
    return jax.nn.sigmoid(x)


def _silu(x):
    return x * jax.nn.sigmoid(x)


def _mod_kernel(cond_ref, w_ref, b_ref, o_ref):
    o_ref[...] = _mm(_silu(cond_ref[...]), w_ref[...], True) + b_ref[...]


def _modulation(cond, mod_w, mod_b):
    depth, d, width = mod_w.shape
    tn = 1536
    return pl.pallas_call(
        _mod_kernel,
        grid=(depth, width // tn),
        in_specs=[
            pl.BlockSpec((COND_ROWS, d), lambda l, j: (0, 0)),
            pl.BlockSpec((None, d, tn), lambda l, j: (l, 0, j)),
            pl.BlockSpec((None, 1, tn), lambda l, j: (l, 0, j)),
        ],
        out_specs=pl.BlockSpec((None, COND_ROWS, tn), lambda l, j: (l, 0, j)),
        out_shape=jax.ShapeDtypeStruct((depth, COND_ROWS, width), F32),
        compiler_params=_cparams("parallel", "parallel"),
        name="modulation",
    )(cond, mod_w, mod_b.reshape(depth, 1, width))


def _head_rms(xb, g2, ones_blk):
    ss = _mm_sel(xb * xb, ones_blk, True)
    return xb * lax.rsqrt(ss * (1.0 / HEAD_DIM) + EPS) * g2


def _rope(yb, cos, sin_signed, first16):
    partner = jnp.where(first16, pltpu.roll(yb, LANES - 16, 1), pltpu.roll(yb, 16, 1))
    return yb * cos + partner * sin_signed


def _q_attention_layout(blocks, lane_half):
    out = []
    for j in range(2 * len(blocks)):
        src = blocks[j // 2]
        dst_half = j // GQA_GROUP
        if j % 2 != dst_half:
            src = pltpu.roll(src, HEAD_DIM, 1)
        out.append(jnp.where(lane_half == dst_half, src, 0.0))
    return jnp.concatenate(out, axis=1)


def _inproj_kernel(*refs, precise_ctx, lat_tiles):
    if precise_ctx:
        (xl_ref, xc_ref, mod_ref, g1_ref, w_ref, wlo_ref, cos_ref, sin_ref, qg_ref, kg_ref, ones_ref,
         za_ref, qga_ref, kvg_ref, qwa_ref, kvw_ref, kv32_ref, qg32_ref, qw32_ref, z_ref) = refs
    else:
        (xl_ref, xc_ref, mod_ref, g1_ref, w_ref, cos_ref, sin_ref, qg_ref, kg_ref, ones_ref,
         za_ref, qga_ref, kvg_ref, qwa_ref, kvw_ref, kv32_ref, qg32_ref, qw32_ref, z_ref) = refs
        wlo_ref = None
    x = jnp.where(pl.program_id(0) < lat_tiles, xl_ref[...], xc_ref[...])
    y = x * lax.rsqrt(jnp.mean(x * x, axis=-1, keepdims=True) + EPS) * g1_ref[...]
    mod = mod_ref[...]
    h = y * (1.0 + mod[:, D_MODEL:2 * D_MODEL]) + mod[:, 0:D_MODEL]

    def one_pass():
        z_ref[...] = _dot(h.astype(BF16), w_ref[...])

    def three_pass():
        hh, hl = _split2(h)
        z_ref[...] = _dot(hh, w_ref[...]) + _dot(hl, w_ref[...]) + _dot(hh, wlo_ref[...])

    if precise_ctx:
        is_lat = pl.program_id(0) < lat_tiles
        pl.when(is_lat)(one_pass)
        pl.when(jnp.logical_not(is_lat))(three_pass)
    else:
        one_pass()
    z = z_ref[...]
    za_ref[...] = z[:, :5 * A_WIDTH]

    cos = cos_ref[...]
    sin = sin_ref[...]
    ones_blk = ones_ref[...]
    lane = lax.broadcasted_iota(jnp.int32, (x.shape[0], LANES), 1)
    first16 = (lane % 32) < 16
    lane_half = lane // HEAD_DIM
    scale = HEAD_DIM ** -0.5

    o = 5 * A_WIDTH
    qg = [z[:, o + LANES * b:o + LANES * (b + 1)] for b in range(3)]
    o += Q_WIDTH
    kg = z[:, o:o + KV_WIDTH]
    vg = z[:, o + KV_WIDTH:o + 2 * KV_WIDTH]
    o += 2 * KV_WIDTH
    qw = [z[:, o + LANES * b:o + LANES * (b + 1)] for b in range(3)]
    o += Q_WIDTH
    kw = z[:, o:o + KV_WIDTH]
    vw = z[:, o + KV_WIDTH:o + 2 * KV_WIDTH]

    qg = [_rope(_head_rms(b, qg_ref[...], ones_blk), cos, sin, first16) * scale for b in qg]
    kg = _rope(_head_rms(kg, kg_ref[...], ones_blk), cos, sin, first16)
    qw = [_rope(b, cos, sin, first16) * scale for b in qw]
    kw = _rope(kw, cos, sin, first16)

    qg_att = _q_attention_layout(qg, lane_half)
    qw_att = _q_attention_layout(qw, lane_half)
    qga_ref[...] = qg_att.astype(BF16)
    qwa_ref[...] = qw_att.astype(BF16)
    kvg_ref[...] = jnp.concatenate([kg, vg], axis=1).astype(BF16)
    kvw_ref[...] = jnp.concatenate([kw, vw], axis=1).astype(BF16)
    kv32_ref[...] = jnp.concatenate([kg, vg, kw, vw], axis=1)
    qg32_ref[...] = qg_att
    qw32_ref[...] = qw_att


def _token_specs(x_pair, tm, lat_tiles):
    off = lat_tiles if x_pair[1] is x_pair[0] else 0
    lat = pl.BlockSpec((tm, D_MODEL), lambda i, *_: (jnp.minimum(i, lat_tiles - 1), 0))
    ctx = pl.BlockSpec((tm, D_MODEL), lambda i, *_: (jnp.maximum(i, lat_tiles) - lat_tiles + off, 0))
    return [lat, ctx]


def _in_projection(x_pair, mod, g1, w_hi, w_lo, cos_t, sin_t, qn_g, kn_g, ones_blk, dims):
    n = dims["n_lat"] + dims["n_ctx"]
    tm = TOKEN_TILE
    n_tiles = n // tm
    lat_tiles = dims["n_lat"] // tm
    ctx_tiles = n_tiles - lat_tiles
    tiles_per_seq = dims["t_lat"] // tm
    ctx_cond = dims["ctx_cond"]
    precise_ctx = w_lo is not None

    def cond_idx(i):
        return jnp.where(i < lat_tiles, i // tiles_per_seq, ctx_cond)

    def rope_idx(i):
        return jnp.where(i < lat_tiles, 1 + i % tiles_per_seq, 0)

    def ctx_idx(i):
        return jnp.where(i < lat_tiles, ctx_tiles, i - lat_tiles)

    row = lambda w: pl.BlockSpec((tm, w), lambda i: (i, 0))
    ctx_row = lambda w: pl.BlockSpec((tm, w), lambda i: (ctx_idx(i), 0))
    const = lambda a: pl.BlockSpec(a.shape, lambda i: (0,) * a.ndim)
    weights = [w_hi, w_lo] if precise_ctx else [w_hi]
    ctx_rows = (ctx_tiles + 1) * tm
    return pl.pallas_call(
        functools.partial(_inproj_kernel, precise_ctx=precise_ctx, lat_tiles=lat_tiles),
        grid=(n_tiles,),
        in_specs=_token_specs(x_pair, tm, lat_tiles) + [
            pl.BlockSpec((None, 1, 6 * D_MODEL), lambda i: (cond_idx(i), 0, 0)),
            const(g1)] + [const(w) for w in weights] + [
            pl.BlockSpec((tm, LANES), lambda i: (rope_idx(i), 0)),
            pl.BlockSpec((tm, LANES), lambda i: (rope_idx(i), 0)),
            const(qn_g), const(kn_g), const(ones_blk),
        ],
        out_specs=[
            row(5 * A_WIDTH), row(2 * Q_WIDTH), row(2 * KV_WIDTH), row(2 * Q_WIDTH), row(2 * KV_WIDTH),
            ctx_row(4 * KV_WIDTH), ctx_row(2 * Q_WIDTH), ctx_row(2 * Q_WIDTH),
        ],
        out_shape=[
            jax.ShapeDtypeStruct((n, 5 * A_WIDTH), F32),
            jax.ShapeDtypeStruct((n, 2 * Q_WIDTH), BF16),
            jax.ShapeDtypeStruct((n, 2 * KV_WIDTH), BF16),
            jax.ShapeDtypeStruct((n, 2 * Q_WIDTH), BF16),
            jax.ShapeDtypeStruct((n, 2 * KV_WIDTH), BF16),
            jax.ShapeDtypeStruct((ctx_rows, 4 * KV_WIDTH), F32),
            jax.ShapeDtypeStruct((ctx_rows, 2 * Q_WIDTH), F32),
            jax.ShapeDtypeStruct((ctx_rows, 2 * Q_WIDTH), F32),
        ],
        scratch_shapes=[pltpu.VMEM((tm, IN_WIDTH), F32)],
        compiler_params=_cparams("arbitrary"),
        name="in_projection_precise_ctx" if precise_ctx else "in_projection",
    )(*x_pair, mod, g1, *weights, cos_t, sin_t, qn_g, kn_g, ones_blk)


def _hgrn_chunk(qa, v, fz, lb, st, consts, reverse, precise):
    cum, tot_sel, e_sel, x_sel, p_mask, v_mask, head_mask = consts
    log2e = 1.0 / np.log(2.0)
    q = _silu(qa)
    f = lb + (1.0 - lb) * _sigmoid(fz)
    logf2 = jnp.log(jnp.maximum(f, TINY)) * log2e
    kk = (1.0 - lb) * _sigmoid(-fz)

    b = _dot_sel(cum, logf2)
    tot = _dot_sel(tot_sel, logf2)
    q_in = q * jnp.exp2(b)
    k_out = kk * jnp.exp2(tot - b)
    d_sub = jnp.exp2(tot)

    n_sub = CHUNK // SUB
    t_local = lax.broadcasted_iota(jnp.int32, (CHUNK, A_WIDTH), 0) % SUB

    c3 = (b - jnp.log(kk) * log2e).reshape(n_sub, SUB, A_WIDTH)
    pieces = []
    for s in range(SUB):
        cs = jnp.broadcast_to(c3[:, s:s + 1, :], (n_sub, SUB, A_WIDTH)).reshape(CHUNK, A_WIDTH)
        valid = (t_local <= s) if reverse else (t_local >= s)
        piece = q * jnp.exp2(jnp.where(valid, b - cs, MASK_VALUE))
        pieces.append(piece if precise else piece.astype(BF16))
    w_cat = jnp.concatenate(pieces, axis=1)
    s_local = _mm_sel(w_cat, e_sel, precise)
    p = _mm_sel(s_local, x_sel, precise) * p_mask
    v_bd = jnp.concatenate([v] * A_HEADS, axis=0) * v_mask
    o_intra = _mm(p, v_bd, precise)

    v_op = v if precise else v.astype(BF16)
    k_op = k_out if precise else k_out.astype(BF16)
    q_op = q_in if precise else q_in.astype(BF16)
    o_inter = [None] * n_sub
    order = range(n_sub - 1, -1, -1) if reverse else range(n_sub)
    for j in order:
        rows = slice(SUB * j, SUB * (j + 1))
        o_inter[j] = _mm(q_op[rows], st, precise, _NT)
        u_t = _mm(v_op[rows], k_op[rows], precise, _TN)
        st = d_sub[SUB * j:SUB * j + 1, :] * st + head_mask * u_t
    return o_intra + jnp.concatenate(o_inter, axis=0), st


def _hgrn_kernel(rowf_ref, rowb_ref, first_ref, last_ref, s0idx_ref, isctx_ref, soutidx_ref,
                 qaf_ref, iaf_ref, fzf_ref, qab_ref, iab_ref, fzb_ref, lb_ref, s0_ref,
                 cumf_ref, cumb_ref, tot_ref, e_ref, x_ref, pm_ref, vm_ref, hm_ref,
                 of_ref, ob_ref, sout_ref, st_ref, *, precise_ctx):
    del rowf_ref, rowb_ref, s0idx_ref, soutidx_ref
    step = pl.program_id(0)
    is_ctx = isctx_ref[step] == 1

    @pl.when(first_ref[step] == 1)
    def _():
        st_ref[...] = s0_ref[...] * jnp.where(is_ctx, 0.0, 1.0)

    def run(precise):
        shared = (tot_ref[...], e_ref[...], x_ref[...], pm_ref[...], vm_ref[...], hm_ref[...])
        lb = lb_ref[...]
        o_f, st_f = _hgrn_chunk(qaf_ref[...], iaf_ref[...], fzf_ref[...], lb[0:1], st_ref[0],
                                (cumf_ref[...],) + shared, False, precise)
        o_b, st_b = _hgrn_chunk(qab_ref[...], iab_ref[...], fzb_ref[...], lb[1:2], st_ref[1],
                                (cumb_ref[...],) + shared, True, precise)
        of_ref[...] = o_f
        ob_ref[...] = o_b
        st_ref[0] = st_f
        st_ref[1] = st_b

    if precise_ctx:
        pl.when(is_ctx)(functools.partial(run, True))
        pl.when(jnp.logical_not(is_ctx))(functools.partial(run, False))
    else:
        run(False)

    @pl.when(last_ref[step] == 1)
    def _():
        sout_ref[...] = st_ref[...]


def _hgrn_tables(dims):
    rowf, rowb, first, last, s0idx, isctx, sout = [], [], [], [], [], [], []
    n_ctx_seq = dims["n_ctx_seq"]
    for kind, n_seq, t in (("lat", dims["n_lat_seq"], dims["t_lat"]), ("ctx", n_ctx_seq, dims["t_ctx"])):
        base = 0 if kind == "lat" else dims["n_lat"] // CHUNK
        nc = t // CHUNK
        for s in range(n_seq):
            for c in range(nc):
                rowf.append(base + s * nc + c)
                rowb.append(base + s * nc + nc - 1 - c)
                first.append(int(c == 0))
                last.append(int(c == nc - 1))
                s0idx.append(s if kind == "lat" else 0)
                isctx.append(int(kind == "ctx"))
                sout.append(n_ctx_seq if kind == "lat" else s)
    return [jnp.asarray(np.asarray(a, np.int32)) for a in (rowf, rowb, first, last, s0idx, isctx, sout)]


def _hgrn_consts():
    r = np.arange(CHUNK)
    same = (r[:, None] // SUB) == (r[None, :] // SUB)
    cum_f = (same & (r[None, :] <= r[:, None])).astype(np.float32)
    cum_b = (same & (r[None, :] >= r[:, None])).astype(np.float32)
    tot = same.astype(np.float32)
    e = np.zeros((SUB, A_HEADS, HEAD_DIM, A_HEADS, SUB), np.float32)
    for s in range(SUB):
        for h in range(A_HEADS):
            e[s, h, :, h, s] = 1.0
    e = e.reshape(SUB * A_WIDTH, A_HEADS * SUB)
    x = np.zeros((A_HEADS, SUB, A_HEADS, CHUNK), np.float32)
    for h in range(A_HEADS):
        for s in range(CHUNK):
            x[h, s % SUB, h, s] = 1.0
    x = x.reshape(A_HEADS * SUB, A_HEADS * CHUNK)
    p_mask = np.tile(tot, (1, A_HEADS))
    hv = np.arange(A_HEADS * CHUNK) // CHUNK
    hk = np.arange(A_WIDTH) // HEAD_DIM
    v_mask = (hv[:, None] == hk[None, :]).astype(np.float32)
    head_mask = (hk[:, None] == hk[None, :]).astype(np.float32)
    sel = [jnp.asarray(a, BF16) for a in (cum_f, cum_b, tot, e, x)]
    return sel + [jnp.asarray(a, F32) for a in (p_mask, v_mask, head_mask)]


def _hgrn_scan(za, lb2, s0_t, dims, precise_ctx):
    n = za.shape[0]
    tables = _hgrn_tables(dims)
    n_steps = tables[0].shape[0]
    consts = _hgrn_consts()
    const = lambda a: pl.BlockSpec(a.shape, lambda i, *_: (0,) * a.ndim)
    fwd = lambda c: pl.BlockSpec((CHUNK, A_WIDTH), lambda i, rf, rb, *_: (rf[i], c))
    bwd = lambda c: pl.BlockSpec((CHUNK, A_WIDTH), lambda i, rf, rb, *_: (rb[i], c))
    grid_spec = pltpu.PrefetchScalarGridSpec(
        num_scalar_prefetch=7,
        grid=(n_steps,),
        in_specs=[
            fwd(0), fwd(1), fwd(2), bwd(0), bwd(1), bwd(3),
            const(lb2),
            pl.BlockSpec((None, 2, A_WIDTH, A_WIDTH), lambda i, rf, rb, fi, la, s0, *_: (s0[i], 0, 0, 0)),
        ] + [const(c) for c in consts],
        out_specs=[
            fwd(0), bwd(0),
            pl.BlockSpec((None, 2, A_WIDTH, A_WIDTH), lambda i, rf, rb, fi, la, s0, ic, so: (so[i], 0, 0, 0)),
        ],
        scratch_shapes=[pltpu.VMEM((2, A_WIDTH, A_WIDTH), F32)],
    )
    return pl.pallas_call(
        functools.partial(_hgrn_kernel, precise_ctx=precise_ctx),
        grid_spec=grid_spec,
        out_shape=[
            jax.ShapeDtypeStruct((n, A_WIDTH), F32),
            jax.ShapeDtypeStruct((n, A_WIDTH), F32),
            jax.ShapeDtypeStruct((dims["n_ctx_seq"] + 1, 2, A_WIDTH, A_WIDTH), F32),
        ],
        compiler_params=_cparams("arbitrary"),
        name="hgrn_precise_ctx" if precise_ctx else "hgrn",
    )(*tables, za, za, za, za, za, za, lb2, s0_t, *consts)


def _attend(q3, segments, sink_col, precise=False):
    scores = []
    for k, _, valid in segments:
        s = _mm(q3, k, precise, _NT)
        if valid is not None:
            s = jnp.where(valid, s, MASK_VALUE)
        scores.append(s)
    m = scores[0].max(axis=-1, keepdims=True)
    for s in scores[1:]:
        m = jnp.maximum(m, s.max(axis=-1, keepdims=True))
    if sink_col is not None:
        m = jnp.maximum(m, sink_col)
    denom = jnp.exp(sink_col - m) if sink_col is not None else 0.0
    acc = 0.0
    for s, (_, v, _) in zip(scores, segments):
        p = jnp.exp(s - m)
        denom = denom + p.sum(axis=-1, keepdims=True)
        acc = acc + _mm(p, v, precise)
    return acc / denom


def _stack_heads(q, kvh):
    base = kvh * GQA_GROUP
    return jnp.concatenate([q[:, LANES * (base + g):LANES * (base + g + 1)] for g in range(GQA_GROUP)], axis=0)


def _merge_heads(res, tq):
    lane = lax.broadcasted_iota(jnp.int32, (tq, LANES), 1)
    heads = []
    for j in range(B_Q_HEADS):
        kvh, g = divmod(j, GQA_GROUP)
        o = res[kvh][g * tq:(g + 1) * tq]
        if kvh != j % 2:
            o = pltpu.roll(o, HEAD_DIM, 1)
        heads.append(o)
    blocks = [jnp.where(lane < HEAD_DIM, heads[2 * b], heads[2 * b + 1]) for b in range(B_Q_HEADS // 2)]
    return jnp.concatenate(blocks, axis=1)


def _sink_column(sink_ref, kvh, tq):
    return jnp.concatenate(
        [jnp.full((tq, 1), sink_ref[kvh * GQA_GROUP + g], F32) for g in range(GQA_GROUP)], axis=0)


def _global_attn_kernel(q_ref, kv_ref, ck_ref, cv_ref, o_ref, *, tq):
    q = q_ref[...]
    kv = kv_ref[...]
    k, v = kv[:, :KV_WIDTH], kv[:, KV_WIDTH:]
    ck, cv = ck_ref[...], cv_ref[...]
    res = [_attend(_stack_heads(q, kvh), [(k, v, None), (ck, cv, None)], None) for kvh in range(B_KV_HEADS)]
    o_ref[...] = _merge_heads(res, tq).astype(BF16)


def _global_attention(q_att, kv, ck, cv, dims):
    t, n_seq = dims["t_lat"], dims["n_lat_seq"]
    tq = 256
    nq = t // tq
    return pl.pallas_call(
        functools.partial(_global_attn_kernel, tq=tq),
        grid=(n_seq, nq),
        in_specs=[
            pl.BlockSpec((tq, 2 * Q_WIDTH), lambda b, i: (b * nq + i, 0)),
            pl.BlockSpec((t, 2 * KV_WIDTH), lambda b, i: (b, 0)),
            pl.BlockSpec((None,) + ck.shape[1:], lambda b, i: (b, 0, 0)),
            pl.BlockSpec((None,) + cv.shape[1:], lambda b, i: (b, 0, 0)),
        ],
        out_specs=pl.BlockSpec((tq, Q_WIDTH), lambda b, i: (b * nq + i, 0)),
        out_shape=jax.ShapeDtypeStruct((n_seq * t, Q_WIDTH), BF16),
        compiler_params=_cparams("parallel", "arbitrary"),
        name="global_attention",
    )(q_att, kv, ck, cv)


def _window_attn_kernel(sink_ref, q_ref, kv_ref, ck_ref, cv_ref, o_ref, *, tq, t):
    j = pl.program_id(1)
    span = 3 * tq
    start = pl.multiple_of(jnp.clip((j - 1) * tq, 0, t - span), tq)
    q = q_ref[...]
    kv = kv_ref[pl.ds(start, span), :]
    k, v = kv[:, :KV_WIDTH], kv[:, KV_WIDTH:]
    ck, cv = ck_ref[...], cv_ref[...]
    rows = lax.broadcasted_iota(jnp.int32, (GQA_GROUP * tq, span), 0)
    cols = lax.broadcasted_iota(jnp.int32, (GQA_GROUP * tq, span), 1)
    valid = jnp.abs(j * tq + rows % tq - (start + cols)) <= WINDOW
    res = [_attend(_stack_heads(q, kvh), [(k, v, valid), (ck, cv, None)], _sink_column(sink_ref, kvh, tq))
           for kvh in range(C_KV_HEADS)]
    o_ref[...] = _merge_heads(res, tq).astype(BF16)


def _window_attention(sink, q_att, kv, ck, cv, dims):
    t, n_seq = dims["t_lat"], dims["n_lat_seq"]
    tq = WINDOW
    nq = t // tq
    return pl.pallas_call(
        functools.partial(_window_attn_kernel, tq=tq, t=t),
        grid=(n_seq, nq),
        in_specs=[
            pl.BlockSpec(memory_space=pltpu.SMEM),
            pl.BlockSpec((tq, 2 * Q_WIDTH), lambda b, i: (b * nq + i, 0)),
            pl.BlockSpec((t, 2 * KV_WIDTH), lambda b, i: (b, 0)),
            pl.BlockSpec((None,) + ck.shape[1:], lambda b, i: (b, 0, 0)),
            pl.BlockSpec((None,) + cv.shape[1:], lambda b, i: (b, 0, 0)),
        ],
        out_specs=pl.BlockSpec((tq, Q_WIDTH), lambda b, i: (b * nq + i, 0)),
        out_shape=jax.ShapeDtypeStruct((n_seq * t, Q_WIDTH), BF16),
        compiler_params=_cparams("parallel", "arbitrary"),
        name="window_attention",
    )(sink, q_att, kv, ck, cv)


def _context_attn_kernel(sink_ref, qg_ref, qw_ref, kv_ref, og_ref, ow_ref, *, t, precise):
    kv = kv_ref[...]
    for q_ref, o_ref, off, use_sink in ((qg_ref, og_ref, 0, False), (qw_ref, ow_ref, 2 * KV_WIDTH, True)):
        q = q_ref[...]
        k, v = kv[:, off:off + KV_WIDTH], kv[:, off + KV_WIDTH:off + 2 * KV_WIDTH]
        res = [_attend(_stack_heads(q, kvh), [(k, v, None)],
                       _sink_column(sink_ref, kvh, t) if use_sink else None, precise)
               for kvh in range(B_KV_HEADS)]
        o_ref[...] = _merge_heads(res, t)


def _context_attention(sink, qg32, qw32, kv32, dims, precise):
    t, n_seq = dims["t_ctx"], dims["n_ctx_seq"]
    seq = lambda w: pl.BlockSpec((t, w), lambda b: (b, 0))
    return pl.pallas_call(
        functools.partial(_context_attn_kernel, t=t, precise=precise),
        grid=(n_seq,),
        in_specs=[pl.BlockSpec(memory_space=pltpu.SMEM), seq(2 * Q_WIDTH), seq(2 * Q_WIDTH), seq(4 * KV_WIDTH)],
        out_specs=[seq(Q_WIDTH), seq(Q_WIDTH)],
        out_shape=[jax.ShapeDtypeStruct((n_seq * t, Q_WIDTH), F32)] * 2,
        compiler_params=_cparams("parallel"),
        name="context_attention_precise" if precise else "context_attention",
    )(sink, qg32, qw32, kv32)


def _route(scores_t, sel_t):
    s = [scores_t[e:e + 1, :] for e in range(N_EXPERTS)]
    z = [sel_t[e:e + 1, :] for e in range(N_EXPERTS)]
    gs = []
    for g in range(N_GROUPS):
        m = z[g * EXPERTS_PER_GROUP:(g + 1) * EXPERTS_PER_GROUP]
        best = None
        for a in range(EXPERTS_PER_GROUP):
            for b in range(a + 1, EXPERTS_PER_GROUP):
                pair = m[a] + m[b]
                best = pair if best is None else jnp.maximum(best, pair)
        gs.append(best)
    combine = []
    for g in range(N_GROUPS):
        chosen_g = None
        for g2 in range(N_GROUPS):
            if g2 == g:
                continue
            c = (gs[g] > gs[g2]) if g2 < g else (gs[g] >= gs[g2])
            chosen_g = c if chosen_g is None else jnp.logical_and(chosen_g, c)
        base = g * EXPERTS_PER_GROUP
        picked = []
        for a in range(EXPERTS_PER_GROUP):
            rank = 0.0
            for b in range(EXPERTS_PER_GROUP):
                if b == a:
                    continue
                ahead = (z[base + b] >= z[base + a]) if b < a else (z[base + b] > z[base + a])
                rank = rank + ahead.astype(F32)
            picked.append(jnp.where(jnp.logical_and(chosen_g, rank < 2.0), s[base + a], 0.0))
        denom = picked[0] + picked[1] + picked[2] + picked[3]
        denom = jnp.where(chosen_g, denom, 1.0)
        combine.extend(pk / denom for pk in picked)
    return jnp.concatenate(combine, axis=0)


def _outproj_kernel(*refs, lat_tiles, precise_ctx):
    if precise_ctx:
        (of_ref, ob_ref, ga_ref, ogl_ref, ogc_ref, owl_ref, owc_ref, xl_ref, xc_ref, mod_ref, ag_ref,
         ones_ref, w_ref, wlo_ref, g2_ref, rw_ref, rb_ref, x1_ref, h2_ref, comb_ref, mix_ref) = refs
    else:
        (of_ref, ob_ref, ga_ref, ogl_ref, ogc_ref, owl_ref, owc_ref, xl_ref, xc_ref, mod_ref, ag_ref,
         ones_ref, w_ref, g2_ref, rw_ref, rb_ref, x1_ref, h2_ref, comb_ref, mix_ref) = refs
        wlo_ref = None
    is_lat = pl.program_id(0) < lat_tiles
    o = of_ref[...] + ob_ref[...]
    oa = _head_rms(o, ag_ref[...], ones_ref[...]) * _sigmoid(ga_ref[...])
    parts = ((0, A_WIDTH), (A_WIDTH, A_WIDTH + Q_WIDTH), (A_WIDTH + Q_WIDTH, D_MODEL))

    def mix(operands, precise):
        acc = 0.0
        for a, (r0, r1) in zip(operands, parts):
            if precise:
                ah, al = _split2(a)
                acc = acc + _dot(ah, w_ref[r0:r1, :]) + _dot(al, w_ref[r0:r1, :]) + _dot(ah, wlo_ref[r0:r1, :])
            else:
                acc = acc + _dot(a.astype(BF16), w_ref[r0:r1, :])
        mix_ref[...] = acc

    pl.when(is_lat)(lambda: mix((oa, ogl_ref[...], owl_ref[...]), False))
    pl.when(jnp.logical_not(is_lat))(lambda: mix((oa, ogc_ref[...], owc_ref[...]), precise_ctx))

    mod = mod_ref[...]
    gate1 = mod[:, 2 * D_MODEL:3 * D_MODEL]
    sh2 = mod[:, 3 * D_MODEL:4 * D_MODEL]
    sc2 = mod[:, 4 * D_MODEL:5 * D_MODEL]
    x1 = jnp.where(is_lat, xl_ref[...], xc_ref[...]) + gate1 * mix_ref[...]
    x1_ref[...] = x1
    y = x1 * lax.rsqrt(jnp.mean(x1 * x1, axis=-1, keepdims=True) + EPS) * g2_ref[...]
    h2 = y * (1.0 + sc2) + sh2
    h2_ref[...] = h2.astype(BF16)
    logits_t = _mm(rw_ref[...], h2, True, _NT)
    scores_t = _sigmoid(logits_t)
    comb_ref[...] = _route(scores_t, scores_t + rb_ref[...])


def _out_projection(o_f, o_b, za, og_lat, og_ctx, ow_lat, ow_ctx, x_pair, mod, an_g, ones_blk, w_hi, w_lo, g2,
                    rw_t, rb, dims):
    n = dims["n_lat"] + dims["n_ctx"]
    tm = TOKEN_TILE
    n_tiles = n // tm
    lat_tiles = dims["n_lat"] // tm
    tiles_per_seq = dims["t_lat"] // tm
    ctx_cond = dims["ctx_cond"]
    precise_ctx = w_lo is not None
    cond_idx = lambda i: jnp.where(i < lat_tiles, i // tiles_per_seq, ctx_cond)
    row = lambda w: pl.BlockSpec((tm, w), lambda i: (i, 0))
    lat = lambda w: pl.BlockSpec((tm, w), lambda i: (jnp.minimum(i, lat_tiles - 1), 0))
    ctx = lambda w: pl.BlockSpec((tm, w), lambda i: (jnp.maximum(i - lat_tiles, 0), 0))
    const = lambda a: pl.BlockSpec(a.shape, lambda i: (0,) * a.ndim)
    weights = [w_hi, w_lo] if precise_ctx else [w_hi]
    return pl.pallas_call(
        functools.partial(_outproj_kernel, lat_tiles=lat_tiles, precise_ctx=precise_ctx),
        grid=(n_tiles,),
        in_specs=[
            row(A_WIDTH), row(A_WIDTH),
            pl.BlockSpec((tm, A_WIDTH), lambda i: (i, 4)),
            lat(Q_WIDTH), ctx(Q_WIDTH), lat(Q_WIDTH), ctx(Q_WIDTH)] + _token_specs(x_pair, tm, lat_tiles) + [
            pl.BlockSpec((None, 1, 6 * D_MODEL), lambda i: (cond_idx(i), 0, 0)),
            const(an_g), const(ones_blk)] + [const(w) for w in weights] + [
            const(g2), const(rw_t), const(rb),
        ],
        out_specs=[row(D_MODEL), row(D_MODEL), pl.BlockSpec((N_EXPERTS, tm), lambda i: (0, i))],
        out_shape=[
            jax.ShapeDtypeStruct((n, D_MODEL), F32),
            jax.ShapeDtypeStruct((n, D_MODEL), BF16),
            jax.ShapeDtypeStruct((N_EXPERTS, n), F32),
        ],
        scratch_shapes=[pltpu.VMEM((tm, D_MODEL), F32)],
        compiler_params=_cparams("arbitrary"),
        name="out_projection_precise_ctx" if precise_ctx else "out_projection",
    )(o_f, o_b, za, og_lat, og_ctx, ow_lat, ow_ctx, *x_pair, mod, an_g, ones_blk, *weights, g2, rw_t, rb)


def _moe_kernel(h_ref, comb_ref, x1_ref, mod_ref, wgu_ref, wd_ref, fg_ref, *out_and_scratch, final, lat_tiles):
    acc_ref = out_and_scratch[-1]
    e = pl.program_id(1)

    @pl.when(e == 0)
    def _():
        acc_ref[...] = jnp.zeros_like(acc_ref)

    gu = _dot(h_ref[...], wgu_ref[...])
    g, u = gu[:, :D_EXPERT], gu[:, D_EXPERT:]
    comb = comb_ref[...]
    lane = lax.broadcasted_iota(jnp.int32, comb.shape, 1)
    ce = jnp.sum(jnp.where(lane == e, comb, 0.0), axis=1, keepdims=True)
    act = (_silu(g) * u * ce).astype(BF16)
    acc_ref[...] += _dot(act, wd_ref[...])

    @pl.when(e == N_EXPERTS - 1)
    def _():
        x2 = x1_ref[...] + mod_ref[...][:, 5 * D_MODEL:] * acc_ref[...]
        if not final:
            out_and_scratch[0][...] = x2
        else:
            y = x2 * lax.rsqrt(jnp.mean(x2 * x2, axis=-1, keepdims=True) + EPS) * fg_ref[...]
            is_lat = pl.program_id(0) < lat_tiles

            @pl.when(is_lat)
            def _():
                out_and_scratch[0][...] = y

            @pl.when(jnp.logical_not(is_lat))
            def _():
                out_and_scratch[1][...] = y


def _moe(h2, comb, x1, mod, w_gu, w_d, final_g, dims, final):
    n = x1.shape[0]
    tm = MOE_TILE
    lat_tiles = dims["n_lat"] // tm
    tiles_per_seq = dims["t_lat"] // tm
    ctx_cond = dims["ctx_cond"]
    cond_idx = lambda i: jnp.where(i < lat_tiles, i // tiles_per_seq, ctx_cond)
    if final:
        out_specs = [pl.BlockSpec((tm, D_MODEL), lambda i, e: (jnp.minimum(i, lat_tiles - 1), 0)),
                     pl.BlockSpec((tm, D_MODEL), lambda i, e: (jnp.maximum(i - lat_tiles, 0), 0))]
        out_shape = [jax.ShapeDtypeStruct((dims["n_lat"], D_MODEL), F32),
                     jax.ShapeDtypeStruct((dims["n_ctx"], D_MODEL), F32)]
    else:
        out_specs = pl.BlockSpec((tm, D_MODEL), lambda i, e: (i, 0))
        out_shape = jax.ShapeDtypeStruct((n, D_MODEL), F32)
    return pl.pallas_call(
        functools.partial(_moe_kernel, final=final, lat_tiles=lat_tiles),
        grid=(n // tm, N_EXPERTS),
        in_specs=[
            pl.BlockSpec((tm, D_MODEL), lambda i, e: (i, 0)),
            pl.BlockSpec((tm, N_EXPERTS), lambda i, e: (i, 0)),
            pl.BlockSpec((tm, D_MODEL), lambda i, e: (i, 0)),
            pl.BlockSpec((None, 1, 6 * D_MODEL), lambda i, e: (cond_idx(i), 0, 0)),
            pl.BlockSpec((None, D_MODEL, 2 * D_EXPERT), lambda i, e: (e, 0, 0)),
            pl.BlockSpec((None, D_EXPERT, D_MODEL), lambda i, e: (e, 0, 0)),
            pl.BlockSpec((1, D_MODEL), lambda i, e: (0, 0)),
        ],
        out_specs=out_specs,
        out_shape=out_shape,
        scratch_shapes=[pltpu.VMEM((tm, D_MODEL), F32)],
        compiler_params=_cparams("arbitrary", "arbitrary"),
        name="moe_final" if final else "moe",
    )(h2, comb, x1, mod, w_gu, w_d, final_g)


def _rope_tables(t, tile):
    n_freq = HEAD_DIM // 4
    pos = jnp.arange(t)
    freqs = ROPE_THETA ** (-jnp.arange(n_freq, dtype=F32) / n_freq)
    ang_row = (pos // GRID_W).astype(F32)[:, None] * freqs
    ang_col = (pos % GRID_W).astype(F32)[:, None] * freqs
    cos_h = jnp.concatenate([jnp.cos(ang_row)] * 2 + [jnp.cos(ang_col)] * 2, axis=1)
    sin_h = jnp.concatenate([-jnp.sin(ang_row), jnp.sin(ang_row), -jnp.sin(ang_col), jnp.sin(ang_col)], axis=1)
    cos_t = jnp.concatenate([jnp.ones((tile, LANES), F32), jnp.concatenate([cos_h, cos_h], axis=1)], axis=0)
    sin_t = jnp.concatenate([jnp.zeros((tile, LANES), F32), jnp.concatenate([sin_h, sin_h], axis=1)], axis=0)
    return cos_t, sin_t


def _head_ones(width):
    r = np.arange(width) // HEAD_DIM
    return jnp.asarray((r[:, None] == r[None, :]).astype(np.float32), BF16)


def _split_kernel(w_ref, hi_ref, lo_ref):
    hi, lo = _split2(w_ref[...])
    hi_ref[...] = hi
    lo_ref[...] = lo


def _hi_lo(w, want_lo):
    if not want_lo:
        return w.astype(BF16), None
    rows, width = w.shape
    tr = 256
    spec = pl.BlockSpec((tr, width), lambda i: (i, 0))
    return pl.pallas_call(
        _split_kernel,
        grid=(rows // tr,),
        in_specs=[spec],
        out_specs=[spec, spec],
        out_shape=[jax.ShapeDtypeStruct(w.shape, BF16)] * 2,
        compiler_params=_cparams("parallel"),
        name="split_weight",
    )(w)


def _state_to_block_diag_t(s):
    st = jnp.swapaxes(s, -1, -2)
    eye = jnp.eye(A_HEADS, dtype=s.dtype)
    return jnp.einsum("bdhvk,hg->bdhvgk", st, eye).reshape(s.shape[:2] + (A_WIDTH, A_WIDTH))


def _block_diag_t_to_state(st):
    s6 = st.reshape(st.shape[:2] + (A_HEADS, HEAD_DIM, A_HEADS, HEAD_DIM))
    diag = jnp.stack([s6[:, :, h, :, h, :] for h in range(A_HEADS)], axis=2)
    return jnp.swapaxes(diag, -1, -2)


def kernel(x_prompt, x_sample, cache_glob_k, cache_glob_v, cache_win_k, cache_win_v, state_hgrn, c, c_ctx,
           mod_w, mod_b, norm1_g, norm2_g, w_in, w_out, hgrn_lb, hgrn_norm_g, q_norm_g, k_norm_g, win_sink,
           router_w, router_b, w_gate_up, w_down, final_g):
    n_ctx_seq, t_ctx, d = x_prompt.shape
    n_lat_seq, t_lat, _ = x_sample.shape
    depth = mod_w.shape[0]
    past = cache_glob_k.shape[2]
    n_lat, n_ctx = n_lat_seq * t_lat, n_ctx_seq * t_ctx
    dims = dict(n_ctx_seq=n_ctx_seq, t_ctx=t_ctx, n_lat_seq=n_lat_seq, t_lat=t_lat,
                n_lat=n_lat, n_ctx=n_ctx, ctx_cond=n_lat_seq)
    assert d == D_MODEL and n_lat_seq < COND_ROWS
    assert n_lat % MOE_TILE == 0 and n_ctx % MOE_TILE == 0 and t_lat % MOE_TILE == 0
    assert t_ctx % CHUNK == 0 and t_lat >= 3 * WINDOW

    x_pair = (x_sample.reshape(n_lat, d), x_prompt.reshape(n_ctx, d))
    cond = jnp.concatenate([c, c_ctx[None, :], jnp.zeros((COND_ROWS - n_lat_seq - 1, d), F32)], axis=0)
    mod = _modulation(cond, mod_w, mod_b).reshape(depth, COND_ROWS, 1, 6 * d)

    p_lb = jax.nn.softmax(hgrn_lb.astype(F32), axis=1)
    lbs = jnp.cumsum(p_lb, axis=1) - p_lb[:, :1]

    cos_t, sin_t = _rope_tables(t_lat, TOKEN_TILE)
    ones128, ones256 = _head_ones(LANES), _head_ones(A_WIDTH)
    tile2 = lambda g, reps: jnp.tile(g, reps)[None, :]
    rw_t = router_w.T
    rb = router_b[:, None]
    fg = final_g[None, :]

    new_kv, new_state = [], []
    for l in range(depth):
        precise = l < depth - 1
        win_hi, win_lo = _hi_lo(w_in[l], precise)
        za, qg_att, kvg, qw_att, kvw, kv32, qg32, qw32 = _in_projection(
            x_pair, mod[l], norm1_g[l][None, :], win_hi, win_lo, cos_t, sin_t,
            tile2(q_norm_g[l], 2), tile2(k_norm_g[l], 2), ones128, dims)

        s0 = state_hgrn[:, l].astype(F32)
        o_f, o_b, s_fin = _hgrn_scan(za, lbs[:, l], _state_to_block_diag_t(s0), dims, precise)

        cast_cache = lambda a: a[:, l].reshape(n_lat_seq, past, KV_WIDTH).astype(BF16)
        og_lat = _global_attention(qg_att, kvg, cast_cache(cache_glob_k), cast_cache(cache_glob_v), dims)
        ow_lat = _window_attention(win_sink[l], qw_att, kvw, cast_cache(cache_win_k),
                                   cast_cache(cache_win_v), dims)
        og_ctx, ow_ctx = _context_attention(win_sink[l], qg32, qw32, kv32, dims, precise)

        wout_hi, wout_lo = _hi_lo(w_out[l], precise)
        x1, h2, comb_t = _out_projection(
            o_f, o_b, za, og_lat, og_ctx, ow_lat, ow_ctx, x_pair, mod[l], tile2(hgrn_norm_g[l], A_HEADS),
            ones256, wout_hi, wout_lo, norm2_g[l][None, :], rw_t, rb, dims)

        x = _moe(h2, comb_t.T, x1, mod[l], w_gate_up[l].astype(BF16), w_down[l].astype(BF16), fg, dims,
                 final=(l == depth - 1))
        x_pair = (x, x)

        new_kv.append(kv32[:n_ctx].reshape(n_ctx_seq, t_ctx, 4, B_KV_HEADS, HEAD_DIM))
        new_state.append(_block_diag_t_to_state(s_fin[:n_ctx_seq]))

    y_sample = x[0].reshape(n_lat_seq, t_lat, d)
    y_prompt = x[1].reshape(n_ctx_seq, t_ctx, d)
    kv = jnp.stack(new_kv, axis=1)
    return (y_prompt, y_sample, kv[:, :, :, 0], kv[:, :, :, 1], kv[:, :, :, 2], kv[:, :, :, 3],
            jnp.stack(new_state, axis=1).astype(x_prompt.dtype))
```

```python
import functools

import numpy as np
import jax
import jax.numpy as jnp
from jax import lax
from jax.experimental import pallas as pl
from jax.experimental.pallas import tpu as pltpu

F32 = jnp.float32
BF16 = jnp.bfloat16

D_MODEL = 1024
HEAD_DIM = 64
GRID_W = 64
A_HEADS = 4
A_WIDTH = A_HEADS * HEAD_DIM
B_Q_HEADS = 6
B_KV_HEADS = 2
C_Q_HEADS = 6
C_KV_HEADS = 2
Q_WIDTH = B_Q_HEADS * HEAD_DIM
KV_WIDTH = B_KV_HEADS * HEAD_DIM
GQA_GROUP = B_Q_HEADS // B_KV_HEADS
WINDOW = 128
ROPE_THETA = 10000.0
IN_WIDTH = 5 * A_WIDTH + 2 * (Q_WIDTH + 2 * KV_WIDTH)
N_EXPERTS = 16
N_GROUPS = 4
EXPERTS_PER_GROUP = N_EXPERTS // N_GROUPS
D_EXPERT = 256
EPS = 1e-6
MASK_VALUE = -1e30
TINY = 1e-30
LOG2E = 1.4426950408889634

LANES = 128
SUB = 16
CHUNK = 128
TOKEN_TILE = 512
MOE_BLOCK = 128
KEY_CHUNK = 256
COND_ROWS = 16
VMEM_LIMIT = 56 * 1024 * 1024

_NT = (((1,), (1,)), ((), ()))
_TN = (((0,), (0,)), ((), ()))


def _cparams(*sem):
    return pltpu.CompilerParams(dimension_semantics=sem, vmem_limit_bytes=VMEM_LIMIT)


def _dot(a, b):
    return jnp.dot(a, b, preferred_element_type=F32)


def _split2(x):
    h = x.astype(BF16)
    return h, (x - h.astype(F32)).astype(BF16)


def _split3(x):
    h = x.astype(BF16)
    r = x - h.astype(F32)
    m = r.astype(BF16)
    l = (r - m.astype(F32)).astype(BF16)
    return h, m, l


def _dot_sel(c, x):
    h, m, l = _split3(x)
    return _dot(c, h) + _dot(c, m) + _dot(c, l)


def _mm(a, b, precise, dims=None):
    if dims is None:
        dims = (((a.ndim - 1,), (0,)), ((), ()))
    dg = lambda x, y: lax.dot_general(x, y, dims, preferred_element_type=F32)
    if not precise:
        return dg(a.astype(BF16), b.astype(BF16))
    ah, al = _split2(a)
    bh, bl = _split2(b)
    return dg(ah, bh) + dg(ah, bl) + dg(al, bh)


def _mm_sel(a, sel, precise):
    if not precise:
        return _dot(a.astype(BF16), sel)
    ah, al = _split2(a)
    return _dot(ah, sel) + _dot(al, sel)


def _sigmoid(x):
    return jax.nn.sigmoid(x)


def _silu(x):
    return x * jax.nn.sigmoid(x)


def _mod_kernel(cond_ref, w_ref, b_ref, o_ref):
    o_ref[...] = _mm(_silu(cond_ref[...]), w_ref[...], True) + b_ref[...]


def _modulation(cond, mod_w, mod_b):
    depth, d, width = mod_w.shape
    tn = 1536
    return pl.pallas_call(
        _mod_kernel,
        grid=(depth, width // tn),
        in_specs=[
            pl.BlockSpec((COND_ROWS, d), lambda l, j: (0, 0)),
            pl.BlockSpec((None, d, tn), lambda l, j: (l, 0, j)),
            pl.BlockSpec((None, 1, tn), lambda l, j: (l, 0, j)),
        ],
        out_specs=pl.BlockSpec((None, COND_ROWS, tn), lambda l, j: (l, 0, j)),
        out_shape=jax.ShapeDtypeStruct((depth, COND_ROWS, width), F32),
        compiler_params=_cparams("parallel", "parallel"),
        name="modulation",
    )(cond, mod_w, mod_b.reshape(depth, 1, width))


def _head_rms(xb, g2, ones_blk):
    ss = _mm_sel(xb * xb, ones_blk, True)
    return xb * lax.rsqrt(ss * (1.0 / HEAD_DIM) + EPS) * g2


def _rope(yb, cos, sin_signed, first16):
    partner = jnp.where(first16, pltpu.roll(yb, LANES - 16, 1), pltpu.roll(yb, 16, 1))
    return yb * cos + partner * sin_signed


def _q_attention_layout(blocks, lane_half):
    out = []
    for j in range(2 * len(blocks)):
        src = blocks[j // 2]
        dst_half = j // GQA_GROUP
        if j % 2 != dst_half:
            src = pltpu.roll(src, HEAD_DIM, 1)
        out.append(jnp.where(lane_half == dst_half, src, 0.0))
    return jnp.concatenate(out, axis=1)


def _inproj_kernel(*refs, precise_ctx, lat_tiles):
    if precise_ctx:
        (xl_ref, xc_ref, mod_ref, g1_ref, w_ref, wlo_ref, cos_ref, sin_ref, qg_ref, kg_ref, ones_ref,
         za_ref, qga_ref, kvg_ref, qwa_ref, kvw_ref, kv32_ref, qg32_ref, qw32_ref, z_ref) = refs
    else:
        (xl_ref, xc_ref, mod_ref, g1_ref, w_ref, cos_ref, sin_ref, qg_ref, kg_ref, ones_ref,
         za_ref, qga_ref, kvg_ref, qwa_ref, kvw_ref, kv32_ref, qg32_ref, qw32_ref, z_ref) = refs
        wlo_ref = None
    x = jnp.where(pl.program_id(0) < lat_tiles, xl_ref[...], xc_ref[...])
    y = x * lax.rsqrt(jnp.mean(x * x, axis=-1, keepdims=True) + EPS) * g1_ref[...]
    mod = mod_ref[...]
    h = y * (1.0 + mod[:, D_MODEL:2 * D_MODEL]) + mod[:, 0:D_MODEL]

    def one_pass():
        z_ref[...] = _dot(h.astype(BF16), w_ref[...])

    def three_pass():
        hh, hl = _split2(h)
        z_ref[...] = _dot(hh, w_ref[...]) + _dot(hl, w_ref[...]) + _dot(hh, wlo_ref[...])

    if precise_ctx:
        is_lat = pl.program_id(0) < lat_tiles
        pl.when(is_lat)(one_pass)
        pl.when(jnp.logical_not(is_lat))(three_pass)
    else:
        one_pass()
    z = z_ref[...]
    za_ref[...] = z[:, :5 * A_WIDTH]

    cos = cos_ref[...]
    sin = sin_ref[...]
    ones_blk = ones_ref[...]
    lane = lax.broadcasted_iota(jnp.int32, (x.shape[0], LANES), 1)
    first16 = (lane % 32) < 16
    lane_half = lane // HEAD_DIM
    scale = LOG2E * HEAD_DIM ** -0.5

    o = 5 * A_WIDTH
    qg = [z[:, o + LANES * b:o + LANES * (b + 1)] for b in range(3)]
    o += Q_WIDTH
    kg = z[:, o:o + KV_WIDTH]
    vg = z[:, o + KV_WIDTH:o + 2 * KV_WIDTH]
    o += 2 * KV_WIDTH
    qw = [z[:, o + LANES * b:o + LANES * (b + 1)] for b in range(3)]
    o += Q_WIDTH
    kw = z[:, o:o + KV_WIDTH]
    vw = z[:, o + KV_WIDTH:o + 2 * KV_WIDTH]

    qg = [_rope(_head_rms(b, qg_ref[...], ones_blk), cos, sin, first16) * scale for b in qg]
    kg = _rope(_head_rms(kg, kg_ref[...], ones_blk), cos, sin, first16)
    qw = [_rope(b, cos, sin, first16) * scale for b in qw]
    kw = _rope(kw, cos, sin, first16)

    qg_att = _q_attention_layout(qg, lane_half)
    qw_att = _q_attention_layout(qw, lane_half)
    qga_ref[...] = qg_att.astype(BF16)
    qwa_ref[...] = qw_att.astype(BF16)
    kvg_ref[...] = jnp.concatenate([kg, vg], axis=1).astype(BF16)
    kvw_ref[...] = jnp.concatenate([kw, vw], axis=1).astype(BF16)
    kv32_ref[...] = jnp.concatenate([kg, vg, kw, vw], axis=1)
    qg32_ref[...] = qg_att
    qw32_ref[...] = qw_att


def _token_specs(x_pair, tm, lat_tiles):
    off = lat_tiles if x_pair[1] is x_pair[0] else 0
    lat = pl.BlockSpec((tm, D_MODEL), lambda i, *_: (jnp.minimum(i, lat_tiles - 1), 0))
    ctx = pl.BlockSpec((tm, D_MODEL), lambda i, *_: (jnp.maximum(i, lat_tiles) - lat_tiles + off, 0))
    return [lat, ctx]


def _in_projection(x_pair, mod, g1, w_hi, w_lo, cos_t, sin_t, qn_g, kn_g, ones_blk, dims):
    n = dims["n_lat"] + dims["n_ctx"]
    tm = TOKEN_TILE
    n_tiles = n // tm
    lat_tiles = dims["n_lat"] // tm
    ctx_tiles = n_tiles - lat_tiles
    tiles_per_seq = dims["t_lat"] // tm
    ctx_cond = dims["ctx_cond"]
    precise_ctx = w_lo is not None

    def cond_idx(i):
        return jnp.where(i < lat_tiles, i // tiles_per_seq, ctx_cond)

    def rope_idx(i):
        return jnp.where(i < lat_tiles, 1 + i % tiles_per_seq, 0)

    def ctx_idx(i):
        return jnp.where(i < lat_tiles, ctx_tiles, i - lat_tiles)

    row = lambda w: pl.BlockSpec((tm, w), lambda i: (i, 0))
    ctx_row = lambda w: pl.BlockSpec((tm, w), lambda i: (ctx_idx(i), 0))
    const = lambda a: pl.BlockSpec(a.shape, lambda i: (0,) * a.ndim)
    weights = [w_hi, w_lo] if precise_ctx else [w_hi]
    ctx_rows = (ctx_tiles + 1) * tm
    return pl.pallas_call(
        functools.partial(_inproj_kernel, precise_ctx=precise_ctx, lat_tiles=lat_tiles),
        grid=(n_tiles,),
        in_specs=_token_specs(x_pair, tm, lat_tiles) + [
            pl.BlockSpec((None, 1, 6 * D_MODEL), lambda i: (cond_idx(i), 0, 0)),
            const(g1)] + [const(w) for w in weights] + [
            pl.BlockSpec((tm, LANES), lambda i: (rope_idx(i), 0)),
            pl.BlockSpec((tm, LANES), lambda i: (rope_idx(i), 0)),
            const(qn_g), const(kn_g), const(ones_blk),
        ],
        out_specs=[
            row(5 * A_WIDTH), row(2 * Q_WIDTH), row(2 * KV_WIDTH), row(2 * Q_WIDTH), row(2 * KV_WIDTH),
            ctx_row(4 * KV_WIDTH), ctx_row(2 * Q_WIDTH), ctx_row(2 * Q_WIDTH),
        ],
        out_shape=[
            jax.ShapeDtypeStruct((n, 5 * A_WIDTH), F32),
            jax.ShapeDtypeStruct((n, 2 * Q_WIDTH), BF16),
            jax.ShapeDtypeStruct((n, 2 * KV_WIDTH), BF16),
            jax.ShapeDtypeStruct((n, 2 * Q_WIDTH), BF16),
            jax.ShapeDtypeStruct((n, 2 * KV_WIDTH), BF16),
            jax.ShapeDtypeStruct((ctx_rows, 4 * KV_WIDTH), F32),
            jax.ShapeDtypeStruct((ctx_rows, 2 * Q_WIDTH), F32),
            jax.ShapeDtypeStruct((ctx_rows, 2 * Q_WIDTH), F32),
        ],
        scratch_shapes=[pltpu.VMEM((tm, IN_WIDTH), F32)],
        compiler_params=_cparams("arbitrary"),
        name="in_projection_precise_ctx" if precise_ctx else "in_projection",
    )(*x_pair, mod, g1, *weights, cos_t, sin_t, qn_g, kn_g, ones_blk)


def _hgrn_chunk(qa, v, fz, lb, st, consts, reverse, precise):
    cum, tot_sel, e_sel, x_sel, p_mask, v_mask, head_mask = consts
    log2e = LOG2E
    q = _silu(qa)
    f = lb + (1.0 - lb) * _sigmoid(fz)
    logf2 = jnp.log(jnp.maximum(f, TINY)) * log2e
    kk = (1.0 - lb) * _sigmoid(-fz)

    b = _dot_sel(cum, logf2)
    tot = _dot_sel(tot_sel, logf2)
    q_in = q * jnp.exp2(b)
    k_out = kk * jnp.exp2(tot - b)
    d_sub = jnp.exp2(tot)

    n_sub = CHUNK // SUB
    t_local = lax.broadcasted_iota(jnp.int32, (CHUNK, A_WIDTH), 0) % SUB

    c3 = (b - jnp.log(kk) * log2e).reshape(n_sub, SUB, A_WIDTH)
    pieces = []
    for s in range(SUB):
        cs = jnp.broadcast_to(c3[:, s:s + 1, :], (n_sub, SUB, A_WIDTH)).reshape(CHUNK, A_WIDTH)
        valid = (t_local <= s) if reverse else (t_local >= s)
        piece = q * jnp.exp2(jnp.where(valid, b - cs, MASK_VALUE))
        pieces.append(piece if precise else piece.astype(BF16))
    w_cat = jnp.concatenate(pieces, axis=1)
    s_local = _mm_sel(w_cat, e_sel, precise)
    p = _mm_sel(s_local, x_sel, precise) * p_mask
    v_bd = jnp.concatenate([v] * A_HEADS, axis=0) * v_mask
    o_intra = _mm(p, v_bd, precise)

    v_op = v if precise else v.astype(BF16)
    k_op = k_out if precise else k_out.astype(BF16)
    q_op = q_in if precise else q_in.astype(BF16)
    o_inter = [None] * n_sub
    order = range(n_sub - 1, -1, -1) if reverse else range(n_sub)
    for j in order:
        rows = slice(SUB * j, SUB * (j + 1))
        o_inter[j] = _mm(q_op[rows], st, precise, _NT)
        u_t = _mm(v_op[rows], k_op[rows], precise, _TN)
        st = d_sub[SUB * j:SUB * j + 1, :] * st + head_mask * u_t
    return o_intra + jnp.concatenate(o_inter, axis=0), st


def _hgrn_kernel(rowf_ref, rowb_ref, first_ref, last_ref, s0idx_ref, isctx_ref, soutidx_ref,
                 qaf_ref, iaf_ref, fzf_ref, qab_ref, iab_ref, fzb_ref, lb_ref, s0_ref,
                 cumf_ref, cumb_ref, tot_ref, e_ref, x_ref, pm_ref, vm_ref, hm_ref,
                 of_ref, ob_ref, sout_ref, st_ref, *, precise_ctx):
    del rowf_ref, rowb_ref, s0idx_ref, soutidx_ref
    step = pl.program_id(0)
    is_ctx = isctx_ref[step] == 1

    @pl.when(first_ref[step] == 1)
    def _():
        st_ref[...] = s0_ref[...] * jnp.where(is_ctx, 0.0, 1.0)

    def run(precise):
        shared = (tot_ref[...], e_ref[...], x_ref[...], pm_ref[...], vm_ref[...], hm_ref[...])
        lb = lb_ref[...]
        o_f, st_f = _hgrn_chunk(qaf_ref[...], iaf_ref[...], fzf_ref[...], lb[0:1], st_ref[0],
                                (cumf_ref[...],) + shared, False, precise)
        o_b, st_b = _hgrn_chunk(qab_ref[...], iab_ref[...], fzb_ref[...], lb[1:2], st_ref[1],
                                (cumb_ref[...],) + shared, True, precise)
        of_ref[...] = o_f
        ob_ref[...] = o_b
        st_ref[0] = st_f
        st_ref[1] = st_b

    if precise_ctx:
        pl.when(is_ctx)(functools.partial(run, True))
        pl.when(jnp.logical_not(is_ctx))(functools.partial(run, False))
    else:
        run(False)

    @pl.when(last_ref[step] == 1)
    def _():
        sout_ref[...] = st_ref[...]


def _hgrn_tables(dims):
    rowf, rowb, first, last, s0idx, isctx, sout = [], [], [], [], [], [], []
    n_ctx_seq = dims["n_ctx_seq"]
    for kind, n_seq, t in (("lat", dims["n_lat_seq"], dims["t_lat"]), ("ctx", n_ctx_seq, dims["t_ctx"])):
        base = 0 if kind == "lat" else dims["n_lat"] // CHUNK
        nc = t // CHUNK
        for s in range(n_seq):
            for c in range(nc):
                rowf.append(base + s * nc + c)
                rowb.append(base + s * nc + nc - 1 - c)
                first.append(int(c == 0))
                last.append(int(c == nc - 1))
                s0idx.append(s if kind == "lat" else 0)
                isctx.append(int(kind == "ctx"))
                sout.append(n_ctx_seq if kind == "lat" else s)
    return [jnp.asarray(np.asarray(a, np.int32)) for a in (rowf, rowb, first, last, s0idx, isctx, sout)]


def _hgrn_consts():
    r = np.arange(CHUNK)
    same = (r[:, None] // SUB) == (r[None, :] // SUB)
    cum_f = (same & (r[None, :] <= r[:, None])).astype(np.float32)
    cum_b = (same & (r[None, :] >= r[:, None])).astype(np.float32)
    tot = same.astype(np.float32)
    e = np.zeros((SUB, A_HEADS, HEAD_DIM, A_HEADS, SUB), np.float32)
    for s in range(SUB):
        for h in range(A_HEADS):
            e[s, h, :, h, s] = 1.0
    e = e.reshape(SUB * A_WIDTH, A_HEADS * SUB)
    x = np.zeros((A_HEADS, SUB, A_HEADS, CHUNK), np.float32)
    for h in range(A_HEADS):
        for s in range(CHUNK):
            x[h, s % SUB, h, s] = 1.0
    x = x.reshape(A_HEADS * SUB, A_HEADS * CHUNK)
    p_mask = np.tile(tot, (1, A_HEADS))
    hv = np.arange(A_HEADS * CHUNK) // CHUNK
    hk = np.arange(A_WIDTH) // HEAD_DIM
    v_mask = (hv[:, None] == hk[None, :]).astype(np.float32)
    head_mask = (hk[:, None] == hk[None, :]).astype(np.float32)
    sel = [jnp.asarray(a, BF16) for a in (cum_f, cum_b, tot, e, x)]
    return sel + [jnp.asarray(a, F32) for a in (p_mask, v_mask, head_mask)]


def _hgrn_scan(za, lb2, s0_t, dims, precise_ctx):
    n = za.shape[0]
    tables = _hgrn_tables(dims)
    n_steps = tables[0].shape[0]
    consts = _hgrn_consts()
    const = lambda a: pl.BlockSpec(a.shape, lambda i, *_: (0,) * a.ndim)
    fwd = lambda c: pl.BlockSpec((CHUNK, A_WIDTH), lambda i, rf, rb, *_: (rf[i], c))
    bwd = lambda c: pl.BlockSpec((CHUNK, A_WIDTH), lambda i, rf, rb, *_: (rb[i], c))
    grid_spec = pltpu.PrefetchScalarGridSpec(
        num_scalar_prefetch=7,
        grid=(n_steps,),
        in_specs=[
            fwd(0), fwd(1), fwd(2), bwd(0), bwd(1), bwd(3),
            const(lb2),
            pl.BlockSpec((None, 2, A_WIDTH, A_WIDTH), lambda i, rf, rb, fi, la, s0, *_: (s0[i], 0, 0, 0)),
        ] + [const(c) for c in consts],
        out_specs=[
            fwd(0), bwd(0),
            pl.BlockSpec((None, 2, A_WIDTH, A_WIDTH), lambda i, rf, rb, fi, la, s0, ic, so: (so[i], 0, 0, 0)),
        ],
        scratch_shapes=[pltpu.VMEM((2, A_WIDTH, A_WIDTH), F32)],
    )
    return pl.pallas_call(
        functools.partial(_hgrn_kernel, precise_ctx=precise_ctx),
        grid_spec=grid_spec,
        out_shape=[
            jax.ShapeDtypeStruct((n, A_WIDTH), F32),
            jax.ShapeDtypeStruct((n, A_WIDTH), F32),
            jax.ShapeDtypeStruct((dims["n_ctx_seq"] + 1, 2, A_WIDTH, A_WIDTH), F32),
        ],
        compiler_params=_cparams("arbitrary"),
        name="hgrn_precise_ctx" if precise_ctx else "hgrn",
    )(*tables, za, za, za, za, za, za, lb2, s0_t, *consts)


def _attend(q3, segments, sink_row, precise=False):
    scores = []
    for k, _, valid in segments:
        s = _mm(k, q3, precise, _NT)
        if valid is not None:
            s = jnp.where(valid, s, MASK_VALUE)
        scores.append(s)
    m = scores[0].max(axis=0, keepdims=True)
    for s in scores[1:]:
        m = jnp.maximum(m, s.max(axis=0, keepdims=True))
    if sink_row is not None:
        m = jnp.maximum(m, sink_row)
    denom = jnp.exp2(sink_row - m) if sink_row is not None else 0.0
    acc = 0.0
    for s, (_, v, _) in zip(scores, segments):
        p = jnp.exp2(s - m)
        denom = denom + p.sum(axis=0, keepdims=True)
        acc = acc + _mm(v, p, precise, _TN)
    return (acc / denom).T


def _attend_vt(q3, segments, sink_row):
    scores = []
    for k, _, valid in segments:
        s = lax.dot_general(k, q3, _NT, preferred_element_type=F32)
        if valid is not None:
            s = jnp.where(valid, s, MASK_VALUE)
        scores.append(s)
    m = scores[0].max(axis=0, keepdims=True)
    for s in scores[1:]:
        m = jnp.maximum(m, s.max(axis=0, keepdims=True))
    if sink_row is not None:
        m = jnp.maximum(m, sink_row)
    acc = 0.0
    for s, (_, v_t, _) in zip(scores, segments):
        acc = acc + _dot(v_t, jnp.exp2(s - m).astype(BF16))
    denom = acc[KV_WIDTH:KV_WIDTH + 1]
    if sink_row is not None:
        denom = denom + jnp.exp2(sink_row - m)
    return (acc[:KV_WIDTH] / denom).T


def _fill_v_t(vt_ref, v_parts):
    v_all = jnp.concatenate([v.astype(F32) for v in v_parts], axis=0)
    vt_ref[0:KV_WIDTH, :] = v_all.T.astype(BF16)
    vt_ref[KV_WIDTH:, :] = jnp.ones((KV_WIDTH, v_all.shape[0]), BF16)


def _stack_heads(q, kvh):
    base = kvh * GQA_GROUP
    return jnp.concatenate([q[:, LANES * (base + g):LANES * (base + g + 1)] for g in range(GQA_GROUP)], axis=0)


def _merge_heads(res, tq):
    lane = lax.broadcasted_iota(jnp.int32, (tq, LANES), 1)
    heads = []
    for j in range(B_Q_HEADS):
        kvh, g = divmod(j, GQA_GROUP)
        o = res[kvh][g * tq:(g + 1) * tq]
        if kvh != j % 2:
            o = pltpu.roll(o, HEAD_DIM, 1)
        heads.append(o)
    blocks = [jnp.where(lane < HEAD_DIM, heads[2 * b], heads[2 * b + 1]) for b in range(B_Q_HEADS // 2)]
    return jnp.concatenate(blocks, axis=1)


def _sink_row(sink_ref, kvh, tq):
    return jnp.concatenate(
        [jnp.full((1, tq), sink_ref[kvh * GQA_GROUP + g] * LOG2E, F32) for g in range(GQA_GROUP)], axis=1)


def _global_attn_kernel(q_ref, kv_ref, ck_ref, cv_ref, o_ref, *, tq):
    q = q_ref[...]
    kv = kv_ref[...]
    k, v = kv[:, :KV_WIDTH], kv[:, KV_WIDTH:]
    ck, cv = ck_ref[...], cv_ref[...]
    res = [_attend(_stack_heads(q, kvh), [(k, v, None), (ck, cv, None)], None) for kvh in range(B_KV_HEADS)]
    o_ref[...] = _merge_heads(res, tq).astype(BF16)


def _global_attention(q_att, kv, ck, cv, dims):
    t, n_seq = dims["t_lat"], dims["n_lat_seq"]
    tq = 256
    nq = t // tq
    return pl.pallas_call(
        functools.partial(_global_attn_kernel, tq=tq),
        grid=(n_seq, nq),
        in_specs=[
            pl.BlockSpec((tq, 2 * Q_WIDTH), lambda b, i: (b * nq + i, 0)),
            pl.BlockSpec((t, 2 * KV_WIDTH), lambda b, i: (b, 0)),
            pl.BlockSpec((None,) + ck.shape[1:], lambda b, i: (b, 0, 0)),
            pl.BlockSpec((None,) + cv.shape[1:], lambda b, i: (b, 0, 0)),
        ],
        out_specs=pl.BlockSpec((tq, Q_WIDTH), lambda b, i: (b * nq + i, 0)),
        out_shape=jax.ShapeDtypeStruct((n_seq * t, Q_WIDTH), BF16),
        compiler_params=_cparams("parallel", "arbitrary"),
        name="global_attention",
    )(q_att, kv, ck, cv)


def _window_attn_kernel(sink_ref, q_ref, kv_ref, ck_ref, cv_ref, o_ref, vt_ref, *, tq, t):
    j = pl.program_id(1)

    @pl.when(j == 0)
    def _():
        _fill_v_t(vt_ref, [kv_ref[:, KV_WIDTH:], cv_ref[...]])

    span = tq + 2 * WINDOW
    start = pl.multiple_of(jnp.clip(j * tq - WINDOW, 0, t - span), WINDOW)
    q = q_ref[...]
    k = kv_ref[pl.ds(start, span), :KV_WIDTH]
    ck = ck_ref[...]
    key_pos = start + lax.broadcasted_iota(jnp.int32, (span, GQA_GROUP * tq), 0)
    query_pos = j * tq + lax.broadcasted_iota(jnp.int32, (span, GQA_GROUP * tq), 1) % tq
    valid = jnp.abs(query_pos - key_pos) <= WINDOW
    segments = [(k, vt_ref[:, pl.ds(start, span)], valid), (ck, vt_ref[:, t:], None)]
    res = [_attend_vt(_stack_heads(q, kvh), segments, _sink_row(sink_ref, kvh, tq))
           for kvh in range(C_KV_HEADS)]
    o_ref[...] = _merge_heads(res, tq).astype(BF16)


def _window_attention(sink, q_att, kv, ck, cv, dims):
    t, n_seq = dims["t_lat"], dims["n_lat_seq"]
    tq = 2 * WINDOW
    nq = t // tq
    return pl.pallas_call(
        functools.partial(_window_attn_kernel, tq=tq, t=t),
        grid=(n_seq, nq),
        in_specs=[
            pl.BlockSpec(memory_space=pltpu.SMEM),
            pl.BlockSpec((tq, 2 * Q_WIDTH), lambda b, i: (b * nq + i, 0)),
            pl.BlockSpec((t, 2 * KV_WIDTH), lambda b, i: (b, 0)),
            pl.BlockSpec((None,) + ck.shape[1:], lambda b, i: (b, 0, 0)),
            pl.BlockSpec((None,) + cv.shape[1:], lambda b, i: (b, 0, 0)),
        ],
        out_specs=pl.BlockSpec((tq, Q_WIDTH), lambda b, i: (b * nq + i, 0)),
        out_shape=jax.ShapeDtypeStruct((n_seq * t, Q_WIDTH), BF16),
        scratch_shapes=[pltpu.VMEM((2 * KV_WIDTH, t + ck.shape[1]), BF16)],
        compiler_params=_cparams("parallel", "arbitrary"),
        name="window_attention",
    )(sink, q_att, kv, ck, cv)


def _context_attn_kernel(sink_ref, qg_ref, qw_ref, kv_ref, og_ref, ow_ref, *, t, precise):
    kv = kv_ref[...]
    for q_ref, o_ref, off, use_sink in ((qg_ref, og_ref, 0, False), (qw_ref, ow_ref, 2 * KV_WIDTH, True)):
        q = q_ref[...]
        k, v = kv[:, off:off + KV_WIDTH], kv[:, off + KV_WIDTH:off + 2 * KV_WIDTH]
        res = [_attend(_stack_heads(q, kvh), [(k, v, None)],
                       _sink_row(sink_ref, kvh, t) if use_sink else None, precise)
               for kvh in range(B_KV_HEADS)]
        o_ref[...] = _merge_heads(res, t)


def _context_attention(sink, qg32, qw32, kv32, dims, precise):
    t, n_seq = dims["t_ctx"], dims["n_ctx_seq"]
    seq = lambda w: pl.BlockSpec((t, w), lambda b: (b, 0))
    return pl.pallas_call(
        functools.partial(_context_attn_kernel, t=t, precise=precise),
        grid=(n_seq,),
        in_specs=[pl.BlockSpec(memory_space=pltpu.SMEM), seq(2 * Q_WIDTH), seq(2 * Q_WIDTH), seq(4 * KV_WIDTH)],
        out_specs=[seq(Q_WIDTH), seq(Q_WIDTH)],
        out_shape=[jax.ShapeDtypeStruct((n_seq * t, Q_WIDTH), F32)] * 2,
        compiler_params=_cparams("parallel"),
        name="context_attention_precise" if precise else "context_attention",
    )(sink, qg32, qw32, kv32)


def _route(scores_t, sel_t):
    s = [scores_t[e:e + 1, :] for e in range(N_EXPERTS)]
    z = [sel_t[e:e + 1, :] for e in range(N_EXPERTS)]
    gs = []
    for g in range(N_GROUPS):
        m = z[g * EXPERTS_PER_GROUP:(g + 1) * EXPERTS_PER_GROUP]
        best = None
        for a in range(EXPERTS_PER_GROUP):
            for b in range(a + 1, EXPERTS_PER_GROUP):
                pair = m[a] + m[b]
                best = pair if best is None else jnp.maximum(best, pair)
        gs.append(best)
    combine, groups = [], []
    for g in range(N_GROUPS):
        chosen_g = None
        for g2 in range(N_GROUPS):
            if g2 == g:
                continue
            c = (gs[g] > gs[g2]) if g2 < g else (gs[g] >= gs[g2])
            chosen_g = c if chosen_g is None else jnp.logical_and(chosen_g, c)
        base = g * EXPERTS_PER_GROUP
        picked = []
        for a in range(EXPERTS_PER_GROUP):
            rank = 0.0
            for b in range(EXPERTS_PER_GROUP):
                if b == a:
                    continue
                ahead = (z[base + b] >= z[base + a]) if b < a else (z[base + b] > z[base + a])
                rank = rank + ahead.astype(F32)
            picked.append(jnp.where(jnp.logical_and(chosen_g, rank < 2.0), s[base + a], 0.0))
        denom = picked[0] + picked[1] + picked[2] + picked[3]
        denom = jnp.where(chosen_g, denom, 1.0)
        combine.extend(pk / denom for pk in picked)
        groups.append(jnp.where(chosen_g, 1.0, 0.0))
    return jnp.concatenate(combine, axis=0), jnp.concatenate(groups, axis=0)


def _outproj_kernel(*refs, lat_tiles, precise_ctx):
    if precise_ctx:
        (of_ref, ob_ref, ga_ref, ogl_ref, ogc_ref, owl_ref, owc_ref, xl_ref, xc_ref, mod_ref, ag_ref,
         ones_ref, w_ref, wlo_ref, g2_ref, rw_ref, rb_ref, before_ref,
         x1_ref, h2_ref, comb_ref, pos_ref, cnt_ref, mix_ref) = refs
    else:
        (of_ref, ob_ref, ga_ref, ogl_ref, ogc_ref, owl_ref, owc_ref, xl_ref, xc_ref, mod_ref, ag_ref,
         ones_ref, w_ref, g2_ref, rw_ref, rb_ref, before_ref,
         x1_ref, h2_ref, comb_ref, pos_ref, cnt_ref, mix_ref) = refs
        wlo_ref = None
    is_lat = pl.program_id(0) < lat_tiles
    o = of_ref[...] + ob_ref[...]
    oa = _head_rms(o, ag_ref[...], ones_ref[...]) * _sigmoid(ga_ref[...])
    parts = ((0, A_WIDTH), (A_WIDTH, A_WIDTH + Q_WIDTH), (A_WIDTH + Q_WIDTH, D_MODEL))

    def mix(operands, precise):
        acc = 0.0
        for a, (r0, r1) in zip(operands, parts):
            if precise:
                ah, al = _split2(a)
                acc = acc + _dot(ah, w_ref[r0:r1, :]) + _dot(al, w_ref[r0:r1, :]) + _dot(ah, wlo_ref[r0:r1, :])
            else:
                acc = acc + _dot(a.astype(BF16), w_ref[r0:r1, :])
        mix_ref[...] = acc

    pl.when(is_lat)(lambda: mix((oa, ogl_ref[...], owl_ref[...]), False))
    pl.when(jnp.logical_not(is_lat))(lambda: mix((oa, ogc_ref[...], owc_ref[...]), precise_ctx))

    mod = mod_ref[...]
    gate1 = mod[:, 2 * D_MODEL:3 * D_MODEL]
    sh2 = mod[:, 3 * D_MODEL:4 * D_MODEL]
    sc2 = mod[:, 4 * D_MODEL:5 * D_MODEL]
    x1 = jnp.where(is_lat, xl_ref[...], xc_ref[...]) + gate1 * mix_ref[...]
    x1_ref[...] = x1
    y = x1 * lax.rsqrt(jnp.mean(x1 * x1, axis=-1, keepdims=True) + EPS) * g2_ref[...]
    h2 = y * (1.0 + sc2) + sh2
    h2_ref[...] = h2.astype(BF16)
    logits_t = _mm(rw_ref[...], h2, True, _NT)
    scores_t = _sigmoid(logits_t)
    comb_t, group_t = _route(scores_t, scores_t + rb_ref[...])
    comb_ref[...] = comb_t
    rank = _dot(group_t.astype(BF16), before_ref[...])
    counts = group_t.sum(axis=1, keepdims=True)
    pos, offset = 0.0, 0.0
    for g in range(N_GROUPS):
        pos = pos + group_t[g:g + 1] * (rank[g:g + 1] + offset)
        offset = offset + counts[g:g + 1]
    pos_ref[...] = pos
    pad = jnp.zeros((8 - N_GROUPS, 1), F32)
    cnt_ref[...] = jnp.broadcast_to(jnp.concatenate([counts, pad], axis=0), (8, LANES)).astype(jnp.int32)


def _out_projection(o_f, o_b, za, og_lat, og_ctx, ow_lat, ow_ctx, x_pair, mod, an_g, ones_blk, w_hi, w_lo, g2,
                    rw_t, rb, dims):
    n = dims["n_lat"] + dims["n_ctx"]
    tm = TOKEN_TILE
    n_tiles = n // tm
    lat_tiles = dims["n_lat"] // tm
    tiles_per_seq = dims["t_lat"] // tm
    ctx_cond = dims["ctx_cond"]
    precise_ctx = w_lo is not None
    cond_idx = lambda i: jnp.where(i < lat_tiles, i // tiles_per_seq, ctx_cond)
    row = lambda w: pl.BlockSpec((tm, w), lambda i: (i, 0))
    lat = lambda w: pl.BlockSpec((tm, w), lambda i: (jnp.minimum(i, lat_tiles - 1), 0))
    ctx = lambda w: pl.BlockSpec((tm, w), lambda i: (jnp.maximum(i - lat_tiles, 0), 0))
    const = lambda a: pl.BlockSpec(a.shape, lambda i: (0,) * a.ndim)
    weights = [w_hi, w_lo] if precise_ctx else [w_hi]
    t_idx = np.arange(tm)
    before = jnp.asarray((t_idx[:, None] < t_idx[None, :]).astype(np.float32), BF16)
    return pl.pallas_call(
        functools.partial(_outproj_kernel, lat_tiles=lat_tiles, precise_ctx=precise_ctx),
        grid=(n_tiles,),
        in_specs=[
            row(A_WIDTH), row(A_WIDTH),
            pl.BlockSpec((tm, A_WIDTH), lambda i: (i, 4)),
            lat(Q_WIDTH), ctx(Q_WIDTH), lat(Q_WIDTH), ctx(Q_WIDTH)] + _token_specs(x_pair, tm, lat_tiles) + [
            pl.BlockSpec((None, 1, 6 * D_MODEL), lambda i: (cond_idx(i), 0, 0)),
            const(an_g), const(ones_blk)] + [const(w) for w in weights] + [
            const(g2), const(rw_t), const(rb), const(before),
        ],
        out_specs=[row(D_MODEL), row(D_MODEL), pl.BlockSpec((N_EXPERTS, tm), lambda i: (0, i)),
                   pl.BlockSpec((1, tm), lambda i: (0, i)), pl.BlockSpec((8, LANES), lambda i: (i, 0))],
        out_shape=[
            jax.ShapeDtypeStruct((n, D_MODEL), F32),
            jax.ShapeDtypeStruct((n, D_MODEL), BF16),
            jax.ShapeDtypeStruct((N_EXPERTS, n), F32),
            jax.ShapeDtypeStruct((1, n), F32),
            jax.ShapeDtypeStruct((n_tiles * 8, LANES), jnp.int32),
        ],
        scratch_shapes=[pltpu.VMEM((tm, D_MODEL), F32)],
        compiler_params=_cparams("arbitrary"),
        name="out_projection_precise_ctx" if precise_ctx else "out_projection",
    )(o_f, o_b, za, og_lat, og_ctx, ow_lat, ow_ctx, *x_pair, mod, an_g, ones_blk, *weights, g2, rw_t, rb, before)


def _moe_kernel(off_ref, h_ref, comb_ref, pos_ref, x1_ref, mod_ref, wgu_ref, wd_ref, fg_ref, *rest,
                final, lat_tiles):
    *outs, hs_ref, cs_ref, acc_ref = rest
    i = pl.program_id(0)
    tm = h_ref.shape[0]
    row = lax.broadcasted_iota(jnp.int32, (tm, tm), 0).astype(F32)
    perm = jnp.where(row == pos_ref[...], 1.0, 0.0).astype(BF16)
    hs_ref[...] = _dot(perm, h_ref[...]).astype(BF16)
    cs_ref[...] = sum(lax.dot_general(perm, c, _NT, preferred_element_type=F32)
                      for c in _split3(comb_ref[...]))
    acc_ref[...] = jnp.zeros_like(acc_ref)
    lane = lax.broadcasted_iota(jnp.int32, (MOE_BLOCK, N_EXPERTS), 1)

    for r in range(tm // MOE_BLOCK):
        rows = slice(r * MOE_BLOCK, (r + 1) * MOE_BLOCK)

        def group_body(g, carry, rows=rows, r=r):
            has_tokens = jnp.logical_and(off_ref[i, g] < (r + 1) * MOE_BLOCK, off_ref[i, g + 1] > r * MOE_BLOCK)

            @pl.when(has_tokens)
            def _():
                h = hs_ref[rows, :]
                cs = cs_ref[rows, :]
                y = 0.0
                for a in range(EXPERTS_PER_GROUP):
                    e = g * EXPERTS_PER_GROUP + a
                    gu = _dot(h, wgu_ref[e])
                    gate, up = gu[:, :D_EXPERT], gu[:, D_EXPERT:]
                    ce = jnp.sum(jnp.where(lane == e, cs, 0.0), axis=1, keepdims=True)
                    y = y + _dot((_silu(gate) * up * ce).astype(BF16), wd_ref[e])
                acc_ref[rows, :] += y

            return carry

        lax.fori_loop(0, N_GROUPS, group_body, 0)

    y_hi, y_lo = _split2(acc_ref[...])
    y = (lax.dot_general(perm, y_hi, _TN, preferred_element_type=F32)
         + lax.dot_general(perm, y_lo, _TN, preferred_element_type=F32))
    x2 = x1_ref[...] + mod_ref[...][:, 5 * D_MODEL:] * y
    if not final:
        outs[0][...] = x2
    else:
        yn = x2 * lax.rsqrt(jnp.mean(x2 * x2, axis=-1, keepdims=True) + EPS) * fg_ref[...]
        is_lat = i < lat_tiles

        @pl.when(is_lat)
        def _():
            outs[0][...] = yn

        @pl.when(jnp.logical_not(is_lat))
        def _():
            outs[1][...] = yn


def _moe(h2, comb_t, pos, counts, x1, mod, w_gu, w_d, final_g, dims, final):
    n = x1.shape[0]
    tm = TOKEN_TILE
    lat_tiles = dims["n_lat"] // tm
    tiles_per_seq = dims["t_lat"] // tm
    ctx_cond = dims["ctx_cond"]
    cond_idx = lambda i: jnp.where(i < lat_tiles, i // tiles_per_seq, ctx_cond)
    offsets = jnp.concatenate([jnp.zeros((n // tm, 1), jnp.int32), jnp.cumsum(counts, axis=1)], axis=1)
    if final:
        out_specs = [pl.BlockSpec((tm, D_MODEL), lambda i, *_: (jnp.minimum(i, lat_tiles - 1), 0)),
                     pl.BlockSpec((tm, D_MODEL), lambda i, *_: (jnp.maximum(i - lat_tiles, 0), 0))]
        out_shape = [jax.ShapeDtypeStruct((dims["n_lat"], D_MODEL), F32),
                     jax.ShapeDtypeStruct((dims["n_ctx"], D_MODEL), F32)]
    else:
        out_specs = pl.BlockSpec((tm, D_MODEL), lambda i, *_: (i, 0))
        out_shape = jax.ShapeDtypeStruct((n, D_MODEL), F32)
    resident = lambda a: pl.BlockSpec(a.shape, lambda i, *_: (0,) * a.ndim, pipeline_mode=pl.Buffered(1))
    grid_spec = pltpu.PrefetchScalarGridSpec(
        num_scalar_prefetch=1,
        grid=(n // tm,),
        in_specs=[
            pl.BlockSpec((tm, D_MODEL), lambda i, *_: (i, 0)),
            pl.BlockSpec((N_EXPERTS, tm), lambda i, *_: (0, i)),
            pl.BlockSpec((1, tm), lambda i, *_: (0, i)),
            pl.BlockSpec((tm, D_MODEL), lambda i, *_: (i, 0)),
            pl.BlockSpec((None, 1, 6 * D_MODEL), lambda i, *_: (cond_idx(i), 0, 0)),
            resident(w_gu), resident(w_d),
            pl.BlockSpec((1, D_MODEL), lambda i, *_: (0, 0)),
        ],
        out_specs=out_specs,
        scratch_shapes=[pltpu.VMEM((tm, D_MODEL), BF16), pltpu.VMEM((tm, N_EXPERTS), F32),
                        pltpu.VMEM((tm, D_MODEL), F32)],
    )
    return pl.pallas_call(
        functools.partial(_moe_kernel, final=final, lat_tiles=lat_tiles),
        grid_spec=grid_spec,
        out_shape=out_shape,
        compiler_params=_cparams("arbitrary"),
        name="moe_final" if final else "moe",
    )(offsets, h2, comb_t, pos, x1, mod, w_gu, w_d, final_g)


def _rope_tables(t, tile):
    n_freq = HEAD_DIM // 4
    pos = jnp.arange(t)
    freqs = ROPE_THETA ** (-jnp.arange(n_freq, dtype=F32) / n_freq)
    ang_row = (pos // GRID_W).astype(F32)[:, None] * freqs
    ang_col = (pos % GRID_W).astype(F32)[:, None] * freqs
    cos_h = jnp.concatenate([jnp.cos(ang_row)] * 2 + [jnp.cos(ang_col)] * 2, axis=1)
    sin_h = jnp.concatenate([-jnp.sin(ang_row), jnp.sin(ang_row), -jnp.sin(ang_col), jnp.sin(ang_col)], axis=1)
    cos_t = jnp.concatenate([jnp.ones((tile, LANES), F32), jnp.concatenate([cos_h, cos_h], axis=1)], axis=0)
    sin_t = jnp.concatenate([jnp.zeros((tile, LANES), F32), jnp.concatenate([sin_h, sin_h], axis=1)], axis=0)
    return cos_t, sin_t


def _head_ones(width):
    r = np.arange(width) // HEAD_DIM
    return jnp.asarray((r[:, None] == r[None, :]).astype(np.float32), BF16)


def _split_kernel(w_ref, hi_ref, lo_ref):
    hi, lo = _split2(w_ref[...])
    hi_ref[...] = hi
    lo_ref[...] = lo


def _hi_lo(w, want_lo):
    if not want_lo:
        return w.astype(BF16), None
    rows, width = w.shape
    tr = 256
    spec = pl.BlockSpec((tr, width), lambda i: (i, 0))
    return pl.pallas_call(
        _split_kernel,
        grid=(rows // tr,),
        in_specs=[spec],
        out_specs=[spec, spec],
        out_shape=[jax.ShapeDtypeStruct(w.shape, BF16)] * 2,
        compiler_params=_cparams("parallel"),
        name="split_weight",
    )(w)


def _state_to_block_diag_t(s):
    st = jnp.swapaxes(s, -1, -2)
    eye = jnp.eye(A_HEADS, dtype=s.dtype)
    return jnp.einsum("bdhvk,hg->bdhvgk", st, eye).reshape(s.shape[:2] + (A_WIDTH, A_WIDTH))


def _block_diag_t_to_state(st):
    s6 = st.reshape(st.shape[:2] + (A_HEADS, HEAD_DIM, A_HEADS, HEAD_DIM))
    diag = jnp.stack([s6[:, :, h, :, h, :] for h in range(A_HEADS)], axis=2)
    return jnp.swapaxes(diag, -1, -2)


def kernel(x_prompt, x_sample, cache_glob_k, cache_glob_v, cache_win_k, cache_win_v, state_hgrn, c, c_ctx,
           mod_w, mod_b, norm1_g, norm2_g, w_in, w_out, hgrn_lb, hgrn_norm_g, q_norm_g, k_norm_g, win_sink,
           router_w, router_b, w_gate_up, w_down, final_g):
    n_ctx_seq, t_ctx, d = x_prompt.shape
    n_lat_seq, t_lat, _ = x_sample.shape
    depth = mod_w.shape[0]
    past = cache_glob_k.shape[2]
    n_lat, n_ctx = n_lat_seq * t_lat, n_ctx_seq * t_ctx
    dims = dict(n_ctx_seq=n_ctx_seq, t_ctx=t_ctx, n_lat_seq=n_lat_seq, t_lat=t_lat,
                n_lat=n_lat, n_ctx=n_ctx, ctx_cond=n_lat_seq)
    assert d == D_MODEL and n_lat_seq < COND_ROWS
    assert n_ctx % TOKEN_TILE == 0 and t_lat % TOKEN_TILE == 0
    assert t_ctx % CHUNK == 0 and t_lat >= 4 * WINDOW

    x_pair = (x_sample.reshape(n_lat, d), x_prompt.reshape(n_ctx, d))
    cond = jnp.concatenate([c, c_ctx[None, :], jnp.zeros((COND_ROWS - n_lat_seq - 1, d), F32)], axis=0)
    mod = _modulation(cond, mod_w, mod_b).reshape(depth, COND_ROWS, 1, 6 * d)

    p_lb = jax.nn.softmax(hgrn_lb.astype(F32), axis=1)
    lbs = jnp.cumsum(p_lb, axis=1) - p_lb[:, :1]

    cos_t, sin_t = _rope_tables(t_lat, TOKEN_TILE)
    ones128, ones256 = _head_ones(LANES), _head_ones(A_WIDTH)
    tile2 = lambda g, reps: jnp.tile(g, reps)[None, :]
    rw_t = router_w.T
    rb = router_b[:, None]
    fg = final_g[None, :]

    new_kv, new_state = [], []
    for l in range(depth):
        precise = l < depth - 1
        win_hi, win_lo = _hi_lo(w_in[l], precise)
        za, qg_att, kvg, qw_att, kvw, kv32, qg32, qw32 = _in_projection(
            x_pair, mod[l], norm1_g[l][None, :], win_hi, win_lo, cos_t, sin_t,
            tile2(q_norm_g[l], 2), tile2(k_norm_g[l], 2), ones128, dims)

        s0 = state_hgrn[:, l].astype(F32)
        o_f, o_b, s_fin = _hgrn_scan(za, lbs[:, l], _state_to_block_diag_t(s0), dims, precise)

        cast_cache = lambda a: a[:, l].reshape(n_lat_seq, past, KV_WIDTH).astype(BF16)
        og_lat = _global_attention(qg_att, kvg, cast_cache(cache_glob_k), cast_cache(cache_glob_v), dims)
        ow_lat = _window_attention(win_sink[l], qw_att, kvw, cast_cache(cache_win_k),
                                   cast_cache(cache_win_v), dims)
        og_ctx, ow_ctx = _context_attention(win_sink[l], qg32, qw32, kv32, dims, precise)

        wout_hi, wout_lo = _hi_lo(w_out[l], precise)
        x1, h2, comb_t, pos, cnt = _out_projection(
            o_f, o_b, za, og_lat, og_ctx, ow_lat, ow_ctx, x_pair, mod[l], tile2(hgrn_norm_g[l], A_HEADS),
            ones256, wout_hi, wout_lo, norm2_g[l][None, :], rw_t, rb, dims)

        counts = cnt.reshape(-1, 8, LANES)[:, :N_GROUPS, 0]
        x = _moe(h2, comb_t, pos, counts, x1, mod[l], w_gate_up[l].astype(BF16), w_down[l].astype(BF16), fg,
                 dims, final=(l == depth - 1))
        x_pair = (x, x)

        new_kv.append(kv32[:n_ctx].reshape(n_ctx_seq, t_ctx, 4, B_KV_HEADS, HEAD_DIM))
        new_state.append(_block_diag_t_to_state(s_fin[:n_ctx_seq]))

    y_sample = x[0].reshape(n_lat_seq, t_lat, d)
    y_prompt = x[1].reshape(n_ctx_seq, t_ctx, d)
    kv = jnp.stack(new_kv, axis=1)
    return (y_prompt, y_sample, kv[:, :, :, 0], kv[:, :, :, 1], kv[:, :, :, 2], kv[:, :, :, 3],
            jnp.stack(new_state, axis=1).astype(x_prompt.dtype))
```

```python
import functools

import numpy as np
import jax
import jax.numpy as jnp
from jax import lax
from jax.experimental import pallas as pl
from jax.experimental.pallas import tpu as pltpu

F32 = jnp.float32
BF16 = jnp.bfloat16

D_MODEL = 1024
HEAD_DIM = 64
GRID_W = 64
A_HEADS = 4
A_WIDTH = A_HEADS * HEAD_DIM
B_Q_HEADS = 6
B_KV_HEADS = 2
C_Q_HEADS = 6
C_KV_HEADS = 2
Q_WIDTH = B_Q_HEADS * HEAD_DIM
KV_WIDTH = B_KV_HEADS * HEAD_DIM
GQA_GROUP = B_Q_HEADS // B_KV_HEADS
WINDOW = 128
ROPE_THETA = 10000.0
IN_WIDTH = 5 * A_WIDTH + 2 * (Q_WIDTH + 2 * KV_WIDTH)
N_EXPERTS = 16
N_GROUPS = 4
EXPERTS_PER_GROUP = N_EXPERTS // N_GROUPS
D_EXPERT = 256
EPS = 1e-6
MASK_VALUE = -1e30
TINY = 1e-30
LOG2E = 1.4426950408889634

LANES = 128
SUB = 16
CHUNK = 128
HGRN_SEQS = 2
TOKEN_TILE = 512
MOE_BLOCK = 128
KEY_CHUNK = 256
COND_ROWS = 16
VMEM_LIMIT = 58 * 1024 * 1024

_NT = (((1,), (1,)), ((), ()))
_TN = (((0,), (0,)), ((), ()))


def _cparams(*sem):
    return pltpu.CompilerParams(dimension_semantics=sem, vmem_limit_bytes=VMEM_LIMIT)


def _dot(a, b):
    return jnp.dot(a, b, preferred_element_type=F32)


def _split2(x):
    h = x.astype(BF16)
    return h, (x - h.astype(F32)).astype(BF16)


def _split3(x):
    h = x.astype(BF16)
    r = x - h.astype(F32)
    m = r.astype(BF16)
    l = (r - m.astype(F32)).astype(BF16)
    return h, m, l


def _dot_sel(c, x):
    h, m, l = _split3(x)
    return _dot(c, h) + _dot(c, m) + _dot(c, l)


def _mm(a, b, precise, dims=None):
    if dims is None:
        dims = (((a.ndim - 1,), (0,)), ((), ()))
    dg = lambda x, y: lax.dot_general(x, y, dims, preferred_element_type=F32)
    if not precise:
        return dg(a.astype(BF16), b.astype(BF16))
    ah, al = _split2(a)
    bh, bl = _split2(b)
    return dg(ah, bh) + dg(ah, bl) + dg(al, bh)


def _mm_sel(a, sel, precise):
    if not precise:
        return _dot(a.astype(BF16), sel)
    ah, al = _split2(a)
    return _dot(ah, sel) + _dot(al, sel)


def _sigmoid(x):
    return jax.nn.sigmoid(x)


def _silu(x):
    return x * jax.nn.sigmoid(x)


def _mod_kernel(cond_ref, w_ref, b_ref, o_ref):
    o_ref[...] = _mm(_silu(cond_ref[...]), w_ref[...], True) + b_ref[...]


def _modulation(cond, mod_w, mod_b):
    depth, d, width = mod_w.shape
    tn = 1536
    return pl.pallas_call(
        _mod_kernel,
        grid=(depth, width // tn),
        in_specs=[
            pl.BlockSpec((COND_ROWS, d), lambda l, j: (0, 0)),
            pl.BlockSpec((None, d, tn), lambda l, j: (l, 0, j)),
            pl.BlockSpec((None, 1, tn), lambda l, j: (l, 0, j)),
        ],
        out_specs=pl.BlockSpec((None, COND_ROWS, tn), lambda l, j: (l, 0, j)),
        out_shape=jax.ShapeDtypeStruct((depth, COND_ROWS, width), F32),
        compiler_params=_cparams("parallel", "parallel"),
        name="modulation",
    )(cond, mod_w, mod_b.reshape(depth, 1, width))


def _head_rms(xb, g2, ones_blk):
    ss = _mm_sel(xb * xb, ones_blk, True)
    return xb * lax.rsqrt(ss * (1.0 / HEAD_DIM) + EPS) * g2


def _rope(yb, cos, sin_signed, first16):
    partner = jnp.where(first16, pltpu.roll(yb, LANES - 16, 1), pltpu.roll(yb, 16, 1))
    return yb * cos + partner * sin_signed


def _q_attention_layout(blocks, lane_half):
    out = []
    for j in range(2 * len(blocks)):
        src = blocks[j // 2]
        dst_half = j // GQA_GROUP
        if j % 2 != dst_half:
            src = pltpu.roll(src, HEAD_DIM, 1)
        out.append(jnp.where(lane_half == dst_half, src, 0.0))
    return jnp.concatenate(out, axis=1)


def _inproj_kernel(*refs, precise_ctx, lat_tiles):
    if precise_ctx:
        (xl_ref, xc_ref, mod_ref, g1_ref, w_ref, wlo_ref, cos_ref, sin_ref, qg_ref, kg_ref, ones_ref,
         zal_ref, zac_ref, qga_ref, kvg_ref, qwa_ref, kvw_ref, kv32_ref, qg32_ref, qw32_ref, z_ref) = refs
    else:
        (xl_ref, xc_ref, mod_ref, g1_ref, w_ref, cos_ref, sin_ref, qg_ref, kg_ref, ones_ref,
         zal_ref, zac_ref, qga_ref, kvg_ref, qwa_ref, kvw_ref, kv32_ref, qg32_ref, qw32_ref, z_ref) = refs
        wlo_ref = None
    is_lat = pl.program_id(0) < lat_tiles
    x = jnp.where(is_lat, xl_ref[...], xc_ref[...])
    y = x * lax.rsqrt(jnp.mean(x * x, axis=-1, keepdims=True) + EPS) * g1_ref[...]
    mod = mod_ref[...]
    h = y * (1.0 + mod[:, D_MODEL:2 * D_MODEL]) + mod[:, 0:D_MODEL]

    def one_pass():
        z_ref[...] = _dot(h.astype(BF16), w_ref[...])

    def three_pass():
        hh, hl = _split2(h)
        z_ref[...] = _dot(hh, w_ref[...]) + _dot(hl, w_ref[...]) + _dot(hh, wlo_ref[...])

    if precise_ctx:
        pl.when(is_lat)(one_pass)
        pl.when(jnp.logical_not(is_lat))(three_pass)
    else:
        one_pass()
    z = z_ref[...]

    @pl.when(is_lat)
    def _():
        zal_ref[...] = z[:, :5 * A_WIDTH]

    @pl.when(jnp.logical_not(is_lat))
    def _():
        zac_ref[...] = z[:, :5 * A_WIDTH]

    cos = cos_ref[...]
    sin = sin_ref[...]
    ones_blk = ones_ref[...]
    lane = lax.broadcasted_iota(jnp.int32, (x.shape[0], LANES), 1)
    first16 = (lane % 32) < 16
    lane_half = lane // HEAD_DIM
    scale = LOG2E * HEAD_DIM ** -0.5

    o = 5 * A_WIDTH
    qg = [z[:, o + LANES * b:o + LANES * (b + 1)] for b in range(3)]
    o += Q_WIDTH
    kg = z[:, o:o + KV_WIDTH]
    vg = z[:, o + KV_WIDTH:o + 2 * KV_WIDTH]
    o += 2 * KV_WIDTH
    qw = [z[:, o + LANES * b:o + LANES * (b + 1)] for b in range(3)]
    o += Q_WIDTH
    kw = z[:, o:o + KV_WIDTH]
    vw = z[:, o + KV_WIDTH:o + 2 * KV_WIDTH]

    qg = [_rope(_head_rms(b, qg_ref[...], ones_blk), cos, sin, first16) * scale for b in qg]
    kg = _rope(_head_rms(kg, kg_ref[...], ones_blk), cos, sin, first16)
    qw = [_rope(b, cos, sin, first16) * scale for b in qw]
    kw = _rope(kw, cos, sin, first16)

    qg_att = _q_attention_layout(qg, lane_half)
    qw_att = _q_attention_layout(qw, lane_half)
    qga_ref[...] = qg_att.astype(BF16)
    qwa_ref[...] = qw_att.astype(BF16)
    kvg_ref[...] = jnp.concatenate([kg, vg], axis=1).astype(BF16)
    kvw_ref[...] = jnp.concatenate([kw, vw], axis=1).astype(BF16)
    kv32_ref[...] = jnp.concatenate([kg, vg, kw, vw], axis=1)
    qg32_ref[...] = qg_att
    qw32_ref[...] = qw_att


def _token_specs(x_pair, tm, lat_tiles):
    off = lat_tiles if x_pair[1] is x_pair[0] else 0
    lat = pl.BlockSpec((tm, D_MODEL), lambda i, *_: (jnp.minimum(i, lat_tiles - 1), 0))
    ctx = pl.BlockSpec((tm, D_MODEL), lambda i, *_: (jnp.maximum(i, lat_tiles) - lat_tiles + off, 0))
    return [lat, ctx]


def _in_projection(x_pair, mod, g1, w_hi, w_lo, cos_t, sin_t, qn_g, kn_g, ones_blk, dims):
    n = dims["n_lat"] + dims["n_ctx"]
    tm = TOKEN_TILE
    n_tiles = n // tm
    lat_tiles = dims["n_lat"] // tm
    ctx_tiles = n_tiles - lat_tiles
    tiles_per_seq = dims["t_lat"] // tm
    ctx_cond = dims["ctx_cond"]
    precise_ctx = w_lo is not None

    def cond_idx(i):
        return jnp.where(i < lat_tiles, i // tiles_per_seq, ctx_cond)

    def rope_idx(i):
        return jnp.where(i < lat_tiles, 1 + i % tiles_per_seq, 0)

    def ctx_idx(i):
        return jnp.where(i < lat_tiles, ctx_tiles, i - lat_tiles)

    row = lambda w: pl.BlockSpec((tm, w), lambda i: (i, 0))
    ctx_row = lambda w: pl.BlockSpec((tm, w), lambda i: (ctx_idx(i), 0))
    const = lambda a: pl.BlockSpec(a.shape, lambda i: (0,) * a.ndim, pipeline_mode=pl.Buffered(1))
    weights = [w_hi, w_lo] if precise_ctx else [w_hi]
    ctx_rows = (ctx_tiles + 1) * tm
    return pl.pallas_call(
        functools.partial(_inproj_kernel, precise_ctx=precise_ctx, lat_tiles=lat_tiles),
        grid=(n_tiles,),
        in_specs=_token_specs(x_pair, tm, lat_tiles) + [
            pl.BlockSpec((None, 1, 6 * D_MODEL), lambda i: (cond_idx(i), 0, 0)),
            const(g1)] + [const(w) for w in weights] + [
            pl.BlockSpec((tm, LANES), lambda i: (rope_idx(i), 0)),
            pl.BlockSpec((tm, LANES), lambda i: (rope_idx(i), 0)),
            const(qn_g), const(kn_g), const(ones_blk),
        ],
        out_specs=[
            pl.BlockSpec((tm, 5 * A_WIDTH), lambda i: (jnp.minimum(i, lat_tiles - 1), 0)),
            pl.BlockSpec((tm, 5 * A_WIDTH), lambda i: (jnp.maximum(i - lat_tiles, 0), 0)),
            row(2 * Q_WIDTH), row(2 * KV_WIDTH), row(2 * Q_WIDTH), row(2 * KV_WIDTH),
            ctx_row(4 * KV_WIDTH), ctx_row(2 * Q_WIDTH), ctx_row(2 * Q_WIDTH),
        ],
        out_shape=[
            jax.ShapeDtypeStruct((dims["n_lat"], 5 * A_WIDTH), F32),
            jax.ShapeDtypeStruct((dims["n_ctx"], 5 * A_WIDTH), F32),
            jax.ShapeDtypeStruct((n, 2 * Q_WIDTH), BF16),
            jax.ShapeDtypeStruct((n, 2 * KV_WIDTH), BF16),
            jax.ShapeDtypeStruct((n, 2 * Q_WIDTH), BF16),
            jax.ShapeDtypeStruct((n, 2 * KV_WIDTH), BF16),
            jax.ShapeDtypeStruct((ctx_rows, 4 * KV_WIDTH), F32),
            jax.ShapeDtypeStruct((ctx_rows, 2 * Q_WIDTH), F32),
            jax.ShapeDtypeStruct((ctx_rows, 2 * Q_WIDTH), F32),
        ],
        scratch_shapes=[pltpu.VMEM((tm, IN_WIDTH), F32)],
        compiler_params=_cparams("arbitrary"),
        name="in_projection_precise_ctx" if precise_ctx else "in_projection",
    )(*x_pair, mod, g1, *weights, cos_t, sin_t, qn_g, kn_g, ones_blk)


def _hgrn_chunk(qa, v, fz, lb, st, consts, reverse, precise):
    cum, e_sel, x_sel, sub_mask, p_mask, v_mask, head_mask = consts
    log2e = LOG2E
    n_sub, half = CHUNK // SUB, SUB // 2
    q = _silu(qa)
    f = lb + (1.0 - lb) * _sigmoid(fz)
    logf2 = jnp.log(jnp.maximum(f, TINY)) * log2e
    kk = (1.0 - lb) * _sigmoid(-fz)

    b = _dot_sel(cum, logf2)
    b3 = b.reshape(n_sub, SUB, A_WIDTH)
    end = 0 if reverse else SUB - 1
    tot = jnp.broadcast_to(b3[:, end:end + 1, :], (n_sub, SUB, A_WIDTH)).reshape(CHUNK, A_WIDTH)
    q_in = q * jnp.exp2(b)
    k_out = kk * jnp.exp2(tot - b)
    d_sub = jnp.exp2(tot)

    c3 = (b - jnp.log(kk) * log2e).reshape(n_sub, SUB, A_WIDTH)
    b4 = b.reshape(n_sub, 2, half, A_WIDTH)
    q4 = q.reshape(n_sub, 2, half, A_WIDTH)
    r_local = lax.broadcasted_iota(jnp.int32, (n_sub, half, A_WIDTH), 1)
    zero_half = jnp.zeros((n_sub, half, A_WIDTH), F32)
    pieces = []
    for s in range(SUB):
        cs = c3[:, s:s + 1, :]
        parts = []
        for hsel in range(2):
            lo, hi = hsel * half, hsel * half + half - 1
            if (lo > s) if reverse else (hi < s):
                parts.append(zero_half)
                continue
            arg = b4[:, hsel] - cs
            if not ((hi <= s) if reverse else (lo >= s)):
                t_local = r_local + lo
                arg = jnp.where((t_local <= s) if reverse else (t_local >= s), arg, MASK_VALUE)
            parts.append(q4[:, hsel] * jnp.exp2(arg))
        piece = jnp.stack(parts, axis=1).reshape(CHUNK, A_WIDTH)
        pieces.append(piece if precise else piece.astype(BF16))
    w_cat = jnp.concatenate(pieces, axis=1)
    s_local = _mm_sel(w_cat, e_sel, precise)
    p = _mm_sel(s_local, x_sel, precise) * p_mask
    v_bd = jnp.concatenate([v] * A_HEADS, axis=0) * v_mask
    o_intra = _mm(p, v_bd, precise)

    v_op = v if precise else v.astype(BF16)
    k_op = k_out if precise else k_out.astype(BF16)
    q_op = q_in if precise else q_in.astype(BF16)
    sub_m = sub_mask
    order = range(n_sub - 1, -1, -1) if reverse else range(n_sub)
    new_st, o_inter = [], []
    for p in range(A_WIDTH // LANES):
        lanes = slice(LANES * p, LANES * (p + 1))
        k_exp = jnp.concatenate([k_op[:, lanes]] * n_sub, axis=1) * sub_m
        u_all = _mm(v_op[:, lanes], k_exp, precise, _TN)
        s = st[p]
        s_before = [None] * n_sub
        for j in order:
            s_before[j] = s
            s = d_sub[SUB * j:SUB * j + 1, lanes] * s + head_mask * u_all[:, LANES * j:LANES * (j + 1)]
        new_st.append(s)
        q_exp = jnp.concatenate([q_op[:, lanes]] * n_sub, axis=1) * sub_m
        o_inter.append(_mm(q_exp, jnp.concatenate(s_before, axis=1), precise, _NT))
    return o_intra + jnp.concatenate(o_inter, axis=1), jnp.stack(new_st, axis=0)


def _hgrn_kernel(*refs, has_s0, want_state, precise):
    refs = list(refs)
    qaf_ref, iaf_ref, fzf_ref, qab_ref, iab_ref, fzb_ref, lb_ref = refs[:7]
    del refs[:7]
    s0_ref = refs.pop(0) if has_s0 else None
    cumf_ref, cumb_ref, e_ref, x_ref, sm_ref, pm_ref, vm_ref, hm_ref, of_ref, ob_ref = refs[:10]
    del refs[:10]
    sout_ref = refs.pop(0) if want_state else None
    st_ref, = refs
    c = pl.program_id(1)

    @pl.when(c == 0)
    def _():
        st_ref[...] = s0_ref[...] if has_s0 else jnp.zeros_like(st_ref)

    shared = (e_ref[...], x_ref[...], sm_ref[...], pm_ref[...], vm_ref[...], hm_ref[...])
    lb = lb_ref[...]
    for sq in range(HGRN_SEQS):
        o_f, st_f = _hgrn_chunk(qaf_ref[sq], iaf_ref[sq], fzf_ref[sq], lb[0:1], st_ref[sq, 0],
                                (cumf_ref[...],) + shared, False, precise)
        o_b, st_b = _hgrn_chunk(qab_ref[sq], iab_ref[sq], fzb_ref[sq], lb[1:2], st_ref[sq, 1],
                                (cumb_ref[...],) + shared, True, precise)
        of_ref[sq] = o_f
        ob_ref[sq] = o_b
        st_ref[sq, 0] = st_f
        st_ref[sq, 1] = st_b

    if want_state:
        @pl.when(c == pl.num_programs(1) - 1)
        def _():
            sout_ref[...] = st_ref[...]


def _hgrn_consts():
    r = np.arange(CHUNK)
    same = (r[:, None] // SUB) == (r[None, :] // SUB)
    cum_f = (same & (r[None, :] <= r[:, None])).astype(np.float32)
    cum_b = (same & (r[None, :] >= r[:, None])).astype(np.float32)
    tot = same.astype(np.float32)
    e = np.zeros((SUB, A_HEADS, HEAD_DIM, A_HEADS, SUB), np.float32)
    for s in range(SUB):
        for h in range(A_HEADS):
            e[s, h, :, h, s] = 1.0
    e = e.reshape(SUB * A_WIDTH, A_HEADS * SUB)
    x = np.zeros((A_HEADS, SUB, A_HEADS, CHUNK), np.float32)
    for h in range(A_HEADS):
        for s in range(CHUNK):
            x[h, s % SUB, h, s] = 1.0
    x = x.reshape(A_HEADS * SUB, A_HEADS * CHUNK)
    p_mask = np.tile(tot, (1, A_HEADS))
    hv = np.arange(A_HEADS * CHUNK) // CHUNK
    hk = np.arange(A_WIDTH) // HEAD_DIM
    v_mask = (hv[:, None] == hk[None, :]).astype(np.float32)
    head_mask = (hk[:LANES, None] == hk[None, :LANES]).astype(np.float32)
    sub_mask = np.repeat(r[:, None] // SUB == np.arange(CHUNK // SUB)[None, :], LANES, axis=1).astype(np.float32)
    sel = [jnp.asarray(a, BF16) for a in (cum_f, cum_b, e, x, sub_mask)]
    return sel + [jnp.asarray(a, F32) for a in (p_mask, v_mask, head_mask)]


def _hgrn_scan(za, lb2, s0_t, n_seq, t, precise, want_state, name):
    nc = t // CHUNK
    groups = n_seq // HGRN_SEQS
    za4 = za.reshape(groups, HGRN_SEQS, t, za.shape[1])
    consts = _hgrn_consts()
    state_shape = (HGRN_SEQS, 2, A_WIDTH // LANES, LANES, LANES)
    const = lambda a: pl.BlockSpec(a.shape, lambda p, c: (0,) * a.ndim)
    fwd = lambda col: pl.BlockSpec((None, HGRN_SEQS, CHUNK, A_WIDTH), lambda p, c: (p, 0, c, col))
    bwd = lambda col: pl.BlockSpec((None, HGRN_SEQS, CHUNK, A_WIDTH), lambda p, c: (p, 0, nc - 1 - c, col))
    state_spec = pl.BlockSpec((None,) + state_shape, lambda p, c: (p, 0, 0, 0, 0, 0))
    operands = [za4] * 6 + [lb2]
    in_specs = [fwd(0), fwd(1), fwd(2), bwd(0), bwd(1), bwd(3), const(lb2)]
    if s0_t is not None:
        operands.append(s0_t.reshape((groups,) + state_shape))
        in_specs.append(state_spec)
    o_shape = jax.ShapeDtypeStruct((groups, HGRN_SEQS, t, A_WIDTH), F32)
    out_specs, out_shape = [fwd(0), bwd(0)], [o_shape, o_shape]
    if want_state:
        out_specs.append(state_spec)
        out_shape.append(jax.ShapeDtypeStruct((groups,) + state_shape, F32))
    outs = pl.pallas_call(
        functools.partial(_hgrn_kernel, has_s0=s0_t is not None, want_state=want_state, precise=precise),
        grid=(groups, nc),
        in_specs=in_specs + [const(c) for c in consts],
        out_specs=out_specs,
        out_shape=out_shape,
        scratch_shapes=[pltpu.VMEM(state_shape, F32)],
        compiler_params=_cparams("parallel", "arbitrary"),
        name=name,
    )(*operands, *consts)
    o_f, o_b = (o.reshape(n_seq * t, A_WIDTH) for o in outs[:2])
    s_fin = outs[2].reshape((n_seq,) + state_shape[1:]) if want_state else None
    return o_f, o_b, s_fin


def _attend(q3, segments, sink_row, precise=False):
    scores = []
    for k, _, valid in segments:
        s = _mm(k, q3, precise, _NT)
        if valid is not None:
            s = jnp.where(valid, s, MASK_VALUE)
        scores.append(s)
    m = scores[0].max(axis=0, keepdims=True)
    for s in scores[1:]:
        m = jnp.maximum(m, s.max(axis=0, keepdims=True))
    if sink_row is not None:
        m = jnp.maximum(m, sink_row)
    denom = jnp.exp2(sink_row - m) if sink_row is not None else 0.0
    acc = 0.0
    for s, (_, v, _) in zip(scores, segments):
        p = jnp.exp2(s - m)
        denom = denom + p.sum(axis=0, keepdims=True)
        acc = acc + _mm(v, p, precise, _TN)
    return (acc / denom).T


def _attend_vt(q3, segments, sink_row):
    scores = []
    for k, _, valid in segments:
        s = lax.dot_general(k, q3, _NT, preferred_element_type=F32)
        if valid is not None:
            s = jnp.where(valid, s, MASK_VALUE)
        scores.append(s)
    m = scores[0].max(axis=0, keepdims=True)
    for s in scores[1:]:
        m = jnp.maximum(m, s.max(axis=0, keepdims=True))
    if sink_row is not None:
        m = jnp.maximum(m, sink_row)
    acc = 0.0
    for s, (_, v_t, _) in zip(scores, segments):
        acc = acc + _dot(v_t, jnp.exp2(s - m).astype(BF16))
    denom = acc[KV_WIDTH:KV_WIDTH + 1]
    if sink_row is not None:
        denom = denom + jnp.exp2(sink_row - m)
    return (acc[:KV_WIDTH] / denom).T


def _fill_v_t(vt_ref, v_parts):
    v_all = jnp.concatenate([v.astype(F32) for v in v_parts], axis=0)
    vt_ref[0:KV_WIDTH, :] = v_all.T.astype(BF16)
    vt_ref[KV_WIDTH:, :] = jnp.ones((KV_WIDTH, v_all.shape[0]), BF16)


def _stack_heads(q, kvh):
    base = kvh * GQA_GROUP
    return jnp.concatenate([q[:, LANES * (base + g):LANES * (base + g + 1)] for g in range(GQA_GROUP)], axis=0)


def _merge_heads(res, tq):
    lane = lax.broadcasted_iota(jnp.int32, (tq, LANES), 1)
    heads = []
    for j in range(B_Q_HEADS):
        kvh, g = divmod(j, GQA_GROUP)
        o = res[kvh][g * tq:(g + 1) * tq]
        if kvh != j % 2:
            o = pltpu.roll(o, HEAD_DIM, 1)
        heads.append(o)
    blocks = [jnp.where(lane < HEAD_DIM, heads[2 * b], heads[2 * b + 1]) for b in range(B_Q_HEADS // 2)]
    return jnp.concatenate(blocks, axis=1)


def _sink_row(sink_ref, kvh, tq):
    return jnp.concatenate(
        [jnp.full((1, tq), sink_ref[kvh * GQA_GROUP + g] * LOG2E, F32) for g in range(GQA_GROUP)], axis=1)


def _global_attn_kernel(q_ref, kv_ref, ck_ref, cv_ref, o_ref, *, tq):
    q = q_ref[...]
    kv = kv_ref[...]
    k, v = kv[:, :KV_WIDTH], kv[:, KV_WIDTH:]
    ck, cv = ck_ref[...], cv_ref[...]
    res = [_attend(_stack_heads(q, kvh), [(k, v, None), (ck, cv, None)], None) for kvh in range(B_KV_HEADS)]
    o_ref[...] = _merge_heads(res, tq).astype(BF16)


def _global_attention(q_att, kv, ck, cv, dims):
    t, n_seq = dims["t_lat"], dims["n_lat_seq"]
    tq = 256
    nq = t // tq
    return pl.pallas_call(
        functools.partial(_global_attn_kernel, tq=tq),
        grid=(n_seq, nq),
        in_specs=[
            pl.BlockSpec((tq, 2 * Q_WIDTH), lambda b, i: (b * nq + i, 0)),
            pl.BlockSpec((t, 2 * KV_WIDTH), lambda b, i: (b, 0)),
            pl.BlockSpec((None,) + ck.shape[1:], lambda b, i: (b, 0, 0)),
            pl.BlockSpec((None,) + cv.shape[1:], lambda b, i: (b, 0, 0)),
        ],
        out_specs=pl.BlockSpec((tq, Q_WIDTH), lambda b, i: (b * nq + i, 0)),
        out_shape=jax.ShapeDtypeStruct((n_seq * t, Q_WIDTH), BF16),
        compiler_params=_cparams("parallel", "arbitrary"),
        name="global_attention",
    )(q_att, kv, ck, cv)


def _window_attn_kernel(sink_ref, q_ref, kv_ref, ck_ref, cv_ref, o_ref, vt_ref, *, tq, t):
    j = pl.program_id(1)

    @pl.when(j == 0)
    def _():
        _fill_v_t(vt_ref, [kv_ref[:, KV_WIDTH:], cv_ref[...]])

    span = tq + 2 * WINDOW
    start = pl.multiple_of(jnp.clip(j * tq - WINDOW, 0, t - span), WINDOW)
    q = q_ref[...]
    k = kv_ref[pl.ds(start, span), :KV_WIDTH]
    ck = ck_ref[...]
    key_pos = start + lax.broadcasted_iota(jnp.int32, (span, GQA_GROUP * tq), 0)
    query_pos = j * tq + lax.broadcasted_iota(jnp.int32, (span, GQA_GROUP * tq), 1) % tq
    valid = jnp.abs(query_pos - key_pos) <= WINDOW
    segments = [(k, vt_ref[:, pl.ds(start, span)], valid), (ck, vt_ref[:, t:], None)]
    res = [_attend_vt(_stack_heads(q, kvh), segments, _sink_row(sink_ref, kvh, tq))
           for kvh in range(C_KV_HEADS)]
    o_ref[...] = _merge_heads(res, tq).astype(BF16)


def _window_attention(sink, q_att, kv, ck, cv, dims):
    t, n_seq = dims["t_lat"], dims["n_lat_seq"]
    tq = 2 * WINDOW
    nq = t // tq
    return pl.pallas_call(
        functools.partial(_window_attn_kernel, tq=tq, t=t),
        grid=(n_seq, nq),
        in_specs=[
            pl.BlockSpec(memory_space=pltpu.SMEM),
            pl.BlockSpec((tq, 2 * Q_WIDTH), lambda b, i: (b * nq + i, 0)),
            pl.BlockSpec((t, 2 * KV_WIDTH), lambda b, i: (b, 0)),
            pl.BlockSpec((None,) + ck.shape[1:], lambda b, i: (b, 0, 0)),
            pl.BlockSpec((None,) + cv.shape[1:], lambda b, i: (b, 0, 0)),
        ],
        out_specs=pl.BlockSpec((tq, Q_WIDTH), lambda b, i: (b * nq + i, 0)),
        out_shape=jax.ShapeDtypeStruct((n_seq * t, Q_WIDTH), BF16),
        scratch_shapes=[pltpu.VMEM((2 * KV_WIDTH, t + ck.shape[1]), BF16)],
        compiler_params=_cparams("parallel", "arbitrary"),
        name="window_attention",
    )(sink, q_att, kv, ck, cv)


def _context_attn_kernel(sink_ref, qg_ref, qw_ref, kv_ref, og_ref, ow_ref, *, t, precise):
    kv = kv_ref[...]
    for q_ref, o_ref, off, use_sink in ((qg_ref, og_ref, 0, False), (qw_ref, ow_ref, 2 * KV_WIDTH, True)):
        q = q_ref[...]
        k, v = kv[:, off:off + KV_WIDTH], kv[:, off + KV_WIDTH:off + 2 * KV_WIDTH]
        res = [_attend(_stack_heads(q, kvh), [(k, v, None)],
                       _sink_row(sink_ref, kvh, t) if use_sink else None, precise)
               for kvh in range(B_KV_HEADS)]
        o_ref[...] = _merge_heads(res, t)


def _context_attention(sink, qg32, qw32, kv32, dims, precise):
    t, n_seq = dims["t_ctx"], dims["n_ctx_seq"]
    seq = lambda w: pl.BlockSpec((t, w), lambda b: (b, 0))
    return pl.pallas_call(
        functools.partial(_context_attn_kernel, t=t, precise=precise),
        grid=(n_seq,),
        in_specs=[pl.BlockSpec(memory_space=pltpu.SMEM), seq(2 * Q_WIDTH), seq(2 * Q_WIDTH), seq(4 * KV_WIDTH)],
        out_specs=[seq(Q_WIDTH), seq(Q_WIDTH)],
        out_shape=[jax.ShapeDtypeStruct((n_seq * t, Q_WIDTH), F32)] * 2,
        compiler_params=_cparams("parallel"),
        name="context_attention_precise" if precise else "context_attention",
    )(sink, qg32, qw32, kv32)


def _route(scores_t, sel_t):
    s = [scores_t[e:e + 1, :] for e in range(N_EXPERTS)]
    z = [sel_t[e:e + 1, :] for e in range(N_EXPERTS)]
    gs = []
    for g in range(N_GROUPS):
        m = z[g * EXPERTS_PER_GROUP:(g + 1) * EXPERTS_PER_GROUP]
        best = None
        for a in range(EXPERTS_PER_GROUP):
            for b in range(a + 1, EXPERTS_PER_GROUP):
                pair = m[a] + m[b]
                best = pair if best is None else jnp.maximum(best, pair)
        gs.append(best)
    combine, groups = [], []
    for g in range(N_GROUPS):
        chosen_g = None
        for g2 in range(N_GROUPS):
            if g2 == g:
                continue
            c = (gs[g] > gs[g2]) if g2 < g else (gs[g] >= gs[g2])
            chosen_g = c if chosen_g is None else jnp.logical_and(chosen_g, c)
        base = g * EXPERTS_PER_GROUP
        picked = []
        for a in range(EXPERTS_PER_GROUP):
            rank = 0.0
            for b in range(EXPERTS_PER_GROUP):
                if b == a:
                    continue
                ahead = (z[base + b] >= z[base + a]) if b < a else (z[base + b] > z[base + a])
                rank = rank + ahead.astype(F32)
            picked.append(jnp.where(jnp.logical_and(chosen_g, rank < 2.0), s[base + a], 0.0))
        denom = picked[0] + picked[1] + picked[2] + picked[3]
        denom = jnp.where(chosen_g, denom, 1.0)
        combine.extend(pk / denom for pk in picked)
        groups.append(jnp.where(chosen_g, 1.0, 0.0))
    return jnp.concatenate(combine, axis=0), jnp.concatenate(groups, axis=0)


def _outproj_kernel(*refs, lat_tiles, precise_ctx):
    if precise_ctx:
        (ofl_ref, ofc_ref, obl_ref, obc_ref, gal_ref, gac_ref,
         ogl_ref, ogc_ref, owl_ref, owc_ref, xl_ref, xc_ref, mod_ref, ag_ref,
         ones_ref, w_ref, wlo_ref, g2_ref, rw_ref, rb_ref, before_ref,
         x1_ref, h2_ref, comb_ref, pos_ref, cnt_ref, mix_ref) = refs
    else:
        (ofl_ref, ofc_ref, obl_ref, obc_ref, gal_ref, gac_ref,
         ogl_ref, ogc_ref, owl_ref, owc_ref, xl_ref, xc_ref, mod_ref, ag_ref,
         ones_ref, w_ref, g2_ref, rw_ref, rb_ref, before_ref,
         x1_ref, h2_ref, comb_ref, pos_ref, cnt_ref, mix_ref) = refs
        wlo_ref = None
    is_lat = pl.program_id(0) < lat_tiles
    o = jnp.where(is_lat, ofl_ref[...] + obl_ref[...], ofc_ref[...] + obc_ref[...])
    ga = jnp.where(is_lat, gal_ref[...], gac_ref[...])
    oa = _head_rms(o, ag_ref[...], ones_ref[...]) * _sigmoid(ga)
    parts = ((0, A_WIDTH), (A_WIDTH, A_WIDTH + Q_WIDTH), (A_WIDTH + Q_WIDTH, D_MODEL))

    def mix(operands, precise):
        acc = 0.0
        for a, (r0, r1) in zip(operands, parts):
            if precise:
                ah, al = _split2(a)
                acc = acc + _dot(ah, w_ref[r0:r1, :]) + _dot(al, w_ref[r0:r1, :]) + _dot(ah, wlo_ref[r0:r1, :])
            else:
                acc = acc + _dot(a.astype(BF16), w_ref[r0:r1, :])
        mix_ref[...] = acc

    pl.when(is_lat)(lambda: mix((oa, ogl_ref[...], owl_ref[...]), False))
    pl.when(jnp.logical_not(is_lat))(lambda: mix((oa, ogc_ref[...], owc_ref[...]), precise_ctx))

    mod = mod_ref[...]
    gate1 = mod[:, 2 * D_MODEL:3 * D_MODEL]
    sh2 = mod[:, 3 * D_MODEL:4 * D_MODEL]
    sc2 = mod[:, 4 * D_MODEL:5 * D_MODEL]
    x1 = jnp.where(is_lat, xl_ref[...], xc_ref[...]) + gate1 * mix_ref[...]
    x1_ref[...] = x1
    y = x1 * lax.rsqrt(jnp.mean(x1 * x1, axis=-1, keepdims=True) + EPS) * g2_ref[...]
    h2 = y * (1.0 + sc2) + sh2
    h2_ref[...] = h2.astype(BF16)
    logits_t = _mm(rw_ref[...], h2, True, _NT)
    scores_t = _sigmoid(logits_t)
    comb_t, group_t = _route(scores_t, scores_t + rb_ref[...])
    comb_ref[...] = comb_t
    rank = _dot(group_t.astype(BF16), before_ref[...])
    counts = group_t.sum(axis=1, keepdims=True)
    pos, offset = 0.0, 0.0
    for g in range(N_GROUPS):
        pos = pos + group_t[g:g + 1] * (rank[g:g + 1] + offset)
        offset = offset + counts[g:g + 1]
    pos_ref[...] = pos
    pad = jnp.zeros((8 - N_GROUPS, 1), F32)
    cnt_ref[...] = jnp.broadcast_to(jnp.concatenate([counts, pad], axis=0), (8, LANES)).astype(jnp.int32)


def _out_projection(o_f, o_b, za, og_lat, og_ctx, ow_lat, ow_ctx, x_pair, mod, an_g, ones_blk, w_hi, w_lo, g2,
                    rw_t, rb, dims):
    n = dims["n_lat"] + dims["n_ctx"]
    tm = TOKEN_TILE
    n_tiles = n // tm
    lat_tiles = dims["n_lat"] // tm
    tiles_per_seq = dims["t_lat"] // tm
    ctx_cond = dims["ctx_cond"]
    precise_ctx = w_lo is not None
    cond_idx = lambda i: jnp.where(i < lat_tiles, i // tiles_per_seq, ctx_cond)
    row = lambda w: pl.BlockSpec((tm, w), lambda i: (i, 0))
    lat = lambda w: pl.BlockSpec((tm, w), lambda i: (jnp.minimum(i, lat_tiles - 1), 0))
    ctx = lambda w: pl.BlockSpec((tm, w), lambda i: (jnp.maximum(i - lat_tiles, 0), 0))
    const = lambda a: pl.BlockSpec(a.shape, lambda i: (0,) * a.ndim, pipeline_mode=pl.Buffered(1))
    weights = [w_hi, w_lo] if precise_ctx else [w_hi]
    t_idx = np.arange(tm)
    before = jnp.asarray((t_idx[:, None] < t_idx[None, :]).astype(np.float32), BF16)
    return pl.pallas_call(
        functools.partial(_outproj_kernel, lat_tiles=lat_tiles, precise_ctx=precise_ctx),
        grid=(n_tiles,),
        in_specs=[
            lat(A_WIDTH), ctx(A_WIDTH), lat(A_WIDTH), ctx(A_WIDTH),
            pl.BlockSpec((tm, A_WIDTH), lambda i: (jnp.minimum(i, lat_tiles - 1), 4)),
            pl.BlockSpec((tm, A_WIDTH), lambda i: (jnp.maximum(i - lat_tiles, 0), 4)),
            lat(Q_WIDTH), ctx(Q_WIDTH), lat(Q_WIDTH), ctx(Q_WIDTH)] + _token_specs(x_pair, tm, lat_tiles) + [
            pl.BlockSpec((None, 1, 6 * D_MODEL), lambda i: (cond_idx(i), 0, 0)),
            const(an_g), const(ones_blk)] + [const(w) for w in weights] + [
            const(g2), const(rw_t), const(rb), const(before),
        ],
        out_specs=[row(D_MODEL), row(D_MODEL), pl.BlockSpec((N_EXPERTS, tm), lambda i: (0, i)),
                   pl.BlockSpec((1, tm), lambda i: (0, i)), pl.BlockSpec((8, LANES), lambda i: (i, 0))],
        out_shape=[
            jax.ShapeDtypeStruct((n, D_MODEL), F32),
            jax.ShapeDtypeStruct((n, D_MODEL), BF16),
            jax.ShapeDtypeStruct((N_EXPERTS, n), F32),
            jax.ShapeDtypeStruct((1, n), F32),
            jax.ShapeDtypeStruct((n_tiles * 8, LANES), jnp.int32),
        ],
        scratch_shapes=[pltpu.VMEM((tm, D_MODEL), F32)],
        compiler_params=_cparams("arbitrary"),
        name="out_projection_precise_ctx" if precise_ctx else "out_projection",
    )(*o_f, *o_b, *za, og_lat, og_ctx, ow_lat, ow_ctx, *x_pair, mod, an_g, ones_blk, *weights, g2, rw_t, rb, before)


def _moe_kernel(off_ref, h_ref, comb_ref, pos_ref, x1_ref, mod_ref, wgu_ref, wd_ref, fg_ref, *rest,
                final, lat_tiles):
    *outs, hs_ref, cs_ref, acc_ref = rest
    i = pl.program_id(0)
    tm = h_ref.shape[0]
    row = lax.broadcasted_iota(jnp.int32, (tm, tm), 0).astype(F32)
    perm = jnp.where(row == pos_ref[...], 1.0, 0.0).astype(BF16)
    hs_ref[...] = _dot(perm, h_ref[...]).astype(BF16)
    cs_ref[...] = sum(lax.dot_general(perm, c, _NT, preferred_element_type=F32)
                      for c in _split3(comb_ref[...]))
    acc_ref[...] = jnp.zeros_like(acc_ref)
    lane = lax.broadcasted_iota(jnp.int32, (MOE_BLOCK, N_EXPERTS), 1)

    for r in range(tm // MOE_BLOCK):
        rows = slice(r * MOE_BLOCK, (r + 1) * MOE_BLOCK)

        def group_body(g, carry, rows=rows, r=r):
            has_tokens = jnp.logical_and(off_ref[i, g] < (r + 1) * MOE_BLOCK, off_ref[i, g + 1] > r * MOE_BLOCK)

            @pl.when(has_tokens)
            def _():
                h = hs_ref[rows, :]
                cs = cs_ref[rows, :]
                y = 0.0
                for a in range(EXPERTS_PER_GROUP):
                    e = g * EXPERTS_PER_GROUP + a
                    gu = _dot(h, wgu_ref[e])
                    gate, up = gu[:, :D_EXPERT], gu[:, D_EXPERT:]
                    ce = jnp.sum(jnp.where(lane == e, cs, 0.0), axis=1, keepdims=True)
                    y = y + _dot((_silu(gate) * up * ce).astype(BF16), wd_ref[e])
                acc_ref[rows, :] += y

            return carry

        lax.fori_loop(0, N_GROUPS, group_body, 0)

    y_hi, y_lo = _split2(acc_ref[...])
    y = (lax.dot_general(perm, y_hi, _TN, preferred_element_type=F32)
         + lax.dot_general(perm, y_lo, _TN, preferred_element_type=F32))
    x2 = x1_ref[...] + mod_ref[...][:, 5 * D_MODEL:] * y
    if not final:
        outs[0][...] = x2
    else:
        yn = x2 * lax.rsqrt(jnp.mean(x2 * x2, axis=-1, keepdims=True) + EPS) * fg_ref[...]
        is_lat = i < lat_tiles

        @pl.when(is_lat)
        def _():
            outs[0][...] = yn

        @pl.when(jnp.logical_not(is_lat))
        def _():
            outs[1][...] = yn


def _moe(h2, comb_t, pos, counts, x1, mod, w_gu, w_d, final_g, dims, final):
    n = x1.shape[0]
    tm = TOKEN_TILE
    lat_tiles = dims["n_lat"] // tm
    tiles_per_seq = dims["t_lat"] // tm
    ctx_cond = dims["ctx_cond"]
    cond_idx = lambda i: jnp.where(i < lat_tiles, i // tiles_per_seq, ctx_cond)
    offsets = jnp.concatenate([jnp.zeros((n // tm, 1), jnp.int32), jnp.cumsum(counts, axis=1)], axis=1)
    if final:
        out_specs = [pl.BlockSpec((tm, D_MODEL), lambda i, *_: (jnp.minimum(i, lat_tiles - 1), 0)),
                     pl.BlockSpec((tm, D_MODEL), lambda i, *_: (jnp.maximum(i - lat_tiles, 0), 0))]
        out_shape = [jax.ShapeDtypeStruct((dims["n_lat"], D_MODEL), F32),
                     jax.ShapeDtypeStruct((dims["n_ctx"], D_MODEL), F32)]
    else:
        out_specs = pl.BlockSpec((tm, D_MODEL), lambda i, *_: (i, 0))
        out_shape = jax.ShapeDtypeStruct((n, D_MODEL), F32)
    resident = lambda a: pl.BlockSpec(a.shape, lambda i, *_: (0,) * a.ndim, pipeline_mode=pl.Buffered(1))
    grid_spec = pltpu.PrefetchScalarGridSpec(
        num_scalar_prefetch=1,
        grid=(n // tm,),
        in_specs=[
            pl.BlockSpec((tm, D_MODEL), lambda i, *_: (i, 0)),
            pl.BlockSpec((N_EXPERTS, tm), lambda i, *_: (0, i)),
            pl.BlockSpec((1, tm), lambda i, *_: (0, i)),
            pl.BlockSpec((tm, D_MODEL), lambda i, *_: (i, 0)),
            pl.BlockSpec((None, 1, 6 * D_MODEL), lambda i, *_: (cond_idx(i), 0, 0)),
            resident(w_gu), resident(w_d),
            pl.BlockSpec((1, D_MODEL), lambda i, *_: (0, 0)),
        ],
        out_specs=out_specs,
        scratch_shapes=[pltpu.VMEM((tm, D_MODEL), BF16), pltpu.VMEM((tm, N_EXPERTS), F32),
                        pltpu.VMEM((tm, D_MODEL), F32)],
    )
    return pl.pallas_call(
        functools.partial(_moe_kernel, final=final, lat_tiles=lat_tiles),
        grid_spec=grid_spec,
        out_shape=out_shape,
        compiler_params=_cparams("arbitrary"),
        name="moe_final" if final else "moe",
    )(offsets, h2, comb_t, pos, x1, mod, w_gu, w_d, final_g)


def _rope_tables(t, tile):
    n_freq = HEAD_DIM // 4
    pos = jnp.arange(t)
    freqs = ROPE_THETA ** (-jnp.arange(n_freq, dtype=F32) / n_freq)
    ang_row = (pos // GRID_W).astype(F32)[:, None] * freqs
    ang_col = (pos % GRID_W).astype(F32)[:, None] * freqs
    cos_h = jnp.concatenate([jnp.cos(ang_row)] * 2 + [jnp.cos(ang_col)] * 2, axis=1)
    sin_h = jnp.concatenate([-jnp.sin(ang_row), jnp.sin(ang_row), -jnp.sin(ang_col), jnp.sin(ang_col)], axis=1)
    cos_t = jnp.concatenate([jnp.ones((tile, LANES), F32), jnp.concatenate([cos_h, cos_h], axis=1)], axis=0)
    sin_t = jnp.concatenate([jnp.zeros((tile, LANES), F32), jnp.concatenate([sin_h, sin_h], axis=1)], axis=0)
    return cos_t, sin_t


def _head_ones(width):
    r = np.arange(width) // HEAD_DIM
    return jnp.asarray((r[:, None] == r[None, :]).astype(np.float32), BF16)


def _split_kernel(w_ref, hi_ref, lo_ref):
    hi, lo = _split2(w_ref[...])
    hi_ref[...] = hi
    lo_ref[...] = lo


def _hi_lo(w, want_lo):
    if not want_lo:
        return w.astype(BF16), None
    rows, width = w.shape
    tr = 256
    spec = pl.BlockSpec((tr, width), lambda i: (i, 0))
    return pl.pallas_call(
        _split_kernel,
        grid=(rows // tr,),
        in_specs=[spec],
        out_specs=[spec, spec],
        out_shape=[jax.ShapeDtypeStruct(w.shape, BF16)] * 2,
        compiler_params=_cparams("parallel"),
        name="split_weight",
    )(w)


def _state_to_block_diag_t(s):
    per_pair = LANES // HEAD_DIM
    st = jnp.swapaxes(s, -1, -2).reshape(s.shape[:2] + (A_HEADS // per_pair, per_pair, HEAD_DIM, HEAD_DIM))
    eye = jnp.eye(per_pair, dtype=s.dtype)
    out = jnp.einsum("bdphvk,hg->bdphvgk", st, eye)
    return out.reshape(s.shape[:2] + (A_HEADS // per_pair, LANES, LANES))


def _block_diag_t_to_state(st):
    per_pair = LANES // HEAD_DIM
    s7 = st.reshape(st.shape[:3] + (per_pair, HEAD_DIM, per_pair, HEAD_DIM))
    diag = jnp.stack([s7[:, :, :, h, :, h, :] for h in range(per_pair)], axis=3)
    return jnp.swapaxes(diag, -1, -2).reshape(st.shape[:2] + (A_HEADS, HEAD_DIM, HEAD_DIM))


def kernel(x_prompt, x_sample, cache_glob_k, cache_glob_v, cache_win_k, cache_win_v, state_hgrn, c, c_ctx,
           mod_w, mod_b, norm1_g, norm2_g, w_in, w_out, hgrn_lb, hgrn_norm_g, q_norm_g, k_norm_g, win_sink,
           router_w, router_b, w_gate_up, w_down, final_g):
    n_ctx_seq, t_ctx, d = x_prompt.shape
    n_lat_seq, t_lat, _ = x_sample.shape
    depth = mod_w.shape[0]
    past = cache_glob_k.shape[2]
    n_lat, n_ctx = n_lat_seq * t_lat, n_ctx_seq * t_ctx
    dims = dict(n_ctx_seq=n_ctx_seq, t_ctx=t_ctx, n_lat_seq=n_lat_seq, t_lat=t_lat,
                n_lat=n_lat, n_ctx=n_ctx, ctx_cond=n_lat_seq)
    assert d == D_MODEL and n_lat_seq < COND_ROWS
    assert n_ctx % TOKEN_TILE == 0 and t_lat % TOKEN_TILE == 0
    assert t_ctx % CHUNK == 0 and t_lat >= 4 * WINDOW

    x_pair = (x_sample.reshape(n_lat, d), x_prompt.reshape(n_ctx, d))
    cond = jnp.concatenate([c, c_ctx[None, :], jnp.zeros((COND_ROWS - n_lat_seq - 1, d), F32)], axis=0)
    mod = _modulation(cond, mod_w, mod_b).reshape(depth, COND_ROWS, 1, 6 * d)

    p_lb = jax.nn.softmax(hgrn_lb.astype(F32), axis=1)
    lbs = jnp.cumsum(p_lb, axis=1) - p_lb[:, :1]

    cos_t, sin_t = _rope_tables(t_lat, TOKEN_TILE)
    ones128, ones256 = _head_ones(LANES), _head_ones(A_WIDTH)
    tile2 = lambda g, reps: jnp.tile(g, reps)[None, :]
    rw_t = router_w.T
    rb = router_b[:, None]
    fg = final_g[None, :]

    new_kv, new_state = [], []
    for l in range(depth):
        precise = l < depth - 1
        win_hi, win_lo = _hi_lo(w_in[l], precise)
        za_lat, za_ctx, qg_att, kvg, qw_att, kvw, kv32, qg32, qw32 = _in_projection(
            x_pair, mod[l], norm1_g[l][None, :], win_hi, win_lo, cos_t, sin_t,
            tile2(q_norm_g[l], 2), tile2(k_norm_g[l], 2), ones128, dims)

        s0 = state_hgrn[:, l].astype(F32)
        of_lat, ob_lat, _ = _hgrn_scan(za_lat, lbs[:, l], _state_to_block_diag_t(s0), n_lat_seq, t_lat,
                                       False, False, "hgrn_latent")
        of_ctx, ob_ctx, s_fin = _hgrn_scan(za_ctx, lbs[:, l], None, n_ctx_seq, t_ctx, precise, True,
                                           "hgrn_context_precise" if precise else "hgrn_context")

        cast_cache = lambda a: a[:, l].reshape(n_lat_seq, past, KV_WIDTH).astype(BF16)
        og_lat = _global_attention(qg_att, kvg, cast_cache(cache_glob_k), cast_cache(cache_glob_v), dims)
        ow_lat = _window_attention(win_sink[l], qw_att, kvw, cast_cache(cache_win_k),
                                   cast_cache(cache_win_v), dims)
        og_ctx, ow_ctx = _context_attention(win_sink[l], qg32, qw32, kv32, dims, precise)

        wout_hi, wout_lo = _hi_lo(w_out[l], precise)
        x1, h2, comb_t, pos, cnt = _out_projection(
            (of_lat, of_ctx), (ob_lat, ob_ctx), (za_lat, za_ctx), og_lat, og_ctx, ow_lat, ow_ctx, x_pair, mod[l],
            tile2(hgrn_norm_g[l], A_HEADS), ones256, wout_hi, wout_lo, norm2_g[l][None, :], rw_t, rb, dims)

        counts = cnt.reshape(-1, 8, LANES)[:, :N_GROUPS, 0]
        x = _moe(h2, comb_t, pos, counts, x1, mod[l], w_gate_up[l].astype(BF16), w_down[l].astype(BF16), fg,
                 dims, final=(l == depth - 1))
        x_pair = (x, x)

        new_kv.append(kv32[:n_ctx].reshape(n_ctx_seq, t_ctx, 4, B_KV_HEADS, HEAD_DIM))
        new_state.append(_block_diag_t_to_state(s_fin))

    y_sample = x[0].reshape(n_lat_seq, t_lat, d)
    y_prompt = x[1].reshape(n_ctx_seq, t_ctx, d)
    kv = jnp.stack(new_kv, axis=1)
    return (y_prompt, y_sample, kv[:, :, :, 0], kv[:, :, :, 1], kv[:, :, :, 2], kv[:, :, :, 3],
            jnp.stack(new_state, axis=1).astype(x_prompt.dtype))
```

```python
import functools

import numpy as np
import jax
import jax.numpy as jnp
from jax import lax
from jax.experimental import pallas as pl
from jax.experimental.pallas import tpu as pltpu

F32 = jnp.float32
BF16 = jnp.bfloat16

D_MODEL = 1024
HEAD_DIM = 64
GRID_W = 64
A_HEADS = 4
A_WIDTH = A_HEADS * HEAD_DIM
B_Q_HEADS = 6
B_KV_HEADS = 2
C_Q_HEADS = 6
C_KV_HEADS = 2
Q_WIDTH = B_Q_HEADS * HEAD_DIM
KV_WIDTH = B_KV_HEADS * HEAD_DIM
GQA_GROUP = B_Q_HEADS // B_KV_HEADS
WINDOW = 128
ROPE_THETA = 10000.0
IN_WIDTH = 5 * A_WIDTH + 2 * (Q_WIDTH + 2 * KV_WIDTH)
N_EXPERTS = 16
N_GROUPS = 4
EXPERTS_PER_GROUP = N_EXPERTS // N_GROUPS
D_EXPERT = 256
EPS = 1e-6
MASK_VALUE = -1e30
TINY = 1e-30
LOG2E = 1.4426950408889634

LANES = 128
SUB = 16
CHUNK = 128
HGRN_SEQS = 2
TOKEN_TILE = 512
MOE_BLOCK = 128
KEY_CHUNK = 256
COND_ROWS = 16
VMEM_LIMIT = 58 * 1024 * 1024

_NT = (((1,), (1,)), ((), ()))
_TN = (((0,), (0,)), ((), ()))


def _cparams(*sem):
    return pltpu.CompilerParams(dimension_semantics=sem, vmem_limit_bytes=VMEM_LIMIT)


def _dot(a, b):
    return jnp.dot(a, b, preferred_element_type=F32)


def _split2(x):
    h = x.astype(BF16)
    return h, (x - h.astype(F32)).astype(BF16)


def _split3(x):
    h = x.astype(BF16)
    r = x - h.astype(F32)
    m = r.astype(BF16)
    l = (r - m.astype(F32)).astype(BF16)
    return h, m, l


def _dot_sel(c, x):
    h, m, l = _split3(x)
    return _dot(c, h) + _dot(c, m) + _dot(c, l)


def _mm(a, b, precise, dims=None):
    if dims is None:
        dims = (((a.ndim - 1,), (0,)), ((), ()))
    dg = lambda x, y: lax.dot_general(x, y, dims, preferred_element_type=F32)
    if not precise:
        return dg(a.astype(BF16), b.astype(BF16))
    ah, al = _split2(a)
    bh, bl = _split2(b)
    return dg(ah, bh) + dg(ah, bl) + dg(al, bh)


def _mm_sel(a, sel, precise):
    if not precise:
        return _dot(a.astype(BF16), sel)
    ah, al = _split2(a)
    return _dot(ah, sel) + _dot(al, sel)


def _sigmoid(x):
    return jax.nn.sigmoid(x)


def _silu(x):
    return x * jax.nn.sigmoid(x)


def _mod_kernel(cond_ref, w_ref, b_ref, o_ref):
    o_ref[...] = _mm(_silu(cond_ref[...]), w_ref[...], True) + b_ref[...]


def _modulation(cond, mod_w, mod_b):
    depth, d, width = mod_w.shape
    tn = 1536
    return pl.pallas_call(
        _mod_kernel,
        grid=(depth, width // tn),
        in_specs=[
            pl.BlockSpec((COND_ROWS, d), lambda l, j: (0, 0)),
            pl.BlockSpec((None, d, tn), lambda l, j: (l, 0, j)),
            pl.BlockSpec((None, 1, tn), lambda l, j: (l, 0, j)),
        ],
        out_specs=pl.BlockSpec((None, COND_ROWS, tn), lambda l, j: (l, 0, j)),
        out_shape=jax.ShapeDtypeStruct((depth, COND_ROWS, width), F32),
        compiler_params=_cparams("parallel", "parallel"),
        name="modulation",
    )(cond, mod_w, mod_b.reshape(depth, 1, width))


def _head_rms(xb, g2, ones_blk):
    ss = _mm_sel(xb * xb, ones_blk, True)
    return xb * lax.rsqrt(ss * (1.0 / HEAD_DIM) + EPS) * g2


def _rope(yb, cos, sin_signed, first16):
    partner = jnp.where(first16, pltpu.roll(yb, LANES - 16, 1), pltpu.roll(yb, 16, 1))
    return yb * cos + partner * sin_signed


def _q_attention_layout(blocks, lane_half):
    out = []
    for j in range(2 * len(blocks)):
        src = blocks[j // 2]
        dst_half = j // GQA_GROUP
        if j % 2 != dst_half:
            src = pltpu.roll(src, HEAD_DIM, 1)
        out.append(jnp.where(lane_half == dst_half, src, 0.0))
    return jnp.concatenate(out, axis=1)


def _inproj_kernel(*refs, precise_ctx, lat_tiles):
    if precise_ctx:
        (xl_ref, xc_ref, mod_ref, g1_ref, w_ref, wlo_ref, cos_ref, sin_ref, qg_ref, kg_ref, ones_ref,
         za_ref, qga_ref, kvg_ref, qwa_ref, kvw_ref, kv32_ref, qg32_ref, qw32_ref, z_ref) = refs
    else:
        (xl_ref, xc_ref, mod_ref, g1_ref, w_ref, cos_ref, sin_ref, qg_ref, kg_ref, ones_ref,
         za_ref, qga_ref, kvg_ref, qwa_ref, kvw_ref, kv32_ref, qg32_ref, qw32_ref, z_ref) = refs
        wlo_ref = None
    is_lat = pl.program_id(0) < lat_tiles
    x = jnp.where(is_lat, xl_ref[...], xc_ref[...])
    y = x * lax.rsqrt(jnp.mean(x * x, axis=-1, keepdims=True) + EPS) * g1_ref[...]
    mod = mod_ref[...]
    h = y * (1.0 + mod[:, D_MODEL:2 * D_MODEL]) + mod[:, 0:D_MODEL]

    def one_pass():
        z_ref[...] = _dot(h.astype(BF16), w_ref[...])

    def three_pass():
        hh, hl = _split2(h)
        z_ref[...] = _dot(hh, w_ref[...]) + _dot(hl, w_ref[...]) + _dot(hh, wlo_ref[...])

    if precise_ctx:
        pl.when(is_lat)(one_pass)
        pl.when(jnp.logical_not(is_lat))(three_pass)
    else:
        one_pass()
    z = z_ref[...]
    za_ref[...] = z[:, :5 * A_WIDTH]

    cos = cos_ref[...]
    sin = sin_ref[...]
    ones_blk = ones_ref[...]
    lane = lax.broadcasted_iota(jnp.int32, (x.shape[0], LANES), 1)
    first16 = (lane % 32) < 16
    lane_half = lane // HEAD_DIM
    scale = LOG2E * HEAD_DIM ** -0.5

    o = 5 * A_WIDTH
    qg = [z[:, o + LANES * b:o + LANES * (b + 1)] for b in range(3)]
    o += Q_WIDTH
    kg = z[:, o:o + KV_WIDTH]
    vg = z[:, o + KV_WIDTH:o + 2 * KV_WIDTH]
    o += 2 * KV_WIDTH
    qw = [z[:, o + LANES * b:o + LANES * (b + 1)] for b in range(3)]
    o += Q_WIDTH
    kw = z[:, o:o + KV_WIDTH]
    vw = z[:, o + KV_WIDTH:o + 2 * KV_WIDTH]

    qg = [_rope(_head_rms(b, qg_ref[...], ones_blk), cos, sin, first16) * scale for b in qg]
    kg = _rope(_head_rms(kg, kg_ref[...], ones_blk), cos, sin, first16)
    qw = [_rope(b, cos, sin, first16) * scale for b in qw]
    kw = _rope(kw, cos, sin, first16)

    qg_att = _q_attention_layout(qg, lane_half)
    qw_att = _q_attention_layout(qw, lane_half)
    qga_ref[...] = qg_att.astype(BF16)
    qwa_ref[...] = qw_att.astype(BF16)
    kvg_ref[...] = jnp.concatenate([kg, vg], axis=1).astype(BF16)
    kvw_ref[...] = jnp.concatenate([kw, vw], axis=1).astype(BF16)
    kv32_ref[...] = jnp.concatenate([kg, vg, kw, vw], axis=1)
    qg32_ref[...] = qg_att
    qw32_ref[...] = qw_att


def _token_specs(x_pair, tm, lat_tiles):
    off = lat_tiles if x_pair[1] is x_pair[0] else 0
    lat = pl.BlockSpec((tm, D_MODEL), lambda i, *_: (jnp.minimum(i, lat_tiles - 1), 0))
    ctx = pl.BlockSpec((tm, D_MODEL), lambda i, *_: (jnp.maximum(i, lat_tiles) - lat_tiles + off, 0))
    return [lat, ctx]


def _in_projection(x_pair, mod, g1, w_hi, w_lo, cos_t, sin_t, qn_g, kn_g, ones_blk, dims):
    n = dims["n_lat"] + dims["n_ctx"]
    tm = TOKEN_TILE
    n_tiles = n // tm
    lat_tiles = dims["n_lat"] // tm
    ctx_tiles = n_tiles - lat_tiles
    tiles_per_seq = dims["t_lat"] // tm
    ctx_cond = dims["ctx_cond"]
    precise_ctx = w_lo is not None

    def cond_idx(i):
        return jnp.where(i < lat_tiles, i // tiles_per_seq, ctx_cond)

    def rope_idx(i):
        return jnp.where(i < lat_tiles, 1 + i % tiles_per_seq, 0)

    def ctx_idx(i):
        return jnp.where(i < lat_tiles, ctx_tiles, i - lat_tiles)

    row = lambda w: pl.BlockSpec((tm, w), lambda i: (i, 0))
    ctx_row = lambda w: pl.BlockSpec((tm, w), lambda i: (ctx_idx(i), 0))
    const = lambda a: pl.BlockSpec(a.shape, lambda i: (0,) * a.ndim, pipeline_mode=pl.Buffered(1))
    weights = [w_hi, w_lo] if precise_ctx else [w_hi]
    ctx_rows = (ctx_tiles + 1) * tm
    return pl.pallas_call(
        functools.partial(_inproj_kernel, precise_ctx=precise_ctx, lat_tiles=lat_tiles),
        grid=(n_tiles,),
        in_specs=_token_specs(x_pair, tm, lat_tiles) + [
            pl.BlockSpec((None, 1, 6 * D_MODEL), lambda i: (cond_idx(i), 0, 0)),
            const(g1)] + [const(w) for w in weights] + [
            pl.BlockSpec((tm, LANES), lambda i: (rope_idx(i), 0)),
            pl.BlockSpec((tm, LANES), lambda i: (rope_idx(i), 0)),
            const(qn_g), const(kn_g), const(ones_blk),
        ],
        out_specs=[
            row(5 * A_WIDTH), row(2 * Q_WIDTH), row(2 * KV_WIDTH), row(2 * Q_WIDTH), row(2 * KV_WIDTH),
            ctx_row(4 * KV_WIDTH), ctx_row(2 * Q_WIDTH), ctx_row(2 * Q_WIDTH),
        ],
        out_shape=[
            jax.ShapeDtypeStruct((n, 5 * A_WIDTH), F32),
            jax.ShapeDtypeStruct((n, 2 * Q_WIDTH), BF16),
            jax.ShapeDtypeStruct((n, 2 * KV_WIDTH), BF16),
            jax.ShapeDtypeStruct((n, 2 * Q_WIDTH), BF16),
            jax.ShapeDtypeStruct((n, 2 * KV_WIDTH), BF16),
            jax.ShapeDtypeStruct((ctx_rows, 4 * KV_WIDTH), F32),
            jax.ShapeDtypeStruct((ctx_rows, 2 * Q_WIDTH), F32),
            jax.ShapeDtypeStruct((ctx_rows, 2 * Q_WIDTH), F32),
        ],
        scratch_shapes=[pltpu.VMEM((tm, IN_WIDTH), F32)],
        compiler_params=_cparams("arbitrary"),
        name="in_projection_precise_ctx" if precise_ctx else "in_projection",
    )(*x_pair, mod, g1, *weights, cos_t, sin_t, qn_g, kn_g, ones_blk)


def _hgrn_chunk(qa, v, fz, lb, st, consts, reverse, precise):
    cum, e_sel, x_sel, sub_mask, p_mask, v_mask, head_mask = consts
    log2e = LOG2E
    n_sub, half = CHUNK // SUB, SUB // 2
    q = _silu(qa)
    f = lb + (1.0 - lb) * _sigmoid(fz)
    logf2 = jnp.log(jnp.maximum(f, TINY)) * log2e
    kk = (1.0 - lb) * _sigmoid(-fz)

    b = _dot_sel(cum, logf2)
    b3 = b.reshape(n_sub, SUB, A_WIDTH)
    end = 0 if reverse else SUB - 1
    tot = jnp.broadcast_to(b3[:, end:end + 1, :], (n_sub, SUB, A_WIDTH)).reshape(CHUNK, A_WIDTH)
    q_in = q * jnp.exp2(b)
    k_out = kk * jnp.exp2(tot - b)
    d_sub = jnp.exp2(tot)

    c3 = (b - jnp.log(kk) * log2e).reshape(n_sub, SUB, A_WIDTH)
    b4 = b.reshape(n_sub, 2, half, A_WIDTH)
    q4 = q.reshape(n_sub, 2, half, A_WIDTH)
    r_local = lax.broadcasted_iota(jnp.int32, (n_sub, half, A_WIDTH), 1)
    zero_half = jnp.zeros((n_sub, half, A_WIDTH), F32)
    pieces = []
    for s in range(SUB):
        cs = c3[:, s:s + 1, :]
        parts = []
        for hsel in range(2):
            lo, hi = hsel * half, hsel * half + half - 1
            if (lo > s) if reverse else (hi < s):
                parts.append(zero_half)
                continue
            arg = b4[:, hsel] - cs
            if not ((hi <= s) if reverse else (lo >= s)):
                t_local = r_local + lo
                arg = jnp.where((t_local <= s) if reverse else (t_local >= s), arg, MASK_VALUE)
            parts.append(q4[:, hsel] * jnp.exp2(arg))
        piece = jnp.stack(parts, axis=1).reshape(CHUNK, A_WIDTH)
        pieces.append(piece if precise else piece.astype(BF16))
    w_cat = jnp.concatenate(pieces, axis=1)
    s_local = _mm_sel(w_cat, e_sel, precise)
    p = _mm_sel(s_local, x_sel, precise) * p_mask
    v_bd = jnp.concatenate([v] * A_HEADS, axis=0) * v_mask
    o_intra = _mm(p, v_bd, precise)

    v_op = v if precise else v.astype(BF16)
    k_op = k_out if precise else k_out.astype(BF16)
    q_op = q_in if precise else q_in.astype(BF16)
    sub_m = sub_mask
    order = range(n_sub - 1, -1, -1) if reverse else range(n_sub)
    new_st, o_inter = [], []
    for p in range(A_WIDTH // LANES):
        lanes = slice(LANES * p, LANES * (p + 1))
        k_exp = jnp.concatenate([k_op[:, lanes]] * n_sub, axis=1) * sub_m
        u_all = _mm(v_op[:, lanes], k_exp, precise, _TN)
        s = st[p]
        s_before = [None] * n_sub
        for j in order:
            s_before[j] = s
            s = d_sub[SUB * j:SUB * j + 1, lanes] * s + head_mask * u_all[:, LANES * j:LANES * (j + 1)]
        new_st.append(s)
        q_exp = jnp.concatenate([q_op[:, lanes]] * n_sub, axis=1) * sub_m
        o_inter.append(_mm(q_exp, jnp.concatenate(s_before, axis=1), precise, _NT))
    return o_intra + jnp.concatenate(o_inter, axis=1), jnp.stack(new_st, axis=0)


def _hgrn_kernel(*refs, has_s0, want_state, precise):
    refs = list(refs)
    seq_refs = [refs[6 * sq:6 * sq + 6] for sq in range(HGRN_SEQS)]
    del refs[:6 * HGRN_SEQS]
    lb_ref = refs.pop(0)
    s0_ref = refs.pop(0) if has_s0 else None
    cumf_ref, cumb_ref, e_ref, x_ref, sm_ref, pm_ref, vm_ref, hm_ref, of_ref, ob_ref = refs[:10]
    del refs[:10]
    sout_ref = refs.pop(0) if want_state else None
    st_ref, = refs
    c = pl.program_id(1)

    @pl.when(c == 0)
    def _():
        st_ref[...] = s0_ref[...] if has_s0 else jnp.zeros_like(st_ref)

    shared = (e_ref[...], x_ref[...], sm_ref[...], pm_ref[...], vm_ref[...], hm_ref[...])
    lb = lb_ref[...]
    for sq in range(HGRN_SEQS):
        qaf_ref, iaf_ref, fzf_ref, qab_ref, iab_ref, fzb_ref = seq_refs[sq]
        o_f, st_f = _hgrn_chunk(qaf_ref[...], iaf_ref[...], fzf_ref[...], lb[0:1], st_ref[sq, 0],
                                (cumf_ref[...],) + shared, False, precise)
        o_b, st_b = _hgrn_chunk(qab_ref[...], iab_ref[...], fzb_ref[...], lb[1:2], st_ref[sq, 1],
                                (cumb_ref[...],) + shared, True, precise)
        of_ref[sq] = o_f
        ob_ref[sq] = o_b
        st_ref[sq, 0] = st_f
        st_ref[sq, 1] = st_b

    if want_state:
        @pl.when(c == pl.num_programs(1) - 1)
        def _():
            sout_ref[...] = st_ref[...]


def _hgrn_consts():
    r = np.arange(CHUNK)
    same = (r[:, None] // SUB) == (r[None, :] // SUB)
    cum_f = (same & (r[None, :] <= r[:, None])).astype(np.float32)
    cum_b = (same & (r[None, :] >= r[:, None])).astype(np.float32)
    tot = same.astype(np.float32)
    e = np.zeros((SUB, A_HEADS, HEAD_DIM, A_HEADS, SUB), np.float32)
    for s in range(SUB):
        for h in range(A_HEADS):
            e[s, h, :, h, s] = 1.0
    e = e.reshape(SUB * A_WIDTH, A_HEADS * SUB)
    x = np.zeros((A_HEADS, SUB, A_HEADS, CHUNK), np.float32)
    for h in range(A_HEADS):
        for s in range(CHUNK):
            x[h, s % SUB, h, s] = 1.0
    x = x.reshape(A_HEADS * SUB, A_HEADS * CHUNK)
    p_mask = np.tile(tot, (1, A_HEADS))
    hv = np.arange(A_HEADS * CHUNK) // CHUNK
    hk = np.arange(A_WIDTH) // HEAD_DIM
    v_mask = (hv[:, None] == hk[None, :]).astype(np.float32)
    head_mask = (hk[:LANES, None] == hk[None, :LANES]).astype(np.float32)
    sub_mask = np.repeat(r[:, None] // SUB == np.arange(CHUNK // SUB)[None, :], LANES, axis=1).astype(np.float32)
    sel = [jnp.asarray(a, BF16) for a in (cum_f, cum_b, e, x, sub_mask)]
    return sel + [jnp.asarray(a, F32) for a in (p_mask, v_mask, head_mask)]


def _hgrn_scan(za, first_row, lb2, s0_t, n_seq, t, precise, want_state, name):
    nc = t // CHUNK
    groups = n_seq // HGRN_SEQS
    base = first_row // CHUNK
    za3 = za.reshape(za.shape[0] // CHUNK, CHUNK, za.shape[1])
    consts = _hgrn_consts()
    state_shape = (HGRN_SEQS, 2, A_WIDTH // LANES, LANES, LANES)
    const = lambda a: pl.BlockSpec(a.shape, lambda p, c: (0,) * a.ndim)

    def chunk_spec(sq, col, backward):
        def index(p, c):
            return (base + (p * HGRN_SEQS + sq) * nc + (nc - 1 - c if backward else c), 0, col)
        return pl.BlockSpec((None, CHUNK, A_WIDTH), index)

    fwd = lambda col: pl.BlockSpec((None, HGRN_SEQS, CHUNK, A_WIDTH), lambda p, c: (p, 0, c, col))
    bwd = lambda col: pl.BlockSpec((None, HGRN_SEQS, CHUNK, A_WIDTH), lambda p, c: (p, 0, nc - 1 - c, col))
    state_spec = pl.BlockSpec((None,) + state_shape, lambda p, c: (p, 0, 0, 0, 0, 0))
    operands = [za3] * (6 * HGRN_SEQS) + [lb2]
    in_specs = [chunk_spec(sq, col, backward) for sq in range(HGRN_SEQS)
                for col, backward in ((0, False), (1, False), (2, False), (0, True), (1, True), (3, True))]
    in_specs.append(const(lb2))
    if s0_t is not None:
        operands.append(s0_t.reshape((groups,) + state_shape))
        in_specs.append(state_spec)
    o_shape = jax.ShapeDtypeStruct((groups, HGRN_SEQS, t, A_WIDTH), F32)
    out_specs, out_shape = [fwd(0), bwd(0)], [o_shape, o_shape]
    if want_state:
        out_specs.append(state_spec)
        out_shape.append(jax.ShapeDtypeStruct((groups,) + state_shape, F32))
    outs = pl.pallas_call(
        functools.partial(_hgrn_kernel, has_s0=s0_t is not None, want_state=want_state, precise=precise),
        grid=(groups, nc),
        in_specs=in_specs + [const(c) for c in consts],
        out_specs=out_specs,
        out_shape=out_shape,
        scratch_shapes=[pltpu.VMEM(state_shape, F32)],
        compiler_params=_cparams("parallel", "arbitrary"),
        name=name,
    )(*operands, *consts)
    o_f, o_b = (o.reshape(n_seq * t, A_WIDTH) for o in outs[:2])
    s_fin = outs[2].reshape((n_seq,) + state_shape[1:]) if want_state else None
    return o_f, o_b, s_fin


def _attend(q3, segments, sink_row, precise=False):
    scores = []
    for k, _, valid in segments:
        s = _mm(k, q3, precise, _NT)
        if valid is not None:
            s = jnp.where(valid, s, MASK_VALUE)
        scores.append(s)
    m = scores[0].max(axis=0, keepdims=True)
    for s in scores[1:]:
        m = jnp.maximum(m, s.max(axis=0, keepdims=True))
    if sink_row is not None:
        m = jnp.maximum(m, sink_row)
    denom = jnp.exp2(sink_row - m) if sink_row is not None else 0.0
    acc = 0.0
    for s, (_, v, _) in zip(scores, segments):
        p = jnp.exp2(s - m)
        denom = denom + p.sum(axis=0, keepdims=True)
        acc = acc + _mm(v, p, precise, _TN)
    return (acc / denom).T


def _attend_vt(q3, segments, sink_row):
    scores = []
    for k, _, valid in segments:
        s = lax.dot_general(k, q3, _NT, preferred_element_type=F32)
        if valid is not None:
            s = jnp.where(valid, s, MASK_VALUE)
        scores.append(s)
    m = scores[0].max(axis=0, keepdims=True)
    for s in scores[1:]:
        m = jnp.maximum(m, s.max(axis=0, keepdims=True))
    if sink_row is not None:
        m = jnp.maximum(m, sink_row)
    acc = 0.0
    for s, (_, v_t, _) in zip(scores, segments):
        acc = acc + _dot(v_t, jnp.exp2(s - m).astype(BF16))
    denom = acc[KV_WIDTH:KV_WIDTH + 1]
    if sink_row is not None:
        denom = denom + jnp.exp2(sink_row - m)
    return (acc[:KV_WIDTH] / denom).T


def _fill_v_t(vt_ref, v_parts):
    v_all = jnp.concatenate([v.astype(F32) for v in v_parts], axis=0)
    vt_ref[0:KV_WIDTH, :] = v_all.T.astype(BF16)
    vt_ref[KV_WIDTH:, :] = jnp.ones((KV_WIDTH, v_all.shape[0]), BF16)


def _stack_heads(q, kvh):
    base = kvh * GQA_GROUP
    return jnp.concatenate([q[:, LANES * (base + g):LANES * (base + g + 1)] for g in range(GQA_GROUP)], axis=0)


def _merge_heads(res, tq):
    lane = lax.broadcasted_iota(jnp.int32, (tq, LANES), 1)
    heads = []
    for j in range(B_Q_HEADS):
        kvh, g = divmod(j, GQA_GROUP)
        o = res[kvh][g * tq:(g + 1) * tq]
        if kvh != j % 2:
            o = pltpu.roll(o, HEAD_DIM, 1)
        heads.append(o)
    blocks = [jnp.where(lane < HEAD_DIM, heads[2 * b], heads[2 * b + 1]) for b in range(B_Q_HEADS // 2)]
    return jnp.concatenate(blocks, axis=1)


def _sink_row(sink_ref, kvh, tq):
    return jnp.concatenate(
        [jnp.full((1, tq), sink_ref[kvh * GQA_GROUP + g] * LOG2E, F32) for g in range(GQA_GROUP)], axis=1)


def _global_attn_kernel(q_ref, kv_ref, ck_ref, cv_ref, o_ref, *, tq):
    q = q_ref[...]
    kv = kv_ref[...]
    k, v = kv[:, :KV_WIDTH], kv[:, KV_WIDTH:]
    ck, cv = ck_ref[...], cv_ref[...]
    res = [_attend(_stack_heads(q, kvh), [(k, v, None), (ck, cv, None)], None) for kvh in range(B_KV_HEADS)]
    o_ref[...] = _merge_heads(res, tq).astype(BF16)


def _global_attention(q_att, kv, ck, cv, dims):
    t, n_seq = dims["t_lat"], dims["n_lat_seq"]
    tq = 256
    nq = t // tq
    return pl.pallas_call(
        functools.partial(_global_attn_kernel, tq=tq),
        grid=(n_seq, nq),
        in_specs=[
            pl.BlockSpec((tq, 2 * Q_WIDTH), lambda b, i: (b * nq + i, 0)),
            pl.BlockSpec((t, 2 * KV_WIDTH), lambda b, i: (b, 0)),
            pl.BlockSpec((None,) + ck.shape[1:], lambda b, i: (b, 0, 0)),
            pl.BlockSpec((None,) + cv.shape[1:], lambda b, i: (b, 0, 0)),
        ],
        out_specs=pl.BlockSpec((tq, Q_WIDTH), lambda b, i: (b * nq + i, 0)),
        out_shape=jax.ShapeDtypeStruct((n_seq * t, Q_WIDTH), BF16),
        compiler_params=_cparams("parallel", "arbitrary"),
        name="global_attention",
    )(q_att, kv, ck, cv)


def _window_attn_kernel(sink_ref, q_ref, kv_ref, ck_ref, cv_ref, o_ref, vt_ref, *, tq, t):
    j = pl.program_id(1)

    @pl.when(j == 0)
    def _():
        _fill_v_t(vt_ref, [kv_ref[:, KV_WIDTH:], cv_ref[...]])

    span = tq + 2 * WINDOW
    start = pl.multiple_of(jnp.clip(j * tq - WINDOW, 0, t - span), WINDOW)
    q = q_ref[...]
    k = kv_ref[pl.ds(start, span), :KV_WIDTH]
    ck = ck_ref[...]
    key_pos = start + lax.broadcasted_iota(jnp.int32, (span, GQA_GROUP * tq), 0)
    query_pos = j * tq + lax.broadcasted_iota(jnp.int32, (span, GQA_GROUP * tq), 1) % tq
    valid = jnp.abs(query_pos - key_pos) <= WINDOW
    segments = [(k, vt_ref[:, pl.ds(start, span)], valid), (ck, vt_ref[:, t:], None)]
    res = [_attend_vt(_stack_heads(q, kvh), segments, _sink_row(sink_ref, kvh, tq))
           for kvh in range(C_KV_HEADS)]
    o_ref[...] = _merge_heads(res, tq).astype(BF16)


def _window_attention(sink, q_att, kv, ck, cv, dims):
    t, n_seq = dims["t_lat"], dims["n_lat_seq"]
    tq = 2 * WINDOW
    nq = t // tq
    return pl.pallas_call(
        functools.partial(_window_attn_kernel, tq=tq, t=t),
        grid=(n_seq, nq),
        in_specs=[
            pl.BlockSpec(memory_space=pltpu.SMEM),
            pl.BlockSpec((tq, 2 * Q_WIDTH), lambda b, i: (b * nq + i, 0)),
            pl.BlockSpec((t, 2 * KV_WIDTH), lambda b, i: (b, 0)),
            pl.BlockSpec((None,) + ck.shape[1:], lambda b, i: (b, 0, 0)),
            pl.BlockSpec((None,) + cv.shape[1:], lambda b, i: (b, 0, 0)),
        ],
        out_specs=pl.BlockSpec((tq, Q_WIDTH), lambda b, i: (b * nq + i, 0)),
        out_shape=jax.ShapeDtypeStruct((n_seq * t, Q_WIDTH), BF16),
        scratch_shapes=[pltpu.VMEM((2 * KV_WIDTH, t + ck.shape[1]), BF16)],
        compiler_params=_cparams("parallel", "arbitrary"),
        name="window_attention",
    )(sink, q_att, kv, ck, cv)


def _context_attn_kernel(sink_ref, qg_ref, qw_ref, kv_ref, og_ref, ow_ref, *, t, precise):
    kv = kv_ref[...]
    for q_ref, o_ref, off, use_sink in ((qg_ref, og_ref, 0, False), (qw_ref, ow_ref, 2 * KV_WIDTH, True)):
        q = q_ref[...]
        k, v = kv[:, off:off + KV_WIDTH], kv[:, off + KV_WIDTH:off + 2 * KV_WIDTH]
        res = [_attend(_stack_heads(q, kvh), [(k, v, None)],
                       _sink_row(sink_ref, kvh, t) if use_sink else None, precise)
               for kvh in range(B_KV_HEADS)]
        o_ref[...] = _merge_heads(res, t)


def _context_attention(sink, qg32, qw32, kv32, dims, precise):
    t, n_seq = dims["t_ctx"], dims["n_ctx_seq"]
    seq = lambda w: pl.BlockSpec((t, w), lambda b: (b, 0))
    return pl.pallas_call(
        functools.partial(_context_attn_kernel, t=t, precise=precise),
        grid=(n_seq,),
        in_specs=[pl.BlockSpec(memory_space=pltpu.SMEM), seq(2 * Q_WIDTH), seq(2 * Q_WIDTH), seq(4 * KV_WIDTH)],
        out_specs=[seq(Q_WIDTH), seq(Q_WIDTH)],
        out_shape=[jax.ShapeDtypeStruct((n_seq * t, Q_WIDTH), F32)] * 2,
        compiler_params=_cparams("parallel"),
        name="context_attention_precise" if precise else "context_attention",
    )(sink, qg32, qw32, kv32)


def _route(scores_t, sel_t):
    s = [scores_t[e:e + 1, :] for e in range(N_EXPERTS)]
    z = [sel_t[e:e + 1, :] for e in range(N_EXPERTS)]
    gs = []
    for g in range(N_GROUPS):
        m = z[g * EXPERTS_PER_GROUP:(g + 1) * EXPERTS_PER_GROUP]
        best = None
        for a in range(EXPERTS_PER_GROUP):
            for b in range(a + 1, EXPERTS_PER_GROUP):
                pair = m[a] + m[b]
                best = pair if best is None else jnp.maximum(best, pair)
        gs.append(best)
    combine, groups = [], []
    for g in range(N_GROUPS):
        chosen_g = None
        for g2 in range(N_GROUPS):
            if g2 == g:
                continue
            c = (gs[g] > gs[g2]) if g2 < g else (gs[g] >= gs[g2])
            chosen_g = c if chosen_g is None else jnp.logical_and(chosen_g, c)
        base = g * EXPERTS_PER_GROUP
        picked = []
        for a in range(EXPERTS_PER_GROUP):
            rank = 0.0
            for b in range(EXPERTS_PER_GROUP):
                if b == a:
                    continue
                ahead = (z[base + b] >= z[base + a]) if b < a else (z[base + b] > z[base + a])
                rank = rank + ahead.astype(F32)
            picked.append(jnp.where(jnp.logical_and(chosen_g, rank < 2.0), s[base + a], 0.0))
        denom = picked[0] + picked[1] + picked[2] + picked[3]
        denom = jnp.where(chosen_g, denom, 1.0)
        combine.extend(pk / denom for pk in picked)
        groups.append(jnp.where(chosen_g, 1.0, 0.0))
    return jnp.concatenate(combine, axis=0), jnp.concatenate(groups, axis=0)


def _outproj_kernel(*refs, lat_tiles, precise_ctx):
    if precise_ctx:
        (ofl_ref, ofc_ref, obl_ref, obc_ref, ga_ref,
         ogl_ref, ogc_ref, owl_ref, owc_ref, xl_ref, xc_ref, mod_ref, ag_ref,
         ones_ref, w_ref, wlo_ref, g2_ref, rw_ref, rb_ref, before_ref,
         x1_ref, h2_ref, comb_ref, pos_ref, cnt_ref, mix_ref) = refs
    else:
        (ofl_ref, ofc_ref, obl_ref, obc_ref, ga_ref,
         ogl_ref, ogc_ref, owl_ref, owc_ref, xl_ref, xc_ref, mod_ref, ag_ref,
         ones_ref, w_ref, g2_ref, rw_ref, rb_ref, before_ref,
         x1_ref, h2_ref, comb_ref, pos_ref, cnt_ref, mix_ref) = refs
        wlo_ref = None
    is_lat = pl.program_id(0) < lat_tiles
    o = jnp.where(is_lat, ofl_ref[...] + obl_ref[...], ofc_ref[...] + obc_ref[...])
    oa = _head_rms(o, ag_ref[...], ones_ref[...]) * _sigmoid(ga_ref[...])
    parts = ((0, A_WIDTH), (A_WIDTH, A_WIDTH + Q_WIDTH), (A_WIDTH + Q_WIDTH, D_MODEL))

    def mix(operands, precise):
        acc = 0.0
        for a, (r0, r1) in zip(operands, parts):
            if precise:
                ah, al = _split2(a)
                acc = acc + _dot(ah, w_ref[r0:r1, :]) + _dot(al, w_ref[r0:r1, :]) + _dot(ah, wlo_ref[r0:r1, :])
            else:
                acc = acc + _dot(a.astype(BF16), w_ref[r0:r1, :])
        mix_ref[...] = acc

    pl.when(is_lat)(lambda: mix((oa, ogl_ref[...], owl_ref[...]), False))
    pl.when(jnp.logical_not(is_lat))(lambda: mix((oa, ogc_ref[...], owc_ref[...]), precise_ctx))

    mod = mod_ref[...]
    gate1 = mod[:, 2 * D_MODEL:3 * D_MODEL]
    sh2 = mod[:, 3 * D_MODEL:4 * D_MODEL]
    sc2 = mod[:, 4 * D_MODEL:5 * D_MODEL]
    x1 = jnp.where(is_lat, xl_ref[...], xc_ref[...]) + gate1 * mix_ref[...]
    x1_ref[...] = x1
    y = x1 * lax.rsqrt(jnp.mean(x1 * x1, axis=-1, keepdims=True) + EPS) * g2_ref[...]
    h2 = y * (1.0 + sc2) + sh2
    h2_ref[...] = h2.astype(BF16)
    logits_t = _mm(rw_ref[...], h2, True, _NT)
    scores_t = _sigmoid(logits_t)
    comb_t, group_t = _route(scores_t, scores_t + rb_ref[...])
    comb_ref[...] = comb_t
    rank = _dot(group_t.astype(BF16), before_ref[...])
    counts = group_t.sum(axis=1, keepdims=True)
    pos, offset = 0.0, 0.0
    for g in range(N_GROUPS):
        pos = pos + group_t[g:g + 1] * (rank[g:g + 1] + offset)
        offset = offset + counts[g:g + 1]
    pos_ref[...] = pos
    pad = jnp.zeros((8 - N_GROUPS, 1), F32)
    cnt_ref[...] = jnp.broadcast_to(jnp.concatenate([counts, pad], axis=0), (8, LANES)).astype(jnp.int32)


def _out_projection(o_f, o_b, za, og_lat, og_ctx, ow_lat, ow_ctx, x_pair, mod, an_g, ones_blk, w_hi, w_lo, g2,
                    rw_t, rb, dims):
    n = dims["n_lat"] + dims["n_ctx"]
    tm = TOKEN_TILE
    n_tiles = n // tm
    lat_tiles = dims["n_lat"] // tm
    tiles_per_seq = dims["t_lat"] // tm
    ctx_cond = dims["ctx_cond"]
    precise_ctx = w_lo is not None
    cond_idx = lambda i: jnp.where(i < lat_tiles, i // tiles_per_seq, ctx_cond)
    row = lambda w: pl.BlockSpec((tm, w), lambda i: (i, 0))
    lat = lambda w: pl.BlockSpec((tm, w), lambda i: (jnp.minimum(i, lat_tiles - 1), 0))
    ctx = lambda w: pl.BlockSpec((tm, w), lambda i: (jnp.maximum(i - lat_tiles, 0), 0))
    const = lambda a: pl.BlockSpec(a.shape, lambda i: (0,) * a.ndim, pipeline_mode=pl.Buffered(1))
    weights = [w_hi, w_lo] if precise_ctx else [w_hi]
    t_idx = np.arange(tm)
    before = jnp.asarray((t_idx[:, None] < t_idx[None, :]).astype(np.float32), BF16)
    return pl.pallas_call(
        functools.partial(_outproj_kernel, lat_tiles=lat_tiles, precise_ctx=precise_ctx),
        grid=(n_tiles,),
        in_specs=[
            lat(A_WIDTH), ctx(A_WIDTH), lat(A_WIDTH), ctx(A_WIDTH),
            pl.BlockSpec((tm, A_WIDTH), lambda i: (i, 4)),
            lat(Q_WIDTH), ctx(Q_WIDTH), lat(Q_WIDTH), ctx(Q_WIDTH)] + _token_specs(x_pair, tm, lat_tiles) + [
            pl.BlockSpec((None, 1, 6 * D_MODEL), lambda i: (cond_idx(i), 0, 0)),
            const(an_g), const(ones_blk)] + [const(w) for w in weights] + [
            const(g2), const(rw_t), const(rb), const(before),
        ],
        out_specs=[row(D_MODEL), row(D_MODEL), pl.BlockSpec((N_EXPERTS, tm), lambda i: (0, i)),
                   pl.BlockSpec((1, tm), lambda i: (0, i)), pl.BlockSpec((8, LANES), lambda i: (i, 0))],
        out_shape=[
            jax.ShapeDtypeStruct((n, D_MODEL), F32),
            jax.ShapeDtypeStruct((n, D_MODEL), BF16),
            jax.ShapeDtypeStruct((N_EXPERTS, n), F32),
            jax.ShapeDtypeStruct((1, n), F32),
            jax.ShapeDtypeStruct((n_tiles * 8, LANES), jnp.int32),
        ],
        scratch_shapes=[pltpu.VMEM((tm, D_MODEL), F32)],
        compiler_params=_cparams("arbitrary"),
        name="out_projection_precise_ctx" if precise_ctx else "out_projection",
    )(*o_f, *o_b, za, og_lat, og_ctx, ow_lat, ow_ctx, *x_pair, mod, an_g, ones_blk, *weights, g2, rw_t, rb, before)


def _moe_kernel(off_ref, h_ref, comb_ref, pos_ref, x1_ref, mod_ref, wgu_ref, wd_ref, fg_ref, *rest,
                final, lat_tiles):
    *outs, hs_ref, cs_ref, acc_ref = rest
    i = pl.program_id(0)
    tm = h_ref.shape[0]
    row = lax.broadcasted_iota(jnp.int32, (tm, tm), 0).astype(F32)
    perm = jnp.where(row == pos_ref[...], 1.0, 0.0).astype(BF16)
    hs_ref[...] = _dot(perm, h_ref[...]).astype(BF16)
    cs_ref[...] = sum(lax.dot_general(perm, c, _NT, preferred_element_type=F32)
                      for c in _split3(comb_ref[...]))
    acc_ref[...] = jnp.zeros_like(acc_ref)
    lane = lax.broadcasted_iota(jnp.int32, (MOE_BLOCK, N_EXPERTS), 1)

    for r in range(tm // MOE_BLOCK):
        rows = slice(r * MOE_BLOCK, (r + 1) * MOE_BLOCK)

        def group_body(g, carry, rows=rows, r=r):
            has_tokens = jnp.logical_and(off_ref[i, g] < (r + 1) * MOE_BLOCK, off_ref[i, g + 1] > r * MOE_BLOCK)

            @pl.when(has_tokens)
            def _():
                h = hs_ref[rows, :]
                cs = cs_ref[rows, :]
                y = 0.0
                for a in range(EXPERTS_PER_GROUP):
                    e = g * EXPERTS_PER_GROUP + a
                    gu = _dot(h, wgu_ref[e])
                    gate, up = gu[:, :D_EXPERT], gu[:, D_EXPERT:]
                    ce = jnp.sum(jnp.where(lane == e, cs, 0.0), axis=1, keepdims=True)
                    y = y + _dot((_silu(gate) * up * ce).astype(BF16), wd_ref[e])
                acc_ref[rows, :] += y

            return carry

        lax.fori_loop(0, N_GROUPS, group_body, 0)

    y_hi, y_lo = _split2(acc_ref[...])
    y = (lax.dot_general(perm, y_hi, _TN, preferred_element_type=F32)
         + lax.dot_general(perm, y_lo, _TN, preferred_element_type=F32))
    x2 = x1_ref[...] + mod_ref[...][:, 5 * D_MODEL:] * y
    if not final:
        outs[0][...] = x2
    else:
        yn = x2 * lax.rsqrt(jnp.mean(x2 * x2, axis=-1, keepdims=True) + EPS) * fg_ref[...]
        is_lat = i < lat_tiles

        @pl.when(is_lat)
        def _():
            outs[0][...] = yn

        @pl.when(jnp.logical_not(is_lat))
        def _():
            outs[1][...] = yn


def _moe(h2, comb_t, pos, counts, x1, mod, w_gu, w_d, final_g, dims, final):
    n = x1.shape[0]
    tm = TOKEN_TILE
    lat_tiles = dims["n_lat"] // tm
    tiles_per_seq = dims["t_lat"] // tm
    ctx_cond = dims["ctx_cond"]
    cond_idx = lambda i: jnp.where(i < lat_tiles, i // tiles_per_seq, ctx_cond)
    offsets = jnp.concatenate([jnp.zeros((n // tm, 1), jnp.int32), jnp.cumsum(counts, axis=1)], axis=1)
    if final:
        out_specs = [pl.BlockSpec((tm, D_MODEL), lambda i, *_: (jnp.minimum(i, lat_tiles - 1), 0)),
                     pl.BlockSpec((tm, D_MODEL), lambda i, *_: (jnp.maximum(i - lat_tiles, 0), 0))]
        out_shape = [jax.ShapeDtypeStruct((dims["n_lat"], D_MODEL), F32),
                     jax.ShapeDtypeStruct((dims["n_ctx"], D_MODEL), F32)]
    else:
        out_specs = pl.BlockSpec((tm, D_MODEL), lambda i, *_: (i, 0))
        out_shape = jax.ShapeDtypeStruct((n, D_MODEL), F32)
    resident = lambda a: pl.BlockSpec(a.shape, lambda i, *_: (0,) * a.ndim, pipeline_mode=pl.Buffered(1))
    grid_spec = pltpu.PrefetchScalarGridSpec(
        num_scalar_prefetch=1,
        grid=(n // tm,),
        in_specs=[
            pl.BlockSpec((tm, D_MODEL), lambda i, *_: (i, 0)),
            pl.BlockSpec((N_EXPERTS, tm), lambda i, *_: (0, i)),
            pl.BlockSpec((1, tm), lambda i, *_: (0, i)),
            pl.BlockSpec((tm, D_MODEL), lambda i, *_: (i, 0)),
            pl.BlockSpec((None, 1, 6 * D_MODEL), lambda i, *_: (cond_idx(i), 0, 0)),
            resident(w_gu), resident(w_d),
            pl.BlockSpec((1, D_MODEL), lambda i, *_: (0, 0)),
        ],
        out_specs=out_specs,
        scratch_shapes=[pltpu.VMEM((tm, D_MODEL), BF16), pltpu.VMEM((tm, N_EXPERTS), F32),
                        pltpu.VMEM((tm, D_MODEL), F32)],
    )
    return pl.pallas_call(
        functools.partial(_moe_kernel, final=final, lat_tiles=lat_tiles),
        grid_spec=grid_spec,
        out_shape=out_shape,
        compiler_params=_cparams("arbitrary"),
        name="moe_final" if final else "moe",
    )(offsets, h2, comb_t, pos, x1, mod, w_gu, w_d, final_g)


def _rope_tables(t, tile):
    n_freq = HEAD_DIM // 4
    pos = jnp.arange(t)
    freqs = ROPE_THETA ** (-jnp.arange(n_freq, dtype=F32) / n_freq)
    ang_row = (pos // GRID_W).astype(F32)[:, None] * freqs
    ang_col = (pos % GRID_W).astype(F32)[:, None] * freqs
    cos_h = jnp.concatenate([jnp.cos(ang_row)] * 2 + [jnp.cos(ang_col)] * 2, axis=1)
    sin_h = jnp.concatenate([-jnp.sin(ang_row), jnp.sin(ang_row), -jnp.sin(ang_col), jnp.sin(ang_col)], axis=1)
    cos_t = jnp.concatenate([jnp.ones((tile, LANES), F32), jnp.concatenate([cos_h, cos_h], axis=1)], axis=0)
    sin_t = jnp.concatenate([jnp.zeros((tile, LANES), F32), jnp.concatenate([sin_h, sin_h], axis=1)], axis=0)
    return cos_t, sin_t


def _head_ones(width):
    r = np.arange(width) // HEAD_DIM
    return jnp.asarray((r[:, None] == r[None, :]).astype(np.float32), BF16)


def _split_kernel(w_ref, hi_ref, lo_ref):
    hi, lo = _split2(w_ref[...])
    hi_ref[...] = hi
    lo_ref[...] = lo


def _hi_lo(w, want_lo):
    if not want_lo:
        return w.astype(BF16), None
    rows, width = w.shape
    tr = 256
    spec = pl.BlockSpec((tr, width), lambda i: (i, 0))
    return pl.pallas_call(
        _split_kernel,
        grid=(rows // tr,),
        in_specs=[spec],
        out_specs=[spec, spec],
        out_shape=[jax.ShapeDtypeStruct(w.shape, BF16)] * 2,
        compiler_params=_cparams("parallel"),
        name="split_weight",
    )(w)


def _state_to_block_diag_t(s):
    per_pair = LANES // HEAD_DIM
    st = jnp.swapaxes(s, -1, -2).reshape(s.shape[:2] + (A_HEADS // per_pair, per_pair, HEAD_DIM, HEAD_DIM))
    eye = jnp.eye(per_pair, dtype=s.dtype)
    out = jnp.einsum("bdphvk,hg->bdphvgk", st, eye)
    return out.reshape(s.shape[:2] + (A_HEADS // per_pair, LANES, LANES))


def _block_diag_t_to_state(st):
    per_pair = LANES // HEAD_DIM
    s7 = st.reshape(st.shape[:3] + (per_pair, HEAD_DIM, per_pair, HEAD_DIM))
    diag = jnp.stack([s7[:, :, :, h, :, h, :] for h in range(per_pair)], axis=3)
    return jnp.swapaxes(diag, -1, -2).reshape(st.shape[:2] + (A_HEADS, HEAD_DIM, HEAD_DIM))


def kernel(x_prompt, x_sample, cache_glob_k, cache_glob_v, cache_win_k, cache_win_v, state_hgrn, c, c_ctx,
           mod_w, mod_b, norm1_g, norm2_g, w_in, w_out, hgrn_lb, hgrn_norm_g, q_norm_g, k_norm_g, win_sink,
           router_w, router_b, w_gate_up, w_down, final_g):
    n_ctx_seq, t_ctx, d = x_prompt.shape
    n_lat_seq, t_lat, _ = x_sample.shape
    depth = mod_w.shape[0]
    past = cache_glob_k.shape[2]
    n_lat, n_ctx = n_lat_seq * t_lat, n_ctx_seq * t_ctx
    dims = dict(n_ctx_seq=n_ctx_seq, t_ctx=t_ctx, n_lat_seq=n_lat_seq, t_lat=t_lat,
                n_lat=n_lat, n_ctx=n_ctx, ctx_cond=n_lat_seq)
    assert d == D_MODEL and n_lat_seq < COND_ROWS
    assert n_ctx % TOKEN_TILE == 0 and t_lat % TOKEN_TILE == 0
    assert t_ctx % CHUNK == 0 and t_lat >= 4 * WINDOW

    x_pair = (x_sample.reshape(n_lat, d), x_prompt.reshape(n_ctx, d))
    cond = jnp.concatenate([c, c_ctx[None, :], jnp.zeros((COND_ROWS - n_lat_seq - 1, d), F32)], axis=0)
    mod = _modulation(cond, mod_w, mod_b).reshape(depth, COND_ROWS, 1, 6 * d)

    p_lb = jax.nn.softmax(hgrn_lb.astype(F32), axis=1)
    lbs = jnp.cumsum(p_lb, axis=1) - p_lb[:, :1]

    cos_t, sin_t = _rope_tables(t_lat, TOKEN_TILE)
    ones128, ones256 = _head_ones(LANES), _head_ones(A_WIDTH)
    tile2 = lambda g, reps: jnp.tile(g, reps)[None, :]
    rw_t = router_w.T
    rb = router_b[:, None]
    fg = final_g[None, :]

    new_kv, new_state = [], []
    for l in range(depth):
        precise = l < depth - 1
        win_hi, win_lo = _hi_lo(w_in[l], precise)
        za, qg_att, kvg, qw_att, kvw, kv32, qg32, qw32 = _in_projection(
            x_pair, mod[l], norm1_g[l][None, :], win_hi, win_lo, cos_t, sin_t,
            tile2(q_norm_g[l], 2), tile2(k_norm_g[l], 2), ones128, dims)

        s0 = state_hgrn[:, l].astype(F32)
        of_lat, ob_lat, _ = _hgrn_scan(za, 0, lbs[:, l], _state_to_block_diag_t(s0), n_lat_seq, t_lat,
                                       False, False, "hgrn_latent")
        of_ctx, ob_ctx, s_fin = _hgrn_scan(za, n_lat, lbs[:, l], None, n_ctx_seq, t_ctx, precise, True,
                                           "hgrn_context_precise" if precise else "hgrn_context")

        cast_cache = lambda a: a[:, l].reshape(n_lat_seq, past, KV_WIDTH).astype(BF16)
        og_lat = _global_attention(qg_att, kvg, cast_cache(cache_glob_k), cast_cache(cache_glob_v), dims)
        ow_lat = _window_attention(win_sink[l], qw_att, kvw, cast_cache(cache_win_k),
                                   cast_cache(cache_win_v), dims)
        og_ctx, ow_ctx = _context_attention(win_sink[l], qg32, qw32, kv32, dims, precise)

        wout_hi, wout_lo = _hi_lo(w_out[l], precise)
        x1, h2, comb_t, pos, cnt = _out_projection(
            (of_lat, of_ctx), (ob_lat, ob_ctx), za, og_lat, og_ctx, ow_lat, ow_ctx, x_pair, mod[l],
            tile2(hgrn_norm_g[l], A_HEADS), ones256, wout_hi, wout_lo, norm2_g[l][None, :], rw_t, rb, dims)

        counts = cnt.reshape(-1, 8, LANES)[:, :N_GROUPS, 0]
        x = _moe(h2, comb_t, pos, counts, x1, mod[l], w_gate_up[l].astype(BF16), w_down[l].astype(BF16), fg,
                 dims, final=(l == depth - 1))
        x_pair = (x, x)

        new_kv.append(kv32[:n_ctx].reshape(n_ctx_seq, t_ctx, 4, B_KV_HEADS, HEAD_DIM))
        new_state.append(_block_diag_t_to_state(s_fin))

    y_sample = x[0].reshape(n_lat_seq, t_lat, d)
    y_prompt = x[1].reshape(n_ctx_seq, t_ctx, d)
    kv = jnp.stack(new_kv, axis=1)
    return (y_prompt, y_sample, kv[:, :, :, 0], kv[:, :, :, 1], kv[:, :, :, 2], kv[:, :, :, 3],
            jnp.stack(new_state, axis=1).astype(x_prompt.dtype))
```

```python
import functools

import numpy as np
import jax
import jax.numpy as jnp
from jax import lax
from jax.experimental import pallas as pl
from jax.experimental.pallas import tpu as pltpu

F32 = jnp.float32
BF16 = jnp.bfloat16

D_MODEL = 1024
HEAD_DIM = 64
GRID_W = 64
A_HEADS = 4
A_WIDTH = A_HEADS * HEAD_DIM
B_Q_HEADS = 6
B_KV_HEADS = 2
C_Q_HEADS = 6
C_KV_HEADS = 2
Q_WIDTH = B_Q_HEADS * HEAD_DIM
KV_WIDTH = B_KV_HEADS * HEAD_DIM
GQA_GROUP = B_Q_HEADS // B_KV_HEADS
WINDOW = 128
ROPE_THETA = 10000.0
IN_WIDTH = 5 * A_WIDTH + 2 * (Q_WIDTH + 2 * KV_WIDTH)
N_EXPERTS = 16
N_GROUPS = 4
EXPERTS_PER_GROUP = N_EXPERTS // N_GROUPS
D_EXPERT = 256
EPS = 1e-6
MASK_VALUE = -1e30
TINY = 1e-30
LOG2E = 1.4426950408889634

LANES = 128
SUB = 16
CHUNK = 128
HGRN_SEQS = 2
TOKEN_TILE = 512
MOE_BLOCK = 128
KEY_CHUNK = 512
COND_ROWS = 16
VMEM_LIMIT = 58 * 1024 * 1024

_NT = (((1,), (1,)), ((), ()))
_TN = (((0,), (0,)), ((), ()))


def _cparams(*sem):
    return pltpu.CompilerParams(dimension_semantics=sem, vmem_limit_bytes=VMEM_LIMIT)


def _dot(a, b):
    return jnp.dot(a, b, preferred_element_type=F32)


def _split2(x):
    h = x.astype(BF16)
    return h, (x - h.astype(F32)).astype(BF16)


def _split3(x):
    h = x.astype(BF16)
    r = x - h.astype(F32)
    m = r.astype(BF16)
    l = (r - m.astype(F32)).astype(BF16)
    return h, m, l


def _dot_sel(c, x):
    h, m, l = _split3(x)
    return _dot(c, h) + _dot(c, m) + _dot(c, l)


def _mm(a, b, precise, dims=None):
    if dims is None:
        dims = (((a.ndim - 1,), (0,)), ((), ()))
    dg = lambda x, y: lax.dot_general(x, y, dims, preferred_element_type=F32)
    if not precise:
        return dg(a.astype(BF16), b.astype(BF16))
    ah, al = _split2(a)
    bh, bl = _split2(b)
    return dg(ah, bh) + dg(ah, bl) + dg(al, bh)


def _mm_sel(a, sel, precise):
    if not precise:
        return _dot(a.astype(BF16), sel)
    ah, al = _split2(a)
    return _dot(ah, sel) + _dot(al, sel)


def _sigmoid(x):
    return jax.nn.sigmoid(x)


def _silu(x):
    return x * jax.nn.sigmoid(x)


def _mod_kernel(cond_ref, w_ref, b_ref, o_ref):
    o_ref[...] = _mm(_silu(cond_ref[...]), w_ref[...], True) + b_ref[...]


def _modulation(cond, mod_w, mod_b):
    depth, d, width = mod_w.shape
    tn = 1536
    return pl.pallas_call(
        _mod_kernel,
        grid=(depth, width // tn),
        in_specs=[
            pl.BlockSpec((COND_ROWS, d), lambda l, j: (0, 0)),
            pl.BlockSpec((None, d, tn), lambda l, j: (l, 0, j)),
            pl.BlockSpec((None, 1, tn), lambda l, j: (l, 0, j)),
        ],
        out_specs=pl.BlockSpec((None, COND_ROWS, tn), lambda l, j: (l, 0, j)),
        out_shape=jax.ShapeDtypeStruct((depth, COND_ROWS, width), F32),
        compiler_params=_cparams("parallel", "parallel"),
        name="modulation",
    )(cond, mod_w, mod_b.reshape(depth, 1, width))


def _head_rms(xb, g2, ones_blk):
    ss = _mm_sel(xb * xb, ones_blk, True)
    return xb * lax.rsqrt(ss * (1.0 / HEAD_DIM) + EPS) * g2


def _rope(yb, cos, sin_signed, first16):
    partner = jnp.where(first16, pltpu.roll(yb, LANES - 16, 1), pltpu.roll(yb, 16, 1))
    return yb * cos + partner * sin_signed


def _q_attention_layout(blocks, lane_half):
    out = []
    for j in range(2 * len(blocks)):
        src = blocks[j // 2]
        dst_half = j // GQA_GROUP
        if j % 2 != dst_half:
            src = pltpu.roll(src, HEAD_DIM, 1)
        out.append(jnp.where(lane_half == dst_half, src, 0.0))
    return jnp.concatenate(out, axis=1)


def _inproj_kernel(*refs, precise_ctx, lat_tiles):
    if precise_ctx:
        (xl_ref, xc_ref, mod_ref, g1_ref, w_ref, wlo_ref, cos_ref, sin_ref, qg_ref, kg_ref, ones_ref,
         za_ref, qga_ref, kvg_ref, qwa_ref, kvw_ref, kv32_ref, qg32_ref, qw32_ref, z_ref) = refs
    else:
        (xl_ref, xc_ref, mod_ref, g1_ref, w_ref, cos_ref, sin_ref, qg_ref, kg_ref, ones_ref,
         za_ref, qga_ref, kvg_ref, qwa_ref, kvw_ref, kv32_ref, qg32_ref, qw32_ref, z_ref) = refs
        wlo_ref = None
    is_lat = pl.program_id(0) < lat_tiles
    x = jnp.where(is_lat, xl_ref[...], xc_ref[...])
    mod = mod_ref[...]
    gain = g1_ref[...] * (1.0 + mod[:, D_MODEL:2 * D_MODEL])
    h = x * lax.rsqrt(jnp.mean(x * x, axis=-1, keepdims=True) + EPS) * gain + mod[:, 0:D_MODEL]

    def one_pass():
        z_ref[...] = _dot(h.astype(BF16), w_ref[...])

    def three_pass():
        hh, hl = _split2(h)
        z_ref[...] = _dot(hh, w_ref[...]) + _dot(hl, w_ref[...]) + _dot(hh, wlo_ref[...])

    if precise_ctx:
        pl.when(is_lat)(one_pass)
        pl.when(jnp.logical_not(is_lat))(three_pass)
    else:
        one_pass()
    z = z_ref[...]
    za_ref[...] = z[:, :5 * A_WIDTH]

    cos = cos_ref[...]
    sin = sin_ref[...]
    ones_blk = ones_ref[...]
    lane = lax.broadcasted_iota(jnp.int32, (x.shape[0], LANES), 1)
    first16 = (lane % 32) < 16
    lane_half = lane // HEAD_DIM
    scale = LOG2E * HEAD_DIM ** -0.5

    o = 5 * A_WIDTH
    qg = [z[:, o + LANES * b:o + LANES * (b + 1)] for b in range(3)]
    o += Q_WIDTH
    kg = z[:, o:o + KV_WIDTH]
    vg = z[:, o + KV_WIDTH:o + 2 * KV_WIDTH]
    o += 2 * KV_WIDTH
    qw = [z[:, o + LANES * b:o + LANES * (b + 1)] for b in range(3)]
    o += Q_WIDTH
    kw = z[:, o:o + KV_WIDTH]
    vw = z[:, o + KV_WIDTH:o + 2 * KV_WIDTH]

    qg = [_rope(_head_rms(b, qg_ref[...], ones_blk), cos, sin, first16) * scale for b in qg]
    kg = _rope(_head_rms(kg, kg_ref[...], ones_blk), cos, sin, first16)
    qw = [_rope(b, cos, sin, first16) * scale for b in qw]
    kw = _rope(kw, cos, sin, first16)

    qg_att = _q_attention_layout(qg, lane_half)
    qw_att = _q_attention_layout(qw, lane_half)
    qga_ref[...] = qg_att.astype(BF16)
    qwa_ref[...] = qw_att.astype(BF16)
    kvg_ref[...] = jnp.concatenate([kg, vg], axis=1).astype(BF16)
    kvw_ref[...] = jnp.concatenate([kw, vw], axis=1).astype(BF16)
    kv32_ref[...] = jnp.concatenate([kg, vg, kw, vw], axis=1)
    qg32_ref[...] = qg_att
    qw32_ref[...] = qw_att


def _token_specs(x_pair, tm, lat_tiles):
    off = lat_tiles if x_pair[1] is x_pair[0] else 0
    lat = pl.BlockSpec((tm, D_MODEL), lambda i, *_: (jnp.minimum(i, lat_tiles - 1), 0))
    ctx = pl.BlockSpec((tm, D_MODEL), lambda i, *_: (jnp.maximum(i, lat_tiles) - lat_tiles + off, 0))
    return [lat, ctx]


def _in_projection(x_pair, mod, g1, w_hi, w_lo, cos_t, sin_t, qn_g, kn_g, ones_blk, dims):
    n = dims["n_lat"] + dims["n_ctx"]
    tm = TOKEN_TILE
    n_tiles = n // tm
    lat_tiles = dims["n_lat"] // tm
    ctx_tiles = n_tiles - lat_tiles
    tiles_per_seq = dims["t_lat"] // tm
    ctx_cond = dims["ctx_cond"]
    precise_ctx = w_lo is not None

    def cond_idx(i):
        return jnp.where(i < lat_tiles, i // tiles_per_seq, ctx_cond)

    def rope_idx(i):
        return jnp.where(i < lat_tiles, 1 + i % tiles_per_seq, 0)

    def ctx_idx(i):
        return jnp.where(i < lat_tiles, ctx_tiles, i - lat_tiles)

    row = lambda w: pl.BlockSpec((tm, w), lambda i: (i, 0))
    ctx_row = lambda w: pl.BlockSpec((tm, w), lambda i: (ctx_idx(i), 0))
    const = lambda a: pl.BlockSpec(a.shape, lambda i: (0,) * a.ndim, pipeline_mode=pl.Buffered(1))
    weights = [w_hi, w_lo] if precise_ctx else [w_hi]
    ctx_rows = (ctx_tiles + 1) * tm
    return pl.pallas_call(
        functools.partial(_inproj_kernel, precise_ctx=precise_ctx, lat_tiles=lat_tiles),
        grid=(n_tiles,),
        in_specs=_token_specs(x_pair, tm, lat_tiles) + [
            pl.BlockSpec((None, 1, 6 * D_MODEL), lambda i: (cond_idx(i), 0, 0)),
            const(g1)] + [const(w) for w in weights] + [
            pl.BlockSpec((tm, LANES), lambda i: (rope_idx(i), 0)),
            pl.BlockSpec((tm, LANES), lambda i: (rope_idx(i), 0)),
            const(qn_g), const(kn_g), const(ones_blk),
        ],
        out_specs=[
            row(5 * A_WIDTH), row(2 * Q_WIDTH), row(2 * KV_WIDTH), row(2 * Q_WIDTH), row(2 * KV_WIDTH),
            ctx_row(4 * KV_WIDTH), ctx_row(2 * Q_WIDTH), ctx_row(2 * Q_WIDTH),
        ],
        out_shape=[
            jax.ShapeDtypeStruct((n, 5 * A_WIDTH), F32),
            jax.ShapeDtypeStruct((n, 2 * Q_WIDTH), BF16),
            jax.ShapeDtypeStruct((n, 2 * KV_WIDTH), BF16),
            jax.ShapeDtypeStruct((n, 2 * Q_WIDTH), BF16),
            jax.ShapeDtypeStruct((n, 2 * KV_WIDTH), BF16),
            jax.ShapeDtypeStruct((ctx_rows, 4 * KV_WIDTH), F32),
            jax.ShapeDtypeStruct((ctx_rows, 2 * Q_WIDTH), F32),
            jax.ShapeDtypeStruct((ctx_rows, 2 * Q_WIDTH), F32),
        ],
        scratch_shapes=[pltpu.VMEM((tm, IN_WIDTH), F32)],
        compiler_params=_cparams("arbitrary"),
        name="in_projection_precise_ctx" if precise_ctx else "in_projection",
    )(*x_pair, mod, g1, *weights, cos_t, sin_t, qn_g, kn_g, ones_blk)


def _hgrn_chunk(qa, v, fz, lb, st, consts, reverse, precise):
    cum, e_sel, x_sel, sub_mask, p_mask, v_mask, head_mask = consts
    log2e = LOG2E
    n_sub, half = CHUNK // SUB, SUB // 2
    q = _silu(qa)
    f = lb + (1.0 - lb) * _sigmoid(fz)
    logf2 = jnp.log(jnp.maximum(f, TINY)) * log2e
    kk = (1.0 - lb) * _sigmoid(-fz)

    b = _dot_sel(cum, logf2)
    b3 = b.reshape(n_sub, SUB, A_WIDTH)
    end = 0 if reverse else SUB - 1
    tot = jnp.broadcast_to(b3[:, end:end + 1, :], (n_sub, SUB, A_WIDTH)).reshape(CHUNK, A_WIDTH)
    q_in = q * jnp.exp2(b)
    k_out = kk * jnp.exp2(tot - b)
    d_sub = jnp.exp2(tot)

    c3 = (b - jnp.log(kk) * log2e).reshape(n_sub, SUB, A_WIDTH)
    b4 = b.reshape(n_sub, 2, half, A_WIDTH)
    q4 = q.reshape(n_sub, 2, half, A_WIDTH)
    r_local = lax.broadcasted_iota(jnp.int32, (n_sub, half, A_WIDTH), 1)
    zero_half = jnp.zeros((n_sub, half, A_WIDTH), F32)
    pieces = []
    for s in range(SUB):
        cs = c3[:, s:s + 1, :]
        parts = []
        for hsel in range(2):
            lo, hi = hsel * half, hsel * half + half - 1
            if (lo > s) if reverse else (hi < s):
                parts.append(zero_half)
                continue
            arg = b4[:, hsel] - cs
            if not ((hi <= s) if reverse else (lo >= s)):
                t_local = r_local + lo
                arg = jnp.where((t_local <= s) if reverse else (t_local >= s), arg, MASK_VALUE)
            parts.append(q4[:, hsel] * jnp.exp2(arg))
        piece = jnp.stack(parts, axis=1).reshape(CHUNK, A_WIDTH)
        pieces.append(piece if precise else piece.astype(BF16))
    w_cat = jnp.concatenate(pieces, axis=1)
    s_local = _mm_sel(w_cat, e_sel, precise)
    p = _mm_sel(s_local, x_sel, precise) * p_mask
    v_bd = jnp.concatenate([v] * A_HEADS, axis=0) * v_mask
    o_intra = _mm(p, v_bd, precise)

    v_op = v if precise else v.astype(BF16)
    k_op = k_out if precise else k_out.astype(BF16)
    q_op = q_in if precise else q_in.astype(BF16)
    sub_m = sub_mask
    order = range(n_sub - 1, -1, -1) if reverse else range(n_sub)
    new_st, o_inter = [], []
    for p in range(A_WIDTH // LANES):
        lanes = slice(LANES * p, LANES * (p + 1))
        k_exp = jnp.concatenate([k_op[:, lanes]] * n_sub, axis=1) * sub_m
        u_all = _mm(v_op[:, lanes], k_exp, precise, _TN)
        s = st[p]
        s_before = [None] * n_sub
        for j in order:
            s_before[j] = s
            s = d_sub[SUB * j:SUB * j + 1, lanes] * s + head_mask * u_all[:, LANES * j:LANES * (j + 1)]
        new_st.append(s)
        q_exp = jnp.concatenate([q_op[:, lanes]] * n_sub, axis=1) * sub_m
        o_inter.append(_mm(q_exp, jnp.concatenate(s_before, axis=1), precise, _NT))
    return o_intra + jnp.concatenate(o_inter, axis=1), jnp.stack(new_st, axis=0)


def _hgrn_kernel(*refs, has_s0, want_state, precise):
    refs = list(refs)
    seq_refs = [refs[6 * sq:6 * sq + 6] for sq in range(HGRN_SEQS)]
    del refs[:6 * HGRN_SEQS]
    lb_ref = refs.pop(0)
    s0_ref = refs.pop(0) if has_s0 else None
    cumf_ref, cumb_ref, e_ref, x_ref, sm_ref, pm_ref, vm_ref, hm_ref, of_ref, ob_ref = refs[:10]
    del refs[:10]
    sout_ref = refs.pop(0) if want_state else None
    st_ref, = refs
    c = pl.program_id(1)

    @pl.when(c == 0)
    def _():
        st_ref[...] = s0_ref[...] if has_s0 else jnp.zeros_like(st_ref)

    shared = (e_ref[...], x_ref[...], sm_ref[...], pm_ref[...], vm_ref[...], hm_ref[...])
    lb = lb_ref[...]
    for sq in range(HGRN_SEQS):
        qaf_ref, iaf_ref, fzf_ref, qab_ref, iab_ref, fzb_ref = seq_refs[sq]
        o_f, st_f = _hgrn_chunk(qaf_ref[...], iaf_ref[...], fzf_ref[...], lb[0:1], st_ref[sq, 0],
                                (cumf_ref[...],) + shared, False, precise)
        o_b, st_b = _hgrn_chunk(qab_ref[...], iab_ref[...], fzb_ref[...], lb[1:2], st_ref[sq, 1],
                                (cumb_ref[...],) + shared, True, precise)
        of_ref[sq] = o_f
        ob_ref[sq] = o_b
        st_ref[sq, 0] = st_f
        st_ref[sq, 1] = st_b

    if want_state:
        @pl.when(c == pl.num_programs(1) - 1)
        def _():
            sout_ref[...] = st_ref[...]


def _hgrn_consts():
    r = np.arange(CHUNK)
    same = (r[:, None] // SUB) == (r[None, :] // SUB)
    cum_f = (same & (r[None, :] <= r[:, None])).astype(np.float32)
    cum_b = (same & (r[None, :] >= r[:, None])).astype(np.float32)
    tot = same.astype(np.float32)
    e = np.zeros((SUB, A_HEADS, HEAD_DIM, A_HEADS, SUB), np.float32)
    for s in range(SUB):
        for h in range(A_HEADS):
            e[s, h, :, h, s] = 1.0
    e = e.reshape(SUB * A_WIDTH, A_HEADS * SUB)
    x = np.zeros((A_HEADS, SUB, A_HEADS, CHUNK), np.float32)
    for h in range(A_HEADS):
        for s in range(CHUNK):
            x[h, s % SUB, h, s] = 1.0
    x = x.reshape(A_HEADS * SUB, A_HEADS * CHUNK)
    p_mask = np.tile(tot, (1, A_HEADS))
    hv = np.arange(A_HEADS * CHUNK) // CHUNK
    hk = np.arange(A_WIDTH) // HEAD_DIM
    v_mask = (hv[:, None] == hk[None, :]).astype(np.float32)
    head_mask = (hk[:LANES, None] == hk[None, :LANES]).astype(np.float32)
    sub_mask = np.repeat(r[:, None] // SUB == np.arange(CHUNK // SUB)[None, :], LANES, axis=1).astype(np.float32)
    sel = [jnp.asarray(a, BF16) for a in (cum_f, cum_b, e, x, sub_mask)]
    return sel + [jnp.asarray(a, F32) for a in (p_mask, v_mask, head_mask)]


def _hgrn_scan(za, first_row, lb2, s0_t, n_seq, t, precise, want_state, name):
    nc = t // CHUNK
    groups = n_seq // HGRN_SEQS
    base = first_row // CHUNK
    za3 = za.reshape(za.shape[0] // CHUNK, CHUNK, za.shape[1])
    consts = _hgrn_consts()
    state_shape = (HGRN_SEQS, 2, A_WIDTH // LANES, LANES, LANES)
    const = lambda a: pl.BlockSpec(a.shape, lambda p, c: (0,) * a.ndim)

    def chunk_spec(sq, col, backward):
        def index(p, c):
            return (base + (p * HGRN_SEQS + sq) * nc + (nc - 1 - c if backward else c), 0, col)
        return pl.BlockSpec((None, CHUNK, A_WIDTH), index)

    fwd = lambda col: pl.BlockSpec((None, HGRN_SEQS, CHUNK, A_WIDTH), lambda p, c: (p, 0, c, col))
    bwd = lambda col: pl.BlockSpec((None, HGRN_SEQS, CHUNK, A_WIDTH), lambda p, c: (p, 0, nc - 1 - c, col))
    state_spec = pl.BlockSpec((None,) + state_shape, lambda p, c: (p, 0, 0, 0, 0, 0))
    operands = [za3] * (6 * HGRN_SEQS) + [lb2]
    in_specs = [chunk_spec(sq, col, backward) for sq in range(HGRN_SEQS)
                for col, backward in ((0, False), (1, False), (2, False), (0, True), (1, True), (3, True))]
    in_specs.append(const(lb2))
    if s0_t is not None:
        operands.append(s0_t.reshape((groups,) + state_shape))
        in_specs.append(state_spec)
    o_shape = jax.ShapeDtypeStruct((groups, HGRN_SEQS, t, A_WIDTH), F32)
    out_specs, out_shape = [fwd(0), bwd(0)], [o_shape, o_shape]
    if want_state:
        out_specs.append(state_spec)
        out_shape.append(jax.ShapeDtypeStruct((groups,) + state_shape, F32))
    outs = pl.pallas_call(
        functools.partial(_hgrn_kernel, has_s0=s0_t is not None, want_state=want_state, precise=precise),
        grid=(groups, nc),
        in_specs=in_specs + [const(c) for c in consts],
        out_specs=out_specs,
        out_shape=out_shape,
        scratch_shapes=[pltpu.VMEM(state_shape, F32)],
        compiler_params=_cparams("parallel", "arbitrary"),
        name=name,
    )(*operands, *consts)
    o_f, o_b = (o.reshape(n_seq * t, A_WIDTH) for o in outs[:2])
    s_fin = outs[2].reshape((n_seq,) + state_shape[1:]) if want_state else None
    return o_f, o_b, s_fin


def _attend(q3, segments, sink_row, precise=False):
    scores = []
    for k, _, valid in segments:
        s = _mm(k, q3, precise, _NT)
        if valid is not None:
            s = jnp.where(valid, s, MASK_VALUE)
        scores.append(s)
    m = scores[0].max(axis=0, keepdims=True)
    for s in scores[1:]:
        m = jnp.maximum(m, s.max(axis=0, keepdims=True))
    if sink_row is not None:
        m = jnp.maximum(m, sink_row)
    denom = jnp.exp2(sink_row - m) if sink_row is not None else 0.0
    acc = 0.0
    for s, (_, v, _) in zip(scores, segments):
        p = jnp.exp2(s - m)
        denom = denom + p.sum(axis=0, keepdims=True)
        acc = acc + _mm(v, p, precise, _TN)
    return (acc / denom).T


def _stack_heads(q, kvh):
    base = kvh * GQA_GROUP
    return jnp.concatenate([q[:, LANES * (base + g):LANES * (base + g + 1)] for g in range(GQA_GROUP)], axis=0)


def _merge_heads(res, tq):
    lane = lax.broadcasted_iota(jnp.int32, (tq, LANES), 1)
    heads = []
    for j in range(B_Q_HEADS):
        kvh, g = divmod(j, GQA_GROUP)
        o = res[kvh][g * tq:(g + 1) * tq]
        if kvh != j % 2:
            o = pltpu.roll(o, HEAD_DIM, 1)
        heads.append(o)
    blocks = [jnp.where(lane < HEAD_DIM, heads[2 * b], heads[2 * b + 1]) for b in range(B_Q_HEADS // 2)]
    return jnp.concatenate(blocks, axis=1)


def _sink_row(sink_ref, kvh, tq):
    return jnp.concatenate(
        [jnp.full((1, tq), sink_ref[kvh * GQA_GROUP + g] * LOG2E, F32) for g in range(GQA_GROUP)], axis=1)


def _global_attn_kernel(q_ref, kv_ref, ck_ref, cv_ref, o_ref, *, tq):
    q = q_ref[...]
    kv = kv_ref[...]
    k, v = kv[:, :KV_WIDTH], kv[:, KV_WIDTH:]
    ck, cv = ck_ref[...], cv_ref[...]
    segments = [(k[c:c + KEY_CHUNK], v[c:c + KEY_CHUNK], None) for c in range(0, k.shape[0], KEY_CHUNK)]
    segments.append((ck, cv, None))
    res = [_attend(_stack_heads(q, kvh), segments, None) for kvh in range(B_KV_HEADS)]
    o_ref[...] = _merge_heads(res, tq).astype(BF16)


def _global_attention(q_att, kv, ck, cv, dims):
    t, n_seq = dims["t_lat"], dims["n_lat_seq"]
    tq = 256
    nq = t // tq
    return pl.pallas_call(
        functools.partial(_global_attn_kernel, tq=tq),
        grid=(n_seq, nq),
        in_specs=[
            pl.BlockSpec((tq, 2 * Q_WIDTH), lambda b, i: (b * nq + i, 0)),
            pl.BlockSpec((t, 2 * KV_WIDTH), lambda b, i: (b, 0)),
            pl.BlockSpec((None,) + ck.shape[1:], lambda b, i: (b, 0, 0)),
            pl.BlockSpec((None,) + cv.shape[1:], lambda b, i: (b, 0, 0)),
        ],
        out_specs=pl.BlockSpec((tq, Q_WIDTH), lambda b, i: (b * nq + i, 0)),
        out_shape=jax.ShapeDtypeStruct((n_seq * t, Q_WIDTH), BF16),
        compiler_params=_cparams("parallel", "arbitrary"),
        name="global_attention",
    )(q_att, kv, ck, cv)


def _window_attn_kernel(sink_ref, q_ref, kv_ref, ck_ref, cv_ref, o_ref, *, tq, t):
    j = pl.program_id(1)
    span = tq + 2 * WINDOW
    start = pl.multiple_of(jnp.clip(j * tq - WINDOW, 0, t - span), WINDOW)
    q = q_ref[...]
    kv = kv_ref[pl.ds(start, span), :]
    k, v = kv[:, :KV_WIDTH], kv[:, KV_WIDTH:]
    key_pos = start + lax.broadcasted_iota(jnp.int32, (span, GQA_GROUP * tq), 0)
    query_pos = j * tq + lax.broadcasted_iota(jnp.int32, (span, GQA_GROUP * tq), 1) % tq
    valid = jnp.abs(query_pos - key_pos) <= WINDOW
    segments = [(k, v, valid), (ck_ref[...], cv_ref[...], None)]
    res = [_attend(_stack_heads(q, kvh), segments, _sink_row(sink_ref, kvh, tq)) for kvh in range(C_KV_HEADS)]
    o_ref[...] = _merge_heads(res, tq).astype(BF16)


def _window_attention(sink, q_att, kv, ck, cv, dims):
    t, n_seq = dims["t_lat"], dims["n_lat_seq"]
    tq = 2 * WINDOW
    nq = t // tq
    return pl.pallas_call(
        functools.partial(_window_attn_kernel, tq=tq, t=t),
        grid=(n_seq, nq),
        in_specs=[
            pl.BlockSpec(memory_space=pltpu.SMEM),
            pl.BlockSpec((tq, 2 * Q_WIDTH), lambda b, i: (b * nq + i, 0)),
            pl.BlockSpec((t, 2 * KV_WIDTH), lambda b, i: (b, 0)),
            pl.BlockSpec((None,) + ck.shape[1:], lambda b, i: (b, 0, 0)),
            pl.BlockSpec((None,) + cv.shape[1:], lambda b, i: (b, 0, 0)),
        ],
        out_specs=pl.BlockSpec((tq, Q_WIDTH), lambda b, i: (b * nq + i, 0)),
        out_shape=jax.ShapeDtypeStruct((n_seq * t, Q_WIDTH), BF16),
        compiler_params=_cparams("parallel", "arbitrary"),
        name="window_attention",
    )(sink, q_att, kv, ck, cv)


def _context_attn_kernel(sink_ref, qg_ref, qw_ref, kv_ref, og_ref, ow_ref, *, t, precise):
    kv = kv_ref[...]
    for q_ref, o_ref, off, use_sink in ((qg_ref, og_ref, 0, False), (qw_ref, ow_ref, 2 * KV_WIDTH, True)):
        q = q_ref[...]
        k, v = kv[:, off:off + KV_WIDTH], kv[:, off + KV_WIDTH:off + 2 * KV_WIDTH]
        res = [_attend(_stack_heads(q, kvh), [(k, v, None)],
                       _sink_row(sink_ref, kvh, t) if use_sink else None, precise)
               for kvh in range(B_KV_HEADS)]
        o_ref[...] = _merge_heads(res, t)


def _context_attention(sink, qg32, qw32, kv32, dims, precise):
    t, n_seq = dims["t_ctx"], dims["n_ctx_seq"]
    seq = lambda w: pl.BlockSpec((t, w), lambda b: (b, 0))
    return pl.pallas_call(
        functools.partial(_context_attn_kernel, t=t, precise=precise),
        grid=(n_seq,),
        in_specs=[pl.BlockSpec(memory_space=pltpu.SMEM), seq(2 * Q_WIDTH), seq(2 * Q_WIDTH), seq(4 * KV_WIDTH)],
        out_specs=[seq(Q_WIDTH), seq(Q_WIDTH)],
        out_shape=[jax.ShapeDtypeStruct((n_seq * t, Q_WIDTH), F32)] * 2,
        compiler_params=_cparams("parallel"),
        name="context_attention_precise" if precise else "context_attention",
    )(sink, qg32, qw32, kv32)


def _route(scores_t, sel_t):
    s = [scores_t[e:e + 1, :] for e in range(N_EXPERTS)]
    z = [sel_t[e:e + 1, :] for e in range(N_EXPERTS)]
    gs = []
    for g in range(N_GROUPS):
        m = z[g * EXPERTS_PER_GROUP:(g + 1) * EXPERTS_PER_GROUP]
        best = None
        for a in range(EXPERTS_PER_GROUP):
            for b in range(a + 1, EXPERTS_PER_GROUP):
                pair = m[a] + m[b]
                best = pair if best is None else jnp.maximum(best, pair)
        gs.append(best)
    combine, groups = [], []
    for g in range(N_GROUPS):
        chosen_g = None
        for g2 in range(N_GROUPS):
            if g2 == g:
                continue
            c = (gs[g] > gs[g2]) if g2 < g else (gs[g] >= gs[g2])
            chosen_g = c if chosen_g is None else jnp.logical_and(chosen_g, c)
        base = g * EXPERTS_PER_GROUP
        picked = []
        for a in range(EXPERTS_PER_GROUP):
            rank = 0.0
            for b in range(EXPERTS_PER_GROUP):
                if b == a:
                    continue
                ahead = (z[base + b] >= z[base + a]) if b < a else (z[base + b] > z[base + a])
                rank = rank + ahead.astype(F32)
            picked.append(jnp.where(jnp.logical_and(chosen_g, rank < 2.0), s[base + a], 0.0))
        denom = picked[0] + picked[1] + picked[2] + picked[3]
        denom = jnp.where(chosen_g, denom, 1.0)
        combine.extend(pk / denom for pk in picked)
        groups.append(jnp.where(chosen_g, 1.0, 0.0))
    return jnp.concatenate(combine, axis=0), jnp.concatenate(groups, axis=0)


def _outproj_kernel(*refs, lat_tiles, precise_ctx):
    if precise_ctx:
        (ofl_ref, ofc_ref, obl_ref, obc_ref, ga_ref,
         ogl_ref, ogc_ref, owl_ref, owc_ref, xl_ref, xc_ref, mod_ref, ag_ref,
         ones_ref, w_ref, wlo_ref, g2_ref, rw_ref, rb_ref, before_ref,
         x1_ref, h2_ref, comb_ref, pos_ref, cnt_ref, mix_ref) = refs
    else:
        (ofl_ref, ofc_ref, obl_ref, obc_ref, ga_ref,
         ogl_ref, ogc_ref, owl_ref, owc_ref, xl_ref, xc_ref, mod_ref, ag_ref,
         ones_ref, w_ref, g2_ref, rw_ref, rb_ref, before_ref,
         x1_ref, h2_ref, comb_ref, pos_ref, cnt_ref, mix_ref) = refs
        wlo_ref = None
    is_lat = pl.program_id(0) < lat_tiles
    o = jnp.where(is_lat, ofl_ref[...] + obl_ref[...], ofc_ref[...] + obc_ref[...])
    oa = _head_rms(o, ag_ref[...], ones_ref[...]) * _sigmoid(ga_ref[...])
    parts = ((0, A_WIDTH), (A_WIDTH, A_WIDTH + Q_WIDTH), (A_WIDTH + Q_WIDTH, D_MODEL))

    def mix(operands, precise):
        acc = 0.0
        for a, (r0, r1) in zip(operands, parts):
            if precise:
                ah, al = _split2(a)
                acc = acc + _dot(ah, w_ref[r0:r1, :]) + _dot(al, w_ref[r0:r1, :]) + _dot(ah, wlo_ref[r0:r1, :])
            else:
                acc = acc + _dot(a.astype(BF16), w_ref[r0:r1, :])
        mix_ref[...] = acc

    pl.when(is_lat)(lambda: mix((oa, ogl_ref[...], owl_ref[...]), False))
    pl.when(jnp.logical_not(is_lat))(lambda: mix((oa, ogc_ref[...], owc_ref[...]), precise_ctx))

    mod = mod_ref[...]
    gate1 = mod[:, 2 * D_MODEL:3 * D_MODEL]
    sh2 = mod[:, 3 * D_MODEL:4 * D_MODEL]
    sc2 = mod[:, 4 * D_MODEL:5 * D_MODEL]
    x1 = jnp.where(is_lat, xl_ref[...], xc_ref[...]) + gate1 * mix_ref[...]
    x1_ref[...] = x1
    gain = g2_ref[...] * (1.0 + sc2)
    h2 = x1 * lax.rsqrt(jnp.mean(x1 * x1, axis=-1, keepdims=True) + EPS) * gain + sh2
    h2_ref[...] = h2.astype(BF16)
    scores_t = _sigmoid(_mm(rw_ref[...], h2, True, _NT))
    comb_t, group_t = _route(scores_t, scores_t + rb_ref[...])
    comb_ref[...] = comb_t
    rank = _dot(group_t.astype(BF16), before_ref[...])
    counts = group_t.sum(axis=1, keepdims=True)
    pos, offset = 0.0, 0.0
    for g in range(N_GROUPS):
        pos = pos + group_t[g:g + 1] * (rank[g:g + 1] + offset)
        offset = offset + counts[g:g + 1]
    pos_ref[...] = pos
    pad = jnp.zeros((8 - N_GROUPS, 1), F32)
    cnt_ref[...] = jnp.broadcast_to(jnp.concatenate([counts, pad], axis=0), (8, LANES)).astype(jnp.int32)


def _out_projection(o_f, o_b, za, og_lat, og_ctx, ow_lat, ow_ctx, x_pair, mod, an_g, ones_blk, w_hi, w_lo, g2,
                    rw_t, rb, dims):
    n = dims["n_lat"] + dims["n_ctx"]
    tm = TOKEN_TILE
    n_tiles = n // tm
    lat_tiles = dims["n_lat"] // tm
    tiles_per_seq = dims["t_lat"] // tm
    ctx_cond = dims["ctx_cond"]
    precise_ctx = w_lo is not None
    cond_idx = lambda i: jnp.where(i < lat_tiles, i // tiles_per_seq, ctx_cond)
    row = lambda w: pl.BlockSpec((tm, w), lambda i: (i, 0))
    lat = lambda w: pl.BlockSpec((tm, w), lambda i: (jnp.minimum(i, lat_tiles - 1), 0))
    ctx = lambda w: pl.BlockSpec((tm, w), lambda i: (jnp.maximum(i - lat_tiles, 0), 0))
    const = lambda a: pl.BlockSpec(a.shape, lambda i: (0,) * a.ndim, pipeline_mode=pl.Buffered(1))
    weights = [w_hi, w_lo] if precise_ctx else [w_hi]
    t_idx = np.arange(tm)
    before = jnp.asarray((t_idx[:, None] < t_idx[None, :]).astype(np.float32), BF16)
    return pl.pallas_call(
        functools.partial(_outproj_kernel, lat_tiles=lat_tiles, precise_ctx=precise_ctx),
        grid=(n_tiles,),
        in_specs=[
            lat(A_WIDTH), ctx(A_WIDTH), lat(A_WIDTH), ctx(A_WIDTH),
            pl.BlockSpec((tm, A_WIDTH), lambda i: (i, 4)),
            lat(Q_WIDTH), ctx(Q_WIDTH), lat(Q_WIDTH), ctx(Q_WIDTH)] + _token_specs(x_pair, tm, lat_tiles) + [
            pl.BlockSpec((None, 1, 6 * D_MODEL), lambda i: (cond_idx(i), 0, 0)),
            const(an_g), const(ones_blk)] + [const(w) for w in weights] + [
            const(g2), const(rw_t), const(rb), const(before),
        ],
        out_specs=[row(D_MODEL), row(D_MODEL), pl.BlockSpec((N_EXPERTS, tm), lambda i: (0, i)),
                   pl.BlockSpec((1, tm), lambda i: (0, i)), pl.BlockSpec((8, LANES), lambda i: (i, 0))],
        out_shape=[
            jax.ShapeDtypeStruct((n, D_MODEL), F32),
            jax.ShapeDtypeStruct((n, D_MODEL), BF16),
            jax.ShapeDtypeStruct((N_EXPERTS, n), F32),
            jax.ShapeDtypeStruct((1, n), F32),
            jax.ShapeDtypeStruct((n_tiles * 8, LANES), jnp.int32),
        ],
        scratch_shapes=[pltpu.VMEM((tm, D_MODEL), F32)],
        compiler_params=_cparams("arbitrary"),
        name="out_projection_precise_ctx" if precise_ctx else "out_projection",
    )(*o_f, *o_b, za, og_lat, og_ctx, ow_lat, ow_ctx, *x_pair, mod, an_g, ones_blk, *weights, g2, rw_t, rb, before)


def _moe_kernel(off_ref, h_ref, comb_ref, pos_ref, x1_ref, mod_ref, wgu_ref, wd_ref, fg_ref, *rest,
                final, lat_tiles):
    *outs, hs_ref, cs_ref, acc_ref = rest
    i = pl.program_id(0)
    tm = h_ref.shape[0]
    row = lax.broadcasted_iota(jnp.int32, (tm, tm), 0).astype(F32)
    perm = jnp.where(row == pos_ref[...], 1.0, 0.0).astype(BF16)
    hs_ref[...] = _dot(perm, h_ref[...]).astype(BF16)
    cs_ref[...] = sum(lax.dot_general(perm, c, _NT, preferred_element_type=F32)
                      for c in _split3(comb_ref[...]))
    acc_ref[...] = jnp.zeros_like(acc_ref)
    lane = lax.broadcasted_iota(jnp.int32, (MOE_BLOCK, N_EXPERTS), 1)

    for r in range(tm // MOE_BLOCK):
        rows = slice(r * MOE_BLOCK, (r + 1) * MOE_BLOCK)

        def group_body(g, carry, rows=rows, r=r):
            has_tokens = jnp.logical_and(off_ref[i, g] < (r + 1) * MOE_BLOCK, off_ref[i, g + 1] > r * MOE_BLOCK)

            @pl.when(has_tokens)
            def _():
                h = hs_ref[rows, :]
                cs = cs_ref[rows, :]
                y = 0.0
                for a in range(EXPERTS_PER_GROUP):
                    e = g * EXPERTS_PER_GROUP + a
                    gu = _dot(h, wgu_ref[e])
                    gate, up = gu[:, :D_EXPERT], gu[:, D_EXPERT:]
                    ce = jnp.sum(jnp.where(lane == e, cs, 0.0), axis=1, keepdims=True)
                    y = y + _dot((_silu(gate) * up * ce).astype(BF16), wd_ref[e])
                acc_ref[rows, :] += y

            return carry

        lax.fori_loop(0, N_GROUPS, group_body, 0)

    y_hi, y_lo = _split2(acc_ref[...])
    y = (lax.dot_general(perm, y_hi, _TN, preferred_element_type=F32)
         + lax.dot_general(perm, y_lo, _TN, preferred_element_type=F32))
    x2 = x1_ref[...] + mod_ref[...][:, 5 * D_MODEL:] * y
    if not final:
        outs[0][...] = x2
    else:
        yn = x2 * lax.rsqrt(jnp.mean(x2 * x2, axis=-1, keepdims=True) + EPS) * fg_ref[...]
        is_lat = i < lat_tiles

        @pl.when(is_lat)
        def _():
            outs[0][...] = yn

        @pl.when(jnp.logical_not(is_lat))
        def _():
            outs[1][...] = yn


def _moe(h2, comb_t, pos, counts, x1, mod, w_gu, w_d, final_g, dims, final):
    n = x1.shape[0]
    tm = TOKEN_TILE
    lat_tiles = dims["n_lat"] // tm
    tiles_per_seq = dims["t_lat"] // tm
    ctx_cond = dims["ctx_cond"]
    cond_idx = lambda i: jnp.where(i < lat_tiles, i // tiles_per_seq, ctx_cond)
    offsets = jnp.concatenate([jnp.zeros((n // tm, 1), jnp.int32), jnp.cumsum(counts, axis=1)], axis=1)
    if final:
        out_specs = [pl.BlockSpec((tm, D_MODEL), lambda i, *_: (jnp.minimum(i, lat_tiles - 1), 0)),
                     pl.BlockSpec((tm, D_MODEL), lambda i, *_: (jnp.maximum(i - lat_tiles, 0), 0))]
        out_shape = [jax.ShapeDtypeStruct((dims["n_lat"], D_MODEL), F32),
                     jax.ShapeDtypeStruct((dims["n_ctx"], D_MODEL), F32)]
    else:
        out_specs = pl.BlockSpec((tm, D_MODEL), lambda i, *_: (i, 0))
        out_shape = jax.ShapeDtypeStruct((n, D_MODEL), F32)
    resident = lambda a: pl.BlockSpec(a.shape, lambda i, *_: (0,) * a.ndim, pipeline_mode=pl.Buffered(1))
    grid_spec = pltpu.PrefetchScalarGridSpec(
        num_scalar_prefetch=1,
        grid=(n // tm,),
        in_specs=[
            pl.BlockSpec((tm, D_MODEL), lambda i, *_: (i, 0)),
            pl.BlockSpec((N_EXPERTS, tm), lambda i, *_: (0, i)),
            pl.BlockSpec((1, tm), lambda i, *_: (0, i)),
            pl.BlockSpec((tm, D_MODEL), lambda i, *_: (i, 0)),
            pl.BlockSpec((None, 1, 6 * D_MODEL), lambda i, *_: (cond_idx(i), 0, 0)),
            resident(w_gu), resident(w_d),
            pl.BlockSpec((1, D_MODEL), lambda i, *_: (0, 0)),
        ],
        out_specs=out_specs,
        scratch_shapes=[pltpu.VMEM((tm, D_MODEL), BF16), pltpu.VMEM((tm, N_EXPERTS), F32),
                        pltpu.VMEM((tm, D_MODEL), F32)],
    )
    return pl.pallas_call(
        functools.partial(_moe_kernel, final=final, lat_tiles=lat_tiles),
        grid_spec=grid_spec,
        out_shape=out_shape,
        compiler_params=_cparams("arbitrary"),
        name="moe_final" if final else "moe",
    )(offsets, h2, comb_t, pos, x1, mod, w_gu, w_d, final_g)


def _rope_tables(t, tile):
    n_freq = HEAD_DIM // 4
    pos = jnp.arange(t)
    freqs = ROPE_THETA ** (-jnp.arange(n_freq, dtype=F32) / n_freq)
    ang_row = (pos // GRID_W).astype(F32)[:, None] * freqs
    ang_col = (pos % GRID_W).astype(F32)[:, None] * freqs
    cos_h = jnp.concatenate([jnp.cos(ang_row)] * 2 + [jnp.cos(ang_col)] * 2, axis=1)
    sin_h = jnp.concatenate([-jnp.sin(ang_row), jnp.sin(ang_row), -jnp.sin(ang_col), jnp.sin(ang_col)], axis=1)
    cos_t = jnp.concatenate([jnp.ones((tile, LANES), F32), jnp.concatenate([cos_h, cos_h], axis=1)], axis=0)
    sin_t = jnp.concatenate([jnp.zeros((tile, LANES), F32), jnp.concatenate([sin_h, sin_h], axis=1)], axis=0)
    return cos_t, sin_t


def _head_ones(width):
    r = np.arange(width) // HEAD_DIM
    return jnp.asarray((r[:, None] == r[None, :]).astype(np.float32), BF16)


def _split_kernel(w_ref, hi_ref, lo_ref):
    hi, lo = _split2(w_ref[...])
    hi_ref[...] = hi
    lo_ref[...] = lo


def _hi_lo(w, want_lo):
    if not want_lo:
        return w.astype(BF16), None
    rows, width = w.shape
    tr = 256
    spec = pl.BlockSpec((tr, width), lambda i: (i, 0))
    return pl.pallas_call(
        _split_kernel,
        grid=(rows // tr,),
        in_specs=[spec],
        out_specs=[spec, spec],
        out_shape=[jax.ShapeDtypeStruct(w.shape, BF16)] * 2,
        compiler_params=_cparams("parallel"),
        name="split_weight",
    )(w)


def _state_to_block_diag_t(s):
    per_pair = LANES // HEAD_DIM
    st = jnp.swapaxes(s, -1, -2).reshape(s.shape[:2] + (A_HEADS // per_pair, per_pair, HEAD_DIM, HEAD_DIM))
    eye = jnp.eye(per_pair, dtype=s.dtype)
    out = jnp.einsum("bdphvk,hg->bdphvgk", st, eye)
    return out.reshape(s.shape[:2] + (A_HEADS // per_pair, LANES, LANES))


def _block_diag_t_to_state(st):
    per_pair = LANES // HEAD_DIM
    s7 = st.reshape(st.shape[:3] + (per_pair, HEAD_DIM, per_pair, HEAD_DIM))
    diag = jnp.stack([s7[:, :, :, h, :, h, :] for h in range(per_pair)], axis=3)
    return jnp.swapaxes(diag, -1, -2).reshape(st.shape[:2] + (A_HEADS, HEAD_DIM, HEAD_DIM))


def kernel(x_prompt, x_sample, cache_glob_k, cache_glob_v, cache_win_k, cache_win_v, state_hgrn, c, c_ctx,
           mod_w, mod_b, norm1_g, norm2_g, w_in, w_out, hgrn_lb, hgrn_norm_g, q_norm_g, k_norm_g, win_sink,
           router_w, router_b, w_gate_up, w_down, final_g):
    n_ctx_seq, t_ctx, d = x_prompt.shape
    n_lat_seq, t_lat, _ = x_sample.shape
    depth = mod_w.shape[0]
    past = cache_glob_k.shape[2]
    n_lat, n_ctx = n_lat_seq * t_lat, n_ctx_seq * t_ctx
    dims = dict(n_ctx_seq=n_ctx_seq, t_ctx=t_ctx, n_lat_seq=n_lat_seq, t_lat=t_lat,
                n_lat=n_lat, n_ctx=n_ctx, ctx_cond=n_lat_seq)
    assert d == D_MODEL and n_lat_seq < COND_ROWS
    assert n_ctx % TOKEN_TILE == 0 and t_lat % TOKEN_TILE == 0
    assert t_ctx % CHUNK == 0 and t_lat >= 4 * WINDOW

    x_pair = (x_sample.reshape(n_lat, d), x_prompt.reshape(n_ctx, d))
    cond = jnp.concatenate([c, c_ctx[None, :], jnp.zeros((COND_ROWS - n_lat_seq - 1, d), F32)], axis=0)
    mod = _modulation(cond, mod_w, mod_b).reshape(depth, COND_ROWS, 1, 6 * d)

    p_lb = jax.nn.softmax(hgrn_lb.astype(F32), axis=1)
    lbs = jnp.cumsum(p_lb, axis=1) - p_lb[:, :1]

    cos_t, sin_t = _rope_tables(t_lat, TOKEN_TILE)
    ones128, ones256 = _head_ones(LANES), _head_ones(A_WIDTH)
    tile2 = lambda g, reps: jnp.tile(g, reps)[None, :]
    rw_t = router_w.T
    rb = router_b[:, None]
    fg = final_g[None, :]

    new_kv, new_state = [], []
    for l in range(depth):
        precise = l < depth - 1
        win_hi, win_lo = _hi_lo(w_in[l], precise)
        za, qg_att, kvg, qw_att, kvw, kv32, qg32, qw32 = _in_projection(
            x_pair, mod[l], norm1_g[l][None, :], win_hi, win_lo, cos_t, sin_t,
            tile2(q_norm_g[l], 2), tile2(k_norm_g[l], 2), ones128, dims)

        s0 = state_hgrn[:, l].astype(F32)
        of_lat, ob_lat, _ = _hgrn_scan(za, 0, lbs[:, l], _state_to_block_diag_t(s0), n_lat_seq, t_lat,
                                       False, False, "hgrn_latent")
        of_ctx, ob_ctx, s_fin = _hgrn_scan(za, n_lat, lbs[:, l], None, n_ctx_seq, t_ctx, precise, True,
                                           "hgrn_context_precise" if precise else "hgrn_context")

        cast_cache = lambda a: a[:, l].reshape(n_lat_seq, past, KV_WIDTH).astype(BF16)
        og_lat = _global_attention(qg_att, kvg, cast_cache(cache_glob_k), cast_cache(cache_glob_v), dims)
        ow_lat = _window_attention(win_sink[l], qw_att, kvw, cast_cache(cache_win_k),
                                   cast_cache(cache_win_v), dims)
        og_ctx, ow_ctx = _context_attention(win_sink[l], qg32, qw32, kv32, dims, precise)

        wout_hi, wout_lo = _hi_lo(w_out[l], precise)
        x1, h2, comb_t, pos, cnt = _out_projection(
            (of_lat, of_ctx), (ob_lat, ob_ctx), za, og_lat, og_ctx, ow_lat, ow_ctx, x_pair, mod[l],
            tile2(hgrn_norm_g[l], A_HEADS), ones256, wout_hi, wout_lo, norm2_g[l][None, :], rw_t, rb, dims)

        counts = cnt.reshape(-1, 8, LANES)[:, :N_GROUPS, 0]
        x = _moe(h2, comb_t, pos, counts, x1, mod[l], w_gate_up[l].astype(BF16), w_down[l].astype(BF16), fg,
                 dims, final=(l == depth - 1))
        x_pair = (x, x)

        new_kv.append(kv32[:n_ctx].reshape(n_ctx_seq, t_ctx, 4, B_KV_HEADS, HEAD_DIM))
        new_state.append(_block_diag_t_to_state(s_fin))

    y_sample = x[0].reshape(n_lat_seq, t_lat, d)
    y_prompt = x[1].reshape(n_ctx_seq, t_ctx, d)
    kv = jnp.stack(new_kv, axis=1)
    return (y_prompt, y_sample, kv[:, :, :, 0], kv[:, :, :, 1], kv[:, :, :, 2], kv[:, :, :, 3],
            jnp.stack(new_state, axis=1).astype(x_prompt.dtype))
```

```python
import functools

import numpy as np
import jax
import jax.numpy as jnp
from jax import lax
from jax.experimental import pallas as pl
from jax.experimental.pallas import tpu as pltpu

F32 = jnp.float32
BF16 = jnp.bfloat16

D_MODEL = 1024
HEAD_DIM = 64
GRID_W = 64
A_HEADS = 4
A_WIDTH = A_HEADS * HEAD_DIM
B_Q_HEADS = 6
B_KV_HEADS = 2
C_Q_HEADS = 6
C_KV_HEADS = 2
Q_WIDTH = B_Q_HEADS * HEAD_DIM
KV_WIDTH = B_KV_HEADS * HEAD_DIM
GQA_GROUP = B_Q_HEADS // B_KV_HEADS
WINDOW = 128
ROPE_THETA = 10000.0
IN_WIDTH = 5 * A_WIDTH + 2 * (Q_WIDTH + 2 * KV_WIDTH)
N_EXPERTS = 16
N_GROUPS = 4
EXPERTS_PER_GROUP = N_EXPERTS // N_GROUPS
D_EXPERT = 256
EPS = 1e-6
MASK_VALUE = -1e30
TINY = 1e-30
LOG2E = 1.4426950408889634

LANES = 128
SUB = 16
CHUNK = 128
HGRN_SEQS = 2
TOKEN_TILE = 512
MOE_BLOCK = 128
KEY_CHUNK = 512
COND_ROWS = 16
VMEM_LIMIT = 58 * 1024 * 1024

_NT = (((1,), (1,)), ((), ()))
_TN = (((0,), (0,)), ((), ()))


def _cparams(*sem):
    return pltpu.CompilerParams(dimension_semantics=sem, vmem_limit_bytes=VMEM_LIMIT)


def _dot(a, b):
    return jnp.dot(a, b, preferred_element_type=F32)


def _split2(x):
    h = x.astype(BF16)
    return h, (x - h.astype(F32)).astype(BF16)


def _split3(x):
    h = x.astype(BF16)
    r = x - h.astype(F32)
    m = r.astype(BF16)
    l = (r - m.astype(F32)).astype(BF16)
    return h, m, l


def _dot_sel(c, x):
    h, m, l = _split3(x)
    return _dot(c, h) + _dot(c, m) + _dot(c, l)


def _mm(a, b, precise, dims=None):
    if dims is None:
        dims = (((a.ndim - 1,), (0,)), ((), ()))
    dg = lambda x, y: lax.dot_general(x, y, dims, preferred_element_type=F32)
    if not precise:
        return dg(a.astype(BF16), b.astype(BF16))
    ah, al = _split2(a)
    bh, bl = _split2(b)
    return dg(ah, bh) + dg(ah, bl) + dg(al, bh)


def _mm_sel(a, sel, precise):
    if not precise:
        return _dot(a.astype(BF16), sel)
    ah, al = _split2(a)
    return _dot(ah, sel) + _dot(al, sel)


def _sigmoid(x):
    return jax.nn.sigmoid(x)


def _silu(x):
    return x * jax.nn.sigmoid(x)


def _mod_kernel(cond_ref, w_ref, b_ref, o_ref):
    o_ref[...] = _mm(_silu(cond_ref[...]), w_ref[...], True) + b_ref[...]


def _modulation(cond, mod_w, mod_b):
    depth, d, width = mod_w.shape
    tn = 1536
    return pl.pallas_call(
        _mod_kernel,
        grid=(depth, width // tn),
        in_specs=[
            pl.BlockSpec((COND_ROWS, d), lambda l, j: (0, 0)),
            pl.BlockSpec((None, d, tn), lambda l, j: (l, 0, j)),
            pl.BlockSpec((None, 1, tn), lambda l, j: (l, 0, j)),
        ],
        out_specs=pl.BlockSpec((None, COND_ROWS, tn), lambda l, j: (l, 0, j)),
        out_shape=jax.ShapeDtypeStruct((depth, COND_ROWS, width), F32),
        compiler_params=_cparams("parallel", "parallel"),
        name="modulation",
    )(cond, mod_w, mod_b.reshape(depth, 1, width))


def _head_rms(xb, g2, ones_blk):
    ss = _mm_sel(xb * xb, ones_blk, True)
    return xb * lax.rsqrt(ss * (1.0 / HEAD_DIM) + EPS) * g2


def _rope(yb, cos, sin_signed, first16):
    partner = jnp.where(first16, pltpu.roll(yb, LANES - 16, 1), pltpu.roll(yb, 16, 1))
    return yb * cos + partner * sin_signed


def _q_attention_layout(blocks, lane_half):
    out = []
    for j in range(2 * len(blocks)):
        src = blocks[j // 2]
        dst_half = j // GQA_GROUP
        if j % 2 != dst_half:
            src = pltpu.roll(src, HEAD_DIM, 1)
        out.append(jnp.where(lane_half == dst_half, src, 0.0))
    return jnp.concatenate(out, axis=1)


def _inproj_kernel(*refs, precise_ctx, lat_tiles):
    if precise_ctx:
        (xl_ref, xc_ref, mod_ref, g1_ref, w_ref, wlo_ref, cos_ref, sin_ref, qg_ref, kg_ref, ones_ref,
         za_ref, qga_ref, kvg_ref, qwa_ref, kvw_ref, kv32_ref, qg32_ref, qw32_ref, z_ref) = refs
    else:
        (xl_ref, xc_ref, mod_ref, g1_ref, w_ref, cos_ref, sin_ref, qg_ref, kg_ref, ones_ref,
         za_ref, qga_ref, kvg_ref, qwa_ref, kvw_ref, kv32_ref, qg32_ref, qw32_ref) = refs
    is_lat = pl.program_id(0) < lat_tiles
    x = jnp.where(is_lat, xl_ref[...], xc_ref[...])
    mod = mod_ref[...]
    gain = g1_ref[...] * (1.0 + mod[:, D_MODEL:2 * D_MODEL])
    h = x * lax.rsqrt(jnp.mean(x * x, axis=-1, keepdims=True) + EPS) * gain + mod[:, 0:D_MODEL]

    mix_width = Q_WIDTH + 2 * KV_WIDTH
    col_g, col_w = 5 * A_WIDTH, 5 * A_WIDTH + mix_width
    if precise_ctx:
        def one_pass():
            z_ref[...] = _dot(h.astype(BF16), w_ref[...])

        def three_pass():
            hh, hl = _split2(h)
            z_ref[...] = _dot(hh, w_ref[...]) + _dot(hl, w_ref[...]) + _dot(hh, wlo_ref[...])

        pl.when(is_lat)(one_pass)
        pl.when(jnp.logical_not(is_lat))(three_pass)
        z = z_ref[...]
        z_g, z_w = z[:, col_g:col_w], z[:, col_w:]
        za_ref[...] = z[:, :col_g]
    else:
        h_b = h.astype(BF16)
        z_g = _dot(h_b, w_ref[:, col_g:col_w])
        z_w = _dot(h_b, w_ref[:, col_w:])
        za_ref[...] = _dot(h_b, w_ref[:, :col_g])

    cos = cos_ref[...]
    sin = sin_ref[...]
    ones_blk = ones_ref[...]
    lane = lax.broadcasted_iota(jnp.int32, (x.shape[0], LANES), 1)
    first16 = (lane % 32) < 16
    lane_half = lane // HEAD_DIM
    scale = LOG2E * HEAD_DIM ** -0.5

    qg = [z_g[:, LANES * b:LANES * (b + 1)] for b in range(3)]
    kg = z_g[:, Q_WIDTH:Q_WIDTH + KV_WIDTH]
    vg = z_g[:, Q_WIDTH + KV_WIDTH:]
    qw = [z_w[:, LANES * b:LANES * (b + 1)] for b in range(3)]
    kw = z_w[:, Q_WIDTH:Q_WIDTH + KV_WIDTH]
    vw = z_w[:, Q_WIDTH + KV_WIDTH:]

    qg = [_rope(_head_rms(b, qg_ref[...], ones_blk), cos, sin, first16) * scale for b in qg]
    kg = _rope(_head_rms(kg, kg_ref[...], ones_blk), cos, sin, first16)
    qw = [_rope(b, cos, sin, first16) * scale for b in qw]
    kw = _rope(kw, cos, sin, first16)

    qg_att = _q_attention_layout(qg, lane_half)
    qw_att = _q_attention_layout(qw, lane_half)
    qga_ref[...] = qg_att.astype(BF16)
    qwa_ref[...] = qw_att.astype(BF16)
    kvg_ref[...] = jnp.concatenate([kg, vg], axis=1).astype(BF16)
    kvw_ref[...] = jnp.concatenate([kw, vw], axis=1).astype(BF16)
    kv32_ref[...] = jnp.concatenate([kg, vg, kw, vw], axis=1)
    qg32_ref[...] = qg_att
    qw32_ref[...] = qw_att


def _token_specs(x_pair, tm, lat_tiles):
    off = lat_tiles if x_pair[1] is x_pair[0] else 0
    lat = pl.BlockSpec((tm, D_MODEL), lambda i, *_: (jnp.minimum(i, lat_tiles - 1), 0))
    ctx = pl.BlockSpec((tm, D_MODEL), lambda i, *_: (jnp.maximum(i, lat_tiles) - lat_tiles + off, 0))
    return [lat, ctx]


def _in_projection(x_pair, mod, g1, w_hi, w_lo, cos_t, sin_t, qn_g, kn_g, ones_blk, dims):
    n = dims["n_lat"] + dims["n_ctx"]
    tm = TOKEN_TILE
    n_tiles = n // tm
    lat_tiles = dims["n_lat"] // tm
    ctx_tiles = n_tiles - lat_tiles
    tiles_per_seq = dims["t_lat"] // tm
    ctx_cond = dims["ctx_cond"]
    precise_ctx = w_lo is not None

    def cond_idx(i):
        return jnp.where(i < lat_tiles, i // tiles_per_seq, ctx_cond)

    def rope_idx(i):
        return jnp.where(i < lat_tiles, 1 + i % tiles_per_seq, 0)

    def ctx_idx(i):
        return jnp.where(i < lat_tiles, ctx_tiles, i - lat_tiles)

    row = lambda w: pl.BlockSpec((tm, w), lambda i: (i, 0))
    ctx_row = lambda w: pl.BlockSpec((tm, w), lambda i: (ctx_idx(i), 0))
    const = lambda a: pl.BlockSpec(a.shape, lambda i: (0,) * a.ndim, pipeline_mode=pl.Buffered(1))
    weights = [w_hi, w_lo] if precise_ctx else [w_hi]
    ctx_rows = (ctx_tiles + 1) * tm
    return pl.pallas_call(
        functools.partial(_inproj_kernel, precise_ctx=precise_ctx, lat_tiles=lat_tiles),
        grid=(n_tiles,),
        in_specs=_token_specs(x_pair, tm, lat_tiles) + [
            pl.BlockSpec((None, 1, 6 * D_MODEL), lambda i: (cond_idx(i), 0, 0)),
            const(g1)] + [const(w) for w in weights] + [
            pl.BlockSpec((tm, LANES), lambda i: (rope_idx(i), 0)),
            pl.BlockSpec((tm, LANES), lambda i: (rope_idx(i), 0)),
            const(qn_g), const(kn_g), const(ones_blk),
        ],
        out_specs=[
            row(5 * A_WIDTH), row(2 * Q_WIDTH), row(2 * KV_WIDTH), row(2 * Q_WIDTH), row(2 * KV_WIDTH),
            ctx_row(4 * KV_WIDTH), ctx_row(2 * Q_WIDTH), ctx_row(2 * Q_WIDTH),
        ],
        out_shape=[
            jax.ShapeDtypeStruct((n, 5 * A_WIDTH), F32),
            jax.ShapeDtypeStruct((n, 2 * Q_WIDTH), BF16),
            jax.ShapeDtypeStruct((n, 2 * KV_WIDTH), BF16),
            jax.ShapeDtypeStruct((n, 2 * Q_WIDTH), BF16),
            jax.ShapeDtypeStruct((n, 2 * KV_WIDTH), BF16),
            jax.ShapeDtypeStruct((ctx_rows, 4 * KV_WIDTH), F32),
            jax.ShapeDtypeStruct((ctx_rows, 2 * Q_WIDTH), F32),
            jax.ShapeDtypeStruct((ctx_rows, 2 * Q_WIDTH), F32),
        ],
        scratch_shapes=[pltpu.VMEM((tm, IN_WIDTH), F32)] if precise_ctx else [],
        compiler_params=_cparams("arbitrary"),
        name="in_projection_precise_ctx" if precise_ctx else "in_projection",
    )(*x_pair, mod, g1, *weights, cos_t, sin_t, qn_g, kn_g, ones_blk)


def _hgrn_chunk(qa, v, fz, lb, st, consts, reverse, precise):
    cum, e_sel, x_sel, sub_mask, p_mask, v_mask, head_mask = consts
    log2e = LOG2E
    n_sub, half = CHUNK // SUB, SUB // 2
    q = _silu(qa)
    f = lb + (1.0 - lb) * _sigmoid(fz)
    logf2 = jnp.log(jnp.maximum(f, TINY)) * log2e
    kk = (1.0 - lb) * _sigmoid(-fz)

    b = _dot_sel(cum, logf2)
    b3 = b.reshape(n_sub, SUB, A_WIDTH)
    end = 0 if reverse else SUB - 1
    tot = jnp.broadcast_to(b3[:, end:end + 1, :], (n_sub, SUB, A_WIDTH)).reshape(CHUNK, A_WIDTH)
    q_in = q * jnp.exp2(b)
    k_out = kk * jnp.exp2(tot - b)
    d_sub = jnp.exp2(tot)

    c3 = (b - jnp.log(kk) * log2e).reshape(n_sub, SUB, A_WIDTH)
    b4 = b.reshape(n_sub, 2, half, A_WIDTH)
    q4 = q.reshape(n_sub, 2, half, A_WIDTH)
    r_local = lax.broadcasted_iota(jnp.int32, (n_sub, half, A_WIDTH), 1)
    zero_half = jnp.zeros((n_sub, half, A_WIDTH), F32)
    pieces = []
    for s in range(SUB):
        cs = c3[:, s:s + 1, :]
        parts = []
        for hsel in range(2):
            lo, hi = hsel * half, hsel * half + half - 1
            if (lo > s) if reverse else (hi < s):
                parts.append(zero_half)
                continue
            arg = b4[:, hsel] - cs
            if not ((hi <= s) if reverse else (lo >= s)):
                t_local = r_local + lo
                arg = jnp.where((t_local <= s) if reverse else (t_local >= s), arg, MASK_VALUE)
            parts.append(q4[:, hsel] * jnp.exp2(arg))
        piece = jnp.stack(parts, axis=1).reshape(CHUNK, A_WIDTH)
        pieces.append(piece if precise else piece.astype(BF16))
    w_cat = jnp.concatenate(pieces, axis=1)
    s_local = _mm_sel(w_cat, e_sel, precise)
    p = _mm_sel(s_local, x_sel, precise) * p_mask
    v_bd = jnp.concatenate([v] * A_HEADS, axis=0) * v_mask
    o_intra = _mm(p, v_bd, precise)

    v_op = v if precise else v.astype(BF16)
    k_op = k_out if precise else k_out.astype(BF16)
    q_op = q_in if precise else q_in.astype(BF16)
    sub_m = sub_mask
    order = range(n_sub - 1, -1, -1) if reverse else range(n_sub)
    new_st, o_inter = [], []
    for p in range(A_WIDTH // LANES):
        lanes = slice(LANES * p, LANES * (p + 1))
        k_exp = jnp.concatenate([k_op[:, lanes]] * n_sub, axis=1) * sub_m
        u_all = _mm(v_op[:, lanes], k_exp, precise, _TN)
        s = st[p]
        s_before = [None] * n_sub
        for j in order:
            s_before[j] = s
            s = d_sub[SUB * j:SUB * j + 1, lanes] * s + head_mask * u_all[:, LANES * j:LANES * (j + 1)]
        new_st.append(s)
        q_exp = jnp.concatenate([q_op[:, lanes]] * n_sub, axis=1) * sub_m
        o_inter.append(_mm(q_exp, jnp.concatenate(s_before, axis=1), precise, _NT))
    return o_intra + jnp.concatenate(o_inter, axis=1), jnp.stack(new_st, axis=0)


def _hgrn_kernel(*refs, has_s0, want_state, precise):
    refs = list(refs)
    seq_refs = [refs[6 * sq:6 * sq + 6] for sq in range(HGRN_SEQS)]
    del refs[:6 * HGRN_SEQS]
    lb_ref = refs.pop(0)
    s0_ref = refs.pop(0) if has_s0 else None
    cumf_ref, cumb_ref, e_ref, x_ref, sm_ref, pm_ref, vm_ref, hm_ref, of_ref, ob_ref = refs[:10]
    del refs[:10]
    sout_ref = refs.pop(0) if want_state else None
    st_ref, = refs
    c = pl.program_id(1)

    @pl.when(c == 0)
    def _():
        st_ref[...] = s0_ref[...] if has_s0 else jnp.zeros_like(st_ref)

    shared = (e_ref[...], x_ref[...], sm_ref[...], pm_ref[...], vm_ref[...], hm_ref[...])
    lb = lb_ref[...]
    for sq in range(HGRN_SEQS):
        qaf_ref, iaf_ref, fzf_ref, qab_ref, iab_ref, fzb_ref = seq_refs[sq]
        o_f, st_f = _hgrn_chunk(qaf_ref[...], iaf_ref[...], fzf_ref[...], lb[0:1], st_ref[sq, 0],
                                (cumf_ref[...],) + shared, False, precise)
        o_b, st_b = _hgrn_chunk(qab_ref[...], iab_ref[...], fzb_ref[...], lb[1:2], st_ref[sq, 1],
                                (cumb_ref[...],) + shared, True, precise)
        of_ref[sq] = o_f
        ob_ref[sq] = o_b
        st_ref[sq, 0] = st_f
        st_ref[sq, 1] = st_b

    if want_state:
        @pl.when(c == pl.num_programs(1) - 1)
        def _():
            sout_ref[...] = st_ref[...]


def _hgrn_consts():
    r = np.arange(CHUNK)
    same = (r[:, None] // SUB) == (r[None, :] // SUB)
    cum_f = (same & (r[None, :] <= r[:, None])).astype(np.float32)
    cum_b = (same & (r[None, :] >= r[:, None])).astype(np.float32)
    tot = same.astype(np.float32)
    e = np.zeros((SUB, A_HEADS, HEAD_DIM, A_HEADS, SUB), np.float32)
    for s in range(SUB):
        for h in range(A_HEADS):
            e[s, h, :, h, s] = 1.0
    e = e.reshape(SUB * A_WIDTH, A_HEADS * SUB)
    x = np.zeros((A_HEADS, SUB, A_HEADS, CHUNK), np.float32)
    for h in range(A_HEADS):
        for s in range(CHUNK):
            x[h, s % SUB, h, s] = 1.0
    x = x.reshape(A_HEADS * SUB, A_HEADS * CHUNK)
    p_mask = np.tile(tot, (1, A_HEADS))
    hv = np.arange(A_HEADS * CHUNK) // CHUNK
    hk = np.arange(A_WIDTH) // HEAD_DIM
    v_mask = (hv[:, None] == hk[None, :]).astype(np.float32)
    head_mask = (hk[:LANES, None] == hk[None, :LANES]).astype(np.float32)
    sub_mask = np.repeat(r[:, None] // SUB == np.arange(CHUNK // SUB)[None, :], LANES, axis=1).astype(np.float32)
    sel = [jnp.asarray(a, BF16) for a in (cum_f, cum_b, e, x, sub_mask)]
    return sel + [jnp.asarray(a, F32) for a in (p_mask, v_mask, head_mask)]


def _hgrn_scan(za, first_row, lb2, s0_t, n_seq, t, precise, want_state, name):
    nc = t // CHUNK
    groups = n_seq // HGRN_SEQS
    base = first_row // CHUNK
    za3 = za.reshape(za.shape[0] // CHUNK, CHUNK, za.shape[1])
    consts = _hgrn_consts()
    state_shape = (HGRN_SEQS, 2, A_WIDTH // LANES, LANES, LANES)
    const = lambda a: pl.BlockSpec(a.shape, lambda p, c: (0,) * a.ndim)

    def chunk_spec(sq, col, backward):
        def index(p, c):
            return (base + (p * HGRN_SEQS + sq) * nc + (nc - 1 - c if backward else c), 0, col)
        return pl.BlockSpec((None, CHUNK, A_WIDTH), index)

    fwd = lambda col: pl.BlockSpec((None, HGRN_SEQS, CHUNK, A_WIDTH), lambda p, c: (p, 0, c, col))
    bwd = lambda col: pl.BlockSpec((None, HGRN_SEQS, CHUNK, A_WIDTH), lambda p, c: (p, 0, nc - 1 - c, col))
    state_spec = pl.BlockSpec((None,) + state_shape, lambda p, c: (p, 0, 0, 0, 0, 0))
    operands = [za3] * (6 * HGRN_SEQS) + [lb2]
    in_specs = [chunk_spec(sq, col, backward) for sq in range(HGRN_SEQS)
                for col, backward in ((0, False), (1, False), (2, False), (0, True), (1, True), (3, True))]
    in_specs.append(const(lb2))
    if s0_t is not None:
        operands.append(s0_t.reshape((groups,) + state_shape))
        in_specs.append(state_spec)
    o_shape = jax.ShapeDtypeStruct((groups, HGRN_SEQS, t, A_WIDTH), F32)
    out_specs, out_shape = [fwd(0), bwd(0)], [o_shape, o_shape]
    if want_state:
        out_specs.append(state_spec)
        out_shape.append(jax.ShapeDtypeStruct((groups,) + state_shape, F32))
    outs = pl.pallas_call(
        functools.partial(_hgrn_kernel, has_s0=s0_t is not None, want_state=want_state, precise=precise),
        grid=(groups, nc),
        in_specs=in_specs + [const(c) for c in consts],
        out_specs=out_specs,
        out_shape=out_shape,
        scratch_shapes=[pltpu.VMEM(state_shape, F32)],
        compiler_params=_cparams("parallel", "arbitrary"),
        name=name,
    )(*operands, *consts)
    o_f, o_b = (o.reshape(n_seq * t, A_WIDTH) for o in outs[:2])
    s_fin = outs[2].reshape((n_seq,) + state_shape[1:]) if want_state else None
    return o_f, o_b, s_fin


def _attend(q3, segments, sink_row, precise=False):
    scores = []
    for k, _, valid in segments:
        s = _mm(k, q3, precise, _NT)
        if valid is not None:
            s = jnp.where(valid, s, MASK_VALUE)
        scores.append(s)
    m = scores[0].max(axis=0, keepdims=True)
    for s in scores[1:]:
        m = jnp.maximum(m, s.max(axis=0, keepdims=True))
    if sink_row is not None:
        m = jnp.maximum(m, sink_row)
    denom = jnp.exp2(sink_row - m) if sink_row is not None else 0.0
    acc = 0.0
    for s, (_, v, _) in zip(scores, segments):
        p = jnp.exp2(s - m)
        denom = denom + p.sum(axis=0, keepdims=True)
        acc = acc + _mm(v, p, precise, _TN)
    return (acc / denom).T


def _stack_heads(q, kvh):
    base = kvh * GQA_GROUP
    return jnp.concatenate([q[:, LANES * (base + g):LANES * (base + g + 1)] for g in range(GQA_GROUP)], axis=0)


def _merge_heads(res, tq):
    lane = lax.broadcasted_iota(jnp.int32, (tq, LANES), 1)
    heads = []
    for j in range(B_Q_HEADS):
        kvh, g = divmod(j, GQA_GROUP)
        o = res[kvh][g * tq:(g + 1) * tq]
        if kvh != j % 2:
            o = pltpu.roll(o, HEAD_DIM, 1)
        heads.append(o)
    blocks = [jnp.where(lane < HEAD_DIM, heads[2 * b], heads[2 * b + 1]) for b in range(B_Q_HEADS // 2)]
    return jnp.concatenate(blocks, axis=1)


def _sink_row(sink_ref, kvh, tq):
    return jnp.concatenate(
        [jnp.full((1, tq), sink_ref[kvh * GQA_GROUP + g] * LOG2E, F32) for g in range(GQA_GROUP)], axis=1)


def _global_attn_kernel(q_ref, kv_ref, ck_ref, cv_ref, o_ref, *, tq):
    q = q_ref[...]
    kv = kv_ref[...]
    k, v = kv[:, :KV_WIDTH], kv[:, KV_WIDTH:]
    ck, cv = ck_ref[...], cv_ref[...]
    segments = [(k[c:c + KEY_CHUNK], v[c:c + KEY_CHUNK], None) for c in range(0, k.shape[0], KEY_CHUNK)]
    segments.append((ck, cv, None))
    res = [_attend(_stack_heads(q, kvh), segments, None) for kvh in range(B_KV_HEADS)]
    o_ref[...] = _merge_heads(res, tq).astype(BF16)


def _global_attention(q_att, kv, ck, cv, dims):
    t, n_seq = dims["t_lat"], dims["n_lat_seq"]
    tq = 256
    nq = t // tq
    return pl.pallas_call(
        functools.partial(_global_attn_kernel, tq=tq),
        grid=(n_seq, nq),
        in_specs=[
            pl.BlockSpec((tq, 2 * Q_WIDTH), lambda b, i: (b * nq + i, 0)),
            pl.BlockSpec((t, 2 * KV_WIDTH), lambda b, i: (b, 0)),
            pl.BlockSpec((None,) + ck.shape[1:], lambda b, i: (b, 0, 0)),
            pl.BlockSpec((None,) + cv.shape[1:], lambda b, i: (b, 0, 0)),
        ],
        out_specs=pl.BlockSpec((tq, Q_WIDTH), lambda b, i: (b * nq + i, 0)),
        out_shape=jax.ShapeDtypeStruct((n_seq * t, Q_WIDTH), BF16),
        compiler_params=_cparams("parallel", "arbitrary"),
        name="global_attention",
    )(q_att, kv, ck, cv)


def _window_attn_kernel(sink_ref, q_ref, kv_ref, ck_ref, cv_ref, o_ref, *, tq, t):
    j = pl.program_id(1)
    span = tq + 2 * WINDOW
    start = pl.multiple_of(jnp.clip(j * tq - WINDOW, 0, t - span), WINDOW)
    q = q_ref[...]
    kv = kv_ref[pl.ds(start, span), :]
    k, v = kv[:, :KV_WIDTH], kv[:, KV_WIDTH:]
    key_pos = start + lax.broadcasted_iota(jnp.int32, (span, GQA_GROUP * tq), 0)
    query_pos = j * tq + lax.broadcasted_iota(jnp.int32, (span, GQA_GROUP * tq), 1) % tq
    valid = jnp.abs(query_pos - key_pos) <= WINDOW
    segments = [(k, v, valid), (ck_ref[...], cv_ref[...], None)]
    res = [_attend(_stack_heads(q, kvh), segments, _sink_row(sink_ref, kvh, tq)) for kvh in range(C_KV_HEADS)]
    o_ref[...] = _merge_heads(res, tq).astype(BF16)


def _window_attention(sink, q_att, kv, ck, cv, dims):
    t, n_seq = dims["t_lat"], dims["n_lat_seq"]
    tq = 2 * WINDOW
    nq = t // tq
    return pl.pallas_call(
        functools.partial(_window_attn_kernel, tq=tq, t=t),
        grid=(n_seq, nq),
        in_specs=[
            pl.BlockSpec(memory_space=pltpu.SMEM),
            pl.BlockSpec((tq, 2 * Q_WIDTH), lambda b, i: (b * nq + i, 0)),
            pl.BlockSpec((t, 2 * KV_WIDTH), lambda b, i: (b, 0)),
            pl.BlockSpec((None,) + ck.shape[1:], lambda b, i: (b, 0, 0)),
            pl.BlockSpec((None,) + cv.shape[1:], lambda b, i: (b, 0, 0)),
        ],
        out_specs=pl.BlockSpec((tq, Q_WIDTH), lambda b, i: (b * nq + i, 0)),
        out_shape=jax.ShapeDtypeStruct((n_seq * t, Q_WIDTH), BF16),
        compiler_params=_cparams("parallel", "arbitrary"),
        name="window_attention",
    )(sink, q_att, kv, ck, cv)


def _context_attn_kernel(sink_ref, qg_ref, qw_ref, kv_ref, og_ref, ow_ref, *, t, precise):
    kv = kv_ref[...]
    for q_ref, o_ref, off, use_sink in ((qg_ref, og_ref, 0, False), (qw_ref, ow_ref, 2 * KV_WIDTH, True)):
        q = q_ref[...]
        k, v = kv[:, off:off + KV_WIDTH], kv[:, off + KV_WIDTH:off + 2 * KV_WIDTH]
        res = [_attend(_stack_heads(q, kvh), [(k, v, None)],
                       _sink_row(sink_ref, kvh, t) if use_sink else None, precise)
               for kvh in range(B_KV_HEADS)]
        o_ref[...] = _merge_heads(res, t)


def _context_attention(sink, qg32, qw32, kv32, dims, precise):
    t, n_seq = dims["t_ctx"], dims["n_ctx_seq"]
    seq = lambda w: pl.BlockSpec((t, w), lambda b: (b, 0))
    return pl.pallas_call(
        functools.partial(_context_attn_kernel, t=t, precise=precise),
        grid=(n_seq,),
        in_specs=[pl.BlockSpec(memory_space=pltpu.SMEM), seq(2 * Q_WIDTH), seq(2 * Q_WIDTH), seq(4 * KV_WIDTH)],
        out_specs=[seq(Q_WIDTH), seq(Q_WIDTH)],
        out_shape=[jax.ShapeDtypeStruct((n_seq * t, Q_WIDTH), F32)] * 2,
        compiler_params=_cparams("parallel"),
        name="context_attention_precise" if precise else "context_attention",
    )(sink, qg32, qw32, kv32)


def _route(scores_t, sel_t):
    s = [scores_t[e:e + 1, :] for e in range(N_EXPERTS)]
    z = [sel_t[e:e + 1, :] for e in range(N_EXPERTS)]
    gs = []
    for g in range(N_GROUPS):
        m = z[g * EXPERTS_PER_GROUP:(g + 1) * EXPERTS_PER_GROUP]
        best = None
        for a in range(EXPERTS_PER_GROUP):
            for b in range(a + 1, EXPERTS_PER_GROUP):
                pair = m[a] + m[b]
                best = pair if best is None else jnp.maximum(best, pair)
        gs.append(best)
    combine, groups = [], []
    for g in range(N_GROUPS):
        chosen_g = None
        for g2 in range(N_GROUPS):
            if g2 == g:
                continue
            c = (gs[g] > gs[g2]) if g2 < g else (gs[g] >= gs[g2])
            chosen_g = c if chosen_g is None else jnp.logical_and(chosen_g, c)
        base = g * EXPERTS_PER_GROUP
        picked = []
        for a in range(EXPERTS_PER_GROUP):
            rank = 0.0
            for b in range(EXPERTS_PER_GROUP):
                if b == a:
                    continue
                ahead = (z[base + b] >= z[base + a]) if b < a else (z[base + b] > z[base + a])
                rank = rank + ahead.astype(F32)
            picked.append(jnp.where(jnp.logical_and(chosen_g, rank < 2.0), s[base + a], 0.0))
        denom = picked[0] + picked[1] + picked[2] + picked[3]
        denom = jnp.where(chosen_g, denom, 1.0)
        combine.extend(pk / denom for pk in picked)
        groups.append(jnp.where(chosen_g, 1.0, 0.0))
    return jnp.concatenate(combine, axis=0), jnp.concatenate(groups, axis=0)


def _outproj_kernel(*refs, lat_tiles, precise_ctx):
    if precise_ctx:
        (ofl_ref, ofc_ref, obl_ref, obc_ref, ga_ref,
         ogl_ref, ogc_ref, owl_ref, owc_ref, xl_ref, xc_ref, mod_ref, ag_ref,
         ones_ref, w_ref, wlo_ref, g2_ref, rw_ref, rb_ref, before_ref,
         x1_ref, h2_ref, comb_ref, pos_ref, cnt_ref, mix_ref) = refs
    else:
        (ofl_ref, ofc_ref, obl_ref, obc_ref, ga_ref,
         ogl_ref, ogc_ref, owl_ref, owc_ref, xl_ref, xc_ref, mod_ref, ag_ref,
         ones_ref, w_ref, g2_ref, rw_ref, rb_ref, before_ref,
         x1_ref, h2_ref, comb_ref, pos_ref, cnt_ref, mix_ref) = refs
        wlo_ref = None
    is_lat = pl.program_id(0) < lat_tiles
    o = jnp.where(is_lat, ofl_ref[...] + obl_ref[...], ofc_ref[...] + obc_ref[...])
    oa = _head_rms(o, ag_ref[...], ones_ref[...]) * _sigmoid(ga_ref[...])
    parts = ((0, A_WIDTH), (A_WIDTH, A_WIDTH + Q_WIDTH), (A_WIDTH + Q_WIDTH, D_MODEL))

    def mix(operands, precise):
        acc = 0.0
        for a, (r0, r1) in zip(operands, parts):
            if precise:
                ah, al = _split2(a)
                acc = acc + _dot(ah, w_ref[r0:r1, :]) + _dot(al, w_ref[r0:r1, :]) + _dot(ah, wlo_ref[r0:r1, :])
            else:
                acc = acc + _dot(a.astype(BF16), w_ref[r0:r1, :])
        mix_ref[...] = acc

    pl.when(is_lat)(lambda: mix((oa, ogl_ref[...], owl_ref[...]), False))
    pl.when(jnp.logical_not(is_lat))(lambda: mix((oa, ogc_ref[...], owc_ref[...]), precise_ctx))

    mod = mod_ref[...]
    gate1 = mod[:, 2 * D_MODEL:3 * D_MODEL]
    sh2 = mod[:, 3 * D_MODEL:4 * D_MODEL]
    sc2 = mod[:, 4 * D_MODEL:5 * D_MODEL]
    x1 = jnp.where(is_lat, xl_ref[...], xc_ref[...]) + gate1 * mix_ref[...]
    x1_ref[...] = x1
    gain = g2_ref[...] * (1.0 + sc2)
    h2 = x1 * lax.rsqrt(jnp.mean(x1 * x1, axis=-1, keepdims=True) + EPS) * gain + sh2
    h2_ref[...] = h2.astype(BF16)
    scores_t = _sigmoid(_mm(rw_ref[...], h2, True, _NT))
    comb_t, group_t = _route(scores_t, scores_t + rb_ref[...])
    comb_ref[...] = comb_t
    rank = _dot(group_t.astype(BF16), before_ref[...])
    counts = group_t.sum(axis=1, keepdims=True)
    pos, offset = 0.0, 0.0
    for g in range(N_GROUPS):
        pos = pos + group_t[g:g + 1] * (rank[g:g + 1] + offset)
        offset = offset + counts[g:g + 1]
    pos_ref[...] = pos
    pad = jnp.zeros((8 - N_GROUPS, 1), F32)
    cnt_ref[...] = jnp.broadcast_to(jnp.concatenate([counts, pad], axis=0), (8, LANES)).astype(jnp.int32)


def _out_projection(o_f, o_b, za, og_lat, og_ctx, ow_lat, ow_ctx, x_pair, mod, an_g, ones_blk, w_hi, w_lo, g2,
                    rw_t, rb, dims):
    n = dims["n_lat"] + dims["n_ctx"]
    tm = TOKEN_TILE
    n_tiles = n // tm
    lat_tiles = dims["n_lat"] // tm
    tiles_per_seq = dims["t_lat"] // tm
    ctx_cond = dims["ctx_cond"]
    precise_ctx = w_lo is not None
    cond_idx = lambda i: jnp.where(i < lat_tiles, i // tiles_per_seq, ctx_cond)
    row = lambda w: pl.BlockSpec((tm, w), lambda i: (i, 0))
    lat = lambda w: pl.BlockSpec((tm, w), lambda i: (jnp.minimum(i, lat_tiles - 1), 0))
    ctx = lambda w: pl.BlockSpec((tm, w), lambda i: (jnp.maximum(i - lat_tiles, 0), 0))
    const = lambda a: pl.BlockSpec(a.shape, lambda i: (0,) * a.ndim, pipeline_mode=pl.Buffered(1))
    weights = [w_hi, w_lo] if precise_ctx else [w_hi]
    t_idx = np.arange(tm)
    before = jnp.asarray((t_idx[:, None] < t_idx[None, :]).astype(np.float32), BF16)
    return pl.pallas_call(
        functools.partial(_outproj_kernel, lat_tiles=lat_tiles, precise_ctx=precise_ctx),
        grid=(n_tiles,),
        in_specs=[
            lat(A_WIDTH), ctx(A_WIDTH), lat(A_WIDTH), ctx(A_WIDTH),
            pl.BlockSpec((tm, A_WIDTH), lambda i: (i, 4)),
            lat(Q_WIDTH), ctx(Q_WIDTH), lat(Q_WIDTH), ctx(Q_WIDTH)] + _token_specs(x_pair, tm, lat_tiles) + [
            pl.BlockSpec((None, 1, 6 * D_MODEL), lambda i: (cond_idx(i), 0, 0)),
            const(an_g), const(ones_blk)] + [const(w) for w in weights] + [
            const(g2), const(rw_t), const(rb), const(before),
        ],
        out_specs=[row(D_MODEL), row(D_MODEL), pl.BlockSpec((N_EXPERTS, tm), lambda i: (0, i)),
                   pl.BlockSpec((1, tm), lambda i: (0, i)), pl.BlockSpec((8, LANES), lambda i: (i, 0))],
        out_shape=[
            jax.ShapeDtypeStruct((n, D_MODEL), F32),
            jax.ShapeDtypeStruct((n, D_MODEL), BF16),
            jax.ShapeDtypeStruct((N_EXPERTS, n), F32),
            jax.ShapeDtypeStruct((1, n), F32),
            jax.ShapeDtypeStruct((n_tiles * 8, LANES), jnp.int32),
        ],
        scratch_shapes=[pltpu.VMEM((tm, D_MODEL), F32)],
        compiler_params=_cparams("arbitrary"),
        name="out_projection_precise_ctx" if precise_ctx else "out_projection",
    )(*o_f, *o_b, za, og_lat, og_ctx, ow_lat, ow_ctx, *x_pair, mod, an_g, ones_blk, *weights, g2, rw_t, rb, before)


def _moe_kernel(off_ref, h_ref, comb_ref, pos_ref, x1_ref, mod_ref, wgu_ref, wd_ref, fg_ref, *rest,
                final, lat_tiles):
    *outs, hs_ref, cs_ref, acc_ref = rest
    i = pl.program_id(0)
    tm = h_ref.shape[0]
    row = lax.broadcasted_iota(jnp.int32, (tm, tm), 0).astype(F32)
    perm = jnp.where(row == pos_ref[...], 1.0, 0.0).astype(BF16)
    hs_ref[...] = _dot(perm, h_ref[...]).astype(BF16)
    cs_ref[...] = sum(lax.dot_general(perm, c, _NT, preferred_element_type=F32)
                      for c in _split3(comb_ref[...]))
    acc_ref[...] = jnp.zeros_like(acc_ref)
    lane = lax.broadcasted_iota(jnp.int32, (MOE_BLOCK, N_EXPERTS), 1)

    for r in range(tm // MOE_BLOCK):
        rows = slice(r * MOE_BLOCK, (r + 1) * MOE_BLOCK)

        def group_body(g, carry, rows=rows, r=r):
            has_tokens = jnp.logical_and(off_ref[i, g] < (r + 1) * MOE_BLOCK, off_ref[i, g + 1] > r * MOE_BLOCK)

            @pl.when(has_tokens)
            def _():
                h = hs_ref[rows, :]
                cs = cs_ref[rows, :]
                y = 0.0
                for a in range(EXPERTS_PER_GROUP):
                    e = g * EXPERTS_PER_GROUP + a
                    gu = _dot(h, wgu_ref[e])
                    gate, up = gu[:, :D_EXPERT], gu[:, D_EXPERT:]
                    ce = jnp.sum(jnp.where(lane == e, cs, 0.0), axis=1, keepdims=True)
                    y = y + _dot((_silu(gate) * up * ce).astype(BF16), wd_ref[e])
                acc_ref[rows, :] += y

            return carry

        lax.fori_loop(0, N_GROUPS, group_body, 0)

    y_hi, y_lo = _split2(acc_ref[...])
    y = (lax.dot_general(perm, y_hi, _TN, preferred_element_type=F32)
         + lax.dot_general(perm, y_lo, _TN, preferred_element_type=F32))
    x2 = x1_ref[...] + mod_ref[...][:, 5 * D_MODEL:] * y
    if not final:
        outs[0][...] = x2
    else:
        yn = x2 * lax.rsqrt(jnp.mean(x2 * x2, axis=-1, keepdims=True) + EPS) * fg_ref[...]
        is_lat = i < lat_tiles

        @pl.when(is_lat)
        def _():
            outs[0][...] = yn

        @pl.when(jnp.logical_not(is_lat))
        def _():
            outs[1][...] = yn


def _moe(h2, comb_t, pos, counts, x1, mod, w_gu, w_d, final_g, dims, final):
    n = x1.shape[0]
    tm = TOKEN_TILE
    lat_tiles = dims["n_lat"] // tm
    tiles_per_seq = dims["t_lat"] // tm
    ctx_cond = dims["ctx_cond"]
    cond_idx = lambda i: jnp.where(i < lat_tiles, i // tiles_per_seq, ctx_cond)
    offsets = jnp.concatenate([jnp.zeros((n // tm, 1), jnp.int32), jnp.cumsum(counts, axis=1)], axis=1)
    if final:
        out_specs = [pl.BlockSpec((tm, D_MODEL), lambda i, *_: (jnp.minimum(i, lat_tiles - 1), 0)),
                     pl.BlockSpec((tm, D_MODEL), lambda i, *_: (jnp.maximum(i - lat_tiles, 0), 0))]
        out_shape = [jax.ShapeDtypeStruct((dims["n_lat"], D_MODEL), F32),
                     jax.ShapeDtypeStruct((dims["n_ctx"], D_MODEL), F32)]
    else:
        out_specs = pl.BlockSpec((tm, D_MODEL), lambda i, *_: (i, 0))
        out_shape = jax.ShapeDtypeStruct((n, D_MODEL), F32)
    resident = lambda a: pl.BlockSpec(a.shape, lambda i, *_: (0,) * a.ndim, pipeline_mode=pl.Buffered(1))
    grid_spec = pltpu.PrefetchScalarGridSpec(
        num_scalar_prefetch=1,
        grid=(n // tm,),
        in_specs=[
            pl.BlockSpec((tm, D_MODEL), lambda i, *_: (i, 0)),
            pl.BlockSpec((N_EXPERTS, tm), lambda i, *_: (0, i)),
            pl.BlockSpec((1, tm), lambda i, *_: (0, i)),
            pl.BlockSpec((tm, D_MODEL), lambda i, *_: (i, 0)),
            pl.BlockSpec((None, 1, 6 * D_MODEL), lambda i, *_: (cond_idx(i), 0, 0)),
            resident(w_gu), resident(w_d),
            pl.BlockSpec((1, D_MODEL), lambda i, *_: (0, 0)),
        ],
        out_specs=out_specs,
        scratch_shapes=[pltpu.VMEM((tm, D_MODEL), BF16), pltpu.VMEM((tm, N_EXPERTS), F32),
                        pltpu.VMEM((tm, D_MODEL), F32)],
    )
    return pl.pallas_call(
        functools.partial(_moe_kernel, final=final, lat_tiles=lat_tiles),
        grid_spec=grid_spec,
        out_shape=out_shape,
        compiler_params=_cparams("arbitrary"),
        name="moe_final" if final else "moe",
    )(offsets, h2, comb_t, pos, x1, mod, w_gu, w_d, final_g)


def _rope_tables(t, tile):
    n_freq = HEAD_DIM // 4
    pos = np.arange(t)
    freqs = ROPE_THETA ** (-np.arange(n_freq, dtype=np.float64) / n_freq)
    ang_row = (pos // GRID_W)[:, None] * freqs
    ang_col = (pos % GRID_W)[:, None] * freqs
    cos_h = np.concatenate([np.cos(ang_row)] * 2 + [np.cos(ang_col)] * 2, axis=1)
    sin_h = np.concatenate([-np.sin(ang_row), np.sin(ang_row), -np.sin(ang_col), np.sin(ang_col)], axis=1)
    cos_t = np.concatenate([np.ones((tile, LANES)), np.concatenate([cos_h, cos_h], axis=1)], axis=0)
    sin_t = np.concatenate([np.zeros((tile, LANES)), np.concatenate([sin_h, sin_h], axis=1)], axis=0)
    return jnp.asarray(cos_t, F32), jnp.asarray(sin_t, F32)


def _head_ones(width):
    r = np.arange(width) // HEAD_DIM
    return jnp.asarray((r[:, None] == r[None, :]).astype(np.float32), BF16)


def _split_kernel(w_ref, hi_ref, lo_ref):
    hi, lo = _split2(w_ref[...])
    hi_ref[...] = hi
    lo_ref[...] = lo


def _hi_lo(w, want_lo):
    if not want_lo:
        return w.astype(BF16), None
    rows, width = w.shape
    tr = 256
    spec = pl.BlockSpec((tr, width), lambda i: (i, 0))
    return pl.pallas_call(
        _split_kernel,
        grid=(rows // tr,),
        in_specs=[spec],
        out_specs=[spec, spec],
        out_shape=[jax.ShapeDtypeStruct(w.shape, BF16)] * 2,
        compiler_params=_cparams("parallel"),
        name="split_weight",
    )(w)


def _state_to_block_diag_t(s):
    per_pair = LANES // HEAD_DIM
    st = jnp.swapaxes(s, -1, -2).reshape(s.shape[:2] + (A_HEADS // per_pair, per_pair, HEAD_DIM, HEAD_DIM))
    eye = jnp.eye(per_pair, dtype=s.dtype)
    out = jnp.einsum("bdphvk,hg->bdphvgk", st, eye)
    return out.reshape(s.shape[:2] + (A_HEADS // per_pair, LANES, LANES))


def _block_diag_t_to_state(st):
    per_pair = LANES // HEAD_DIM
    s7 = st.reshape(st.shape[:3] + (per_pair, HEAD_DIM, per_pair, HEAD_DIM))
    diag = jnp.stack([s7[:, :, :, h, :, h, :] for h in range(per_pair)], axis=3)
    return jnp.swapaxes(diag, -1, -2).reshape(st.shape[:2] + (A_HEADS, HEAD_DIM, HEAD_DIM))


def kernel(x_prompt, x_sample, cache_glob_k, cache_glob_v, cache_win_k, cache_win_v, state_hgrn, c, c_ctx,
           mod_w, mod_b, norm1_g, norm2_g, w_in, w_out, hgrn_lb, hgrn_norm_g, q_norm_g, k_norm_g, win_sink,
           router_w, router_b, w_gate_up, w_down, final_g):
    n_ctx_seq, t_ctx, d = x_prompt.shape
    n_lat_seq, t_lat, _ = x_sample.shape
    depth = mod_w.shape[0]
    past = cache_glob_k.shape[2]
    n_lat, n_ctx = n_lat_seq * t_lat, n_ctx_seq * t_ctx
    dims = dict(n_ctx_seq=n_ctx_seq, t_ctx=t_ctx, n_lat_seq=n_lat_seq, t_lat=t_lat,
                n_lat=n_lat, n_ctx=n_ctx, ctx_cond=n_lat_seq)
    assert d == D_MODEL and n_lat_seq < COND_ROWS
    assert n_ctx % TOKEN_TILE == 0 and t_lat % TOKEN_TILE == 0
    assert t_ctx % CHUNK == 0 and t_lat >= 4 * WINDOW

    x_pair = (x_sample.reshape(n_lat, d), x_prompt.reshape(n_ctx, d))
    cond = jnp.concatenate([c, c_ctx[None, :], jnp.zeros((COND_ROWS - n_lat_seq - 1, d), F32)], axis=0)
    mod = _modulation(cond, mod_w, mod_b).reshape(depth, COND_ROWS, 1, 6 * d)

    p_lb = jax.nn.softmax(hgrn_lb.astype(F32), axis=1)
    lbs = jnp.cumsum(p_lb, axis=1) - p_lb[:, :1]

    cos_t, sin_t = _rope_tables(t_lat, TOKEN_TILE)
    ones128, ones256 = _head_ones(LANES), _head_ones(A_WIDTH)
    tile2 = lambda g, reps: jnp.tile(g, reps)[None, :]
    rw_t = router_w.T
    rb = router_b[:, None]
    fg = final_g[None, :]

    new_kv, new_state = [], []
    for l in range(depth):
        precise = l < depth - 1
        win_hi, win_lo = _hi_lo(w_in[l], precise)
        za, qg_att, kvg, qw_att, kvw, kv32, qg32, qw32 = _in_projection(
            x_pair, mod[l], norm1_g[l][None, :], win_hi, win_lo, cos_t, sin_t,
            tile2(q_norm_g[l], 2), tile2(k_norm_g[l], 2), ones128, dims)

        s0 = state_hgrn[:, l].astype(F32)
        of_lat, ob_lat, _ = _hgrn_scan(za, 0, lbs[:, l], _state_to_block_diag_t(s0), n_lat_seq, t_lat,
                                       False, False, "hgrn_latent")
        of_ctx, ob_ctx, s_fin = _hgrn_scan(za, n_lat, lbs[:, l], None, n_ctx_seq, t_ctx, precise, True,
                                           "hgrn_context_precise" if precise else "hgrn_context")

        cast_cache = lambda a: a[:, l].reshape(n_lat_seq, past, KV_WIDTH).astype(BF16)
        og_lat = _global_attention(qg_att, kvg, cast_cache(cache_glob_k), cast_cache(cache_glob_v), dims)
        ow_lat = _window_attention(win_sink[l], qw_att, kvw, cast_cache(cache_win_k),
                                   cast_cache(cache_win_v), dims)
        og_ctx, ow_ctx = _context_attention(win_sink[l], qg32, qw32, kv32, dims, precise)

        wout_hi, wout_lo = _hi_lo(w_out[l], precise)
        x1, h2, comb_t, pos, cnt = _out_projection(
            (of_lat, of_ctx), (ob_lat, ob_ctx), za, og_lat, og_ctx, ow_lat, ow_ctx, x_pair, mod[l],
            tile2(hgrn_norm_g[l], A_HEADS), ones256, wout_hi, wout_lo, norm2_g[l][None, :], rw_t, rb, dims)

        counts = cnt.reshape(-1, 8, LANES)[:, :N_GROUPS, 0]
        x = _moe(h2, comb_t, pos, counts, x1, mod[l], w_gate_up[l].astype(BF16), w_down[l].astype(BF16), fg,
                 dims, final=(l == depth - 1))
        x_pair = (x, x)

        new_kv.append(kv32[:n_ctx].reshape(n_ctx_seq, t_ctx, 4, B_KV_HEADS, HEAD_DIM))
        new_state.append(_block_diag_t_to_state(s_fin))

    y_sample = x[0].reshape(n_lat_seq, t_lat, d)
    y_prompt = x[1].reshape(n_ctx_seq, t_ctx, d)
    kv = jnp.stack(new_kv, axis=1)
    return (y_prompt, y_sample, kv[:, :, :, 0], kv[:, :, :, 1], kv[:, :, :, 2], kv[:, :, :, 3],
            jnp.stack(new_state, axis=1).astype(x_prompt.dtype))
```

```python
import functools

import numpy as np
import jax
import jax.numpy as jnp
from jax import lax
from jax.experimental import pallas as pl
from jax.experimental.pallas import tpu as pltpu

F32 = jnp.float32
BF16 = jnp.bfloat16

D_MODEL = 1024
HEAD_DIM = 64
GRID_W = 64
A_HEADS = 4
A_WIDTH = A_HEADS * HEAD_DIM
B_Q_HEADS = 6
B_KV_HEADS = 2
C_Q_HEADS = 6
C_KV_HEADS = 2
Q_WIDTH = B_Q_HEADS * HEAD_DIM
KV_WIDTH = B_KV_HEADS * HEAD_DIM
GQA_GROUP = B_Q_HEADS // B_KV_HEADS
WINDOW = 128
ROPE_THETA = 10000.0
IN_WIDTH = 5 * A_WIDTH + 2 * (Q_WIDTH + 2 * KV_WIDTH)
N_EXPERTS = 16
N_GROUPS = 4
EXPERTS_PER_GROUP = N_EXPERTS // N_GROUPS
D_EXPERT = 256
EPS = 1e-6
MASK_VALUE = -1e30
TINY = 1e-30
LOG2E = 1.4426950408889634

LANES = 128
SUB = 16
CHUNK = 128
HGRN_SEQS = 2
TOKEN_TILE = 512
MOE_BLOCK = 128
KEY_CHUNK = 512
COND_ROWS = 16
VMEM_LIMIT = 58 * 1024 * 1024

_NT = (((1,), (1,)), ((), ()))
_TN = (((0,), (0,)), ((), ()))


def _cparams(*sem):
    return pltpu.CompilerParams(dimension_semantics=sem, vmem_limit_bytes=VMEM_LIMIT)


def _dot(a, b):
    return jnp.dot(a, b, preferred_element_type=F32)


def _split2(x):
    h = x.astype(BF16)
    return h, (x - h.astype(F32)).astype(BF16)


def _split3(x):
    h = x.astype(BF16)
    r = x - h.astype(F32)
    m = r.astype(BF16)
    l = (r - m.astype(F32)).astype(BF16)
    return h, m, l


def _dot_sel(c, x):
    h, m, l = _split3(x)
    return _dot(c, h) + _dot(c, m) + _dot(c, l)


def _mm(a, b, precise, dims=None):
    if dims is None:
        dims = (((a.ndim - 1,), (0,)), ((), ()))
    dg = lambda x, y: lax.dot_general(x, y, dims, preferred_element_type=F32)
    if not precise:
        return dg(a.astype(BF16), b.astype(BF16))
    ah, al = _split2(a)
    bh, bl = _split2(b)
    return dg(ah, bh) + dg(ah, bl) + dg(al, bh)


def _mm_sel(a, sel, precise):
    if not precise:
        return _dot(a.astype(BF16), sel)
    ah, al = _split2(a)
    return _dot(ah, sel) + _dot(al, sel)


def _sigmoid(x):
    return jax.nn.sigmoid(x)


def _silu(x):
    return x * jax.nn.sigmoid(x)


def _mod_kernel(cond_ref, w_ref, b_ref, o_ref):
    o_ref[...] = _mm(_silu(cond_ref[...]), w_ref[...], True) + b_ref[...]


def _modulation(cond, mod_w, mod_b):
    depth, d, width = mod_w.shape
    tn = 1536
    return pl.pallas_call(
        _mod_kernel,
        grid=(depth, width // tn),
        in_specs=[
            pl.BlockSpec((COND_ROWS, d), lambda l, j: (0, 0)),
            pl.BlockSpec((None, d, tn), lambda l, j: (l, 0, j)),
            pl.BlockSpec((None, 1, tn), lambda l, j: (l, 0, j)),
        ],
        out_specs=pl.BlockSpec((None, COND_ROWS, tn), lambda l, j: (l, 0, j)),
        out_shape=jax.ShapeDtypeStruct((depth, COND_ROWS, width), F32),
        compiler_params=_cparams("parallel", "parallel"),
        name="modulation",
    )(cond, mod_w, mod_b.reshape(depth, 1, width))


def _head_rms(xb, g2, ones_blk):
    ss = _mm_sel(xb * xb, ones_blk, True)
    return xb * lax.rsqrt(ss * (1.0 / HEAD_DIM) + EPS) * g2


def _rope(yb, cos, sin_signed, first16):
    partner = jnp.where(first16, pltpu.roll(yb, LANES - 16, 1), pltpu.roll(yb, 16, 1))
    return yb * cos + partner * sin_signed


def _q_attention_layout(blocks, lane_half):
    out = []
    for j in range(2 * len(blocks)):
        src = blocks[j // 2]
        dst_half = j // GQA_GROUP
        if j % 2 != dst_half:
            src = pltpu.roll(src, HEAD_DIM, 1)
        out.append(jnp.where(lane_half == dst_half, src, 0.0))
    return jnp.concatenate(out, axis=1)


def _inproj_kernel(*refs, precise_ctx, lat_tiles):
    if precise_ctx:
        (xl_ref, xc_ref, mod_ref, g1_ref, w_ref, wlo_ref, cos_ref, sin_ref, qg_ref, kg_ref, ones_ref,
         za_ref, qga_ref, kvg_ref, qwa_ref, kvw_ref, kv32_ref, qg32_ref, qw32_ref, z_ref) = refs
    else:
        (xl_ref, xc_ref, mod_ref, g1_ref, w_ref, cos_ref, sin_ref, qg_ref, kg_ref, ones_ref,
         za_ref, qga_ref, kvg_ref, qwa_ref, kvw_ref, kv32_ref, qg32_ref, qw32_ref) = refs
    is_lat = pl.program_id(0) < lat_tiles
    x = jnp.where(is_lat, xl_ref[...], xc_ref[...])
    mod = mod_ref[...]
    gain = g1_ref[...] * (1.0 + mod[:, D_MODEL:2 * D_MODEL])
    h = x * lax.rsqrt(jnp.mean(x * x, axis=-1, keepdims=True) + EPS) * gain + mod[:, 0:D_MODEL]

    mix_width = Q_WIDTH + 2 * KV_WIDTH
    col_g, col_w = 5 * A_WIDTH, 5 * A_WIDTH + mix_width
    if precise_ctx:
        def one_pass():
            z_ref[...] = _dot(h.astype(BF16), w_ref[...])

        def three_pass():
            hh, hl = _split2(h)
            z_ref[...] = _dot(hh, w_ref[...]) + _dot(hl, w_ref[...]) + _dot(hh, wlo_ref[...])

        pl.when(is_lat)(one_pass)
        pl.when(jnp.logical_not(is_lat))(three_pass)
        z = z_ref[...]
        z_g, z_w = z[:, col_g:col_w], z[:, col_w:]
        za_ref[...] = z[:, :col_g]
    else:
        h_b = h.astype(BF16)
        z_g = _dot(h_b, w_ref[:, col_g:col_w])
        z_w = _dot(h_b, w_ref[:, col_w:])
        za_ref[...] = _dot(h_b, w_ref[:, :col_g])

    cos = cos_ref[...]
    sin = sin_ref[...]
    ones_blk = ones_ref[...]
    lane = lax.broadcasted_iota(jnp.int32, (x.shape[0], LANES), 1)
    first16 = (lane % 32) < 16
    lane_half = lane // HEAD_DIM
    scale = LOG2E * HEAD_DIM ** -0.5

    qg = [z_g[:, LANES * b:LANES * (b + 1)] for b in range(3)]
    kg = z_g[:, Q_WIDTH:Q_WIDTH + KV_WIDTH]
    vg = z_g[:, Q_WIDTH + KV_WIDTH:]
    qw = [z_w[:, LANES * b:LANES * (b + 1)] for b in range(3)]
    kw = z_w[:, Q_WIDTH:Q_WIDTH + KV_WIDTH]
    vw = z_w[:, Q_WIDTH + KV_WIDTH:]

    qg = [_rope(_head_rms(b, qg_ref[...], ones_blk), cos, sin, first16) * scale for b in qg]
    kg = _rope(_head_rms(kg, kg_ref[...], ones_blk), cos, sin, first16)
    qw = [_rope(b, cos, sin, first16) * scale for b in qw]
    kw = _rope(kw, cos, sin, first16)

    qg_att = _q_attention_layout(qg, lane_half)
    qw_att = _q_attention_layout(qw, lane_half)
    qga_ref[...] = qg_att.astype(BF16)
    qwa_ref[...] = qw_att.astype(BF16)
    kvg_ref[...] = jnp.concatenate([kg, vg], axis=1).astype(BF16)
    kvw_ref[...] = jnp.concatenate([kw, vw], axis=1).astype(BF16)
    kv32_ref[...] = jnp.concatenate([kg, vg, kw, vw], axis=1)
    qg32_ref[...] = qg_att
    qw32_ref[...] = qw_att


def _token_specs(x_pair, tm, lat_tiles):
    off = lat_tiles if x_pair[1] is x_pair[0] else 0
    lat = pl.BlockSpec((tm, D_MODEL), lambda i, *_: (jnp.minimum(i, lat_tiles - 1), 0))
    ctx = pl.BlockSpec((tm, D_MODEL), lambda i, *_: (jnp.maximum(i, lat_tiles) - lat_tiles + off, 0))
    return [lat, ctx]


def _in_projection(x_pair, mod, g1, w_hi, w_lo, cos_t, sin_t, qn_g, kn_g, ones_blk, dims):
    n = dims["n_lat"] + dims["n_ctx"]
    tm = TOKEN_TILE
    n_tiles = n // tm
    lat_tiles = dims["n_lat"] // tm
    ctx_tiles = n_tiles - lat_tiles
    tiles_per_seq = dims["t_lat"] // tm
    ctx_cond = dims["ctx_cond"]
    precise_ctx = w_lo is not None

    def cond_idx(i):
        return jnp.where(i < lat_tiles, i // tiles_per_seq, ctx_cond)

    def rope_idx(i):
        return jnp.where(i < lat_tiles, 1 + i % tiles_per_seq, 0)

    def ctx_idx(i):
        return jnp.where(i < lat_tiles, ctx_tiles, i - lat_tiles)

    row = lambda w: pl.BlockSpec((tm, w), lambda i: (i, 0))
    ctx_row = lambda w: pl.BlockSpec((tm, w), lambda i: (ctx_idx(i), 0))
    const = lambda a: pl.BlockSpec(a.shape, lambda i: (0,) * a.ndim, pipeline_mode=pl.Buffered(1))
    weights = [w_hi, w_lo] if precise_ctx else [w_hi]
    ctx_rows = (ctx_tiles + 1) * tm
    return pl.pallas_call(
        functools.partial(_inproj_kernel, precise_ctx=precise_ctx, lat_tiles=lat_tiles),
        grid=(n_tiles,),
        in_specs=_token_specs(x_pair, tm, lat_tiles) + [
            pl.BlockSpec((None, 1, 6 * D_MODEL), lambda i: (cond_idx(i), 0, 0)),
            const(g1)] + [const(w) for w in weights] + [
            pl.BlockSpec((tm, LANES), lambda i: (rope_idx(i), 0)),
            pl.BlockSpec((tm, LANES), lambda i: (rope_idx(i), 0)),
            const(qn_g), const(kn_g), const(ones_blk),
        ],
        out_specs=[
            row(5 * A_WIDTH), row(2 * Q_WIDTH), row(2 * KV_WIDTH), row(2 * Q_WIDTH), row(2 * KV_WIDTH),
            ctx_row(4 * KV_WIDTH), ctx_row(2 * Q_WIDTH), ctx_row(2 * Q_WIDTH),
        ],
        out_shape=[
            jax.ShapeDtypeStruct((n, 5 * A_WIDTH), F32),
            jax.ShapeDtypeStruct((n, 2 * Q_WIDTH), BF16),
            jax.ShapeDtypeStruct((n, 2 * KV_WIDTH), BF16),
            jax.ShapeDtypeStruct((n, 2 * Q_WIDTH), BF16),
            jax.ShapeDtypeStruct((n, 2 * KV_WIDTH), BF16),
            jax.ShapeDtypeStruct((ctx_rows, 4 * KV_WIDTH), F32),
            jax.ShapeDtypeStruct((ctx_rows, 2 * Q_WIDTH), F32),
            jax.ShapeDtypeStruct((ctx_rows, 2 * Q_WIDTH), F32),
        ],
        scratch_shapes=[pltpu.VMEM((tm, IN_WIDTH), F32)] if precise_ctx else [],
        compiler_params=_cparams("arbitrary"),
        name="in_projection_precise_ctx" if precise_ctx else "in_projection",
    )(*x_pair, mod, g1, *weights, cos_t, sin_t, qn_g, kn_g, ones_blk)


def _hgrn_chunk(qa, v, fz, lb, st, consts, reverse, precise):
    cum, e_sel, x_sel, sub_mask, p_mask, v_mask, head_mask = consts
    log2e = LOG2E
    n_sub, half = CHUNK // SUB, SUB // 2
    q = _silu(qa)
    f = lb + (1.0 - lb) * _sigmoid(fz)
    logf2 = jnp.log(jnp.maximum(f, TINY)) * log2e
    kk = (1.0 - lb) * _sigmoid(-fz)

    b = _dot_sel(cum, logf2)
    b3 = b.reshape(n_sub, SUB, A_WIDTH)
    end = 0 if reverse else SUB - 1
    tot = jnp.broadcast_to(b3[:, end:end + 1, :], (n_sub, SUB, A_WIDTH)).reshape(CHUNK, A_WIDTH)
    q_in = q * jnp.exp2(b)
    k_out = kk * jnp.exp2(tot - b)
    d_sub = jnp.exp2(tot)

    c3 = (b - jnp.log(kk) * log2e).reshape(n_sub, SUB, A_WIDTH)
    b4 = b.reshape(n_sub, 2, half, A_WIDTH)
    q4 = q.reshape(n_sub, 2, half, A_WIDTH)
    r_local = lax.broadcasted_iota(jnp.int32, (n_sub, half, A_WIDTH), 1)
    zero_half = jnp.zeros((n_sub, half, A_WIDTH), F32)
    pieces = []
    for s in range(SUB):
        cs = c3[:, s:s + 1, :]
        parts = []
        for hsel in range(2):
            lo, hi = hsel * half, hsel * half + half - 1
            if (lo > s) if reverse else (hi < s):
                parts.append(zero_half)
                continue
            arg = b4[:, hsel] - cs
            if not ((hi <= s) if reverse else (lo >= s)):
                t_local = r_local + lo
                arg = jnp.where((t_local <= s) if reverse else (t_local >= s), arg, MASK_VALUE)
            parts.append(q4[:, hsel] * jnp.exp2(arg))
        piece = jnp.stack(parts, axis=1).reshape(CHUNK, A_WIDTH)
        pieces.append(piece if precise else piece.astype(BF16))
    w_cat = jnp.concatenate(pieces, axis=1)
    s_local = _mm_sel(w_cat, e_sel, precise)
    p = _mm_sel(s_local, x_sel, precise) * p_mask
    v_bd = jnp.concatenate([v] * A_HEADS, axis=0) * v_mask
    o_intra = _mm(p, v_bd, precise)

    v_op = v if precise else v.astype(BF16)
    k_op = k_out if precise else k_out.astype(BF16)
    q_op = q_in if precise else q_in.astype(BF16)
    sub_m = sub_mask
    order = range(n_sub - 1, -1, -1) if reverse else range(n_sub)
    new_st, o_inter = [], []
    for p in range(A_WIDTH // LANES):
        lanes = slice(LANES * p, LANES * (p + 1))
        k_exp = jnp.concatenate([k_op[:, lanes]] * n_sub, axis=1) * sub_m
        u_all = _mm(v_op[:, lanes], k_exp, precise, _TN)
        s = st[p]
        s_before = [None] * n_sub
        for j in order:
            s_before[j] = s
            s = d_sub[SUB * j:SUB * j + 1, lanes] * s + head_mask * u_all[:, LANES * j:LANES * (j + 1)]
        new_st.append(s)
        q_exp = jnp.concatenate([q_op[:, lanes]] * n_sub, axis=1) * sub_m
        o_inter.append(_mm(q_exp, jnp.concatenate(s_before, axis=1), precise, _NT))
    return o_intra + jnp.concatenate(o_inter, axis=1), jnp.stack(new_st, axis=0)


def _hgrn_kernel(*refs, has_s0, want_state, precise):
    refs = list(refs)
    seq_refs = [refs[6 * sq:6 * sq + 6] for sq in range(HGRN_SEQS)]
    del refs[:6 * HGRN_SEQS]
    lb_ref = refs.pop(0)
    s0_ref = refs.pop(0) if has_s0 else None
    cumf_ref, cumb_ref, e_ref, x_ref, sm_ref, pm_ref, vm_ref, hm_ref, of_ref, ob_ref = refs[:10]
    del refs[:10]
    sout_ref = refs.pop(0) if want_state else None
    st_ref, = refs
    c = pl.program_id(1)

    @pl.when(c == 0)
    def _():
        st_ref[...] = s0_ref[...] if has_s0 else jnp.zeros_like(st_ref)

    shared = (e_ref[...], x_ref[...], sm_ref[...], pm_ref[...], vm_ref[...], hm_ref[...])
    lb = lb_ref[...]
    for sq in range(HGRN_SEQS):
        qaf_ref, iaf_ref, fzf_ref, qab_ref, iab_ref, fzb_ref = seq_refs[sq]
        o_f, st_f = _hgrn_chunk(qaf_ref[...], iaf_ref[...], fzf_ref[...], lb[0:1], st_ref[sq, 0],
                                (cumf_ref[...],) + shared, False, precise)
        o_b, st_b = _hgrn_chunk(qab_ref[...], iab_ref[...], fzb_ref[...], lb[1:2], st_ref[sq, 1],
                                (cumb_ref[...],) + shared, True, precise)
        of_ref[sq] = o_f
        ob_ref[sq] = o_b
        st_ref[sq, 0] = st_f
        st_ref[sq, 1] = st_b

    if want_state:
        @pl.when(c == pl.num_programs(1) - 1)
        def _():
            sout_ref[...] = st_ref[...]


def _hgrn_consts():
    r = np.arange(CHUNK)
    same = (r[:, None] // SUB) == (r[None, :] // SUB)
    cum_f = (same & (r[None, :] <= r[:, None])).astype(np.float32)
    cum_b = (same & (r[None, :] >= r[:, None])).astype(np.float32)
    tot = same.astype(np.float32)
    e = np.zeros((SUB, A_HEADS, HEAD_DIM, A_HEADS, SUB), np.float32)
    for s in range(SUB):
        for h in range(A_HEADS):
            e[s, h, :, h, s] = 1.0
    e = e.reshape(SUB * A_WIDTH, A_HEADS * SUB)
    x = np.zeros((A_HEADS, SUB, A_HEADS, CHUNK), np.float32)
    for h in range(A_HEADS):
        for s in range(CHUNK):
            x[h, s % SUB, h, s] = 1.0
    x = x.reshape(A_HEADS * SUB, A_HEADS * CHUNK)
    p_mask = np.tile(tot, (1, A_HEADS))
    hv = np.arange(A_HEADS * CHUNK) // CHUNK
    hk = np.arange(A_WIDTH) // HEAD_DIM
    v_mask = (hv[:, None] == hk[None, :]).astype(np.float32)
    head_mask = (hk[:LANES, None] == hk[None, :LANES]).astype(np.float32)
    sub_mask = np.repeat(r[:, None] // SUB == np.arange(CHUNK // SUB)[None, :], LANES, axis=1).astype(np.float32)
    sel = [jnp.asarray(a, BF16) for a in (cum_f, cum_b, e, x, sub_mask)]
    return sel + [jnp.asarray(a, F32) for a in (p_mask, v_mask, head_mask)]


def _hgrn_scan(za, first_row, lb2, s0_t, n_seq, t, precise, want_state, name):
    nc = t // CHUNK
    groups = n_seq // HGRN_SEQS
    base = first_row // CHUNK
    za3 = za.reshape(za.shape[0] // CHUNK, CHUNK, za.shape[1])
    consts = _hgrn_consts()
    state_shape = (HGRN_SEQS, 2, A_WIDTH // LANES, LANES, LANES)
    const = lambda a: pl.BlockSpec(a.shape, lambda p, c: (0,) * a.ndim)

    def chunk_spec(sq, col, backward):
        def index(p, c):
            return (base + (p * HGRN_SEQS + sq) * nc + (nc - 1 - c if backward else c), 0, col)
        return pl.BlockSpec((None, CHUNK, A_WIDTH), index)

    fwd = lambda col: pl.BlockSpec((None, HGRN_SEQS, CHUNK, A_WIDTH), lambda p, c: (p, 0, c, col))
    bwd = lambda col: pl.BlockSpec((None, HGRN_SEQS, CHUNK, A_WIDTH), lambda p, c: (p, 0, nc - 1 - c, col))
    state_spec = pl.BlockSpec((None,) + state_shape, lambda p, c: (p, 0, 0, 0, 0, 0))
    operands = [za3] * (6 * HGRN_SEQS) + [lb2]
    in_specs = [chunk_spec(sq, col, backward) for sq in range(HGRN_SEQS)
                for col, backward in ((0, False), (1, False), (2, False), (0, True), (1, True), (3, True))]
    in_specs.append(const(lb2))
    if s0_t is not None:
        operands.append(s0_t.reshape((groups,) + state_shape))
        in_specs.append(state_spec)
    o_shape = jax.ShapeDtypeStruct((groups, HGRN_SEQS, t, A_WIDTH), F32)
    out_specs, out_shape = [fwd(0), bwd(0)], [o_shape, o_shape]
    if want_state:
        out_specs.append(state_spec)
        out_shape.append(jax.ShapeDtypeStruct((groups,) + state_shape, F32))
    outs = pl.pallas_call(
        functools.partial(_hgrn_kernel, has_s0=s0_t is not None, want_state=want_state, precise=precise),
        grid=(groups, nc),
        in_specs=in_specs + [const(c) for c in consts],
        out_specs=out_specs,
        out_shape=out_shape,
        scratch_shapes=[pltpu.VMEM(state_shape, F32)],
        compiler_params=_cparams("parallel", "arbitrary"),
        name=name,
    )(*operands, *consts)
    o_f, o_b = (o.reshape(n_seq * t, A_WIDTH) for o in outs[:2])
    s_fin = outs[2].reshape((n_seq,) + state_shape[1:]) if want_state else None
    return o_f, o_b, s_fin


def _attend(q3, segments, sink_row, precise=False):
    scores = []
    for k, _, valid in segments:
        s = _mm(k, q3, precise, _NT)
        if valid is not None:
            s = jnp.where(valid, s, MASK_VALUE)
        scores.append(s)
    m = scores[0].max(axis=0, keepdims=True)
    for s in scores[1:]:
        m = jnp.maximum(m, s.max(axis=0, keepdims=True))
    if sink_row is not None:
        m = jnp.maximum(m, sink_row)
    denom = jnp.exp2(sink_row - m) if sink_row is not None else 0.0
    acc = 0.0
    for s, (_, v, _) in zip(scores, segments):
        p = jnp.exp2(s - m)
        denom = denom + p.sum(axis=0, keepdims=True)
        acc = acc + _mm(v, p, precise, _TN)
    return (acc / denom).T


def _stack_heads(q, kvh):
    base = kvh * GQA_GROUP
    return jnp.concatenate([q[:, LANES * (base + g):LANES * (base + g + 1)] for g in range(GQA_GROUP)], axis=0)


def _merge_heads(res, tq):
    lane = lax.broadcasted_iota(jnp.int32, (tq, LANES), 1)
    heads = []
    for j in range(B_Q_HEADS):
        kvh, g = divmod(j, GQA_GROUP)
        o = res[kvh][g * tq:(g + 1) * tq]
        if kvh != j % 2:
            o = pltpu.roll(o, HEAD_DIM, 1)
        heads.append(o)
    blocks = [jnp.where(lane < HEAD_DIM, heads[2 * b], heads[2 * b + 1]) for b in range(B_Q_HEADS // 2)]
    return jnp.concatenate(blocks, axis=1)


def _sink_row(sink_ref, kvh, tq):
    return jnp.concatenate(
        [jnp.full((1, tq), sink_ref[kvh * GQA_GROUP + g] * LOG2E, F32) for g in range(GQA_GROUP)], axis=1)


def _global_attn_kernel(q_ref, kv_ref, ck_ref, cv_ref, o_ref, *, tq):
    q = q_ref[...]
    kv = kv_ref[...]
    k, v = kv[:, :KV_WIDTH], kv[:, KV_WIDTH:]
    ck, cv = ck_ref[...], cv_ref[...]
    segments = [(k[c:c + KEY_CHUNK], v[c:c + KEY_CHUNK], None) for c in range(0, k.shape[0], KEY_CHUNK)]
    segments.append((ck, cv, None))
    res = [_attend(_stack_heads(q, kvh), segments, None) for kvh in range(B_KV_HEADS)]
    o_ref[...] = _merge_heads(res, tq).astype(BF16)


def _global_attention(q_att, kv, ck, cv, dims):
    t, n_seq = dims["t_lat"], dims["n_lat_seq"]
    tq = 256
    nq = t // tq
    return pl.pallas_call(
        functools.partial(_global_attn_kernel, tq=tq),
        grid=(n_seq, nq),
        in_specs=[
            pl.BlockSpec((tq, 2 * Q_WIDTH), lambda b, i: (b * nq + i, 0)),
            pl.BlockSpec((t, 2 * KV_WIDTH), lambda b, i: (b, 0)),
            pl.BlockSpec((None,) + ck.shape[1:], lambda b, i: (b, 0, 0)),
            pl.BlockSpec((None,) + cv.shape[1:], lambda b, i: (b, 0, 0)),
        ],
        out_specs=pl.BlockSpec((tq, Q_WIDTH), lambda b, i: (b * nq + i, 0)),
        out_shape=jax.ShapeDtypeStruct((n_seq * t, Q_WIDTH), BF16),
        compiler_params=_cparams("parallel", "arbitrary"),
        name="global_attention",
    )(q_att, kv, ck, cv)


def _window_attn_kernel(sink_ref, q_ref, kv_ref, ck_ref, cv_ref, o_ref, *, tq, t):
    j = pl.program_id(1)
    span = tq + 2 * WINDOW
    start = pl.multiple_of(jnp.clip(j * tq - WINDOW, 0, t - span), WINDOW)
    q = q_ref[...]
    kv = kv_ref[pl.ds(start, span), :]
    k, v = kv[:, :KV_WIDTH], kv[:, KV_WIDTH:]
    key_pos = start + lax.broadcasted_iota(jnp.int32, (span, GQA_GROUP * tq), 0)
    query_pos = j * tq + lax.broadcasted_iota(jnp.int32, (span, GQA_GROUP * tq), 1) % tq
    valid = jnp.abs(query_pos - key_pos) <= WINDOW
    segments = [(k, v, valid), (ck_ref[...], cv_ref[...], None)]
    res = [_attend(_stack_heads(q, kvh), segments, _sink_row(sink_ref, kvh, tq)) for kvh in range(C_KV_HEADS)]
    o_ref[...] = _merge_heads(res, tq).astype(BF16)


def _window_attention(sink, q_att, kv, ck, cv, dims):
    t, n_seq = dims["t_lat"], dims["n_lat_seq"]
    tq = 2 * WINDOW
    nq = t // tq
    return pl.pallas_call(
        functools.partial(_window_attn_kernel, tq=tq, t=t),
        grid=(n_seq, nq),
        in_specs=[
            pl.BlockSpec(memory_space=pltpu.SMEM),
            pl.BlockSpec((tq, 2 * Q_WIDTH), lambda b, i: (b * nq + i, 0)),
            pl.BlockSpec((t, 2 * KV_WIDTH), lambda b, i: (b, 0)),
            pl.BlockSpec((None,) + ck.shape[1:], lambda b, i: (b, 0, 0)),
            pl.BlockSpec((None,) + cv.shape[1:], lambda b, i: (b, 0, 0)),
        ],
        out_specs=pl.BlockSpec((tq, Q_WIDTH), lambda b, i: (b * nq + i, 0)),
        out_shape=jax.ShapeDtypeStruct((n_seq * t, Q_WIDTH), BF16),
        compiler_params=_cparams("parallel", "arbitrary"),
        name="window_attention",
    )(sink, q_att, kv, ck, cv)


def _context_attn_kernel(sink_ref, qg_ref, qw_ref, kv_ref, og_ref, ow_ref, *, t, precise):
    kv = kv_ref[...]
    for q_ref, o_ref, off, use_sink in ((qg_ref, og_ref, 0, False), (qw_ref, ow_ref, 2 * KV_WIDTH, True)):
        q = q_ref[...]
        k, v = kv[:, off:off + KV_WIDTH], kv[:, off + KV_WIDTH:off + 2 * KV_WIDTH]
        res = [_attend(_stack_heads(q, kvh), [(k, v, None)],
                       _sink_row(sink_ref, kvh, t) if use_sink else None, precise)
               for kvh in range(B_KV_HEADS)]
        o_ref[...] = _merge_heads(res, t)


def _context_attention(sink, qg32, qw32, kv32, dims, precise):
    t, n_seq = dims["t_ctx"], dims["n_ctx_seq"]
    seq = lambda w: pl.BlockSpec((t, w), lambda b: (b, 0))
    return pl.pallas_call(
        functools.partial(_context_attn_kernel, t=t, precise=precise),
        grid=(n_seq,),
        in_specs=[pl.BlockSpec(memory_space=pltpu.SMEM), seq(2 * Q_WIDTH), seq(2 * Q_WIDTH), seq(4 * KV_WIDTH)],
        out_specs=[seq(Q_WIDTH), seq(Q_WIDTH)],
        out_shape=[jax.ShapeDtypeStruct((n_seq * t, Q_WIDTH), F32)] * 2,
        compiler_params=_cparams("parallel"),
        name="context_attention_precise" if precise else "context_attention",
    )(sink, qg32, qw32, kv32)


def _route(scores_t, sel_t):
    s = [scores_t[e:e + 1, :] for e in range(N_EXPERTS)]
    z = [sel_t[e:e + 1, :] for e in range(N_EXPERTS)]
    gs = []
    for g in range(N_GROUPS):
        m = z[g * EXPERTS_PER_GROUP:(g + 1) * EXPERTS_PER_GROUP]
        best = None
        for a in range(EXPERTS_PER_GROUP):
            for b in range(a + 1, EXPERTS_PER_GROUP):
                pair = m[a] + m[b]
                best = pair if best is None else jnp.maximum(best, pair)
        gs.append(best)
    combine, groups = [], []
    for g in range(N_GROUPS):
        chosen_g = None
        for g2 in range(N_GROUPS):
            if g2 == g:
                continue
            c = (gs[g] > gs[g2]) if g2 < g else (gs[g] >= gs[g2])
            chosen_g = c if chosen_g is None else jnp.logical_and(chosen_g, c)
        base = g * EXPERTS_PER_GROUP
        picked = []
        for a in range(EXPERTS_PER_GROUP):
            rank = 0.0
            for b in range(EXPERTS_PER_GROUP):
                if b == a:
                    continue
                ahead = (z[base + b] >= z[base + a]) if b < a else (z[base + b] > z[base + a])
                rank = rank + ahead.astype(F32)
            picked.append(jnp.where(jnp.logical_and(chosen_g, rank < 2.0), s[base + a], 0.0))
        denom = picked[0] + picked[1] + picked[2] + picked[3]
        denom = jnp.where(chosen_g, denom, 1.0)
        combine.extend(pk / denom for pk in picked)
        groups.append(jnp.where(chosen_g, 1.0, 0.0))
    return jnp.concatenate(combine, axis=0), jnp.concatenate(groups, axis=0)


def _outproj_kernel(*refs, lat_tiles, precise_ctx):
    if precise_ctx:
        (ofl_ref, ofc_ref, obl_ref, obc_ref, ga_ref,
         ogl_ref, ogc_ref, owl_ref, owc_ref, xl_ref, xc_ref, mod_ref, ag_ref,
         ones_ref, w_ref, wlo_ref, g2_ref, rw_ref, rb_ref, before_ref,
         x1_ref, h2_ref, comb_ref, pos_ref, cnt_ref, mix_ref) = refs
    else:
        (ofl_ref, ofc_ref, obl_ref, obc_ref, ga_ref,
         ogl_ref, ogc_ref, owl_ref, owc_ref, xl_ref, xc_ref, mod_ref, ag_ref,
         ones_ref, w_ref, g2_ref, rw_ref, rb_ref, before_ref,
         x1_ref, h2_ref, comb_ref, pos_ref, cnt_ref, mix_ref) = refs
        wlo_ref = None
    is_lat = pl.program_id(0) < lat_tiles
    o = jnp.where(is_lat, ofl_ref[...] + obl_ref[...], ofc_ref[...] + obc_ref[...])
    oa = _head_rms(o, ag_ref[...], ones_ref[...]) * _sigmoid(ga_ref[...])
    parts = ((0, A_WIDTH), (A_WIDTH, A_WIDTH + Q_WIDTH), (A_WIDTH + Q_WIDTH, D_MODEL))

    def mix(operands, precise):
        acc = 0.0
        for a, (r0, r1) in zip(operands, parts):
            if precise:
                ah, al = _split2(a)
                acc = acc + _dot(ah, w_ref[r0:r1, :]) + _dot(al, w_ref[r0:r1, :]) + _dot(ah, wlo_ref[r0:r1, :])
            else:
                acc = acc + _dot(a.astype(BF16), w_ref[r0:r1, :])
        mix_ref[...] = acc

    pl.when(is_lat)(lambda: mix((oa, ogl_ref[...], owl_ref[...]), False))
    pl.when(jnp.logical_not(is_lat))(lambda: mix((oa, ogc_ref[...], owc_ref[...]), precise_ctx))

    mod = mod_ref[...]
    gate1 = mod[:, 2 * D_MODEL:3 * D_MODEL]
    sh2 = mod[:, 3 * D_MODEL:4 * D_MODEL]
    sc2 = mod[:, 4 * D_MODEL:5 * D_MODEL]
    x1 = jnp.where(is_lat, xl_ref[...], xc_ref[...]) + gate1 * mix_ref[...]
    x1_ref[...] = x1
    gain = g2_ref[...] * (1.0 + sc2)
    h2 = x1 * lax.rsqrt(jnp.mean(x1 * x1, axis=-1, keepdims=True) + EPS) * gain + sh2
    h2_ref[...] = h2.astype(BF16)
    scores_t = _sigmoid(_mm(rw_ref[...], h2, True, _NT))
    comb_t, group_t = _route(scores_t, scores_t + rb_ref[...])
    comb_ref[...] = comb_t
    rank = _dot(group_t.astype(BF16), before_ref[...])
    counts = group_t.sum(axis=1, keepdims=True)
    pos, offset = 0.0, 0.0
    for g in range(N_GROUPS):
        pos = pos + group_t[g:g + 1] * (rank[g:g + 1] + offset)
        offset = offset + counts[g:g + 1]
    pos_ref[...] = pos
    pad = jnp.zeros((8 - N_GROUPS, 1), F32)
    cnt_ref[...] = jnp.broadcast_to(jnp.concatenate([counts, pad], axis=0), (8, LANES)).astype(jnp.int32)


def _out_projection(o_f, o_b, za, og_lat, og_ctx, ow_lat, ow_ctx, x_pair, mod, an_g, ones_blk, w_hi, w_lo, g2,
                    rw_t, rb, dims):
    n = dims["n_lat"] + dims["n_ctx"]
    tm = TOKEN_TILE
    n_tiles = n // tm
    lat_tiles = dims["n_lat"] // tm
    tiles_per_seq = dims["t_lat"] // tm
    ctx_cond = dims["ctx_cond"]
    precise_ctx = w_lo is not None
    cond_idx = lambda i: jnp.where(i < lat_tiles, i // tiles_per_seq, ctx_cond)
    row = lambda w: pl.BlockSpec((tm, w), lambda i: (i, 0))
    lat = lambda w: pl.BlockSpec((tm, w), lambda i: (jnp.minimum(i, lat_tiles - 1), 0))
    ctx = lambda w: pl.BlockSpec((tm, w), lambda i: (jnp.maximum(i - lat_tiles, 0), 0))
    const = lambda a: pl.BlockSpec(a.shape, lambda i: (0,) * a.ndim, pipeline_mode=pl.Buffered(1))
    weights = [w_hi, w_lo] if precise_ctx else [w_hi]
    t_idx = np.arange(tm)
    before = jnp.asarray((t_idx[:, None] < t_idx[None, :]).astype(np.float32), BF16)
    return pl.pallas_call(
        functools.partial(_outproj_kernel, lat_tiles=lat_tiles, precise_ctx=precise_ctx),
        grid=(n_tiles,),
        in_specs=[
            lat(A_WIDTH), ctx(A_WIDTH), lat(A_WIDTH), ctx(A_WIDTH),
            pl.BlockSpec((tm, A_WIDTH), lambda i: (i, 4)),
            lat(Q_WIDTH), ctx(Q_WIDTH), lat(Q_WIDTH), ctx(Q_WIDTH)] + _token_specs(x_pair, tm, lat_tiles) + [
            pl.BlockSpec((None, 1, 6 * D_MODEL), lambda i: (cond_idx(i), 0, 0)),
            const(an_g), const(ones_blk)] + [const(w) for w in weights] + [
            const(g2), const(rw_t), const(rb), const(before),
        ],
        out_specs=[row(D_MODEL), row(D_MODEL), pl.BlockSpec((N_EXPERTS, tm), lambda i: (0, i)),
                   pl.BlockSpec((1, tm), lambda i: (0, i)), pl.BlockSpec((8, LANES), lambda i: (i, 0))],
        out_shape=[
            jax.ShapeDtypeStruct((n, D_MODEL), F32),
            jax.ShapeDtypeStruct((n, D_MODEL), BF16),
            jax.ShapeDtypeStruct((N_EXPERTS, n), F32),
            jax.ShapeDtypeStruct((1, n), F32),
            jax.ShapeDtypeStruct((n_tiles * 8, LANES), jnp.int32),
        ],
        scratch_shapes=[pltpu.VMEM((tm, D_MODEL), F32)],
        compiler_params=_cparams("arbitrary"),
        name="out_projection_precise_ctx" if precise_ctx else "out_projection",
    )(*o_f, *o_b, za, og_lat, og_ctx, ow_lat, ow_ctx, *x_pair, mod, an_g, ones_blk, *weights, g2, rw_t, rb, before)


def _moe_kernel(off_ref, h_ref, comb_ref, pos_ref, x1_ref, mod_ref, wgu_ref, wd_ref, fg_ref, *rest,
                final, lat_tiles):
    *outs, hs_ref, cs_ref, acc_ref = rest
    i = pl.program_id(0)
    tm = h_ref.shape[0]
    row = lax.broadcasted_iota(jnp.int32, (tm, tm), 0).astype(F32)
    perm = jnp.where(row == pos_ref[...], 1.0, 0.0).astype(BF16)
    hs_ref[...] = _dot(perm, h_ref[...]).astype(BF16)
    cs_ref[...] = sum(lax.dot_general(perm, c, _NT, preferred_element_type=F32)
                      for c in _split3(comb_ref[...]))
    acc_ref[...] = jnp.zeros_like(acc_ref)
    lane = lax.broadcasted_iota(jnp.int32, (MOE_BLOCK, N_EXPERTS), 1)

    for r in range(tm // MOE_BLOCK):
        rows = slice(r * MOE_BLOCK, (r + 1) * MOE_BLOCK)

        def group_body(g, carry, rows=rows, r=r):
            has_tokens = jnp.logical_and(off_ref[i, g] < (r + 1) * MOE_BLOCK, off_ref[i, g + 1] > r * MOE_BLOCK)

            @pl.when(has_tokens)
            def _():
                cs = cs_ref[rows, :]
                gu = _dot(hs_ref[rows, :], wgu_ref[g])
                acts = []
                for a in range(EXPERTS_PER_GROUP):
                    gate = gu[:, 2 * D_EXPERT * a:2 * D_EXPERT * a + D_EXPERT]
                    up = gu[:, 2 * D_EXPERT * a + D_EXPERT:2 * D_EXPERT * (a + 1)]
                    ce = jnp.sum(jnp.where(lane == g * EXPERTS_PER_GROUP + a, cs, 0.0), axis=1, keepdims=True)
                    acts.append((_silu(gate) * up * ce).astype(BF16))
                acc_ref[rows, :] += _dot(jnp.concatenate(acts, axis=1), wd_ref[g])

            return carry

        lax.fori_loop(0, N_GROUPS, group_body, 0)

    y_hi, y_lo = _split2(acc_ref[...])
    y = (lax.dot_general(perm, y_hi, _TN, preferred_element_type=F32)
         + lax.dot_general(perm, y_lo, _TN, preferred_element_type=F32))
    x2 = x1_ref[...] + mod_ref[...][:, 5 * D_MODEL:] * y
    if not final:
        outs[0][...] = x2
    else:
        yn = x2 * lax.rsqrt(jnp.mean(x2 * x2, axis=-1, keepdims=True) + EPS) * fg_ref[...]
        is_lat = i < lat_tiles

        @pl.when(is_lat)
        def _():
            outs[0][...] = yn

        @pl.when(jnp.logical_not(is_lat))
        def _():
            outs[1][...] = yn


def _moe(h2, comb_t, pos, counts, x1, mod, w_gu, w_d, final_g, dims, final):
    n = x1.shape[0]
    tm = TOKEN_TILE
    lat_tiles = dims["n_lat"] // tm
    tiles_per_seq = dims["t_lat"] // tm
    ctx_cond = dims["ctx_cond"]
    cond_idx = lambda i: jnp.where(i < lat_tiles, i // tiles_per_seq, ctx_cond)
    offsets = jnp.concatenate([jnp.zeros((n // tm, 1), jnp.int32), jnp.cumsum(counts, axis=1)], axis=1)
    if final:
        out_specs = [pl.BlockSpec((tm, D_MODEL), lambda i, *_: (jnp.minimum(i, lat_tiles - 1), 0)),
                     pl.BlockSpec((tm, D_MODEL), lambda i, *_: (jnp.maximum(i - lat_tiles, 0), 0))]
        out_shape = [jax.ShapeDtypeStruct((dims["n_lat"], D_MODEL), F32),
                     jax.ShapeDtypeStruct((dims["n_ctx"], D_MODEL), F32)]
    else:
        out_specs = pl.BlockSpec((tm, D_MODEL), lambda i, *_: (i, 0))
        out_shape = jax.ShapeDtypeStruct((n, D_MODEL), F32)
    resident = lambda a: pl.BlockSpec(a.shape, lambda i, *_: (0,) * a.ndim, pipeline_mode=pl.Buffered(1))
    grid_spec = pltpu.PrefetchScalarGridSpec(
        num_scalar_prefetch=1,
        grid=(n // tm,),
        in_specs=[
            pl.BlockSpec((tm, D_MODEL), lambda i, *_: (i, 0)),
            pl.BlockSpec((N_EXPERTS, tm), lambda i, *_: (0, i)),
            pl.BlockSpec((1, tm), lambda i, *_: (0, i)),
            pl.BlockSpec((tm, D_MODEL), lambda i, *_: (i, 0)),
            pl.BlockSpec((None, 1, 6 * D_MODEL), lambda i, *_: (cond_idx(i), 0, 0)),
            resident(w_gu), resident(w_d),
            pl.BlockSpec((1, D_MODEL), lambda i, *_: (0, 0)),
        ],
        out_specs=out_specs,
        scratch_shapes=[pltpu.VMEM((tm, D_MODEL), BF16), pltpu.VMEM((tm, N_EXPERTS), F32),
                        pltpu.VMEM((tm, D_MODEL), F32)],
    )
    return pl.pallas_call(
        functools.partial(_moe_kernel, final=final, lat_tiles=lat_tiles),
        grid_spec=grid_spec,
        out_shape=out_shape,
        compiler_params=_cparams("arbitrary"),
        name="moe_final" if final else "moe",
    )(offsets, h2, comb_t, pos, x1, mod, w_gu, w_d, final_g)


def _rope_tables(t, tile):
    n_freq = HEAD_DIM // 4
    pos = np.arange(t)
    freqs = ROPE_THETA ** (-np.arange(n_freq, dtype=np.float64) / n_freq)
    ang_row = (pos // GRID_W)[:, None] * freqs
    ang_col = (pos % GRID_W)[:, None] * freqs
    cos_h = np.concatenate([np.cos(ang_row)] * 2 + [np.cos(ang_col)] * 2, axis=1)
    sin_h = np.concatenate([-np.sin(ang_row), np.sin(ang_row), -np.sin(ang_col), np.sin(ang_col)], axis=1)
    cos_t = np.concatenate([np.ones((tile, LANES)), np.concatenate([cos_h, cos_h], axis=1)], axis=0)
    sin_t = np.concatenate([np.zeros((tile, LANES)), np.concatenate([sin_h, sin_h], axis=1)], axis=0)
    return jnp.asarray(cos_t, F32), jnp.asarray(sin_t, F32)


def _head_ones(width):
    r = np.arange(width) // HEAD_DIM
    return jnp.asarray((r[:, None] == r[None, :]).astype(np.float32), BF16)


def _split_kernel(w_ref, hi_ref, lo_ref):
    hi, lo = _split2(w_ref[...])
    hi_ref[...] = hi
    lo_ref[...] = lo


def _hi_lo(w, want_lo):
    if not want_lo:
        return w.astype(BF16), None
    rows, width = w.shape
    tr = 256
    spec = pl.BlockSpec((tr, width), lambda i: (i, 0))
    return pl.pallas_call(
        _split_kernel,
        grid=(rows // tr,),
        in_specs=[spec],
        out_specs=[spec, spec],
        out_shape=[jax.ShapeDtypeStruct(w.shape, BF16)] * 2,
        compiler_params=_cparams("parallel"),
        name="split_weight",
    )(w)


def _state_to_block_diag_t(s):
    per_pair = LANES // HEAD_DIM
    st = jnp.swapaxes(s, -1, -2).reshape(s.shape[:2] + (A_HEADS // per_pair, per_pair, HEAD_DIM, HEAD_DIM))
    eye = jnp.eye(per_pair, dtype=s.dtype)
    out = jnp.einsum("bdphvk,hg->bdphvgk", st, eye)
    return out.reshape(s.shape[:2] + (A_HEADS // per_pair, LANES, LANES))


def _block_diag_t_to_state(st):
    per_pair = LANES // HEAD_DIM
    s7 = st.reshape(st.shape[:3] + (per_pair, HEAD_DIM, per_pair, HEAD_DIM))
    diag = jnp.stack([s7[:, :, :, h, :, h, :] for h in range(per_pair)], axis=3)
    return jnp.swapaxes(diag, -1, -2).reshape(st.shape[:2] + (A_HEADS, HEAD_DIM, HEAD_DIM))


def kernel(x_prompt, x_sample, cache_glob_k, cache_glob_v, cache_win_k, cache_win_v, state_hgrn, c, c_ctx,
           mod_w, mod_b, norm1_g, norm2_g, w_in, w_out, hgrn_lb, hgrn_norm_g, q_norm_g, k_norm_g, win_sink,
           router_w, router_b, w_gate_up, w_down, final_g):
    n_ctx_seq, t_ctx, d = x_prompt.shape
    n_lat_seq, t_lat, _ = x_sample.shape
    depth = mod_w.shape[0]
    past = cache_glob_k.shape[2]
    n_lat, n_ctx = n_lat_seq * t_lat, n_ctx_seq * t_ctx
    dims = dict(n_ctx_seq=n_ctx_seq, t_ctx=t_ctx, n_lat_seq=n_lat_seq, t_lat=t_lat,
                n_lat=n_lat, n_ctx=n_ctx, ctx_cond=n_lat_seq)
    assert d == D_MODEL and n_lat_seq < COND_ROWS
    assert n_ctx % TOKEN_TILE == 0 and t_lat % TOKEN_TILE == 0
    assert t_ctx % CHUNK == 0 and t_lat >= 4 * WINDOW

    x_pair = (x_sample.reshape(n_lat, d), x_prompt.reshape(n_ctx, d))
    cond = jnp.concatenate([c, c_ctx[None, :], jnp.zeros((COND_ROWS - n_lat_seq - 1, d), F32)], axis=0)
    mod = _modulation(cond, mod_w, mod_b).reshape(depth, COND_ROWS, 1, 6 * d)

    p_lb = jax.nn.softmax(hgrn_lb.astype(F32), axis=1)
    lbs = jnp.cumsum(p_lb, axis=1) - p_lb[:, :1]

    cos_t, sin_t = _rope_tables(t_lat, TOKEN_TILE)
    ones128, ones256 = _head_ones(LANES), _head_ones(A_WIDTH)
    tile2 = lambda g, reps: jnp.tile(g, reps)[None, :]
    rw_t = router_w.T
    rb = router_b[:, None]
    fg = final_g[None, :]

    new_kv, new_state = [], []
    for l in range(depth):
        precise = l < depth - 1
        win_hi, win_lo = _hi_lo(w_in[l], precise)
        za, qg_att, kvg, qw_att, kvw, kv32, qg32, qw32 = _in_projection(
            x_pair, mod[l], norm1_g[l][None, :], win_hi, win_lo, cos_t, sin_t,
            tile2(q_norm_g[l], 2), tile2(k_norm_g[l], 2), ones128, dims)

        s0 = state_hgrn[:, l].astype(F32)
        of_lat, ob_lat, _ = _hgrn_scan(za, 0, lbs[:, l], _state_to_block_diag_t(s0), n_lat_seq, t_lat,
                                       False, False, "hgrn_latent")
        of_ctx, ob_ctx, s_fin = _hgrn_scan(za, n_lat, lbs[:, l], None, n_ctx_seq, t_ctx, precise, True,
                                           "hgrn_context_precise" if precise else "hgrn_context")

        cast_cache = lambda a: a[:, l].reshape(n_lat_seq, past, KV_WIDTH).astype(BF16)
        og_lat = _global_attention(qg_att, kvg, cast_cache(cache_glob_k), cast_cache(cache_glob_v), dims)
        ow_lat = _window_attention(win_sink[l], qw_att, kvw, cast_cache(cache_win_k),
                                   cast_cache(cache_win_v), dims)
        og_ctx, ow_ctx = _context_attention(win_sink[l], qg32, qw32, kv32, dims, precise)

        wout_hi, wout_lo = _hi_lo(w_out[l], precise)
        x1, h2, comb_t, pos, cnt = _out_projection(
            (of_lat, of_ctx), (ob_lat, ob_ctx), za, og_lat, og_ctx, ow_lat, ow_ctx, x_pair, mod[l],
            tile2(hgrn_norm_g[l], A_HEADS), ones256, wout_hi, wout_lo, norm2_g[l][None, :], rw_t, rb, dims)

        counts = cnt.reshape(-1, 8, LANES)[:, :N_GROUPS, 0]
        w_gu = w_gate_up[l].astype(BF16).reshape(N_GROUPS, EXPERTS_PER_GROUP, d, 2 * D_EXPERT)
        w_gu = w_gu.transpose(0, 2, 1, 3).reshape(N_GROUPS, d, EXPERTS_PER_GROUP * 2 * D_EXPERT)
        w_d = w_down[l].astype(BF16).reshape(N_GROUPS, EXPERTS_PER_GROUP * D_EXPERT, d)
        x = _moe(h2, comb_t, pos, counts, x1, mod[l], w_gu, w_d, fg, dims, final=(l == depth - 1))
        x_pair = (x, x)

        new_kv.append(kv32[:n_ctx].reshape(n_ctx_seq, t_ctx, 4, B_KV_HEADS, HEAD_DIM))
        new_state.append(_block_diag_t_to_state(s_fin))

    y_sample = x[0].reshape(n_lat_seq, t_lat, d)
    y_prompt = x[1].reshape(n_ctx_seq, t_ctx, d)
    kv = jnp.stack(new_kv, axis=1)
    return (y_prompt, y_sample, kv[:, :, :, 0], kv[:, :, :, 1], kv[:, :, :, 2], kv[:, :, :, 3],
            jnp.stack(new_state, axis=1).astype(x_prompt.dtype))
```

```python
import functools

import numpy as np
import jax
import jax.numpy as jnp
from jax import lax
from jax.experimental import pallas as pl
from jax.experimental.pallas import tpu as pltpu

F32 = jnp.float32
BF16 = jnp.bfloat16

D_MODEL = 1024
HEAD_DIM = 64
GRID_W = 64
A_HEADS = 4
A_WIDTH = A_HEADS * HEAD_DIM
B_Q_HEADS = 6
B_KV_HEADS = 2
C_Q_HEADS = 6
C_KV_HEADS = 2
Q_WIDTH = B_Q_HEADS * HEAD_DIM
KV_WIDTH = B_KV_HEADS * HEAD_DIM
GQA_GROUP = B_Q_HEADS // B_KV_HEADS
WINDOW = 128
ROPE_THETA = 10000.0
IN_WIDTH = 5 * A_WIDTH + 2 * (Q_WIDTH + 2 * KV_WIDTH)
N_EXPERTS = 16
N_GROUPS = 4
EXPERTS_PER_GROUP = N_EXPERTS // N_GROUPS
D_EXPERT = 256
EPS = 1e-6
MASK_VALUE = -1e30
TINY = 1e-30
LOG2E = 1.4426950408889634

LANES = 128
SUB = 16
CHUNK = 128
HGRN_SEQS = 2
TOKEN_TILE = 512
MOE_BLOCK = 128
KEY_CHUNK = 512
COND_ROWS = 16
VMEM_LIMIT = 58 * 1024 * 1024

_NT = (((1,), (1,)), ((), ()))
_TN = (((0,), (0,)), ((), ()))


def _cparams(*sem):
    return pltpu.CompilerParams(dimension_semantics=sem, vmem_limit_bytes=VMEM_LIMIT)


def _dot(a, b):
    return jnp.dot(a, b, preferred_element_type=F32)


def _split2(x):
    h = x.astype(BF16)
    return h, (x - h.astype(F32)).astype(BF16)


def _split3(x):
    h = x.astype(BF16)
    r = x - h.astype(F32)
    m = r.astype(BF16)
    l = (r - m.astype(F32)).astype(BF16)
    return h, m, l


def _dot_sel(c, x):
    h, m, l = _split3(x)
    return _dot(c, h) + _dot(c, m) + _dot(c, l)


def _mm(a, b, precise, dims=None):
    if dims is None:
        dims = (((a.ndim - 1,), (0,)), ((), ()))
    dg = lambda x, y: lax.dot_general(x, y, dims, preferred_element_type=F32)
    if not precise:
        return dg(a.astype(BF16), b.astype(BF16))
    ah, al = _split2(a)
    bh, bl = _split2(b)
    return dg(ah, bh) + dg(ah, bl) + dg(al, bh)


def _mm_sel(a, sel, precise):
    if not precise:
        return _dot(a.astype(BF16), sel)
    ah, al = _split2(a)
    return _dot(ah, sel) + _dot(al, sel)


def _sigmoid(x):
    return jax.nn.sigmoid(x)


def _silu(x):
    return x * jax.nn.sigmoid(x)


def _mod_kernel(cond_ref, w_ref, b_ref, o_ref):
    o_ref[...] = _mm(_silu(cond_ref[...]), w_ref[...], True) + b_ref[...]


def _modulation(cond, mod_w, mod_b):
    depth, d, width = mod_w.shape
    tn = 1536
    return pl.pallas_call(
        _mod_kernel,
        grid=(depth, width // tn),
        in_specs=[
            pl.BlockSpec((COND_ROWS, d), lambda l, j: (0, 0)),
            pl.BlockSpec((None, d, tn), lambda l, j: (l, 0, j)),
            pl.BlockSpec((None, 1, tn), lambda l, j: (l, 0, j)),
        ],
        out_specs=pl.BlockSpec((None, COND_ROWS, tn), lambda l, j: (l, 0, j)),
        out_shape=jax.ShapeDtypeStruct((depth, COND_ROWS, width), F32),
        compiler_params=_cparams("parallel", "parallel"),
        name="modulation",
    )(cond, mod_w, mod_b.reshape(depth, 1, width))


def _head_rms(xb, g2, ones_blk):
    ss = _mm_sel(xb * xb, ones_blk, True)
    return xb * lax.rsqrt(ss * (1.0 / HEAD_DIM) + EPS) * g2


def _rope(yb, cos, sin_signed, first16):
    partner = jnp.where(first16, pltpu.roll(yb, LANES - 16, 1), pltpu.roll(yb, 16, 1))
    return yb * cos + partner * sin_signed


def _q_attention_layout(blocks, lane_half):
    out = []
    for j in range(2 * len(blocks)):
        src = blocks[j // 2]
        dst_half = j // GQA_GROUP
        if j % 2 != dst_half:
            src = pltpu.roll(src, HEAD_DIM, 1)
        out.append(jnp.where(lane_half == dst_half, src, 0.0))
    return jnp.concatenate(out, axis=1)


def _inproj_kernel(*refs, precise_ctx, lat_tiles):
    if precise_ctx:
        (xl_ref, xc_ref, mod_ref, g1_ref, w_ref, wlo_ref, cos_ref, sin_ref, qg_ref, kg_ref, ones_ref,
         za_ref, qga_ref, kvg_ref, qwa_ref, kvw_ref, kv32_ref, qg32_ref, qw32_ref, z_ref) = refs
    else:
        (xl_ref, xc_ref, mod_ref, g1_ref, w_ref, cos_ref, sin_ref, qg_ref, kg_ref, ones_ref,
         za_ref, qga_ref, kvg_ref, qwa_ref, kvw_ref, kv32_ref, qg32_ref, qw32_ref) = refs
    is_lat = pl.program_id(0) < lat_tiles
    x = jnp.where(is_lat, xl_ref[...], xc_ref[...])
    mod = mod_ref[...]
    gain = g1_ref[...] * (1.0 + mod[:, D_MODEL:2 * D_MODEL])
    h = x * lax.rsqrt(jnp.mean(x * x, axis=-1, keepdims=True) + EPS) * gain + mod[:, 0:D_MODEL]

    mix_width = Q_WIDTH + 2 * KV_WIDTH
    col_g, col_w = 5 * A_WIDTH, 5 * A_WIDTH + mix_width
    if precise_ctx:
        def one_pass():
            z_ref[...] = _dot(h.astype(BF16), w_ref[...])

        def three_pass():
            hh, hl = _split2(h)
            z_ref[...] = _dot(hh, w_ref[...]) + _dot(hl, w_ref[...]) + _dot(hh, wlo_ref[...])

        pl.when(is_lat)(one_pass)
        pl.when(jnp.logical_not(is_lat))(three_pass)
        z = z_ref[...]
        z_g, z_w = z[:, col_g:col_w], z[:, col_w:]
        za_ref[...] = z[:, :col_g]
    else:
        h_b = h.astype(BF16)
        z_g = _dot(h_b, w_ref[:, col_g:col_w])
        z_w = _dot(h_b, w_ref[:, col_w:])
        za_ref[...] = _dot(h_b, w_ref[:, :col_g])

    cos = cos_ref[...]
    sin = sin_ref[...]
    ones_blk = ones_ref[...]
    lane = lax.broadcasted_iota(jnp.int32, (x.shape[0], LANES), 1)
    first16 = (lane % 32) < 16
    lane_half = lane // HEAD_DIM
    scale = LOG2E * HEAD_DIM ** -0.5

    qg = [z_g[:, LANES * b:LANES * (b + 1)] for b in range(3)]
    kg = z_g[:, Q_WIDTH:Q_WIDTH + KV_WIDTH]
    vg = z_g[:, Q_WIDTH + KV_WIDTH:]
    qw = [z_w[:, LANES * b:LANES * (b + 1)] for b in range(3)]
    kw = z_w[:, Q_WIDTH:Q_WIDTH + KV_WIDTH]
    vw = z_w[:, Q_WIDTH + KV_WIDTH:]

    qg = [_rope(_head_rms(b, qg_ref[...], ones_blk), cos, sin, first16) * scale for b in qg]
    kg = _rope(_head_rms(kg, kg_ref[...], ones_blk), cos, sin, first16)
    qw = [_rope(b, cos, sin, first16) * scale for b in qw]
    kw = _rope(kw, cos, sin, first16)

    qg_att = _q_attention_layout(qg, lane_half)
    qw_att = _q_attention_layout(qw, lane_half)
    qga_ref[...] = qg_att.astype(BF16)
    qwa_ref[...] = qw_att.astype(BF16)
    kvg_ref[...] = jnp.concatenate([kg, vg], axis=1).astype(BF16)
    kvw_ref[...] = jnp.concatenate([kw, vw], axis=1).astype(BF16)
    kv32_ref[...] = jnp.concatenate([kg, vg, kw, vw], axis=1)
    qg32_ref[...] = qg_att
    qw32_ref[...] = qw_att


def _token_specs(x_pair, tm, lat_tiles):
    off = lat_tiles if x_pair[1] is x_pair[0] else 0
    lat = pl.BlockSpec((tm, D_MODEL), lambda i, *_: (jnp.minimum(i, lat_tiles - 1), 0))
    ctx = pl.BlockSpec((tm, D_MODEL), lambda i, *_: (jnp.maximum(i, lat_tiles) - lat_tiles + off, 0))
    return [lat, ctx]


def _in_projection(x_pair, mod, g1, w_hi, w_lo, cos_t, sin_t, qn_g, kn_g, ones_blk, dims):
    n = dims["n_lat"] + dims["n_ctx"]
    tm = TOKEN_TILE
    n_tiles = n // tm
    lat_tiles = dims["n_lat"] // tm
    ctx_tiles = n_tiles - lat_tiles
    tiles_per_seq = dims["t_lat"] // tm
    ctx_cond = dims["ctx_cond"]
    precise_ctx = w_lo is not None

    def cond_idx(i):
        return jnp.where(i < lat_tiles, i // tiles_per_seq, ctx_cond)

    def rope_idx(i):
        return jnp.where(i < lat_tiles, 1 + i % tiles_per_seq, 0)

    def ctx_idx(i):
        return jnp.where(i < lat_tiles, ctx_tiles, i - lat_tiles)

    row = lambda w: pl.BlockSpec((tm, w), lambda i: (i, 0))
    ctx_row = lambda w: pl.BlockSpec((tm, w), lambda i: (ctx_idx(i), 0))
    const = lambda a: pl.BlockSpec(a.shape, lambda i: (0,) * a.ndim, pipeline_mode=pl.Buffered(1))
    weights = [w_hi, w_lo] if precise_ctx else [w_hi]
    ctx_rows = (ctx_tiles + 1) * tm
    return pl.pallas_call(
        functools.partial(_inproj_kernel, precise_ctx=precise_ctx, lat_tiles=lat_tiles),
        grid=(n_tiles,),
        in_specs=_token_specs(x_pair, tm, lat_tiles) + [
            pl.BlockSpec((None, 1, 6 * D_MODEL), lambda i: (cond_idx(i), 0, 0)),
            const(g1)] + [const(w) for w in weights] + [
            pl.BlockSpec((tm, LANES), lambda i: (rope_idx(i), 0)),
            pl.BlockSpec((tm, LANES), lambda i: (rope_idx(i), 0)),
            const(qn_g), const(kn_g), const(ones_blk),
        ],
        out_specs=[
            row(5 * A_WIDTH), row(2 * Q_WIDTH), row(2 * KV_WIDTH), row(2 * Q_WIDTH), row(2 * KV_WIDTH),
            ctx_row(4 * KV_WIDTH), ctx_row(2 * Q_WIDTH), ctx_row(2 * Q_WIDTH),
        ],
        out_shape=[
            jax.ShapeDtypeStruct((n, 5 * A_WIDTH), F32),
            jax.ShapeDtypeStruct((n, 2 * Q_WIDTH), BF16),
            jax.ShapeDtypeStruct((n, 2 * KV_WIDTH), BF16),
            jax.ShapeDtypeStruct((n, 2 * Q_WIDTH), BF16),
            jax.ShapeDtypeStruct((n, 2 * KV_WIDTH), BF16),
            jax.ShapeDtypeStruct((ctx_rows, 4 * KV_WIDTH), F32),
            jax.ShapeDtypeStruct((ctx_rows, 2 * Q_WIDTH), F32),
            jax.ShapeDtypeStruct((ctx_rows, 2 * Q_WIDTH), F32),
        ],
        scratch_shapes=[pltpu.VMEM((tm, IN_WIDTH), F32)] if precise_ctx else [],
        compiler_params=_cparams("arbitrary"),
        name="in_projection_precise_ctx" if precise_ctx else "in_projection",
    )(*x_pair, mod, g1, *weights, cos_t, sin_t, qn_g, kn_g, ones_blk)


def _hgrn_chunk(qa, v, fz, lb, st, consts, reverse, precise):
    cum, e_sel, x_sel, sub_mask, p_mask, v_mask, head_mask = consts
    log2e = LOG2E
    n_sub, half = CHUNK // SUB, SUB // 2
    q = _silu(qa)
    f = lb + (1.0 - lb) * _sigmoid(fz)
    logf2 = jnp.log(jnp.maximum(f, TINY)) * log2e
    kk = (1.0 - lb) * _sigmoid(-fz)

    b = _dot_sel(cum, logf2)
    b3 = b.reshape(n_sub, SUB, A_WIDTH)
    end = 0 if reverse else SUB - 1
    tot = jnp.broadcast_to(b3[:, end:end + 1, :], (n_sub, SUB, A_WIDTH)).reshape(CHUNK, A_WIDTH)
    q_in = q * jnp.exp2(b)
    k_out = kk * jnp.exp2(tot - b)
    d_sub = jnp.exp2(tot)

    c3 = (b - jnp.log(kk) * log2e).reshape(n_sub, SUB, A_WIDTH)
    b4 = b.reshape(n_sub, 2, half, A_WIDTH)
    q4 = q.reshape(n_sub, 2, half, A_WIDTH)
    r_local = lax.broadcasted_iota(jnp.int32, (n_sub, half, A_WIDTH), 1)
    zero_half = jnp.zeros((n_sub, half, A_WIDTH), F32)
    pieces = []
    for s in range(SUB):
        cs = c3[:, s:s + 1, :]
        parts = []
        for hsel in range(2):
            lo, hi = hsel * half, hsel * half + half - 1
            if (lo > s) if reverse else (hi < s):
                parts.append(zero_half)
                continue
            arg = b4[:, hsel] - cs
            if not ((hi <= s) if reverse else (lo >= s)):
                t_local = r_local + lo
                arg = jnp.where((t_local <= s) if reverse else (t_local >= s), arg, MASK_VALUE)
            parts.append(q4[:, hsel] * jnp.exp2(arg))
        piece = jnp.stack(parts, axis=1).reshape(CHUNK, A_WIDTH)
        pieces.append(piece if precise else piece.astype(BF16))
    w_cat = jnp.concatenate(pieces, axis=1)
    s_local = _mm_sel(w_cat, e_sel, precise)
    p = _mm_sel(s_local, x_sel, precise) * p_mask
    v_bd = jnp.concatenate([v] * A_HEADS, axis=0) * v_mask
    o_intra = _mm(p, v_bd, precise)

    v_op = v if precise else v.astype(BF16)
    k_op = k_out if precise else k_out.astype(BF16)
    q_op = q_in if precise else q_in.astype(BF16)
    sub_m = sub_mask
    order = range(n_sub - 1, -1, -1) if reverse else range(n_sub)
    new_st, o_inter = [], []
    for p in range(A_WIDTH // LANES):
        lanes = slice(LANES * p, LANES * (p + 1))
        k_exp = jnp.concatenate([k_op[:, lanes]] * n_sub, axis=1) * sub_m
        u_all = _mm(v_op[:, lanes], k_exp, precise, _TN)
        s = st[p]
        s_before = [None] * n_sub
        for j in order:
            s_before[j] = s
            s = d_sub[SUB * j:SUB * j + 1, lanes] * s + head_mask * u_all[:, LANES * j:LANES * (j + 1)]
        new_st.append(s)
        q_exp = jnp.concatenate([q_op[:, lanes]] * n_sub, axis=1) * sub_m
        o_inter.append(_mm(q_exp, jnp.concatenate(s_before, axis=1), precise, _NT))
    return o_intra + jnp.concatenate(o_inter, axis=1), jnp.stack(new_st, axis=0)


def _hgrn_kernel(*refs, has_s0, want_state, precise):
    refs = list(refs)
    seq_refs = [refs[6 * sq:6 * sq + 6] for sq in range(HGRN_SEQS)]
    del refs[:6 * HGRN_SEQS]
    lb_ref = refs.pop(0)
    s0_ref = refs.pop(0) if has_s0 else None
    cumf_ref, cumb_ref, e_ref, x_ref, sm_ref, pm_ref, vm_ref, hm_ref, of_ref, ob_ref = refs[:10]
    del refs[:10]
    sout_ref = refs.pop(0) if want_state else None
    st_ref, = refs
    c = pl.program_id(1)

    @pl.when(c == 0)
    def _():
        st_ref[...] = s0_ref[...] if has_s0 else jnp.zeros_like(st_ref)

    shared = (e_ref[...], x_ref[...], sm_ref[...], pm_ref[...], vm_ref[...], hm_ref[...])
    lb = lb_ref[...]
    for sq in range(HGRN_SEQS):
        qaf_ref, iaf_ref, fzf_ref, qab_ref, iab_ref, fzb_ref = seq_refs[sq]
        o_f, st_f = _hgrn_chunk(qaf_ref[...], iaf_ref[...], fzf_ref[...], lb[0:1], st_ref[sq, 0],
                                (cumf_ref[...],) + shared, False, precise)
        o_b, st_b = _hgrn_chunk(qab_ref[...], iab_ref[...], fzb_ref[...], lb[1:2], st_ref[sq, 1],
                                (cumb_ref[...],) + shared, True, precise)
        of_ref[sq] = o_f
        ob_ref[sq] = o_b
        st_ref[sq, 0] = st_f
        st_ref[sq, 1] = st_b

    if want_state:
        @pl.when(c == pl.num_programs(1) - 1)
        def _():
            sout_ref[...] = st_ref[...]


def _hgrn_consts():
    r = np.arange(CHUNK)
    same = (r[:, None] // SUB) == (r[None, :] // SUB)
    cum_f = (same & (r[None, :] <= r[:, None])).astype(np.float32)
    cum_b = (same & (r[None, :] >= r[:, None])).astype(np.float32)
    tot = same.astype(np.float32)
    e = np.zeros((SUB, A_HEADS, HEAD_DIM, A_HEADS, SUB), np.float32)
    for s in range(SUB):
        for h in range(A_HEADS):
            e[s, h, :, h, s] = 1.0
    e = e.reshape(SUB * A_WIDTH, A_HEADS * SUB)
    x = np.zeros((A_HEADS, SUB, A_HEADS, CHUNK), np.float32)
    for h in range(A_HEADS):
        for s in range(CHUNK):
            x[h, s % SUB, h, s] = 1.0
    x = x.reshape(A_HEADS * SUB, A_HEADS * CHUNK)
    p_mask = np.tile(tot, (1, A_HEADS))
    hv = np.arange(A_HEADS * CHUNK) // CHUNK
    hk = np.arange(A_WIDTH) // HEAD_DIM
    v_mask = (hv[:, None] == hk[None, :]).astype(np.float32)
    head_mask = (hk[:LANES, None] == hk[None, :LANES]).astype(np.float32)
    sub_mask = np.repeat(r[:, None] // SUB == np.arange(CHUNK // SUB)[None, :], LANES, axis=1).astype(np.float32)
    sel = [jnp.asarray(a, BF16) for a in (cum_f, cum_b, e, x, sub_mask)]
    return sel + [jnp.asarray(a, F32) for a in (p_mask, v_mask, head_mask)]


def _hgrn_scan(za, first_row, lb2, s0_t, n_seq, t, precise, want_state, name):
    nc = t // CHUNK
    groups = n_seq // HGRN_SEQS
    base = first_row // CHUNK
    za3 = za.reshape(za.shape[0] // CHUNK, CHUNK, za.shape[1])
    consts = _hgrn_consts()
    state_shape = (HGRN_SEQS, 2, A_WIDTH // LANES, LANES, LANES)
    const = lambda a: pl.BlockSpec(a.shape, lambda p, c: (0,) * a.ndim)

    def chunk_spec(sq, col, backward):
        def index(p, c):
            return (base + (p * HGRN_SEQS + sq) * nc + (nc - 1 - c if backward else c), 0, col)
        return pl.BlockSpec((None, CHUNK, A_WIDTH), index)

    fwd = lambda col: pl.BlockSpec((None, HGRN_SEQS, CHUNK, A_WIDTH), lambda p, c: (p, 0, c, col))
    bwd = lambda col: pl.BlockSpec((None, HGRN_SEQS, CHUNK, A_WIDTH), lambda p, c: (p, 0, nc - 1 - c, col))
    state_spec = pl.BlockSpec((None,) + state_shape, lambda p, c: (p, 0, 0, 0, 0, 0))
    operands = [za3] * (6 * HGRN_SEQS) + [lb2]
    in_specs = [chunk_spec(sq, col, backward) for sq in range(HGRN_SEQS)
                for col, backward in ((0, False), (1, False), (2, False), (0, True), (1, True), (3, True))]
    in_specs.append(const(lb2))
    if s0_t is not None:
        operands.append(s0_t.reshape((groups,) + state_shape))
        in_specs.append(state_spec)
    o_shape = jax.ShapeDtypeStruct((groups, HGRN_SEQS, t, A_WIDTH), F32)
    out_specs, out_shape = [fwd(0), bwd(0)], [o_shape, o_shape]
    if want_state:
        out_specs.append(state_spec)
        out_shape.append(jax.ShapeDtypeStruct((groups,) + state_shape, F32))
    outs = pl.pallas_call(
        functools.partial(_hgrn_kernel, has_s0=s0_t is not None, want_state=want_state, precise=precise),
        grid=(groups, nc),
        in_specs=in_specs + [const(c) for c in consts],
        out_specs=out_specs,
        out_shape=out_shape,
        scratch_shapes=[pltpu.VMEM(state_shape, F32)],
        compiler_params=_cparams("parallel", "arbitrary"),
        name=name,
    )(*operands, *consts)
    o_f, o_b = (o.reshape(n_seq * t, A_WIDTH) for o in outs[:2])
    s_fin = outs[2].reshape((n_seq,) + state_shape[1:]) if want_state else None
    return o_f, o_b, s_fin


def _attend(q3, segments, sink_row, precise=False):
    scores = []
    for k, _, valid in segments:
        s = _mm(k, q3, precise, _NT)
        if valid is not None:
            s = jnp.where(valid, s, MASK_VALUE)
        scores.append(s)
    m = scores[0].max(axis=0, keepdims=True)
    for s in scores[1:]:
        m = jnp.maximum(m, s.max(axis=0, keepdims=True))
    if sink_row is not None:
        m = jnp.maximum(m, sink_row)
    denom = jnp.exp2(sink_row - m) if sink_row is not None else 0.0
    acc = 0.0
    for s, (_, v, _) in zip(scores, segments):
        p = jnp.exp2(s - m)
        denom = denom + p.sum(axis=0, keepdims=True)
        acc = acc + _mm(v, p, precise, _TN)
    return (acc / denom).T


def _stack_heads(q, kvh):
    base = kvh * GQA_GROUP
    return jnp.concatenate([q[:, LANES * (base + g):LANES * (base + g + 1)] for g in range(GQA_GROUP)], axis=0)


def _merge_heads(res, tq):
    lane = lax.broadcasted_iota(jnp.int32, (tq, LANES), 1)
    heads = []
    for j in range(B_Q_HEADS):
        kvh, g = divmod(j, GQA_GROUP)
        o = res[kvh][g * tq:(g + 1) * tq]
        if kvh != j % 2:
            o = pltpu.roll(o, HEAD_DIM, 1)
        heads.append(o)
    blocks = [jnp.where(lane < HEAD_DIM, heads[2 * b], heads[2 * b + 1]) for b in range(B_Q_HEADS // 2)]
    return jnp.concatenate(blocks, axis=1)


def _sink_row(sink_ref, kvh, tq):
    return jnp.concatenate(
        [jnp.full((1, tq), sink_ref[kvh * GQA_GROUP + g] * LOG2E, F32) for g in range(GQA_GROUP)], axis=1)


def _global_attn_kernel(q_ref, kv_ref, ck_ref, cv_ref, o_ref, *, tq):
    q = q_ref[...]
    kv = kv_ref[...]
    k, v = kv[:, :KV_WIDTH], kv[:, KV_WIDTH:]
    ck, cv = ck_ref[...], cv_ref[...]
    segments = [(k[c:c + KEY_CHUNK], v[c:c + KEY_CHUNK], None) for c in range(0, k.shape[0], KEY_CHUNK)]
    segments.append((ck, cv, None))
    res = [_attend(_stack_heads(q, kvh), segments, None) for kvh in range(B_KV_HEADS)]
    o_ref[...] = _merge_heads(res, tq).astype(BF16)


def _global_attention(q_att, kv, ck, cv, dims):
    t, n_seq = dims["t_lat"], dims["n_lat_seq"]
    tq = 256
    nq = t // tq
    return pl.pallas_call(
        functools.partial(_global_attn_kernel, tq=tq),
        grid=(n_seq, nq),
        in_specs=[
            pl.BlockSpec((tq, 2 * Q_WIDTH), lambda b, i: (b * nq + i, 0)),
            pl.BlockSpec((t, 2 * KV_WIDTH), lambda b, i: (b, 0)),
            pl.BlockSpec((None,) + ck.shape[1:], lambda b, i: (b, 0, 0)),
            pl.BlockSpec((None,) + cv.shape[1:], lambda b, i: (b, 0, 0)),
        ],
        out_specs=pl.BlockSpec((tq, Q_WIDTH), lambda b, i: (b * nq + i, 0)),
        out_shape=jax.ShapeDtypeStruct((n_seq * t, Q_WIDTH), BF16),
        compiler_params=_cparams("parallel", "arbitrary"),
        name="global_attention",
    )(q_att, kv, ck, cv)


def _window_attn_kernel(sink_ref, q_ref, kv_ref, ck_ref, cv_ref, o_ref, *, tq, t):
    j = pl.program_id(1)
    span = tq + 2 * WINDOW
    start = pl.multiple_of(jnp.clip(j * tq - WINDOW, 0, t - span), WINDOW)
    q = q_ref[...]
    kv = kv_ref[pl.ds(start, span), :]
    k, v = kv[:, :KV_WIDTH], kv[:, KV_WIDTH:]
    key_pos = start + lax.broadcasted_iota(jnp.int32, (span, GQA_GROUP * tq), 0)
    query_pos = j * tq + lax.broadcasted_iota(jnp.int32, (span, GQA_GROUP * tq), 1) % tq
    valid = jnp.abs(query_pos - key_pos) <= WINDOW
    segments = [(k, v, valid), (ck_ref[...], cv_ref[...], None)]
    res = [_attend(_stack_heads(q, kvh), segments, _sink_row(sink_ref, kvh, tq)) for kvh in range(C_KV_HEADS)]
    o_ref[...] = _merge_heads(res, tq).astype(BF16)


def _window_attention(sink, q_att, kv, ck, cv, dims):
    t, n_seq = dims["t_lat"], dims["n_lat_seq"]
    tq = 2 * WINDOW
    nq = t // tq
    return pl.pallas_call(
        functools.partial(_window_attn_kernel, tq=tq, t=t),
        grid=(n_seq, nq),
        in_specs=[
            pl.BlockSpec(memory_space=pltpu.SMEM),
            pl.BlockSpec((tq, 2 * Q_WIDTH), lambda b, i: (b * nq + i, 0)),
            pl.BlockSpec((t, 2 * KV_WIDTH), lambda b, i: (b, 0)),
            pl.BlockSpec((None,) + ck.shape[1:], lambda b, i: (b, 0, 0)),
            pl.BlockSpec((None,) + cv.shape[1:], lambda b, i: (b, 0, 0)),
        ],
        out_specs=pl.BlockSpec((tq, Q_WIDTH), lambda b, i: (b * nq + i, 0)),
        out_shape=jax.ShapeDtypeStruct((n_seq * t, Q_WIDTH), BF16),
        compiler_params=_cparams("parallel", "arbitrary"),
        name="window_attention",
    )(sink, q_att, kv, ck, cv)


def _context_attn_kernel(sink_ref, qg_ref, qw_ref, kv_ref, og_ref, ow_ref, *, t, precise):
    kv = kv_ref[...]
    for q_ref, o_ref, off, use_sink in ((qg_ref, og_ref, 0, False), (qw_ref, ow_ref, 2 * KV_WIDTH, True)):
        q = q_ref[...]
        k, v = kv[:, off:off + KV_WIDTH], kv[:, off + KV_WIDTH:off + 2 * KV_WIDTH]
        res = [_attend(_stack_heads(q, kvh), [(k, v, None)],
                       _sink_row(sink_ref, kvh, t) if use_sink else None, precise)
               for kvh in range(B_KV_HEADS)]
        o_ref[...] = _merge_heads(res, t)


def _context_attention(sink, qg32, qw32, kv32, dims, precise):
    t, n_seq = dims["t_ctx"], dims["n_ctx_seq"]
    seq = lambda w: pl.BlockSpec((t, w), lambda b: (b, 0))
    return pl.pallas_call(
        functools.partial(_context_attn_kernel, t=t, precise=precise),
        grid=(n_seq,),
        in_specs=[pl.BlockSpec(memory_space=pltpu.SMEM), seq(2 * Q_WIDTH), seq(2 * Q_WIDTH), seq(4 * KV_WIDTH)],
        out_specs=[seq(Q_WIDTH), seq(Q_WIDTH)],
        out_shape=[jax.ShapeDtypeStruct((n_seq * t, Q_WIDTH), F32)] * 2,
        compiler_params=_cparams("parallel"),
        name="context_attention_precise" if precise else "context_attention",
    )(sink, qg32, qw32, kv32)


def _route(scores_t, sel_t):
    s = [scores_t[e:e + 1, :] for e in range(N_EXPERTS)]
    z = [sel_t[e:e + 1, :] for e in range(N_EXPERTS)]
    gs = []
    for g in range(N_GROUPS):
        m = z[g * EXPERTS_PER_GROUP:(g + 1) * EXPERTS_PER_GROUP]
        best = None
        for a in range(EXPERTS_PER_GROUP):
            for b in range(a + 1, EXPERTS_PER_GROUP):
                pair = m[a] + m[b]
                best = pair if best is None else jnp.maximum(best, pair)
        gs.append(best)
    combine, groups = [], []
    for g in range(N_GROUPS):
        chosen_g = None
        for g2 in range(N_GROUPS):
            if g2 == g:
                continue
            c = (gs[g] > gs[g2]) if g2 < g else (gs[g] >= gs[g2])
            chosen_g = c if chosen_g is None else jnp.logical_and(chosen_g, c)
        base = g * EXPERTS_PER_GROUP
        picked = []
        for a in range(EXPERTS_PER_GROUP):
            rank = 0.0
            for b in range(EXPERTS_PER_GROUP):
                if b == a:
                    continue
                ahead = (z[base + b] >= z[base + a]) if b < a else (z[base + b] > z[base + a])
                rank = rank + ahead.astype(F32)
            picked.append(jnp.where(jnp.logical_and(chosen_g, rank < 2.0), s[base + a], 0.0))
        denom = picked[0] + picked[1] + picked[2] + picked[3]
        denom = jnp.where(chosen_g, denom, 1.0)
        combine.extend(pk / denom for pk in picked)
        groups.append(jnp.where(chosen_g, 1.0, 0.0))
    return jnp.concatenate(combine, axis=0), jnp.concatenate(groups, axis=0)


def _outproj_kernel(*refs, lat_tiles, precise_ctx):
    if precise_ctx:
        (ofl_ref, ofc_ref, obl_ref, obc_ref, ga_ref,
         ogl_ref, ogc_ref, owl_ref, owc_ref, xl_ref, xc_ref, mod_ref, ag_ref,
         ones_ref, w_ref, wlo_ref, g2_ref, rw_ref, rb_ref, before_ref,
         x1_ref, h2_ref, comb_ref, pos_ref, cnt_ref, mix_ref) = refs
    else:
        (ofl_ref, ofc_ref, obl_ref, obc_ref, ga_ref,
         ogl_ref, ogc_ref, owl_ref, owc_ref, xl_ref, xc_ref, mod_ref, ag_ref,
         ones_ref, w_ref, g2_ref, rw_ref, rb_ref, before_ref,
         x1_ref, h2_ref, comb_ref, pos_ref, cnt_ref, mix_ref) = refs
        wlo_ref = None
    is_lat = pl.program_id(0) < lat_tiles
    o = jnp.where(is_lat, ofl_ref[...] + obl_ref[...], ofc_ref[...] + obc_ref[...])
    oa = _head_rms(o, ag_ref[...], ones_ref[...]) * _sigmoid(ga_ref[...])
    def mix(operands, precise):
        if precise:
            ah, al = _split2(jnp.concatenate(operands, axis=1))
            mix_ref[...] = _dot(ah, w_ref[...]) + _dot(al, w_ref[...]) + _dot(ah, wlo_ref[...])
        else:
            mix_ref[...] = _dot(jnp.concatenate([a.astype(BF16) for a in operands], axis=1), w_ref[...])

    pl.when(is_lat)(lambda: mix((oa, ogl_ref[...], owl_ref[...]), False))
    pl.when(jnp.logical_not(is_lat))(lambda: mix((oa, ogc_ref[...], owc_ref[...]), precise_ctx))

    mod = mod_ref[...]
    gate1 = mod[:, 2 * D_MODEL:3 * D_MODEL]
    sh2 = mod[:, 3 * D_MODEL:4 * D_MODEL]
    sc2 = mod[:, 4 * D_MODEL:5 * D_MODEL]
    x1 = jnp.where(is_lat, xl_ref[...], xc_ref[...]) + gate1 * mix_ref[...]
    x1_ref[...] = x1
    gain = g2_ref[...] * (1.0 + sc2)
    h2 = x1 * lax.rsqrt(jnp.mean(x1 * x1, axis=-1, keepdims=True) + EPS) * gain + sh2
    h2_ref[...] = h2.astype(BF16)
    scores_t = _sigmoid(_mm(rw_ref[...], h2, True, _NT))
    comb_t, group_t = _route(scores_t, scores_t + rb_ref[...])
    comb_ref[...] = comb_t
    rank = _dot(group_t.astype(BF16), before_ref[...])
    counts = group_t.sum(axis=1, keepdims=True)
    pos, offset = 0.0, 0.0
    for g in range(N_GROUPS):
        pos = pos + group_t[g:g + 1] * (rank[g:g + 1] + offset)
        offset = offset + counts[g:g + 1]
    pos_ref[...] = pos
    pad = jnp.zeros((8 - N_GROUPS, 1), F32)
    cnt_ref[...] = jnp.broadcast_to(jnp.concatenate([counts, pad], axis=0), (8, LANES)).astype(jnp.int32)


def _out_projection(o_f, o_b, za, og_lat, og_ctx, ow_lat, ow_ctx, x_pair, mod, an_g, ones_blk, w_hi, w_lo, g2,
                    rw_t, rb, dims):
    n = dims["n_lat"] + dims["n_ctx"]
    tm = TOKEN_TILE
    n_tiles = n // tm
    lat_tiles = dims["n_lat"] // tm
    tiles_per_seq = dims["t_lat"] // tm
    ctx_cond = dims["ctx_cond"]
    precise_ctx = w_lo is not None
    cond_idx = lambda i: jnp.where(i < lat_tiles, i // tiles_per_seq, ctx_cond)
    row = lambda w: pl.BlockSpec((tm, w), lambda i: (i, 0))
    lat = lambda w: pl.BlockSpec((tm, w), lambda i: (jnp.minimum(i, lat_tiles - 1), 0))
    ctx = lambda w: pl.BlockSpec((tm, w), lambda i: (jnp.maximum(i - lat_tiles, 0), 0))
    const = lambda a: pl.BlockSpec(a.shape, lambda i: (0,) * a.ndim, pipeline_mode=pl.Buffered(1))
    weights = [w_hi, w_lo] if precise_ctx else [w_hi]
    t_idx = np.arange(tm)
    before = jnp.asarray((t_idx[:, None] < t_idx[None, :]).astype(np.float32), BF16)
    return pl.pallas_call(
        functools.partial(_outproj_kernel, lat_tiles=lat_tiles, precise_ctx=precise_ctx),
        grid=(n_tiles,),
        in_specs=[
            lat(A_WIDTH), ctx(A_WIDTH), lat(A_WIDTH), ctx(A_WIDTH),
            pl.BlockSpec((tm, A_WIDTH), lambda i: (i, 4)),
            lat(Q_WIDTH), ctx(Q_WIDTH), lat(Q_WIDTH), ctx(Q_WIDTH)] + _token_specs(x_pair, tm, lat_tiles) + [
            pl.BlockSpec((None, 1, 6 * D_MODEL), lambda i: (cond_idx(i), 0, 0)),
            const(an_g), const(ones_blk)] + [const(w) for w in weights] + [
            const(g2), const(rw_t), const(rb), const(before),
        ],
        out_specs=[row(D_MODEL), row(D_MODEL), pl.BlockSpec((N_EXPERTS, tm), lambda i: (0, i)),
                   pl.BlockSpec((1, tm), lambda i: (0, i)), pl.BlockSpec((8, LANES), lambda i: (i, 0))],
        out_shape=[
            jax.ShapeDtypeStruct((n, D_MODEL), F32),
            jax.ShapeDtypeStruct((n, D_MODEL), BF16),
            jax.ShapeDtypeStruct((N_EXPERTS, n), F32),
            jax.ShapeDtypeStruct((1, n), F32),
            jax.ShapeDtypeStruct((n_tiles * 8, LANES), jnp.int32),
        ],
        scratch_shapes=[pltpu.VMEM((tm, D_MODEL), F32)],
        compiler_params=_cparams("arbitrary"),
        name="out_projection_precise_ctx" if precise_ctx else "out_projection",
    )(*o_f, *o_b, za, og_lat, og_ctx, ow_lat, ow_ctx, *x_pair, mod, an_g, ones_blk, *weights, g2, rw_t, rb, before)


def _moe_kernel(off_ref, h_ref, comb_ref, pos_ref, x1_ref, mod_ref, wgu_ref, wd_ref, fg_ref, *rest,
                final, lat_tiles):
    *outs, hs_ref, cs_ref, acc_ref = rest
    i = pl.program_id(0)
    tm = h_ref.shape[0]
    row = lax.broadcasted_iota(jnp.int32, (tm, tm), 0).astype(F32)
    perm = jnp.where(row == pos_ref[...], 1.0, 0.0).astype(BF16)
    hs_ref[...] = _dot(perm, h_ref[...]).astype(BF16)
    cs_ref[...] = sum(lax.dot_general(perm, c, _NT, preferred_element_type=F32)
                      for c in _split3(comb_ref[...]))
    acc_ref[...] = jnp.zeros_like(acc_ref)
    lane = lax.broadcasted_iota(jnp.int32, (MOE_BLOCK, N_EXPERTS), 1)

    for r in range(tm // MOE_BLOCK):
        rows = slice(r * MOE_BLOCK, (r + 1) * MOE_BLOCK)

        def group_body(g, carry, rows=rows, r=r):
            has_tokens = jnp.logical_and(off_ref[i, g] < (r + 1) * MOE_BLOCK, off_ref[i, g + 1] > r * MOE_BLOCK)

            @pl.when(has_tokens)
            def _():
                h = hs_ref[rows, :]
                cs = cs_ref[rows, :]
                acts = []
                for a in range(EXPERTS_PER_GROUP):
                    e = g * EXPERTS_PER_GROUP + a
                    gu = _dot(h, wgu_ref[e])
                    gate, up = gu[:, :D_EXPERT], gu[:, D_EXPERT:]
                    ce = jnp.sum(jnp.where(lane == e, cs, 0.0), axis=1, keepdims=True)
                    acts.append((_silu(gate) * up * ce).astype(BF16))
                acc_ref[rows, :] += _dot(jnp.concatenate(acts, axis=1), wd_ref[g])

            return carry

        lax.fori_loop(0, N_GROUPS, group_body, 0)

    y_hi, y_lo = _split2(acc_ref[...])
    y = (lax.dot_general(perm, y_hi, _TN, preferred_element_type=F32)
         + lax.dot_general(perm, y_lo, _TN, preferred_element_type=F32))
    x2 = x1_ref[...] + mod_ref[...][:, 5 * D_MODEL:] * y
    if not final:
        outs[0][...] = x2
    else:
        yn = x2 * lax.rsqrt(jnp.mean(x2 * x2, axis=-1, keepdims=True) + EPS) * fg_ref[...]
        is_lat = i < lat_tiles

        @pl.when(is_lat)
        def _():
            outs[0][...] = yn

        @pl.when(jnp.logical_not(is_lat))
        def _():
            outs[1][...] = yn


def _moe(h2, comb_t, pos, counts, x1, mod, w_gu, w_d, final_g, dims, final):
    n = x1.shape[0]
    tm = TOKEN_TILE
    lat_tiles = dims["n_lat"] // tm
    tiles_per_seq = dims["t_lat"] // tm
    ctx_cond = dims["ctx_cond"]
    cond_idx = lambda i: jnp.where(i < lat_tiles, i // tiles_per_seq, ctx_cond)
    offsets = jnp.concatenate([jnp.zeros((n // tm, 1), jnp.int32), jnp.cumsum(counts, axis=1)], axis=1)
    if final:
        out_specs = [pl.BlockSpec((tm, D_MODEL), lambda i, *_: (jnp.minimum(i, lat_tiles - 1), 0)),
                     pl.BlockSpec((tm, D_MODEL), lambda i, *_: (jnp.maximum(i - lat_tiles, 0), 0))]
        out_shape = [jax.ShapeDtypeStruct((dims["n_lat"], D_MODEL), F32),
                     jax.ShapeDtypeStruct((dims["n_ctx"], D_MODEL), F32)]
    else:
        out_specs = pl.BlockSpec((tm, D_MODEL), lambda i, *_: (i, 0))
        out_shape = jax.ShapeDtypeStruct((n, D_MODEL), F32)
    resident = lambda a: pl.BlockSpec(a.shape, lambda i, *_: (0,) * a.ndim, pipeline_mode=pl.Buffered(1))
    grid_spec = pltpu.PrefetchScalarGridSpec(
        num_scalar_prefetch=1,
        grid=(n // tm,),
        in_specs=[
            pl.BlockSpec((tm, D_MODEL), lambda i, *_: (i, 0)),
            pl.BlockSpec((N_EXPERTS, tm), lambda i, *_: (0, i)),
            pl.BlockSpec((1, tm), lambda i, *_: (0, i)),
            pl.BlockSpec((tm, D_MODEL), lambda i, *_: (i, 0)),
            pl.BlockSpec((None, 1, 6 * D_MODEL), lambda i, *_: (cond_idx(i), 0, 0)),
            resident(w_gu), resident(w_d),
            pl.BlockSpec((1, D_MODEL), lambda i, *_: (0, 0)),
        ],
        out_specs=out_specs,
        scratch_shapes=[pltpu.VMEM((tm, D_MODEL), BF16), pltpu.VMEM((tm, N_EXPERTS), F32),
                        pltpu.VMEM((tm, D_MODEL), F32)],
    )
    return pl.pallas_call(
        functools.partial(_moe_kernel, final=final, lat_tiles=lat_tiles),
        grid_spec=grid_spec,
        out_shape=out_shape,
        compiler_params=_cparams("arbitrary"),
        name="moe_final" if final else "moe",
    )(offsets, h2, comb_t, pos, x1, mod, w_gu, w_d, final_g)


def _rope_tables(t, tile):
    n_freq = HEAD_DIM // 4
    pos = np.arange(t)
    freqs = ROPE_THETA ** (-np.arange(n_freq, dtype=np.float64) / n_freq)
    ang_row = (pos // GRID_W)[:, None] * freqs
    ang_col = (pos % GRID_W)[:, None] * freqs
    cos_h = np.concatenate([np.cos(ang_row)] * 2 + [np.cos(ang_col)] * 2, axis=1)
    sin_h = np.concatenate([-np.sin(ang_row), np.sin(ang_row), -np.sin(ang_col), np.sin(ang_col)], axis=1)
    cos_t = np.concatenate([np.ones((tile, LANES)), np.concatenate([cos_h, cos_h], axis=1)], axis=0)
    sin_t = np.concatenate([np.zeros((tile, LANES)), np.concatenate([sin_h, sin_h], axis=1)], axis=0)
    return jnp.asarray(cos_t, F32), jnp.asarray(sin_t, F32)


def _head_ones(width):
    r = np.arange(width) // HEAD_DIM
    return jnp.asarray((r[:, None] == r[None, :]).astype(np.float32), BF16)


def _split_kernel(w_ref, hi_ref, lo_ref):
    hi, lo = _split2(w_ref[...])
    hi_ref[...] = hi
    lo_ref[...] = lo


def _hi_lo(w, want_lo):
    if not want_lo:
        return w.astype(BF16), None
    rows, width = w.shape
    tr = 256
    spec = pl.BlockSpec((tr, width), lambda i: (i, 0))
    return pl.pallas_call(
        _split_kernel,
        grid=(rows // tr,),
        in_specs=[spec],
        out_specs=[spec, spec],
        out_shape=[jax.ShapeDtypeStruct(w.shape, BF16)] * 2,
        compiler_params=_cparams("parallel"),
        name="split_weight",
    )(w)


def _state_to_block_diag_t(s):
    per_pair = LANES // HEAD_DIM
    st = jnp.swapaxes(s, -1, -2).reshape(s.shape[:2] + (A_HEADS // per_pair, per_pair, HEAD_DIM, HEAD_DIM))
    eye = jnp.eye(per_pair, dtype=s.dtype)
    out = jnp.einsum("bdphvk,hg->bdphvgk", st, eye)
    return out.reshape(s.shape[:2] + (A_HEADS // per_pair, LANES, LANES))


def _block_diag_t_to_state(st):
    per_pair = LANES // HEAD_DIM
    s7 = st.reshape(st.shape[:3] + (per_pair, HEAD_DIM, per_pair, HEAD_DIM))
    diag = jnp.stack([s7[:, :, :, h, :, h, :] for h in range(per_pair)], axis=3)
    return jnp.swapaxes(diag, -1, -2).reshape(st.shape[:2] + (A_HEADS, HEAD_DIM, HEAD_DIM))


def kernel(x_prompt, x_sample, cache_glob_k, cache_glob_v, cache_win_k, cache_win_v, state_hgrn, c, c_ctx,
           mod_w, mod_b, norm1_g, norm2_g, w_in, w_out, hgrn_lb, hgrn_norm_g, q_norm_g, k_norm_g, win_sink,
           router_w, router_b, w_gate_up, w_down, final_g):
    n_ctx_seq, t_ctx, d = x_prompt.shape
    n_lat_seq, t_lat, _ = x_sample.shape
    depth = mod_w.shape[0]
    past = cache_glob_k.shape[2]
    n_lat, n_ctx = n_lat_seq * t_lat, n_ctx_seq * t_ctx
    dims = dict(n_ctx_seq=n_ctx_seq, t_ctx=t_ctx, n_lat_seq=n_lat_seq, t_lat=t_lat,
                n_lat=n_lat, n_ctx=n_ctx, ctx_cond=n_lat_seq)
    assert d == D_MODEL and n_lat_seq < COND_ROWS
    assert n_ctx % TOKEN_TILE == 0 and t_lat % TOKEN_TILE == 0
    assert t_ctx % CHUNK == 0 and t_lat >= 4 * WINDOW

    x_pair = (x_sample.reshape(n_lat, d), x_prompt.reshape(n_ctx, d))
    cond = jnp.concatenate([c, c_ctx[None, :], jnp.zeros((COND_ROWS - n_lat_seq - 1, d), F32)], axis=0)
    mod = _modulation(cond, mod_w, mod_b).reshape(depth, COND_ROWS, 1, 6 * d)

    p_lb = jax.nn.softmax(hgrn_lb.astype(F32), axis=1)
    lbs = jnp.cumsum(p_lb, axis=1) - p_lb[:, :1]

    cos_t, sin_t = _rope_tables(t_lat, TOKEN_TILE)
    ones128, ones256 = _head_ones(LANES), _head_ones(A_WIDTH)
    tile2 = lambda g, reps: jnp.tile(g, reps)[None, :]
    rw_t = router_w.T
    rb = router_b[:, None]
    fg = final_g[None, :]

    new_kv, new_state = [], []
    for l in range(depth):
        precise = l < depth - 1
        win_hi, win_lo = _hi_lo(w_in[l], precise)
        za, qg_att, kvg, qw_att, kvw, kv32, qg32, qw32 = _in_projection(
            x_pair, mod[l], norm1_g[l][None, :], win_hi, win_lo, cos_t, sin_t,
            tile2(q_norm_g[l], 2), tile2(k_norm_g[l], 2), ones128, dims)

        s0 = state_hgrn[:, l].astype(F32)
        of_lat, ob_lat, _ = _hgrn_scan(za, 0, lbs[:, l], _state_to_block_diag_t(s0), n_lat_seq, t_lat,
                                       False, False, "hgrn_latent")
        of_ctx, ob_ctx, s_fin = _hgrn_scan(za, n_lat, lbs[:, l], None, n_ctx_seq, t_ctx, precise, True,
                                           "hgrn_context_precise" if precise else "hgrn_context")

        cast_cache = lambda a: a[:, l].reshape(n_lat_seq, past, KV_WIDTH).astype(BF16)
        og_lat = _global_attention(qg_att, kvg, cast_cache(cache_glob_k), cast_cache(cache_glob_v), dims)
        ow_lat = _window_attention(win_sink[l], qw_att, kvw, cast_cache(cache_win_k),
                                   cast_cache(cache_win_v), dims)
        og_ctx, ow_ctx = _context_attention(win_sink[l], qg32, qw32, kv32, dims, precise)

        wout_hi, wout_lo = _hi_lo(w_out[l], precise)
        x1, h2, comb_t, pos, cnt = _out_projection(
            (of_lat, of_ctx), (ob_lat, ob_ctx), za, og_lat, og_ctx, ow_lat, ow_ctx, x_pair, mod[l],
            tile2(hgrn_norm_g[l], A_HEADS), ones256, wout_hi, wout_lo, norm2_g[l][None, :], rw_t, rb, dims)

        counts = cnt.reshape(-1, 8, LANES)[:, :N_GROUPS, 0]
        w_d = w_down[l].astype(BF16).reshape(N_GROUPS, EXPERTS_PER_GROUP * D_EXPERT, d)
        x = _moe(h2, comb_t, pos, counts, x1, mod[l], w_gate_up[l].astype(BF16), w_d, fg, dims,
                 final=(l == depth - 1))
        x_pair = (x, x)

        new_kv.append(kv32[:n_ctx].reshape(n_ctx_seq, t_ctx, 4, B_KV_HEADS, HEAD_DIM))
        new_state.append(_block_diag_t_to_state(s_fin))

    y_sample = x[0].reshape(n_lat_seq, t_lat, d)
    y_prompt = x[1].reshape(n_ctx_seq, t_ctx, d)
    kv = jnp.stack(new_kv, axis=1)
    return (y_prompt, y_sample, kv[:, :, :, 0], kv[:, :, :, 1], kv[:, :, :, 2], kv[:, :, :, 3],
            jnp.stack(new_state, axis=1).astype(x_prompt.dtype))
```

```python
import functools

import numpy as np
import jax
import jax.numpy as jnp
from jax import lax
from jax.experimental import pallas as pl
from jax.experimental.pallas import tpu as pltpu

F32 = jnp.float32
BF16 = jnp.bfloat16

D_MODEL = 1024
HEAD_DIM = 64
GRID_W = 64
A_HEADS = 4
A_WIDTH = A_HEADS * HEAD_DIM
B_Q_HEADS = 6
B_KV_HEADS = 2
C_Q_HEADS = 6
C_KV_HEADS = 2
Q_WIDTH = B_Q_HEADS * HEAD_DIM
KV_WIDTH = B_KV_HEADS * HEAD_DIM
GQA_GROUP = B_Q_HEADS // B_KV_HEADS
WINDOW = 128
ROPE_THETA = 10000.0
IN_WIDTH = 5 * A_WIDTH + 2 * (Q_WIDTH + 2 * KV_WIDTH)
N_EXPERTS = 16
N_GROUPS = 4
EXPERTS_PER_GROUP = N_EXPERTS // N_GROUPS
D_EXPERT = 256
EPS = 1e-6
MASK_VALUE = -1e30
TINY = 1e-30
LOG2E = 1.4426950408889634

LANES = 128
SUB = 16
CHUNK = 128
HGRN_SEQS = 2
TOKEN_TILE = 512
MOE_BLOCK = 128
KEY_CHUNK = 512
COND_ROWS = 16
VMEM_LIMIT = 58 * 1024 * 1024

_NT = (((1,), (1,)), ((), ()))
_TN = (((0,), (0,)), ((), ()))


def _cparams(*sem):
    return pltpu.CompilerParams(dimension_semantics=sem, vmem_limit_bytes=VMEM_LIMIT)


def _dot(a, b):
    return jnp.dot(a, b, preferred_element_type=F32)


def _split2(x):
    h = x.astype(BF16)
    return h, (x - h.astype(F32)).astype(BF16)


def _split3(x):
    h = x.astype(BF16)
    r = x - h.astype(F32)
    m = r.astype(BF16)
    l = (r - m.astype(F32)).astype(BF16)
    return h, m, l


def _dot_sel(c, x):
    h, m, l = _split3(x)
    return _dot(c, h) + _dot(c, m) + _dot(c, l)


def _mm(a, b, precise, dims=None):
    if dims is None:
        dims = (((a.ndim - 1,), (0,)), ((), ()))
    dg = lambda x, y: lax.dot_general(x, y, dims, preferred_element_type=F32)
    if not precise:
        return dg(a.astype(BF16), b.astype(BF16))
    ah, al = _split2(a)
    bh, bl = _split2(b)
    return dg(ah, bh) + dg(ah, bl) + dg(al, bh)


def _mm_sel(a, sel, precise):
    if not precise:
        return _dot(a.astype(BF16), sel)
    ah, al = _split2(a)
    return _dot(ah, sel) + _dot(al, sel)


def _sigmoid(x):
    return jax.nn.sigmoid(x)


def _silu(x):
    return x * jax.nn.sigmoid(x)


def _mod_kernel(cond_ref, w_ref, b_ref, o_ref):
    o_ref[...] = _mm(_silu(cond_ref[...]), w_ref[...], True) + b_ref[...]


def _modulation(cond, mod_w, mod_b):
    depth, d, width = mod_w.shape
    tn = 1536
    return pl.pallas_call(
        _mod_kernel,
        grid=(depth, width // tn),
        in_specs=[
            pl.BlockSpec((COND_ROWS, d), lambda l, j: (0, 0)),
            pl.BlockSpec((None, d, tn), lambda l, j: (l, 0, j)),
            pl.BlockSpec((None, 1, tn), lambda l, j: (l, 0, j)),
        ],
        out_specs=pl.BlockSpec((None, COND_ROWS, tn), lambda l, j: (l, 0, j)),
        out_shape=jax.ShapeDtypeStruct((depth, COND_ROWS, width), F32),
        compiler_params=_cparams("parallel", "parallel"),
        name="modulation",
    )(cond, mod_w, mod_b.reshape(depth, 1, width))


def _head_rms(xb, g2, ones_blk):
    ss = _mm_sel(xb * xb, ones_blk, True)
    return xb * lax.rsqrt(ss * (1.0 / HEAD_DIM) + EPS) * g2


def _rope(yb, cos, sin_signed, first16):
    partner = jnp.where(first16, pltpu.roll(yb, LANES - 16, 1), pltpu.roll(yb, 16, 1))
    return yb * cos + partner * sin_signed


def _q_attention_layout(blocks, lane_half):
    out = []
    for j in range(2 * len(blocks)):
        src = blocks[j // 2]
        dst_half = j // GQA_GROUP
        if j % 2 != dst_half:
            src = pltpu.roll(src, HEAD_DIM, 1)
        out.append(jnp.where(lane_half == dst_half, src, 0.0))
    return jnp.concatenate(out, axis=1)


def _inproj_kernel(*refs, precise_ctx, lat_tiles):
    if precise_ctx:
        (xl_ref, xc_ref, mod_ref, g1_ref, w_ref, wlo_ref, cos_ref, sin_ref, qg_ref, kg_ref, ones_ref,
         za_ref, qga_ref, kvg_ref, qwa_ref, kvw_ref, kv32_ref, qg32_ref, qw32_ref, z_ref) = refs
    else:
        (xl_ref, xc_ref, mod_ref, g1_ref, w_ref, cos_ref, sin_ref, qg_ref, kg_ref, ones_ref,
         za_ref, qga_ref, kvg_ref, qwa_ref, kvw_ref, kv32_ref, qg32_ref, qw32_ref) = refs
    is_lat = pl.program_id(0) < lat_tiles
    x = jnp.where(is_lat, xl_ref[...], xc_ref[...])
    mod = mod_ref[...]
    gain = g1_ref[...] * (1.0 + mod[:, D_MODEL:2 * D_MODEL])
    h = x * lax.rsqrt(jnp.mean(x * x, axis=-1, keepdims=True) + EPS) * gain + mod[:, 0:D_MODEL]

    mix_width = Q_WIDTH + 2 * KV_WIDTH
    col_g, col_w = 5 * A_WIDTH, 5 * A_WIDTH + mix_width
    if precise_ctx:
        def one_pass():
            z_ref[...] = _dot(h.astype(BF16), w_ref[...])

        def three_pass():
            hh, hl = _split2(h)
            z_ref[...] = _dot(hh, w_ref[...]) + _dot(hl, w_ref[...]) + _dot(hh, wlo_ref[...])

        pl.when(is_lat)(one_pass)
        pl.when(jnp.logical_not(is_lat))(three_pass)
        z = z_ref[...]
        z_g, z_w = z[:, col_g:col_w], z[:, col_w:]
        za_ref[...] = z[:, :col_g]
    else:
        h_b = h.astype(BF16)
        z_g = _dot(h_b, w_ref[:, col_g:col_w])
        z_w = _dot(h_b, w_ref[:, col_w:])
        za_ref[...] = _dot(h_b, w_ref[:, :col_g])

    cos = cos_ref[...]
    sin = sin_ref[...]
    ones_blk = ones_ref[...]
    lane = lax.broadcasted_iota(jnp.int32, (x.shape[0], LANES), 1)
    first16 = (lane % 32) < 16
    lane_half = lane // HEAD_DIM
    scale = LOG2E * HEAD_DIM ** -0.5

    qg = [z_g[:, LANES * b:LANES * (b + 1)] for b in range(3)]
    kg = z_g[:, Q_WIDTH:Q_WIDTH + KV_WIDTH]
    vg = z_g[:, Q_WIDTH + KV_WIDTH:]
    qw = [z_w[:, LANES * b:LANES * (b + 1)] for b in range(3)]
    kw = z_w[:, Q_WIDTH:Q_WIDTH + KV_WIDTH]
    vw = z_w[:, Q_WIDTH + KV_WIDTH:]

    qg = [_rope(_head_rms(b, qg_ref[...], ones_blk), cos, sin, first16) * scale for b in qg]
    kg = _rope(_head_rms(kg, kg_ref[...], ones_blk), cos, sin, first16)
    qw = [_rope(b, cos, sin, first16) * scale for b in qw]
    kw = _rope(kw, cos, sin, first16)

    qg_att = _q_attention_layout(qg, lane_half)
    qw_att = _q_attention_layout(qw, lane_half)
    qga_ref[...] = qg_att.astype(BF16)
    qwa_ref[...] = qw_att.astype(BF16)
    kvg_ref[...] = jnp.concatenate([kg, vg], axis=1).astype(BF16)
    kvw_ref[...] = jnp.concatenate([kw, vw], axis=1).astype(BF16)
    kv32_ref[...] = jnp.concatenate([kg, vg, kw, vw], axis=1)
    qg32_ref[...] = qg_att
    qw32_ref[...] = qw_att


def _token_specs(x_pair, tm, lat_tiles):
    off = lat_tiles if x_pair[1] is x_pair[0] else 0
    lat = pl.BlockSpec((tm, D_MODEL), lambda i, *_: (jnp.minimum(i, lat_tiles - 1), 0))
    ctx = pl.BlockSpec((tm, D_MODEL), lambda i, *_: (jnp.maximum(i, lat_tiles) - lat_tiles + off, 0))
    return [lat, ctx]


def _in_projection(x_pair, mod, g1, w_hi, w_lo, cos_t, sin_t, qn_g, kn_g, ones_blk, dims, layer=0):
    n = dims["n_lat"] + dims["n_ctx"]
    tm = TOKEN_TILE
    n_tiles = n // tm
    lat_tiles = dims["n_lat"] // tm
    ctx_tiles = n_tiles - lat_tiles
    tiles_per_seq = dims["t_lat"] // tm
    ctx_cond = dims["ctx_cond"]
    precise_ctx = w_lo is not None

    def cond_idx(i):
        return jnp.where(i < lat_tiles, i // tiles_per_seq, ctx_cond)

    def rope_idx(i):
        return jnp.where(i < lat_tiles, 1 + i % tiles_per_seq, 0)

    def ctx_idx(i):
        return jnp.where(i < lat_tiles, ctx_tiles, i - lat_tiles)

    row = lambda w: pl.BlockSpec((tm, w), lambda i: (i, 0))
    ctx_row = lambda w: pl.BlockSpec((tm, w), lambda i: (ctx_idx(i), 0))
    const = lambda a: pl.BlockSpec(a.shape, lambda i: (0,) * a.ndim, pipeline_mode=pl.Buffered(1))
    weight = lambda w: const(w) if w.ndim == 2 else pl.BlockSpec(
        (None,) + w.shape[1:], lambda i: (layer, 0, 0), pipeline_mode=pl.Buffered(1))
    weights = [w_hi, w_lo] if precise_ctx else [w_hi]
    ctx_rows = (ctx_tiles + 1) * tm
    return pl.pallas_call(
        functools.partial(_inproj_kernel, precise_ctx=precise_ctx, lat_tiles=lat_tiles),
        grid=(n_tiles,),
        in_specs=_token_specs(x_pair, tm, lat_tiles) + [
            pl.BlockSpec((None, 1, 6 * D_MODEL), lambda i: (cond_idx(i), 0, 0)),
            const(g1)] + [weight(w) for w in weights] + [
            pl.BlockSpec((tm, LANES), lambda i: (rope_idx(i), 0)),
            pl.BlockSpec((tm, LANES), lambda i: (rope_idx(i), 0)),
            const(qn_g), const(kn_g), const(ones_blk),
        ],
        out_specs=[
            row(5 * A_WIDTH), row(2 * Q_WIDTH), row(2 * KV_WIDTH), row(2 * Q_WIDTH), row(2 * KV_WIDTH),
            ctx_row(4 * KV_WIDTH), ctx_row(2 * Q_WIDTH), ctx_row(2 * Q_WIDTH),
        ],
        out_shape=[
            jax.ShapeDtypeStruct((n, 5 * A_WIDTH), F32),
            jax.ShapeDtypeStruct((n, 2 * Q_WIDTH), BF16),
            jax.ShapeDtypeStruct((n, 2 * KV_WIDTH), BF16),
            jax.ShapeDtypeStruct((n, 2 * Q_WIDTH), BF16),
            jax.ShapeDtypeStruct((n, 2 * KV_WIDTH), BF16),
            jax.ShapeDtypeStruct((ctx_rows, 4 * KV_WIDTH), F32),
            jax.ShapeDtypeStruct((ctx_rows, 2 * Q_WIDTH), F32),
            jax.ShapeDtypeStruct((ctx_rows, 2 * Q_WIDTH), F32),
        ],
        scratch_shapes=[pltpu.VMEM((tm, IN_WIDTH), F32)] if precise_ctx else [],
        compiler_params=_cparams("arbitrary"),
        name="in_projection_precise_ctx" if precise_ctx else "in_projection",
    )(*x_pair, mod, g1, *weights, cos_t, sin_t, qn_g, kn_g, ones_blk)


def _hgrn_chunk(qa, v, fz, lb, st, consts, reverse, precise):
    cum, e_sel, x_sel, sub_mask, p_mask, v_mask, head_mask = consts
    log2e = LOG2E
    n_sub, half = CHUNK // SUB, SUB // 2
    q = _silu(qa)
    f = lb + (1.0 - lb) * _sigmoid(fz)
    logf2 = jnp.log(jnp.maximum(f, TINY)) * log2e
    kk = (1.0 - lb) * _sigmoid(-fz)

    b = _dot_sel(cum, logf2)
    b3 = b.reshape(n_sub, SUB, A_WIDTH)
    end = 0 if reverse else SUB - 1
    tot = jnp.broadcast_to(b3[:, end:end + 1, :], (n_sub, SUB, A_WIDTH)).reshape(CHUNK, A_WIDTH)
    q_in = q * jnp.exp2(b)
    k_out = kk * jnp.exp2(tot - b)
    d_sub = jnp.exp2(tot)

    c3 = (b - jnp.log(kk) * log2e).reshape(n_sub, SUB, A_WIDTH)
    b4 = b.reshape(n_sub, 2, half, A_WIDTH)
    q4 = q.reshape(n_sub, 2, half, A_WIDTH)
    r_local = lax.broadcasted_iota(jnp.int32, (n_sub, half, A_WIDTH), 1)
    zero_half = jnp.zeros((n_sub, half, A_WIDTH), F32)
    pieces = []
    for s in range(SUB):
        cs = c3[:, s:s + 1, :]
        parts = []
        for hsel in range(2):
            lo, hi = hsel * half, hsel * half + half - 1
            if (lo > s) if reverse else (hi < s):
                parts.append(zero_half)
                continue
            arg = b4[:, hsel] - cs
            if not ((hi <= s) if reverse else (lo >= s)):
                t_local = r_local + lo
                arg = jnp.where((t_local <= s) if reverse else (t_local >= s), arg, MASK_VALUE)
            parts.append(q4[:, hsel] * jnp.exp2(arg))
        piece = jnp.stack(parts, axis=1).reshape(CHUNK, A_WIDTH)
        pieces.append(piece if precise else piece.astype(BF16))
    w_cat = jnp.concatenate(pieces, axis=1)
    s_local = _mm_sel(w_cat, e_sel, precise)
    p = _mm_sel(s_local, x_sel, precise) * p_mask
    v_bd = jnp.concatenate([v] * A_HEADS, axis=0) * v_mask
    o_intra = _mm(p, v_bd, precise)

    v_op = v if precise else v.astype(BF16)
    k_op = k_out if precise else k_out.astype(BF16)
    q_op = q_in if precise else q_in.astype(BF16)
    sub_m = sub_mask
    order = range(n_sub - 1, -1, -1) if reverse else range(n_sub)
    new_st, o_inter = [], []
    for p in range(A_WIDTH // LANES):
        lanes = slice(LANES * p, LANES * (p + 1))
        k_exp = jnp.concatenate([k_op[:, lanes]] * n_sub, axis=1) * sub_m
        u_all = _mm(v_op[:, lanes], k_exp, precise, _TN)
        s = st[p]
        s_before = [None] * n_sub
        for j in order:
            s_before[j] = s
            s = d_sub[SUB * j:SUB * j + 1, lanes] * s + head_mask * u_all[:, LANES * j:LANES * (j + 1)]
        new_st.append(s)
        q_exp = jnp.concatenate([q_op[:, lanes]] * n_sub, axis=1) * sub_m
        o_inter.append(_mm(q_exp, jnp.concatenate(s_before, axis=1), precise, _NT))
    return o_intra + jnp.concatenate(o_inter, axis=1), jnp.stack(new_st, axis=0)


def _hgrn_kernel(*refs, has_s0, want_state, precise):
    refs = list(refs)
    seq_refs = [refs[6 * sq:6 * sq + 6] for sq in range(HGRN_SEQS)]
    del refs[:6 * HGRN_SEQS]
    lb_ref = refs.pop(0)
    s0_ref = refs.pop(0) if has_s0 else None
    cumf_ref, cumb_ref, e_ref, x_ref, sm_ref, pm_ref, vm_ref, hm_ref, of_ref, ob_ref = refs[:10]
    del refs[:10]
    sout_ref = refs.pop(0) if want_state else None
    st_ref, = refs
    c = pl.program_id(1)

    @pl.when(c == 0)
    def _():
        st_ref[...] = s0_ref[...] if has_s0 else jnp.zeros_like(st_ref)

    shared = (e_ref[...], x_ref[...], sm_ref[...], pm_ref[...], vm_ref[...], hm_ref[...])
    lb = lb_ref[...]
    for sq in range(HGRN_SEQS):
        qaf_ref, iaf_ref, fzf_ref, qab_ref, iab_ref, fzb_ref = seq_refs[sq]
        o_f, st_f = _hgrn_chunk(qaf_ref[...], iaf_ref[...], fzf_ref[...], lb[0:1], st_ref[sq, 0],
                                (cumf_ref[...],) + shared, False, precise)
        o_b, st_b = _hgrn_chunk(qab_ref[...], iab_ref[...], fzb_ref[...], lb[1:2], st_ref[sq, 1],
                                (cumb_ref[...],) + shared, True, precise)
        of_ref[sq] = o_f
        ob_ref[sq] = o_b
        st_ref[sq, 0] = st_f
        st_ref[sq, 1] = st_b

    if want_state:
        @pl.when(c == pl.num_programs(1) - 1)
        def _():
            sout_ref[...] = st_ref[...]


def _hgrn_consts():
    r = np.arange(CHUNK)
    same = (r[:, None] // SUB) == (r[None, :] // SUB)
    cum_f = (same & (r[None, :] <= r[:, None])).astype(np.float32)
    cum_b = (same & (r[None, :] >= r[:, None])).astype(np.float32)
    tot = same.astype(np.float32)
    e = np.zeros((SUB, A_HEADS, HEAD_DIM, A_HEADS, SUB), np.float32)
    for s in range(SUB):
        for h in range(A_HEADS):
            e[s, h, :, h, s] = 1.0
    e = e.reshape(SUB * A_WIDTH, A_HEADS * SUB)
    x = np.zeros((A_HEADS, SUB, A_HEADS, CHUNK), np.float32)
    for h in range(A_HEADS):
        for s in range(CHUNK):
            x[h, s % SUB, h, s] = 1.0
    x = x.reshape(A_HEADS * SUB, A_HEADS * CHUNK)
    p_mask = np.tile(tot, (1, A_HEADS))
    hv = np.arange(A_HEADS * CHUNK) // CHUNK
    hk = np.arange(A_WIDTH) // HEAD_DIM
    v_mask = (hv[:, None] == hk[None, :]).astype(np.float32)
    head_mask = (hk[:LANES, None] == hk[None, :LANES]).astype(np.float32)
    sub_mask = np.repeat(r[:, None] // SUB == np.arange(CHUNK // SUB)[None, :], LANES, axis=1).astype(np.float32)
    sel = [jnp.asarray(a, BF16) for a in (cum_f, cum_b, e, x, sub_mask)]
    return sel + [jnp.asarray(a, F32) for a in (p_mask, v_mask, head_mask)]


def _hgrn_scan(za, first_row, lb2, s0_t, n_seq, t, precise, want_state, name):
    nc = t // CHUNK
    groups = n_seq // HGRN_SEQS
    base = first_row // CHUNK
    za3 = za.reshape(za.shape[0] // CHUNK, CHUNK, za.shape[1])
    consts = _hgrn_consts()
    state_shape = (HGRN_SEQS, 2, A_WIDTH // LANES, LANES, LANES)
    const = lambda a: pl.BlockSpec(a.shape, lambda p, c: (0,) * a.ndim)

    def chunk_spec(sq, col, backward):
        def index(p, c):
            return (base + (p * HGRN_SEQS + sq) * nc + (nc - 1 - c if backward else c), 0, col)
        return pl.BlockSpec((None, CHUNK, A_WIDTH), index)

    fwd = lambda col: pl.BlockSpec((None, HGRN_SEQS, CHUNK, A_WIDTH), lambda p, c: (p, 0, c, col))
    bwd = lambda col: pl.BlockSpec((None, HGRN_SEQS, CHUNK, A_WIDTH), lambda p, c: (p, 0, nc - 1 - c, col))
    state_spec = pl.BlockSpec((None,) + state_shape, lambda p, c: (p, 0, 0, 0, 0, 0))
    operands = [za3] * (6 * HGRN_SEQS) + [lb2]
    in_specs = [chunk_spec(sq, col, backward) for sq in range(HGRN_SEQS)
                for col, backward in ((0, False), (1, False), (2, False), (0, True), (1, True), (3, True))]
    in_specs.append(const(lb2))
    if s0_t is not None:
        operands.append(s0_t.reshape((groups,) + state_shape))
        in_specs.append(state_spec)
    o_shape = jax.ShapeDtypeStruct((groups, HGRN_SEQS, t, A_WIDTH), F32)
    out_specs, out_shape = [fwd(0), bwd(0)], [o_shape, o_shape]
    if want_state:
        out_specs.append(state_spec)
        out_shape.append(jax.ShapeDtypeStruct((groups,) + state_shape, F32))
    outs = pl.pallas_call(
        functools.partial(_hgrn_kernel, has_s0=s0_t is not None, want_state=want_state, precise=precise),
        grid=(groups, nc),
        in_specs=in_specs + [const(c) for c in consts],
        out_specs=out_specs,
        out_shape=out_shape,
        scratch_shapes=[pltpu.VMEM(state_shape, F32)],
        compiler_params=_cparams("parallel", "arbitrary"),
        name=name,
    )(*operands, *consts)
    o_f, o_b = (o.reshape(n_seq * t, A_WIDTH) for o in outs[:2])
    s_fin = outs[2].reshape((n_seq,) + state_shape[1:]) if want_state else None
    return o_f, o_b, s_fin


def _attend(q3, segments, sink_row, precise=False):
    scores = []
    for k, _, valid in segments:
        s = _mm(k, q3, precise, _NT)
        if valid is not None:
            s = jnp.where(valid, s, MASK_VALUE)
        scores.append(s)
    m = scores[0].max(axis=0, keepdims=True)
    for s in scores[1:]:
        m = jnp.maximum(m, s.max(axis=0, keepdims=True))
    if sink_row is not None:
        m = jnp.maximum(m, sink_row)
    denom = jnp.exp2(sink_row - m) if sink_row is not None else 0.0
    acc = 0.0
    for s, (_, v, _) in zip(scores, segments):
        p = jnp.exp2(s - m)
        denom = denom + p.sum(axis=0, keepdims=True)
        acc = acc + _mm(v, p, precise, _TN)
    return (acc / denom).T


def _stack_heads(q, kvh):
    base = kvh * GQA_GROUP
    return jnp.concatenate([q[:, LANES * (base + g):LANES * (base + g + 1)] for g in range(GQA_GROUP)], axis=0)


def _merge_heads(res, tq):
    lane = lax.broadcasted_iota(jnp.int32, (tq, LANES), 1)
    heads = []
    for j in range(B_Q_HEADS):
        kvh, g = divmod(j, GQA_GROUP)
        o = res[kvh][g * tq:(g + 1) * tq]
        if kvh != j % 2:
            o = pltpu.roll(o, HEAD_DIM, 1)
        heads.append(o)
    blocks = [jnp.where(lane < HEAD_DIM, heads[2 * b], heads[2 * b + 1]) for b in range(B_Q_HEADS // 2)]
    return jnp.concatenate(blocks, axis=1)


def _sink_row(sink_ref, kvh, tq):
    return jnp.concatenate(
        [jnp.full((1, tq), sink_ref[kvh * GQA_GROUP + g] * LOG2E, F32) for g in range(GQA_GROUP)], axis=1)


def _global_attn_kernel(q_ref, kv_ref, ck_ref, cv_ref, o_ref, *, tq):
    q = q_ref[...]
    kv = kv_ref[...]
    k, v = kv[:, :KV_WIDTH], kv[:, KV_WIDTH:]
    ck, cv = ck_ref[...], cv_ref[...]
    segments = [(k[c:c + KEY_CHUNK], v[c:c + KEY_CHUNK], None) for c in range(0, k.shape[0], KEY_CHUNK)]
    segments.append((ck, cv, None))
    res = [_attend(_stack_heads(q, kvh), segments, None) for kvh in range(B_KV_HEADS)]
    o_ref[...] = _merge_heads(res, tq).astype(BF16)


def _global_attention(q_att, kv, ck, cv, dims):
    t, n_seq = dims["t_lat"], dims["n_lat_seq"]
    tq = 256
    nq = t // tq
    return pl.pallas_call(
        functools.partial(_global_attn_kernel, tq=tq),
        grid=(n_seq, nq),
        in_specs=[
            pl.BlockSpec((tq, 2 * Q_WIDTH), lambda b, i: (b * nq + i, 0)),
            pl.BlockSpec((t, 2 * KV_WIDTH), lambda b, i: (b, 0)),
            pl.BlockSpec((None,) + ck.shape[1:], lambda b, i: (b, 0, 0)),
            pl.BlockSpec((None,) + cv.shape[1:], lambda b, i: (b, 0, 0)),
        ],
        out_specs=pl.BlockSpec((tq, Q_WIDTH), lambda b, i: (b * nq + i, 0)),
        out_shape=jax.ShapeDtypeStruct((n_seq * t, Q_WIDTH), BF16),
        compiler_params=_cparams("parallel", "arbitrary"),
        name="global_attention",
    )(q_att, kv, ck, cv)


def _window_attn_kernel(sink_ref, q_ref, kv_ref, ck_ref, cv_ref, o_ref, *, tq, t):
    j = pl.program_id(1)
    span = tq + 2 * WINDOW
    start = pl.multiple_of(jnp.clip(j * tq - WINDOW, 0, t - span), WINDOW)
    q = q_ref[...]
    kv = kv_ref[pl.ds(start, span), :]
    k, v = kv[:, :KV_WIDTH], kv[:, KV_WIDTH:]
    key_pos = start + lax.broadcasted_iota(jnp.int32, (span, GQA_GROUP * tq), 0)
    query_pos = j * tq + lax.broadcasted_iota(jnp.int32, (span, GQA_GROUP * tq), 1) % tq
    valid = jnp.abs(query_pos - key_pos) <= WINDOW
    segments = [(k, v, valid), (ck_ref[...], cv_ref[...], None)]
    res = [_attend(_stack_heads(q, kvh), segments, _sink_row(sink_ref, kvh, tq)) for kvh in range(C_KV_HEADS)]
    o_ref[...] = _merge_heads(res, tq).astype(BF16)


def _window_attention(sink, q_att, kv, ck, cv, dims):
    t, n_seq = dims["t_lat"], dims["n_lat_seq"]
    tq = 2 * WINDOW
    nq = t // tq
    return pl.pallas_call(
        functools.partial(_window_attn_kernel, tq=tq, t=t),
        grid=(n_seq, nq),
        in_specs=[
            pl.BlockSpec(memory_space=pltpu.SMEM),
            pl.BlockSpec((tq, 2 * Q_WIDTH), lambda b, i: (b * nq + i, 0)),
            pl.BlockSpec((t, 2 * KV_WIDTH), lambda b, i: (b, 0)),
            pl.BlockSpec((None,) + ck.shape[1:], lambda b, i: (b, 0, 0)),
            pl.BlockSpec((None,) + cv.shape[1:], lambda b, i: (b, 0, 0)),
        ],
        out_specs=pl.BlockSpec((tq, Q_WIDTH), lambda b, i: (b * nq + i, 0)),
        out_shape=jax.ShapeDtypeStruct((n_seq * t, Q_WIDTH), BF16),
        compiler_params=_cparams("parallel", "arbitrary"),
        name="window_attention",
    )(sink, q_att, kv, ck, cv)


def _context_attn_kernel(sink_ref, qg_ref, qw_ref, kv_ref, og_ref, ow_ref, *, t, precise):
    kv = kv_ref[...]
    for q_ref, o_ref, off, use_sink in ((qg_ref, og_ref, 0, False), (qw_ref, ow_ref, 2 * KV_WIDTH, True)):
        q = q_ref[...]
        k, v = kv[:, off:off + KV_WIDTH], kv[:, off + KV_WIDTH:off + 2 * KV_WIDTH]
        res = [_attend(_stack_heads(q, kvh), [(k, v, None)],
                       _sink_row(sink_ref, kvh, t) if use_sink else None, precise)
               for kvh in range(B_KV_HEADS)]
        o_ref[...] = _merge_heads(res, t)


def _context_attention(sink, qg32, qw32, kv32, dims, precise):
    t, n_seq = dims["t_ctx"], dims["n_ctx_seq"]
    seq = lambda w: pl.BlockSpec((t, w), lambda b: (b, 0))
    return pl.pallas_call(
        functools.partial(_context_attn_kernel, t=t, precise=precise),
        grid=(n_seq,),
        in_specs=[pl.BlockSpec(memory_space=pltpu.SMEM), seq(2 * Q_WIDTH), seq(2 * Q_WIDTH), seq(4 * KV_WIDTH)],
        out_specs=[seq(Q_WIDTH), seq(Q_WIDTH)],
        out_shape=[jax.ShapeDtypeStruct((n_seq * t, Q_WIDTH), F32)] * 2,
        compiler_params=_cparams("parallel"),
        name="context_attention_precise" if precise else "context_attention",
    )(sink, qg32, qw32, kv32)


def _route(scores_t, sel_t):
    s = [scores_t[e:e + 1, :] for e in range(N_EXPERTS)]
    z = [sel_t[e:e + 1, :] for e in range(N_EXPERTS)]
    gs = []
    for g in range(N_GROUPS):
        m = z[g * EXPERTS_PER_GROUP:(g + 1) * EXPERTS_PER_GROUP]
        best = None
        for a in range(EXPERTS_PER_GROUP):
            for b in range(a + 1, EXPERTS_PER_GROUP):
                pair = m[a] + m[b]
                best = pair if best is None else jnp.maximum(best, pair)
        gs.append(best)
    combine, groups = [], []
    for g in range(N_GROUPS):
        chosen_g = None
        for g2 in range(N_GROUPS):
            if g2 == g:
                continue
            c = (gs[g] > gs[g2]) if g2 < g else (gs[g] >= gs[g2])
            chosen_g = c if chosen_g is None else jnp.logical_and(chosen_g, c)
        base = g * EXPERTS_PER_GROUP
        picked = []
        for a in range(EXPERTS_PER_GROUP):
            rank = 0.0
            for b in range(EXPERTS_PER_GROUP):
                if b == a:
                    continue
                ahead = (z[base + b] >= z[base + a]) if b < a else (z[base + b] > z[base + a])
                rank = rank + ahead.astype(F32)
            picked.append(jnp.where(jnp.logical_and(chosen_g, rank < 2.0), s[base + a], 0.0))
        denom = picked[0] + picked[1] + picked[2] + picked[3]
        denom = jnp.where(chosen_g, denom, 1.0)
        combine.extend(pk / denom for pk in picked)
        groups.append(jnp.where(chosen_g, 1.0, 0.0))
    return jnp.concatenate(combine, axis=0), jnp.concatenate(groups, axis=0)


def _outproj_kernel(*refs, lat_tiles, precise_ctx):
    if precise_ctx:
        (ofl_ref, ofc_ref, obl_ref, obc_ref, ga_ref,
         ogl_ref, ogc_ref, owl_ref, owc_ref, xl_ref, xc_ref, mod_ref, ag_ref,
         ones_ref, w_ref, wlo_ref, g2_ref, rw_ref, rb_ref, before_ref,
         x1_ref, h2_ref, comb_ref, pos_ref, cnt_ref, mix_ref) = refs
    else:
        (ofl_ref, ofc_ref, obl_ref, obc_ref, ga_ref,
         ogl_ref, ogc_ref, owl_ref, owc_ref, xl_ref, xc_ref, mod_ref, ag_ref,
         ones_ref, w_ref, g2_ref, rw_ref, rb_ref, before_ref,
         x1_ref, h2_ref, comb_ref, pos_ref, cnt_ref, mix_ref) = refs
        wlo_ref = None
    is_lat = pl.program_id(0) < lat_tiles
    o = jnp.where(is_lat, ofl_ref[...] + obl_ref[...], ofc_ref[...] + obc_ref[...])
    oa = _head_rms(o, ag_ref[...], ones_ref[...]) * _sigmoid(ga_ref[...])
    def mix(operands, precise):
        if precise:
            ah, al = _split2(jnp.concatenate(operands, axis=1))
            mix_ref[...] = _dot(ah, w_ref[...]) + _dot(al, w_ref[...]) + _dot(ah, wlo_ref[...])
        else:
            mix_ref[...] = _dot(jnp.concatenate([a.astype(BF16) for a in operands], axis=1), w_ref[...])

    pl.when(is_lat)(lambda: mix((oa, ogl_ref[...], owl_ref[...]), False))
    pl.when(jnp.logical_not(is_lat))(lambda: mix((oa, ogc_ref[...], owc_ref[...]), precise_ctx))

    mod = mod_ref[...]
    gate1 = mod[:, 2 * D_MODEL:3 * D_MODEL]
    sh2 = mod[:, 3 * D_MODEL:4 * D_MODEL]
    sc2 = mod[:, 4 * D_MODEL:5 * D_MODEL]
    x1 = jnp.where(is_lat, xl_ref[...], xc_ref[...]) + gate1 * mix_ref[...]
    x1_ref[...] = x1
    gain = g2_ref[...] * (1.0 + sc2)
    h2 = x1 * lax.rsqrt(jnp.mean(x1 * x1, axis=-1, keepdims=True) + EPS) * gain + sh2
    h2_ref[...] = h2.astype(BF16)
    scores_t = _sigmoid(_mm(rw_ref[...], h2, True, _NT))
    comb_t, group_t = _route(scores_t, scores_t + rb_ref[...])
    comb_ref[...] = comb_t
    rank = _dot(group_t.astype(BF16), before_ref[...])
    counts = group_t.sum(axis=1, keepdims=True)
    pos, offset = 0.0, 0.0
    for g in range(N_GROUPS):
        pos = pos + group_t[g:g + 1] * (rank[g:g + 1] + offset)
        offset = offset + counts[g:g + 1]
    pos_ref[...] = pos
    pad = jnp.zeros((8 - N_GROUPS, 1), F32)
    cnt_ref[...] = jnp.broadcast_to(jnp.concatenate([counts, pad], axis=0), (8, LANES)).astype(jnp.int32)


def _out_projection(o_f, o_b, za, og_lat, og_ctx, ow_lat, ow_ctx, x_pair, mod, an_g, ones_blk, w_hi, w_lo, g2,
                    rw_t, rb, dims, layer=0):
    n = dims["n_lat"] + dims["n_ctx"]
    tm = TOKEN_TILE
    n_tiles = n // tm
    lat_tiles = dims["n_lat"] // tm
    tiles_per_seq = dims["t_lat"] // tm
    ctx_cond = dims["ctx_cond"]
    precise_ctx = w_lo is not None
    cond_idx = lambda i: jnp.where(i < lat_tiles, i // tiles_per_seq, ctx_cond)
    row = lambda w: pl.BlockSpec((tm, w), lambda i: (i, 0))
    lat = lambda w: pl.BlockSpec((tm, w), lambda i: (jnp.minimum(i, lat_tiles - 1), 0))
    ctx = lambda w: pl.BlockSpec((tm, w), lambda i: (jnp.maximum(i - lat_tiles, 0), 0))
    const = lambda a: pl.BlockSpec(a.shape, lambda i: (0,) * a.ndim, pipeline_mode=pl.Buffered(1))
    weight = lambda w: const(w) if w.ndim == 2 else pl.BlockSpec(
        (None,) + w.shape[1:], lambda i: (layer, 0, 0), pipeline_mode=pl.Buffered(1))
    weights = [w_hi, w_lo] if precise_ctx else [w_hi]
    t_idx = np.arange(tm)
    before = jnp.asarray((t_idx[:, None] < t_idx[None, :]).astype(np.float32), BF16)
    return pl.pallas_call(
        functools.partial(_outproj_kernel, lat_tiles=lat_tiles, precise_ctx=precise_ctx),
        grid=(n_tiles,),
        in_specs=[
            lat(A_WIDTH), ctx(A_WIDTH), lat(A_WIDTH), ctx(A_WIDTH),
            pl.BlockSpec((tm, A_WIDTH), lambda i: (i, 4)),
            lat(Q_WIDTH), ctx(Q_WIDTH), lat(Q_WIDTH), ctx(Q_WIDTH)] + _token_specs(x_pair, tm, lat_tiles) + [
            pl.BlockSpec((None, 1, 6 * D_MODEL), lambda i: (cond_idx(i), 0, 0)),
            const(an_g), const(ones_blk)] + [weight(w) for w in weights] + [
            const(g2), const(rw_t), const(rb), const(before),
        ],
        out_specs=[row(D_MODEL), row(D_MODEL), pl.BlockSpec((N_EXPERTS, tm), lambda i: (0, i)),
                   pl.BlockSpec((1, tm), lambda i: (0, i)), pl.BlockSpec((8, LANES), lambda i: (i, 0))],
        out_shape=[
            jax.ShapeDtypeStruct((n, D_MODEL), F32),
            jax.ShapeDtypeStruct((n, D_MODEL), BF16),
            jax.ShapeDtypeStruct((N_EXPERTS, n), F32),
            jax.ShapeDtypeStruct((1, n), F32),
            jax.ShapeDtypeStruct((n_tiles * 8, LANES), jnp.int32),
        ],
        scratch_shapes=[pltpu.VMEM((tm, D_MODEL), F32)],
        compiler_params=_cparams("arbitrary"),
        name="out_projection_precise_ctx" if precise_ctx else "out_projection",
    )(*o_f, *o_b, za, og_lat, og_ctx, ow_lat, ow_ctx, *x_pair, mod, an_g, ones_blk, *weights, g2, rw_t, rb, before)


def _moe_kernel(off_ref, h_ref, comb_ref, pos_ref, x1_ref, mod_ref, wgu_ref, wd_ref, fg_ref, *rest,
                final, lat_tiles):
    *outs, hs_ref, cs_ref, acc_ref = rest
    i = pl.program_id(0)
    tm = h_ref.shape[0]
    row = lax.broadcasted_iota(jnp.int32, (tm, tm), 0).astype(F32)
    perm = jnp.where(row == pos_ref[...], 1.0, 0.0).astype(BF16)
    hs_ref[...] = _dot(perm, h_ref[...]).astype(BF16)
    cs_ref[...] = sum(lax.dot_general(perm, c, _NT, preferred_element_type=F32)
                      for c in _split3(comb_ref[...]))
    acc_ref[...] = jnp.zeros_like(acc_ref)
    lane = lax.broadcasted_iota(jnp.int32, (MOE_BLOCK, N_EXPERTS), 1)

    for r in range(tm // MOE_BLOCK):
        rows = slice(r * MOE_BLOCK, (r + 1) * MOE_BLOCK)

        def group_body(g, carry, rows=rows, r=r):
            has_tokens = jnp.logical_and(off_ref[i, g] < (r + 1) * MOE_BLOCK, off_ref[i, g + 1] > r * MOE_BLOCK)

            @pl.when(has_tokens)
            def _():
                h = hs_ref[rows, :]
                cs = cs_ref[rows, :]
                acts = []
                for a in range(EXPERTS_PER_GROUP):
                    e = g * EXPERTS_PER_GROUP + a
                    gu = _dot(h, wgu_ref[e])
                    gate, up = gu[:, :D_EXPERT], gu[:, D_EXPERT:]
                    ce = jnp.sum(jnp.where(lane == e, cs, 0.0), axis=1, keepdims=True)
                    acts.append((_silu(gate) * up * ce).astype(BF16))
                acc_ref[rows, :] += _dot(jnp.concatenate(acts, axis=1), wd_ref[g])

            return carry

        lax.fori_loop(0, N_GROUPS, group_body, 0)

    y_hi, y_lo = _split2(acc_ref[...])
    y = (lax.dot_general(perm, y_hi, _TN, preferred_element_type=F32)
         + lax.dot_general(perm, y_lo, _TN, preferred_element_type=F32))
    x2 = x1_ref[...] + mod_ref[...][:, 5 * D_MODEL:] * y
    if not final:
        outs[0][...] = x2
    else:
        yn = x2 * lax.rsqrt(jnp.mean(x2 * x2, axis=-1, keepdims=True) + EPS) * fg_ref[...]
        is_lat = i < lat_tiles

        @pl.when(is_lat)
        def _():
            outs[0][...] = yn

        @pl.when(jnp.logical_not(is_lat))
        def _():
            outs[1][...] = yn


def _moe(h2, comb_t, pos, counts, x1, mod, w_gu, w_d, layer, final_g, dims, final):
    n = x1.shape[0]
    tm = TOKEN_TILE
    lat_tiles = dims["n_lat"] // tm
    tiles_per_seq = dims["t_lat"] // tm
    ctx_cond = dims["ctx_cond"]
    cond_idx = lambda i: jnp.where(i < lat_tiles, i // tiles_per_seq, ctx_cond)
    offsets = jnp.concatenate([jnp.zeros((n // tm, 1), jnp.int32), jnp.cumsum(counts, axis=1)], axis=1)
    if final:
        out_specs = [pl.BlockSpec((tm, D_MODEL), lambda i, *_: (jnp.minimum(i, lat_tiles - 1), 0)),
                     pl.BlockSpec((tm, D_MODEL), lambda i, *_: (jnp.maximum(i - lat_tiles, 0), 0))]
        out_shape = [jax.ShapeDtypeStruct((dims["n_lat"], D_MODEL), F32),
                     jax.ShapeDtypeStruct((dims["n_ctx"], D_MODEL), F32)]
    else:
        out_specs = pl.BlockSpec((tm, D_MODEL), lambda i, *_: (i, 0))
        out_shape = jax.ShapeDtypeStruct((n, D_MODEL), F32)
    resident = lambda a: pl.BlockSpec((None,) + a.shape[1:], lambda i, *_: (layer,) + (0,) * (a.ndim - 1),
                                      pipeline_mode=pl.Buffered(1))
    grid_spec = pltpu.PrefetchScalarGridSpec(
        num_scalar_prefetch=1,
        grid=(n // tm,),
        in_specs=[
            pl.BlockSpec((tm, D_MODEL), lambda i, *_: (i, 0)),
            pl.BlockSpec((N_EXPERTS, tm), lambda i, *_: (0, i)),
            pl.BlockSpec((1, tm), lambda i, *_: (0, i)),
            pl.BlockSpec((tm, D_MODEL), lambda i, *_: (i, 0)),
            pl.BlockSpec((None, 1, 6 * D_MODEL), lambda i, *_: (cond_idx(i), 0, 0)),
            resident(w_gu), resident(w_d),
            pl.BlockSpec((1, D_MODEL), lambda i, *_: (0, 0)),
        ],
        out_specs=out_specs,
        scratch_shapes=[pltpu.VMEM((tm, D_MODEL), BF16), pltpu.VMEM((tm, N_EXPERTS), F32),
                        pltpu.VMEM((tm, D_MODEL), F32)],
    )
    return pl.pallas_call(
        functools.partial(_moe_kernel, final=final, lat_tiles=lat_tiles),
        grid_spec=grid_spec,
        out_shape=out_shape,
        compiler_params=_cparams("arbitrary"),
        name="moe_final" if final else "moe",
    )(offsets, h2, comb_t, pos, x1, mod, w_gu, w_d, final_g)


def _rope_tables(t, tile):
    n_freq = HEAD_DIM // 4
    pos = np.arange(t)
    freqs = ROPE_THETA ** (-np.arange(n_freq, dtype=np.float64) / n_freq)
    ang_row = (pos // GRID_W)[:, None] * freqs
    ang_col = (pos % GRID_W)[:, None] * freqs
    cos_h = np.concatenate([np.cos(ang_row)] * 2 + [np.cos(ang_col)] * 2, axis=1)
    sin_h = np.concatenate([-np.sin(ang_row), np.sin(ang_row), -np.sin(ang_col), np.sin(ang_col)], axis=1)
    cos_t = np.concatenate([np.ones((tile, LANES)), np.concatenate([cos_h, cos_h], axis=1)], axis=0)
    sin_t = np.concatenate([np.zeros((tile, LANES)), np.concatenate([sin_h, sin_h], axis=1)], axis=0)
    return jnp.asarray(cos_t, F32), jnp.asarray(sin_t, F32)


def _head_ones(width):
    r = np.arange(width) // HEAD_DIM
    return jnp.asarray((r[:, None] == r[None, :]).astype(np.float32), BF16)


def _split_kernel(w_ref, hi_ref, lo_ref):
    hi, lo = _split2(w_ref[...])
    hi_ref[...] = hi
    lo_ref[...] = lo


def _hi_lo(w_all, layer):
    _, rows, width = w_all.shape
    tr = 256
    spec = pl.BlockSpec((tr, width), lambda i: (i, 0))
    return pl.pallas_call(
        _split_kernel,
        grid=(rows // tr,),
        in_specs=[pl.BlockSpec((None, tr, width), lambda i: (layer, i, 0))],
        out_specs=[spec, spec],
        out_shape=[jax.ShapeDtypeStruct((rows, width), BF16)] * 2,
        compiler_params=_cparams("parallel"),
        name="split_weight",
    )(w_all)


def _state_to_block_diag_t(s):
    per_pair = LANES // HEAD_DIM
    st = jnp.swapaxes(s, -1, -2).reshape(s.shape[:2] + (A_HEADS // per_pair, per_pair, HEAD_DIM, HEAD_DIM))
    eye = jnp.eye(per_pair, dtype=s.dtype)
    out = jnp.einsum("bdphvk,hg->bdphvgk", st, eye)
    return out.reshape(s.shape[:2] + (A_HEADS // per_pair, LANES, LANES))


def _block_diag_t_to_state(st):
    per_pair = LANES // HEAD_DIM
    s7 = st.reshape(st.shape[:3] + (per_pair, HEAD_DIM, per_pair, HEAD_DIM))
    diag = jnp.stack([s7[:, :, :, h, :, h, :] for h in range(per_pair)], axis=3)
    return jnp.swapaxes(diag, -1, -2).reshape(st.shape[:2] + (A_HEADS, HEAD_DIM, HEAD_DIM))


def kernel(x_prompt, x_sample, cache_glob_k, cache_glob_v, cache_win_k, cache_win_v, state_hgrn, c, c_ctx,
           mod_w, mod_b, norm1_g, norm2_g, w_in, w_out, hgrn_lb, hgrn_norm_g, q_norm_g, k_norm_g, win_sink,
           router_w, router_b, w_gate_up, w_down, final_g):
    n_ctx_seq, t_ctx, d = x_prompt.shape
    n_lat_seq, t_lat, _ = x_sample.shape
    depth = mod_w.shape[0]
    past = cache_glob_k.shape[2]
    n_lat, n_ctx = n_lat_seq * t_lat, n_ctx_seq * t_ctx
    dims = dict(n_ctx_seq=n_ctx_seq, t_ctx=t_ctx, n_lat_seq=n_lat_seq, t_lat=t_lat,
                n_lat=n_lat, n_ctx=n_ctx, ctx_cond=n_lat_seq)
    assert d == D_MODEL and n_lat_seq < COND_ROWS
    assert n_ctx % TOKEN_TILE == 0 and t_lat % TOKEN_TILE == 0
    assert t_ctx % CHUNK == 0 and t_lat >= 4 * WINDOW

    x_pair = (x_sample.reshape(n_lat, d), x_prompt.reshape(n_ctx, d))
    cond = jnp.concatenate([c, c_ctx[None, :], jnp.zeros((COND_ROWS - n_lat_seq - 1, d), F32)], axis=0)
    mod = _modulation(cond, mod_w, mod_b).reshape(depth, COND_ROWS, 1, 6 * d)

    p_lb = jax.nn.softmax(hgrn_lb.astype(F32), axis=1)
    lbs = jnp.cumsum(p_lb, axis=1) - p_lb[:, :1]

    cos_t, sin_t = _rope_tables(t_lat, TOKEN_TILE)
    ones128, ones256 = _head_ones(LANES), _head_ones(A_WIDTH)
    tile2 = lambda g, reps: jnp.tile(g, reps)[None, :]
    rw_t = router_w.T
    rb = router_b[:, None]
    fg = final_g[None, :]

    w_gu_all = w_gate_up.astype(BF16)
    w_d_all = w_down.astype(BF16).reshape(depth, N_GROUPS, EXPERTS_PER_GROUP * D_EXPERT, d)
    w_in_last, w_out_last = w_in[depth - 1:].astype(BF16), w_out[depth - 1:].astype(BF16)

    new_kv, new_state = [], []
    for l in range(depth):
        precise = l < depth - 1
        win_hi, win_lo = _hi_lo(w_in, l) if precise else (w_in_last, None)
        za, qg_att, kvg, qw_att, kvw, kv32, qg32, qw32 = _in_projection(
            x_pair, mod[l], norm1_g[l][None, :], win_hi, win_lo, cos_t, sin_t,
            tile2(q_norm_g[l], 2), tile2(k_norm_g[l], 2), ones128, dims)

        s0 = state_hgrn[:, l].astype(F32)
        of_lat, ob_lat, _ = _hgrn_scan(za, 0, lbs[:, l], _state_to_block_diag_t(s0), n_lat_seq, t_lat,
                                       False, False, "hgrn_latent")
        of_ctx, ob_ctx, s_fin = _hgrn_scan(za, n_lat, lbs[:, l], None, n_ctx_seq, t_ctx, precise, True,
                                           "hgrn_context_precise" if precise else "hgrn_context")

        cast_cache = lambda a: a[:, l].reshape(n_lat_seq, past, KV_WIDTH).astype(BF16)
        og_lat = _global_attention(qg_att, kvg, cast_cache(cache_glob_k), cast_cache(cache_glob_v), dims)
        ow_lat = _window_attention(win_sink[l], qw_att, kvw, cast_cache(cache_win_k),
                                   cast_cache(cache_win_v), dims)
        og_ctx, ow_ctx = _context_attention(win_sink[l], qg32, qw32, kv32, dims, precise)

        wout_hi, wout_lo = _hi_lo(w_out, l) if precise else (w_out_last, None)
        x1, h2, comb_t, pos, cnt = _out_projection(
            (of_lat, of_ctx), (ob_lat, ob_ctx), za, og_lat, og_ctx, ow_lat, ow_ctx, x_pair, mod[l],
            tile2(hgrn_norm_g[l], A_HEADS), ones256, wout_hi, wout_lo, norm2_g[l][None, :], rw_t, rb, dims)

        counts = cnt.reshape(-1, 8, LANES)[:, :N_GROUPS, 0]
        x = _moe(h2, comb_t, pos, counts, x1, mod[l], w_gu_all, w_d_all, l, fg, dims, final=(l == depth - 1))
        x_pair = (x, x)

        new_kv.append(kv32[:n_ctx].reshape(n_ctx_seq, t_ctx, 4, B_KV_HEADS, HEAD_DIM))
        new_state.append(_block_diag_t_to_state(s_fin))

    y_sample = x[0].reshape(n_lat_seq, t_lat, d)
    y_prompt = x[1].reshape(n_ctx_seq, t_ctx, d)
    kv = jnp.stack(new_kv, axis=1)
    return (y_prompt, y_sample, kv[:, :, :, 0], kv[:, :, :, 1], kv[:, :, :, 2], kv[:, :, :, 3],
            jnp.stack(new_state, axis=1).astype(x_prompt.dtype))
```

```python
import functools

import numpy as np
import jax
import jax.numpy as jnp
from jax import lax
from jax.experimental import pallas as pl
from jax.experimental.pallas import tpu as pltpu

F32 = jnp.float32
BF16 = jnp.bfloat16

D_MODEL = 1024
HEAD_DIM = 64
GRID_W = 64
A_HEADS = 4
A_WIDTH = A_HEADS * HEAD_DIM
B_Q_HEADS = 6
B_KV_HEADS = 2
C_Q_HEADS = 6
C_KV_HEADS = 2
Q_WIDTH = B_Q_HEADS * HEAD_DIM
KV_WIDTH = B_KV_HEADS * HEAD_DIM
GQA_GROUP = B_Q_HEADS // B_KV_HEADS
WINDOW = 128
ROPE_THETA = 10000.0
IN_WIDTH = 5 * A_WIDTH + 2 * (Q_WIDTH + 2 * KV_WIDTH)
N_EXPERTS = 16
N_GROUPS = 4
EXPERTS_PER_GROUP = N_EXPERTS // N_GROUPS
D_EXPERT = 256
EPS = 1e-6
MASK_VALUE = -1e30
TINY = 1e-30
LOG2E = 1.4426950408889634

LANES = 128
SUB = 16
CHUNK = 128
HGRN_SEQS = 2
TOKEN_TILE = 512
MOE_BLOCK = 128
KEY_CHUNK = 512
COND_ROWS = 16
VMEM_LIMIT = 58 * 1024 * 1024

_NT = (((1,), (1,)), ((), ()))
_TN = (((0,), (0,)), ((), ()))


def _cparams(*sem):
    return pltpu.CompilerParams(dimension_semantics=sem, vmem_limit_bytes=VMEM_LIMIT)


def _dot(a, b):
    return jnp.dot(a, b, preferred_element_type=F32)


def _split2(x):
    h = x.astype(BF16)
    return h, (x - h.astype(F32)).astype(BF16)


def _split3(x):
    h = x.astype(BF16)
    r = x - h.astype(F32)
    m = r.astype(BF16)
    l = (r - m.astype(F32)).astype(BF16)
    return h, m, l


def _dot_sel(c, x):
    h, m, l = _split3(x)
    return _dot(c, h) + _dot(c, m) + _dot(c, l)


def _mm(a, b, precise, dims=None):
    if dims is None:
        dims = (((a.ndim - 1,), (0,)), ((), ()))
    dg = lambda x, y: lax.dot_general(x, y, dims, preferred_element_type=F32)
    if not precise:
        return dg(a.astype(BF16), b.astype(BF16))
    ah, al = _split2(a)
    bh, bl = _split2(b)
    return dg(ah, bh) + dg(ah, bl) + dg(al, bh)


def _mm_sel(a, sel, precise):
    if not precise:
        return _dot(a.astype(BF16), sel)
    ah, al = _split2(a)
    return _dot(ah, sel) + _dot(al, sel)


def _sigmoid(x):
    return jax.nn.sigmoid(x)


def _silu(x):
    return x * jax.nn.sigmoid(x)


def _mod_kernel(cond_ref, w_ref, b_ref, o_ref):
    o_ref[...] = _mm(_silu(cond_ref[...]), w_ref[...], True) + b_ref[...]


def _modulation(cond, mod_w, mod_b):
    depth, d, width = mod_w.shape
    tn = 1536
    return pl.pallas_call(
        _mod_kernel,
        grid=(depth, width // tn),
        in_specs=[
            pl.BlockSpec((COND_ROWS, d), lambda l, j: (0, 0)),
            pl.BlockSpec((None, d, tn), lambda l, j: (l, 0, j)),
            pl.BlockSpec((None, 1, tn), lambda l, j: (l, 0, j)),
        ],
        out_specs=pl.BlockSpec((None, COND_ROWS, tn), lambda l, j: (l, 0, j)),
        out_shape=jax.ShapeDtypeStruct((depth, COND_ROWS, width), F32),
        compiler_params=_cparams("parallel", "parallel"),
        name="modulation",
    )(cond, mod_w, mod_b.reshape(depth, 1, width))


def _head_rms(xb, g2, ones_blk):
    ss = _mm_sel(xb * xb, ones_blk, True)
    return xb * lax.rsqrt(ss * (1.0 / HEAD_DIM) + EPS) * g2


def _rope(yb, cos, sin_signed, first16):
    partner = jnp.where(first16, pltpu.roll(yb, LANES - 16, 1), pltpu.roll(yb, 16, 1))
    return yb * cos + partner * sin_signed


def _q_attention_layout(blocks, lane_half):
    out = []
    for j in range(2 * len(blocks)):
        src = blocks[j // 2]
        dst_half = j // GQA_GROUP
        if j % 2 != dst_half:
            src = pltpu.roll(src, HEAD_DIM, 1)
        out.append(jnp.where(lane_half == dst_half, src, 0.0))
    return jnp.concatenate(out, axis=1)


def _inproj_kernel(*refs, precise_ctx, lat_tiles):
    if precise_ctx:
        (xl_ref, xc_ref, mod_ref, g1_ref, w_ref, wlo_ref, cos_ref, sin_ref, qg_ref, kg_ref, ones_ref,
         za_ref, qga_ref, kvg_ref, qwa_ref, kvw_ref, kv32_ref, qg32_ref, qw32_ref) = refs
    else:
        (xl_ref, xc_ref, mod_ref, g1_ref, w_ref, cos_ref, sin_ref, qg_ref, kg_ref, ones_ref,
         za_ref, qga_ref, kvg_ref, qwa_ref, kvw_ref, kv32_ref, qg32_ref, qw32_ref) = refs
    is_lat = pl.program_id(0) < lat_tiles
    x = jnp.where(is_lat, xl_ref[...], xc_ref[...])
    mod = mod_ref[...]
    gain = g1_ref[...] * (1.0 + mod[:, D_MODEL:2 * D_MODEL])
    h = x * lax.rsqrt(jnp.mean(x * x, axis=-1, keepdims=True) + EPS) * gain + mod[:, 0:D_MODEL]

    mix_width = Q_WIDTH + 2 * KV_WIDTH
    col_g, col_w = 5 * A_WIDTH, 5 * A_WIDTH + mix_width

    def project(c0, c1, pieces):
        z = _dot(pieces[0], w_ref[:, c0:c1])
        if len(pieces) == 2:
            z = z + _dot(pieces[1], w_ref[:, c0:c1]) + _dot(pieces[0], wlo_ref[:, c0:c1])
        return z

    def tile(three_pass):
        pieces = _split2(h) if three_pass else (h.astype(BF16),)
        z_g = project(col_g, col_w, pieces)
        z_w = project(col_w, IN_WIDTH, pieces)
        za_ref[...] = project(0, col_g, pieces)

        cos = cos_ref[...]
        sin = sin_ref[...]
        ones_blk = ones_ref[...]
        lane = lax.broadcasted_iota(jnp.int32, (x.shape[0], LANES), 1)
        first16 = (lane % 32) < 16
        lane_half = lane // HEAD_DIM
        scale = LOG2E * HEAD_DIM ** -0.5

        qg = [z_g[:, LANES * b:LANES * (b + 1)] for b in range(3)]
        kg = z_g[:, Q_WIDTH:Q_WIDTH + KV_WIDTH]
        vg = z_g[:, Q_WIDTH + KV_WIDTH:]
        qw = [z_w[:, LANES * b:LANES * (b + 1)] for b in range(3)]
        kw = z_w[:, Q_WIDTH:Q_WIDTH + KV_WIDTH]
        vw = z_w[:, Q_WIDTH + KV_WIDTH:]

        qg = [_rope(_head_rms(b, qg_ref[...], ones_blk), cos, sin, first16) * scale for b in qg]
        kg = _rope(_head_rms(kg, kg_ref[...], ones_blk), cos, sin, first16)
        qw = [_rope(b, cos, sin, first16) * scale for b in qw]
        kw = _rope(kw, cos, sin, first16)

        qg_att = _q_attention_layout(qg, lane_half)
        qw_att = _q_attention_layout(qw, lane_half)
        qga_ref[...] = qg_att.astype(BF16)
        qwa_ref[...] = qw_att.astype(BF16)
        kvg_ref[...] = jnp.concatenate([kg, vg], axis=1).astype(BF16)
        kvw_ref[...] = jnp.concatenate([kw, vw], axis=1).astype(BF16)
        kv32_ref[...] = jnp.concatenate([kg, vg, kw, vw], axis=1)
        qg32_ref[...] = qg_att
        qw32_ref[...] = qw_att

    if precise_ctx:
        pl.when(is_lat)(functools.partial(tile, False))
        pl.when(jnp.logical_not(is_lat))(functools.partial(tile, True))
    else:
        tile(False)


def _token_specs(x_pair, tm, lat_tiles):
    off = lat_tiles if x_pair[1] is x_pair[0] else 0
    lat = pl.BlockSpec((tm, D_MODEL), lambda i, *_: (jnp.minimum(i, lat_tiles - 1), 0))
    ctx = pl.BlockSpec((tm, D_MODEL), lambda i, *_: (jnp.maximum(i, lat_tiles) - lat_tiles + off, 0))
    return [lat, ctx]


def _in_projection(x_pair, mod, g1, w_hi, w_lo, cos_t, sin_t, qn_g, kn_g, ones_blk, dims, layer=0):
    n = dims["n_lat"] + dims["n_ctx"]
    tm = TOKEN_TILE
    n_tiles = n // tm
    lat_tiles = dims["n_lat"] // tm
    ctx_tiles = n_tiles - lat_tiles
    tiles_per_seq = dims["t_lat"] // tm
    ctx_cond = dims["ctx_cond"]
    precise_ctx = w_lo is not None

    def cond_idx(i):
        return jnp.where(i < lat_tiles, i // tiles_per_seq, ctx_cond)

    def rope_idx(i):
        return jnp.where(i < lat_tiles, 1 + i % tiles_per_seq, 0)

    def ctx_idx(i):
        return jnp.where(i < lat_tiles, ctx_tiles, i - lat_tiles)

    row = lambda w: pl.BlockSpec((tm, w), lambda i: (i, 0))
    ctx_row = lambda w: pl.BlockSpec((tm, w), lambda i: (ctx_idx(i), 0))
    const = lambda a: pl.BlockSpec(a.shape, lambda i: (0,) * a.ndim, pipeline_mode=pl.Buffered(1))
    weight = lambda w: const(w) if w.ndim == 2 else pl.BlockSpec(
        (None,) + w.shape[1:], lambda i: (layer, 0, 0), pipeline_mode=pl.Buffered(1))
    weights = [w_hi, w_lo] if precise_ctx else [w_hi]
    ctx_rows = (ctx_tiles + 1) * tm
    return pl.pallas_call(
        functools.partial(_inproj_kernel, precise_ctx=precise_ctx, lat_tiles=lat_tiles),
        grid=(n_tiles,),
        in_specs=_token_specs(x_pair, tm, lat_tiles) + [
            pl.BlockSpec((None, 1, 6 * D_MODEL), lambda i: (cond_idx(i), 0, 0)),
            const(g1)] + [weight(w) for w in weights] + [
            pl.BlockSpec((tm, LANES), lambda i: (rope_idx(i), 0)),
            pl.BlockSpec((tm, LANES), lambda i: (rope_idx(i), 0)),
            const(qn_g), const(kn_g), const(ones_blk),
        ],
        out_specs=[
            row(5 * A_WIDTH), row(2 * Q_WIDTH), row(2 * KV_WIDTH), row(2 * Q_WIDTH), row(2 * KV_WIDTH),
            ctx_row(4 * KV_WIDTH), ctx_row(2 * Q_WIDTH), ctx_row(2 * Q_WIDTH),
        ],
        out_shape=[
            jax.ShapeDtypeStruct((n, 5 * A_WIDTH), F32),
            jax.ShapeDtypeStruct((n, 2 * Q_WIDTH), BF16),
            jax.ShapeDtypeStruct((n, 2 * KV_WIDTH), BF16),
            jax.ShapeDtypeStruct((n, 2 * Q_WIDTH), BF16),
            jax.ShapeDtypeStruct((n, 2 * KV_WIDTH), BF16),
            jax.ShapeDtypeStruct((ctx_rows, 4 * KV_WIDTH), F32),
            jax.ShapeDtypeStruct((ctx_rows, 2 * Q_WIDTH), F32),
            jax.ShapeDtypeStruct((ctx_rows, 2 * Q_WIDTH), F32),
        ],
        compiler_params=_cparams("arbitrary"),
        name="in_projection_precise_ctx" if precise_ctx else "in_projection",
    )(*x_pair, mod, g1, *weights, cos_t, sin_t, qn_g, kn_g, ones_blk)


def _hgrn_chunk(qa, v, fz, lb, st, consts, reverse, precise):
    cum, e_sel, x_sel, sub_mask, p_mask, v_mask, head_mask = consts
    log2e = LOG2E
    n_sub, half = CHUNK // SUB, SUB // 2
    q = _silu(qa)
    f = lb + (1.0 - lb) * _sigmoid(fz)
    logf2 = jnp.log(jnp.maximum(f, TINY)) * log2e
    kk = (1.0 - lb) * _sigmoid(-fz)

    b = _dot_sel(cum, logf2)
    b3 = b.reshape(n_sub, SUB, A_WIDTH)
    end = 0 if reverse else SUB - 1
    tot = jnp.broadcast_to(b3[:, end:end + 1, :], (n_sub, SUB, A_WIDTH)).reshape(CHUNK, A_WIDTH)
    q_in = q * jnp.exp2(b)
    k_out = kk * jnp.exp2(tot - b)
    d_sub = jnp.exp2(tot)

    c3 = (b - jnp.log(kk) * log2e).reshape(n_sub, SUB, A_WIDTH)
    b4 = b.reshape(n_sub, 2, half, A_WIDTH)
    q4 = q.reshape(n_sub, 2, half, A_WIDTH)
    r_local = lax.broadcasted_iota(jnp.int32, (n_sub, half, A_WIDTH), 1)
    zero_half = jnp.zeros((n_sub, half, A_WIDTH), F32)
    pieces = []
    for s in range(SUB):
        cs = c3[:, s:s + 1, :]
        parts = []
        for hsel in range(2):
            lo, hi = hsel * half, hsel * half + half - 1
            if (lo > s) if reverse else (hi < s):
                parts.append(zero_half)
                continue
            arg = b4[:, hsel] - cs
            if not ((hi <= s) if reverse else (lo >= s)):
                t_local = r_local + lo
                arg = jnp.where((t_local <= s) if reverse else (t_local >= s), arg, MASK_VALUE)
            parts.append(q4[:, hsel] * jnp.exp2(arg))
        piece = jnp.stack(parts, axis=1).reshape(CHUNK, A_WIDTH)
        pieces.append(piece if precise else piece.astype(BF16))
    w_cat = jnp.concatenate(pieces, axis=1)
    s_local = _mm_sel(w_cat, e_sel, precise)
    p = _mm_sel(s_local, x_sel, precise) * p_mask
    v_bd = jnp.concatenate([v] * A_HEADS, axis=0) * v_mask
    o_intra = _mm(p, v_bd, precise)

    v_op = v if precise else v.astype(BF16)
    k_op = k_out if precise else k_out.astype(BF16)
    q_op = q_in if precise else q_in.astype(BF16)
    sub_m = sub_mask
    order = range(n_sub - 1, -1, -1) if reverse else range(n_sub)
    new_st, o_inter = [], []
    for p in range(A_WIDTH // LANES):
        lanes = slice(LANES * p, LANES * (p + 1))
        k_exp = jnp.concatenate([k_op[:, lanes]] * n_sub, axis=1) * sub_m
        u_all = _mm(v_op[:, lanes], k_exp, precise, _TN)
        s = st[p]
        s_before = [None] * n_sub
        for j in order:
            s_before[j] = s
            s = d_sub[SUB * j:SUB * j + 1, lanes] * s + head_mask * u_all[:, LANES * j:LANES * (j + 1)]
        new_st.append(s)
        q_exp = jnp.concatenate([q_op[:, lanes]] * n_sub, axis=1) * sub_m
        o_inter.append(_mm(q_exp, jnp.concatenate(s_before, axis=1), precise, _NT))
    return o_intra + jnp.concatenate(o_inter, axis=1), jnp.stack(new_st, axis=0)


def _hgrn_kernel(*refs, has_s0, want_state, precise):
    refs = list(refs)
    seq_refs = [refs[6 * sq:6 * sq + 6] for sq in range(HGRN_SEQS)]
    del refs[:6 * HGRN_SEQS]
    lb_ref = refs.pop(0)
    s0_ref = refs.pop(0) if has_s0 else None
    cumf_ref, cumb_ref, e_ref, x_ref, sm_ref, pm_ref, vm_ref, hm_ref, of_ref, ob_ref = refs[:10]
    del refs[:10]
    sout_ref = refs.pop(0) if want_state else None
    st_ref, = refs
    c = pl.program_id(1)

    @pl.when(c == 0)
    def _():
        st_ref[...] = s0_ref[...] if has_s0 else jnp.zeros_like(st_ref)

    shared = (e_ref[...], x_ref[...], sm_ref[...], pm_ref[...], vm_ref[...], hm_ref[...])
    lb = lb_ref[...]
    for sq in range(HGRN_SEQS):
        qaf_ref, iaf_ref, fzf_ref, qab_ref, iab_ref, fzb_ref = seq_refs[sq]
        o_f, st_f = _hgrn_chunk(qaf_ref[...], iaf_ref[...], fzf_ref[...], lb[0:1], st_ref[sq, 0],
                                (cumf_ref[...],) + shared, False, precise)
        o_b, st_b = _hgrn_chunk(qab_ref[...], iab_ref[...], fzb_ref[...], lb[1:2], st_ref[sq, 1],
                                (cumb_ref[...],) + shared, True, precise)
        of_ref[sq] = o_f
        ob_ref[sq] = o_b
        st_ref[sq, 0] = st_f
        st_ref[sq, 1] = st_b

    if want_state:
        @pl.when(c == pl.num_programs(1) - 1)
        def _():
            sout_ref[...] = st_ref[...]


def _hgrn_consts():
    r = np.arange(CHUNK)
    same = (r[:, None] // SUB) == (r[None, :] // SUB)
    cum_f = (same & (r[None, :] <= r[:, None])).astype(np.float32)
    cum_b = (same & (r[None, :] >= r[:, None])).astype(np.float32)
    tot = same.astype(np.float32)
    e = np.zeros((SUB, A_HEADS, HEAD_DIM, A_HEADS, SUB), np.float32)
    for s in range(SUB):
        for h in range(A_HEADS):
            e[s, h, :, h, s] = 1.0
    e = e.reshape(SUB * A_WIDTH, A_HEADS * SUB)
    x = np.zeros((A_HEADS, SUB, A_HEADS, CHUNK), np.float32)
    for h in range(A_HEADS):
        for s in range(CHUNK):
            x[h, s % SUB, h, s] = 1.0
    x = x.reshape(A_HEADS * SUB, A_HEADS * CHUNK)
    p_mask = np.tile(tot, (1, A_HEADS))
    hv = np.arange(A_HEADS * CHUNK) // CHUNK
    hk = np.arange(A_WIDTH) // HEAD_DIM
    v_mask = (hv[:, None] == hk[None, :]).astype(np.float32)
    head_mask = (hk[:LANES, None] == hk[None, :LANES]).astype(np.float32)
    sub_mask = np.repeat(r[:, None] // SUB == np.arange(CHUNK // SUB)[None, :], LANES, axis=1).astype(np.float32)
    sel = [jnp.asarray(a, BF16) for a in (cum_f, cum_b, e, x, sub_mask)]
    return sel + [jnp.asarray(a, F32) for a in (p_mask, v_mask, head_mask)]


def _hgrn_scan(za, first_row, lb2, s0_t, n_seq, t, precise, want_state, name):
    nc = t // CHUNK
    groups = n_seq // HGRN_SEQS
    base = first_row // CHUNK
    za3 = za.reshape(za.shape[0] // CHUNK, CHUNK, za.shape[1])
    consts = _hgrn_consts()
    state_shape = (HGRN_SEQS, 2, A_WIDTH // LANES, LANES, LANES)
    const = lambda a: pl.BlockSpec(a.shape, lambda p, c: (0,) * a.ndim)

    def chunk_spec(sq, col, backward):
        def index(p, c):
            return (base + (p * HGRN_SEQS + sq) * nc + (nc - 1 - c if backward else c), 0, col)
        return pl.BlockSpec((None, CHUNK, A_WIDTH), index)

    fwd = lambda col: pl.BlockSpec((None, HGRN_SEQS, CHUNK, A_WIDTH), lambda p, c: (p, 0, c, col))
    bwd = lambda col: pl.BlockSpec((None, HGRN_SEQS, CHUNK, A_WIDTH), lambda p, c: (p, 0, nc - 1 - c, col))
    state_spec = pl.BlockSpec((None,) + state_shape, lambda p, c: (p, 0, 0, 0, 0, 0))
    operands = [za3] * (6 * HGRN_SEQS) + [lb2]
    in_specs = [chunk_spec(sq, col, backward) for sq in range(HGRN_SEQS)
                for col, backward in ((0, False), (1, False), (2, False), (0, True), (1, True), (3, True))]
    in_specs.append(const(lb2))
    if s0_t is not None:
        operands.append(s0_t.reshape((groups,) + state_shape))
        in_specs.append(state_spec)
    o_shape = jax.ShapeDtypeStruct((groups, HGRN_SEQS, t, A_WIDTH), F32)
    out_specs, out_shape = [fwd(0), bwd(0)], [o_shape, o_shape]
    if want_state:
        out_specs.append(state_spec)
        out_shape.append(jax.ShapeDtypeStruct((groups,) + state_shape, F32))
    outs = pl.pallas_call(
        functools.partial(_hgrn_kernel, has_s0=s0_t is not None, want_state=want_state, precise=precise),
        grid=(groups, nc),
        in_specs=in_specs + [const(c) for c in consts],
        out_specs=out_specs,
        out_shape=out_shape,
        scratch_shapes=[pltpu.VMEM(state_shape, F32)],
        compiler_params=_cparams("parallel", "arbitrary"),
        name=name,
    )(*operands, *consts)
    o_f, o_b = (o.reshape(n_seq * t, A_WIDTH) for o in outs[:2])
    s_fin = outs[2].reshape((n_seq,) + state_shape[1:]) if want_state else None
    return o_f, o_b, s_fin


def _attend(q3, segments, sink_row, precise=False):
    scores = []
    for k, _, valid in segments:
        s = _mm(k, q3, precise, _NT)
        if valid is not None:
            s = jnp.where(valid, s, MASK_VALUE)
        scores.append(s)
    m = scores[0].max(axis=0, keepdims=True)
    for s in scores[1:]:
        m = jnp.maximum(m, s.max(axis=0, keepdims=True))
    if sink_row is not None:
        m = jnp.maximum(m, sink_row)
    denom = jnp.exp2(sink_row - m) if sink_row is not None else 0.0
    acc = 0.0
    for s, (_, v, _) in zip(scores, segments):
        p = jnp.exp2(s - m)
        denom = denom + p.sum(axis=0, keepdims=True)
        acc = acc + _mm(v, p, precise, _TN)
    return (acc / denom).T


def _stack_heads(q, kvh):
    base = kvh * GQA_GROUP
    return jnp.concatenate([q[:, LANES * (base + g):LANES * (base + g + 1)] for g in range(GQA_GROUP)], axis=0)


def _merge_heads(res, tq):
    lane = lax.broadcasted_iota(jnp.int32, (tq, LANES), 1)
    heads = []
    for j in range(B_Q_HEADS):
        kvh, g = divmod(j, GQA_GROUP)
        o = res[kvh][g * tq:(g + 1) * tq]
        if kvh != j % 2:
            o = pltpu.roll(o, HEAD_DIM, 1)
        heads.append(o)
    blocks = [jnp.where(lane < HEAD_DIM, heads[2 * b], heads[2 * b + 1]) for b in range(B_Q_HEADS // 2)]
    return jnp.concatenate(blocks, axis=1)


def _sink_row(sink_ref, kvh, tq):
    return jnp.concatenate(
        [jnp.full((1, tq), sink_ref[kvh * GQA_GROUP + g] * LOG2E, F32) for g in range(GQA_GROUP)], axis=1)


def _global_attn_kernel(q_ref, kv_ref, ck_ref, cv_ref, o_ref, *, tq):
    q = q_ref[...]
    kv = kv_ref[...]
    k, v = kv[:, :KV_WIDTH], kv[:, KV_WIDTH:]
    ck, cv = ck_ref[...], cv_ref[...]
    segments = [(k[c:c + KEY_CHUNK], v[c:c + KEY_CHUNK], None) for c in range(0, k.shape[0], KEY_CHUNK)]
    segments.append((ck, cv, None))
    res = [_attend(_stack_heads(q, kvh), segments, None) for kvh in range(B_KV_HEADS)]
    o_ref[...] = _merge_heads(res, tq).astype(BF16)


def _global_attention(q_att, kv, ck, cv, dims):
    t, n_seq = dims["t_lat"], dims["n_lat_seq"]
    tq = 256
    nq = t // tq
    return pl.pallas_call(
        functools.partial(_global_attn_kernel, tq=tq),
        grid=(n_seq, nq),
        in_specs=[
            pl.BlockSpec((tq, 2 * Q_WIDTH), lambda b, i: (b * nq + i, 0)),
            pl.BlockSpec((t, 2 * KV_WIDTH), lambda b, i: (b, 0)),
            pl.BlockSpec((None,) + ck.shape[1:], lambda b, i: (b, 0, 0)),
            pl.BlockSpec((None,) + cv.shape[1:], lambda b, i: (b, 0, 0)),
        ],
        out_specs=pl.BlockSpec((tq, Q_WIDTH), lambda b, i: (b * nq + i, 0)),
        out_shape=jax.ShapeDtypeStruct((n_seq * t, Q_WIDTH), BF16),
        compiler_params=_cparams("parallel", "arbitrary"),
        name="global_attention",
    )(q_att, kv, ck, cv)


def _window_attn_kernel(sink_ref, q_ref, kv_ref, ck_ref, cv_ref, o_ref, *, tq, t):
    j = pl.program_id(1)
    span = tq + 2 * WINDOW
    start = pl.multiple_of(jnp.clip(j * tq - WINDOW, 0, t - span), WINDOW)
    q = q_ref[...]
    kv = kv_ref[pl.ds(start, span), :]
    k, v = kv[:, :KV_WIDTH], kv[:, KV_WIDTH:]
    key_pos = start + lax.broadcasted_iota(jnp.int32, (span, GQA_GROUP * tq), 0)
    query_pos = j * tq + lax.broadcasted_iota(jnp.int32, (span, GQA_GROUP * tq), 1) % tq
    valid = jnp.abs(query_pos - key_pos) <= WINDOW
    segments = [(k, v, valid), (ck_ref[...], cv_ref[...], None)]
    res = [_attend(_stack_heads(q, kvh), segments, _sink_row(sink_ref, kvh, tq)) for kvh in range(C_KV_HEADS)]
    o_ref[...] = _merge_heads(res, tq).astype(BF16)


def _window_attention(sink, q_att, kv, ck, cv, dims):
    t, n_seq = dims["t_lat"], dims["n_lat_seq"]
    tq = 2 * WINDOW
    nq = t // tq
    return pl.pallas_call(
        functools.partial(_window_attn_kernel, tq=tq, t=t),
        grid=(n_seq, nq),
        in_specs=[
            pl.BlockSpec(memory_space=pltpu.SMEM),
            pl.BlockSpec((tq, 2 * Q_WIDTH), lambda b, i: (b * nq + i, 0)),
            pl.BlockSpec((t, 2 * KV_WIDTH), lambda b, i: (b, 0)),
            pl.BlockSpec((None,) + ck.shape[1:], lambda b, i: (b, 0, 0)),
            pl.BlockSpec((None,) + cv.shape[1:], lambda b, i: (b, 0, 0)),
        ],
        out_specs=pl.BlockSpec((tq, Q_WIDTH), lambda b, i: (b * nq + i, 0)),
        out_shape=jax.ShapeDtypeStruct((n_seq * t, Q_WIDTH), BF16),
        compiler_params=_cparams("parallel", "arbitrary"),
        name="window_attention",
    )(sink, q_att, kv, ck, cv)


def _context_attn_kernel(sink_ref, qg_ref, qw_ref, kv_ref, og_ref, ow_ref, *, t, precise):
    kv = kv_ref[...]
    for q_ref, o_ref, off, use_sink in ((qg_ref, og_ref, 0, False), (qw_ref, ow_ref, 2 * KV_WIDTH, True)):
        q = q_ref[...]
        k, v = kv[:, off:off + KV_WIDTH], kv[:, off + KV_WIDTH:off + 2 * KV_WIDTH]
        res = [_attend(_stack_heads(q, kvh), [(k, v, None)],
                       _sink_row(sink_ref, kvh, t) if use_sink else None, precise)
               for kvh in range(B_KV_HEADS)]
        o_ref[...] = _merge_heads(res, t)


def _context_attention(sink, qg32, qw32, kv32, dims, precise):
    t, n_seq = dims["t_ctx"], dims["n_ctx_seq"]
    seq = lambda w: pl.BlockSpec((t, w), lambda b: (b, 0))
    return pl.pallas_call(
        functools.partial(_context_attn_kernel, t=t, precise=precise),
        grid=(n_seq,),
        in_specs=[pl.BlockSpec(memory_space=pltpu.SMEM), seq(2 * Q_WIDTH), seq(2 * Q_WIDTH), seq(4 * KV_WIDTH)],
        out_specs=[seq(Q_WIDTH), seq(Q_WIDTH)],
        out_shape=[jax.ShapeDtypeStruct((n_seq * t, Q_WIDTH), F32)] * 2,
        compiler_params=_cparams("parallel"),
        name="context_attention_precise" if precise else "context_attention",
    )(sink, qg32, qw32, kv32)


def _route(scores_t, sel_t):
    s = [scores_t[e:e + 1, :] for e in range(N_EXPERTS)]
    z = [sel_t[e:e + 1, :] for e in range(N_EXPERTS)]
    gs = []
    for g in range(N_GROUPS):
        m = z[g * EXPERTS_PER_GROUP:(g + 1) * EXPERTS_PER_GROUP]
        best = None
        for a in range(EXPERTS_PER_GROUP):
            for b in range(a + 1, EXPERTS_PER_GROUP):
                pair = m[a] + m[b]
                best = pair if best is None else jnp.maximum(best, pair)
        gs.append(best)
    combine, groups = [], []
    for g in range(N_GROUPS):
        chosen_g = None
        for g2 in range(N_GROUPS):
            if g2 == g:
                continue
            c = (gs[g] > gs[g2]) if g2 < g else (gs[g] >= gs[g2])
            chosen_g = c if chosen_g is None else jnp.logical_and(chosen_g, c)
        base = g * EXPERTS_PER_GROUP
        picked = []
        for a in range(EXPERTS_PER_GROUP):
            rank = 0.0
            for b in range(EXPERTS_PER_GROUP):
                if b == a:
                    continue
                ahead = (z[base + b] >= z[base + a]) if b < a else (z[base + b] > z[base + a])
                rank = rank + ahead.astype(F32)
            picked.append(jnp.where(jnp.logical_and(chosen_g, rank < 2.0), s[base + a], 0.0))
        denom = picked[0] + picked[1] + picked[2] + picked[3]
        denom = jnp.where(chosen_g, denom, 1.0)
        combine.extend(pk / denom for pk in picked)
        groups.append(jnp.where(chosen_g, 1.0, 0.0))
    return jnp.concatenate(combine, axis=0), jnp.concatenate(groups, axis=0)


def _outproj_kernel(*refs, lat_tiles, precise_ctx):
    if precise_ctx:
        (ofl_ref, ofc_ref, obl_ref, obc_ref, ga_ref,
         ogl_ref, ogc_ref, owl_ref, owc_ref, xl_ref, xc_ref, mod_ref, ag_ref,
         ones_ref, w_ref, wlo_ref, g2_ref, rw_ref, rb_ref, before_ref,
         x1_ref, h2_ref, comb_ref, pos_ref, cnt_ref, mix_ref) = refs
    else:
        (ofl_ref, ofc_ref, obl_ref, obc_ref, ga_ref,
         ogl_ref, ogc_ref, owl_ref, owc_ref, xl_ref, xc_ref, mod_ref, ag_ref,
         ones_ref, w_ref, g2_ref, rw_ref, rb_ref, before_ref,
         x1_ref, h2_ref, comb_ref, pos_ref, cnt_ref, mix_ref) = refs
        wlo_ref = None
    is_lat = pl.program_id(0) < lat_tiles
    o = jnp.where(is_lat, ofl_ref[...] + obl_ref[...], ofc_ref[...] + obc_ref[...])
    oa = _head_rms(o, ag_ref[...], ones_ref[...]) * _sigmoid(ga_ref[...])
    def mix(operands, precise):
        if precise:
            ah, al = _split2(jnp.concatenate(operands, axis=1))
            mix_ref[...] = _dot(ah, w_ref[...]) + _dot(al, w_ref[...]) + _dot(ah, wlo_ref[...])
        else:
            mix_ref[...] = _dot(jnp.concatenate([a.astype(BF16) for a in operands], axis=1), w_ref[...])

    pl.when(is_lat)(lambda: mix((oa, ogl_ref[...], owl_ref[...]), False))
    pl.when(jnp.logical_not(is_lat))(lambda: mix((oa, ogc_ref[...], owc_ref[...]), precise_ctx))

    mod = mod_ref[...]
    gate1 = mod[:, 2 * D_MODEL:3 * D_MODEL]
    sh2 = mod[:, 3 * D_MODEL:4 * D_MODEL]
    sc2 = mod[:, 4 * D_MODEL:5 * D_MODEL]
    x1 = jnp.where(is_lat, xl_ref[...], xc_ref[...]) + gate1 * mix_ref[...]
    x1_ref[...] = x1
    gain = g2_ref[...] * (1.0 + sc2)
    h2 = x1 * lax.rsqrt(jnp.mean(x1 * x1, axis=-1, keepdims=True) + EPS) * gain + sh2
    h2_ref[...] = h2.astype(BF16)
    scores_t = _sigmoid(_mm(rw_ref[...], h2, True, _NT))
    comb_t, group_t = _route(scores_t, scores_t + rb_ref[...])
    comb_ref[...] = comb_t
    rank = _dot(group_t.astype(BF16), before_ref[...])
    counts = group_t.sum(axis=1, keepdims=True)
    pos, offset = 0.0, 0.0
    for g in range(N_GROUPS):
        pos = pos + group_t[g:g + 1] * (rank[g:g + 1] + offset)
        offset = offset + counts[g:g + 1]
    pos_ref[...] = pos
    pad = jnp.zeros((8 - N_GROUPS, 1), F32)
    cnt_ref[...] = jnp.broadcast_to(jnp.concatenate([counts, pad], axis=0), (8, LANES)).astype(jnp.int32)


def _out_projection(o_f, o_b, za, og_lat, og_ctx, ow_lat, ow_ctx, x_pair, mod, an_g, ones_blk, w_hi, w_lo, g2,
                    rw_t, rb, dims, layer=0):
    n = dims["n_lat"] + dims["n_ctx"]
    tm = TOKEN_TILE
    n_tiles = n // tm
    lat_tiles = dims["n_lat"] // tm
    tiles_per_seq = dims["t_lat"] // tm
    ctx_cond = dims["ctx_cond"]
    precise_ctx = w_lo is not None
    cond_idx = lambda i: jnp.where(i < lat_tiles, i // tiles_per_seq, ctx_cond)
    row = lambda w: pl.BlockSpec((tm, w), lambda i: (i, 0))
    lat = lambda w: pl.BlockSpec((tm, w), lambda i: (jnp.minimum(i, lat_tiles - 1), 0))
    ctx = lambda w: pl.BlockSpec((tm, w), lambda i: (jnp.maximum(i - lat_tiles, 0), 0))
    const = lambda a: pl.BlockSpec(a.shape, lambda i: (0,) * a.ndim, pipeline_mode=pl.Buffered(1))
    weight = lambda w: const(w) if w.ndim == 2 else pl.BlockSpec(
        (None,) + w.shape[1:], lambda i: (layer, 0, 0), pipeline_mode=pl.Buffered(1))
    weights = [w_hi, w_lo] if precise_ctx else [w_hi]
    t_idx = np.arange(tm)
    before = jnp.asarray((t_idx[:, None] < t_idx[None, :]).astype(np.float32), BF16)
    return pl.pallas_call(
        functools.partial(_outproj_kernel, lat_tiles=lat_tiles, precise_ctx=precise_ctx),
        grid=(n_tiles,),
        in_specs=[
            lat(A_WIDTH), ctx(A_WIDTH), lat(A_WIDTH), ctx(A_WIDTH),
            pl.BlockSpec((tm, A_WIDTH), lambda i: (i, 4)),
            lat(Q_WIDTH), ctx(Q_WIDTH), lat(Q_WIDTH), ctx(Q_WIDTH)] + _token_specs(x_pair, tm, lat_tiles) + [
            pl.BlockSpec((None, 1, 6 * D_MODEL), lambda i: (cond_idx(i), 0, 0)),
            const(an_g), const(ones_blk)] + [weight(w) for w in weights] + [
            const(g2), const(rw_t), const(rb), const(before),
        ],
        out_specs=[row(D_MODEL), row(D_MODEL), pl.BlockSpec((N_EXPERTS, tm), lambda i: (0, i)),
                   pl.BlockSpec((1, tm), lambda i: (0, i)), pl.BlockSpec((8, LANES), lambda i: (i, 0))],
        out_shape=[
            jax.ShapeDtypeStruct((n, D_MODEL), F32),
            jax.ShapeDtypeStruct((n, D_MODEL), BF16),
            jax.ShapeDtypeStruct((N_EXPERTS, n), F32),
            jax.ShapeDtypeStruct((1, n), F32),
            jax.ShapeDtypeStruct((n_tiles * 8, LANES), jnp.int32),
        ],
        scratch_shapes=[pltpu.VMEM((tm, D_MODEL), F32)],
        compiler_params=_cparams("arbitrary"),
        name="out_projection_precise_ctx" if precise_ctx else "out_projection",
    )(*o_f, *o_b, za, og_lat, og_ctx, ow_lat, ow_ctx, *x_pair, mod, an_g, ones_blk, *weights, g2, rw_t, rb, before)


def _moe_kernel(off_ref, h_ref, comb_ref, pos_ref, x1_ref, mod_ref, wgu_ref, wd_ref, fg_ref, *rest,
                final, lat_tiles):
    *outs, hs_ref, cs_ref, acc_ref = rest
    i = pl.program_id(0)
    tm = h_ref.shape[0]
    row = lax.broadcasted_iota(jnp.int32, (tm, tm), 0).astype(F32)
    perm = jnp.where(row == pos_ref[...], 1.0, 0.0).astype(BF16)
    hs_ref[...] = _dot(perm, h_ref[...]).astype(BF16)
    cs_ref[...] = sum(lax.dot_general(perm, c, _NT, preferred_element_type=F32)
                      for c in _split3(comb_ref[...]))
    acc_ref[...] = jnp.zeros_like(acc_ref)
    lane = lax.broadcasted_iota(jnp.int32, (MOE_BLOCK, N_EXPERTS), 1)

    for r in range(tm // MOE_BLOCK):
        rows = slice(r * MOE_BLOCK, (r + 1) * MOE_BLOCK)

        def group_body(g, carry, rows=rows, r=r):
            has_tokens = jnp.logical_and(off_ref[i, g] < (r + 1) * MOE_BLOCK, off_ref[i, g + 1] > r * MOE_BLOCK)

            @pl.when(has_tokens)
            def _():
                h = hs_ref[rows, :]
                cs = cs_ref[rows, :]
                acts = []
                for a in range(EXPERTS_PER_GROUP):
                    e = g * EXPERTS_PER_GROUP + a
                    gu = _dot(h, wgu_ref[e])
                    gate, up = gu[:, :D_EXPERT], gu[:, D_EXPERT:]
                    ce = jnp.sum(jnp.where(lane == e, cs, 0.0), axis=1, keepdims=True)
                    acts.append((_silu(gate) * up * ce).astype(BF16))
                acc_ref[rows, :] += _dot(jnp.concatenate(acts, axis=1), wd_ref[g])

            return carry

        lax.fori_loop(0, N_GROUPS, group_body, 0)

    y_hi, y_lo = _split2(acc_ref[...])
    y = (lax.dot_general(perm, y_hi, _TN, preferred_element_type=F32)
         + lax.dot_general(perm, y_lo, _TN, preferred_element_type=F32))
    x2 = x1_ref[...] + mod_ref[...][:, 5 * D_MODEL:] * y
    if not final:
        outs[0][...] = x2
    else:
        yn = x2 * lax.rsqrt(jnp.mean(x2 * x2, axis=-1, keepdims=True) + EPS) * fg_ref[...]
        is_lat = i < lat_tiles

        @pl.when(is_lat)
        def _():
            outs[0][...] = yn

        @pl.when(jnp.logical_not(is_lat))
        def _():
            outs[1][...] = yn


def _moe(h2, comb_t, pos, counts, x1, mod, w_gu, w_d, layer, final_g, dims, final):
    n = x1.shape[0]
    tm = TOKEN_TILE
    lat_tiles = dims["n_lat"] // tm
    tiles_per_seq = dims["t_lat"] // tm
    ctx_cond = dims["ctx_cond"]
    cond_idx = lambda i: jnp.where(i < lat_tiles, i // tiles_per_seq, ctx_cond)
    offsets = jnp.concatenate([jnp.zeros((n // tm, 1), jnp.int32), jnp.cumsum(counts, axis=1)], axis=1)
    if final:
        out_specs = [pl.BlockSpec((tm, D_MODEL), lambda i, *_: (jnp.minimum(i, lat_tiles - 1), 0)),
                     pl.BlockSpec((tm, D_MODEL), lambda i, *_: (jnp.maximum(i - lat_tiles, 0), 0))]
        out_shape = [jax.ShapeDtypeStruct((dims["n_lat"], D_MODEL), F32),
                     jax.ShapeDtypeStruct((dims["n_ctx"], D_MODEL), F32)]
    else:
        out_specs = pl.BlockSpec((tm, D_MODEL), lambda i, *_: (i, 0))
        out_shape = jax.ShapeDtypeStruct((n, D_MODEL), F32)
    resident = lambda a: pl.BlockSpec((None,) + a.shape[1:], lambda i, *_: (layer,) + (0,) * (a.ndim - 1),
                                      pipeline_mode=pl.Buffered(1))
    grid_spec = pltpu.PrefetchScalarGridSpec(
        num_scalar_prefetch=1,
        grid=(n // tm,),
        in_specs=[
            pl.BlockSpec((tm, D_MODEL), lambda i, *_: (i, 0)),
            pl.BlockSpec((N_EXPERTS, tm), lambda i, *_: (0, i)),
            pl.BlockSpec((1, tm), lambda i, *_: (0, i)),
            pl.BlockSpec((tm, D_MODEL), lambda i, *_: (i, 0)),
            pl.BlockSpec((None, 1, 6 * D_MODEL), lambda i, *_: (cond_idx(i), 0, 0)),
            resident(w_gu), resident(w_d),
            pl.BlockSpec((1, D_MODEL), lambda i, *_: (0, 0)),
        ],
        out_specs=out_specs,
        scratch_shapes=[pltpu.VMEM((tm, D_MODEL), BF16), pltpu.VMEM((tm, N_EXPERTS), F32),
                        pltpu.VMEM((tm, D_MODEL), F32)],
    )
    return pl.pallas_call(
        functools.partial(_moe_kernel, final=final, lat_tiles=lat_tiles),
        grid_spec=grid_spec,
        out_shape=out_shape,
        compiler_params=_cparams("arbitrary"),
        name="moe_final" if final else "moe",
    )(offsets, h2, comb_t, pos, x1, mod, w_gu, w_d, final_g)


def _rope_tables(t, tile):
    n_freq = HEAD_DIM // 4
    pos = np.arange(t)
    freqs = ROPE_THETA ** (-np.arange(n_freq, dtype=np.float64) / n_freq)
    ang_row = (pos // GRID_W)[:, None] * freqs
    ang_col = (pos % GRID_W)[:, None] * freqs
    cos_h = np.concatenate([np.cos(ang_row)] * 2 + [np.cos(ang_col)] * 2, axis=1)
    sin_h = np.concatenate([-np.sin(ang_row), np.sin(ang_row), -np.sin(ang_col), np.sin(ang_col)], axis=1)
    cos_t = np.concatenate([np.ones((tile, LANES)), np.concatenate([cos_h, cos_h], axis=1)], axis=0)
    sin_t = np.concatenate([np.zeros((tile, LANES)), np.concatenate([sin_h, sin_h], axis=1)], axis=0)
    return jnp.asarray(cos_t, F32), jnp.asarray(sin_t, F32)


def _head_ones(width):
    r = np.arange(width) // HEAD_DIM
    return jnp.asarray((r[:, None] == r[None, :]).astype(np.float32), BF16)


def _split_kernel(w_ref, hi_ref, lo_ref):
    hi, lo = _split2(w_ref[...])
    hi_ref[...] = hi
    lo_ref[...] = lo


def _hi_lo(w_all, layer):
    _, rows, width = w_all.shape
    tr = 256
    spec = pl.BlockSpec((tr, width), lambda i: (i, 0))
    return pl.pallas_call(
        _split_kernel,
        grid=(rows // tr,),
        in_specs=[pl.BlockSpec((None, tr, width), lambda i: (layer, i, 0))],
        out_specs=[spec, spec],
        out_shape=[jax.ShapeDtypeStruct((rows, width), BF16)] * 2,
        compiler_params=_cparams("parallel"),
        name="split_weight",
    )(w_all)


def _state_to_block_diag_t(s):
    per_pair = LANES // HEAD_DIM
    st = jnp.swapaxes(s, -1, -2).reshape(s.shape[:2] + (A_HEADS // per_pair, per_pair, HEAD_DIM, HEAD_DIM))
    eye = jnp.eye(per_pair, dtype=s.dtype)
    out = jnp.einsum("bdphvk,hg->bdphvgk", st, eye)
    return out.reshape(s.shape[:2] + (A_HEADS // per_pair, LANES, LANES))


def _block_diag_t_to_state(st):
    per_pair = LANES // HEAD_DIM
    s7 = st.reshape(st.shape[:3] + (per_pair, HEAD_DIM, per_pair, HEAD_DIM))
    diag = jnp.stack([s7[:, :, :, h, :, h, :] for h in range(per_pair)], axis=3)
    return jnp.swapaxes(diag, -1, -2).reshape(st.shape[:2] + (A_HEADS, HEAD_DIM, HEAD_DIM))


def kernel(x_prompt, x_sample, cache_glob_k, cache_glob_v, cache_win_k, cache_win_v, state_hgrn, c, c_ctx,
           mod_w, mod_b, norm1_g, norm2_g, w_in, w_out, hgrn_lb, hgrn_norm_g, q_norm_g, k_norm_g, win_sink,
           router_w, router_b, w_gate_up, w_down, final_g):
    n_ctx_seq, t_ctx, d = x_prompt.shape
    n_lat_seq, t_lat, _ = x_sample.shape
    depth = mod_w.shape[0]
    past = cache_glob_k.shape[2]
    n_lat, n_ctx = n_lat_seq * t_lat, n_ctx_seq * t_ctx
    dims = dict(n_ctx_seq=n_ctx_seq, t_ctx=t_ctx, n_lat_seq=n_lat_seq, t_lat=t_lat,
                n_lat=n_lat, n_ctx=n_ctx, ctx_cond=n_lat_seq)
    assert d == D_MODEL and n_lat_seq < COND_ROWS
    assert n_ctx % TOKEN_TILE == 0 and t_lat % TOKEN_TILE == 0
    assert t_ctx % CHUNK == 0 and t_lat >= 4 * WINDOW

    x_pair = (x_sample.reshape(n_lat, d), x_prompt.reshape(n_ctx, d))
    cond = jnp.concatenate([c, c_ctx[None, :], jnp.zeros((COND_ROWS - n_lat_seq - 1, d), F32)], axis=0)
    mod = _modulation(cond, mod_w, mod_b).reshape(depth, COND_ROWS, 1, 6 * d)

    p_lb = jax.nn.softmax(hgrn_lb.astype(F32), axis=1)
    lbs = jnp.cumsum(p_lb, axis=1) - p_lb[:, :1]

    cos_t, sin_t = _rope_tables(t_lat, TOKEN_TILE)
    ones128, ones256 = _head_ones(LANES), _head_ones(A_WIDTH)
    tile2 = lambda g, reps: jnp.tile(g, reps)[None, :]
    rw_t = router_w.T
    rb = router_b[:, None]
    fg = final_g[None, :]

    w_gu_all = w_gate_up.astype(BF16)
    w_d_all = w_down.astype(BF16).reshape(depth, N_GROUPS, EXPERTS_PER_GROUP * D_EXPERT, d)
    w_in_last, w_out_last = w_in[depth - 1:].astype(BF16), w_out[depth - 1:].astype(BF16)

    new_kv, new_state = [], []
    for l in range(depth):
        precise = l < depth - 1
        win_hi, win_lo = _hi_lo(w_in, l) if precise else (w_in_last, None)
        za, qg_att, kvg, qw_att, kvw, kv32, qg32, qw32 = _in_projection(
            x_pair, mod[l], norm1_g[l][None, :], win_hi, win_lo, cos_t, sin_t,
            tile2(q_norm_g[l], 2), tile2(k_norm_g[l], 2), ones128, dims)

        s0 = state_hgrn[:, l].astype(F32)
        of_lat, ob_lat, _ = _hgrn_scan(za, 0, lbs[:, l], _state_to_block_diag_t(s0), n_lat_seq, t_lat,
                                       False, False, "hgrn_latent")
        of_ctx, ob_ctx, s_fin = _hgrn_scan(za, n_lat, lbs[:, l], None, n_ctx_seq, t_ctx, precise, True,
                                           "hgrn_context_precise" if precise else "hgrn_context")

        cast_cache = lambda a: a[:, l].reshape(n_lat_seq, past, KV_WIDTH).astype(BF16)
        og_lat = _global_attention(qg_att, kvg, cast_cache(cache_glob_k), cast_cache(cache_glob_v), dims)
        ow_lat = _window_attention(win_sink[l], qw_att, kvw, cast_cache(cache_win_k),
                                   cast_cache(cache_win_v), dims)
        og_ctx, ow_ctx = _context_attention(win_sink[l], qg32, qw32, kv32, dims, precise)

        wout_hi, wout_lo = _hi_lo(w_out, l) if precise else (w_out_last, None)
        x1, h2, comb_t, pos, cnt = _out_projection(
            (of_lat, of_ctx), (ob_lat, ob_ctx), za, og_lat, og_ctx, ow_lat, ow_ctx, x_pair, mod[l],
            tile2(hgrn_norm_g[l], A_HEADS), ones256, wout_hi, wout_lo, norm2_g[l][None, :], rw_t, rb, dims)

        counts = cnt.reshape(-1, 8, LANES)[:, :N_GROUPS, 0]
        x = _moe(h2, comb_t, pos, counts, x1, mod[l], w_gu_all, w_d_all, l, fg, dims, final=(l == depth - 1))
        x_pair = (x, x)

        new_kv.append(kv32[:n_ctx].reshape(n_ctx_seq, t_ctx, 4, B_KV_HEADS, HEAD_DIM))
        new_state.append(_block_diag_t_to_state(s_fin))

    y_sample = x[0].reshape(n_lat_seq, t_lat, d)
    y_prompt = x[1].reshape(n_ctx_seq, t_ctx, d)
    kv = jnp.stack(new_kv, axis=1)
    return (y_prompt, y_sample, kv[:, :, :, 0], kv[:, :, :, 1], kv[:, :, :, 2], kv[:, :, :, 3],
            jnp.stack(new_state, axis=1).astype(x_prompt.dtype))
```

```python
import functools

import numpy as np
import jax
import jax.numpy as jnp
from jax import lax
from jax.experimental import pallas as pl
from jax.experimental.pallas import tpu as pltpu

F32 = jnp.float32
BF16 = jnp.bfloat16

D_MODEL = 1024
HEAD_DIM = 64
GRID_W = 64
A_HEADS = 4
A_WIDTH = A_HEADS * HEAD_DIM
B_Q_HEADS = 6
B_KV_HEADS = 2
C_Q_HEADS = 6
C_KV_HEADS = 2
Q_WIDTH = B_Q_HEADS * HEAD_DIM
KV_WIDTH = B_KV_HEADS * HEAD_DIM
GQA_GROUP = B_Q_HEADS // B_KV_HEADS
WINDOW = 128
ROPE_THETA = 10000.0
IN_WIDTH = 5 * A_WIDTH + 2 * (Q_WIDTH + 2 * KV_WIDTH)
N_EXPERTS = 16
N_GROUPS = 4
EXPERTS_PER_GROUP = N_EXPERTS // N_GROUPS
D_EXPERT = 256
EPS = 1e-6
MASK_VALUE = -1e30
TINY = 1e-30
LOG2E = 1.4426950408889634

LANES = 128
SUB = 16
CHUNK = 128
HGRN_SEQS = 2
TOKEN_TILE = 512
MOE_BLOCK = 128
KEY_CHUNK = 512
COND_ROWS = 16
VMEM_LIMIT = 58 * 1024 * 1024

_NT = (((1,), (1,)), ((), ()))
_TN = (((0,), (0,)), ((), ()))


def _cparams(*sem):
    return pltpu.CompilerParams(dimension_semantics=sem, vmem_limit_bytes=VMEM_LIMIT)


def _dot(a, b):
    return jnp.dot(a, b, preferred_element_type=F32)


def _split2(x):
    h = x.astype(BF16)
    return h, (x - h.astype(F32)).astype(BF16)


def _split3(x):
    h = x.astype(BF16)
    r = x - h.astype(F32)
    m = r.astype(BF16)
    l = (r - m.astype(F32)).astype(BF16)
    return h, m, l


def _dot_sel(c, x):
    h, m, l = _split3(x)
    return _dot(c, h) + _dot(c, m) + _dot(c, l)


def _mm(a, b, precise, dims=None):
    if dims is None:
        dims = (((a.ndim - 1,), (0,)), ((), ()))
    dg = lambda x, y: lax.dot_general(x, y, dims, preferred_element_type=F32)
    if not precise:
        return dg(a.astype(BF16), b.astype(BF16))
    ah, al = _split2(a)
    bh, bl = _split2(b)
    return dg(ah, bh) + dg(ah, bl) + dg(al, bh)


def _mm_sel(a, sel, precise):
    if not precise:
        return _dot(a.astype(BF16), sel)
    ah, al = _split2(a)
    return _dot(ah, sel) + _dot(al, sel)


def _sigmoid(x):
    return jax.nn.sigmoid(x)


def _silu(x):
    return x * jax.nn.sigmoid(x)


def _mod_kernel(cond_ref, w_ref, b_ref, o_ref):
    o_ref[...] = _mm(_silu(cond_ref[...]), w_ref[...], True) + b_ref[...]


def _modulation(cond, mod_w, mod_b):
    depth, d, width = mod_w.shape
    tn = 1536
    return pl.pallas_call(
        _mod_kernel,
        grid=(depth, width // tn),
        in_specs=[
            pl.BlockSpec((COND_ROWS, d), lambda l, j: (0, 0)),
            pl.BlockSpec((None, d, tn), lambda l, j: (l, 0, j)),
            pl.BlockSpec((None, 1, tn), lambda l, j: (l, 0, j)),
        ],
        out_specs=pl.BlockSpec((None, COND_ROWS, tn), lambda l, j: (l, 0, j)),
        out_shape=jax.ShapeDtypeStruct((depth, COND_ROWS, width), F32),
        compiler_params=_cparams("parallel", "parallel"),
        name="modulation",
    )(cond, mod_w, mod_b.reshape(depth, 1, width))


def _head_rms(xb, g2, ones_blk):
    ss = _mm_sel(xb * xb, ones_blk, True)
    return xb * lax.rsqrt(ss * (1.0 / HEAD_DIM) + EPS) * g2


def _rope(yb, cos, sin_signed, first16):
    partner = jnp.where(first16, pltpu.roll(yb, LANES - 16, 1), pltpu.roll(yb, 16, 1))
    return yb * cos + partner * sin_signed


def _q_attention_layout(blocks, lane_half):
    out = []
    for j in range(2 * len(blocks)):
        src = blocks[j // 2]
        dst_half = j // GQA_GROUP
        if j % 2 != dst_half:
            src = pltpu.roll(src, HEAD_DIM, 1)
        out.append(jnp.where(lane_half == dst_half, src, 0.0))
    return jnp.concatenate(out, axis=1)


def _inproj_kernel(*refs, precise_ctx, lat_tiles):
    if precise_ctx:
        (xl_ref, xc_ref, mod_ref, g1_ref, w_ref, wlo_ref, cos_ref, sin_ref, qg_ref, kg_ref, ones_ref,
         za_ref, qga_ref, kvg_ref, qwa_ref, kvw_ref, kv32_ref, qg32_ref, qw32_ref) = refs
    else:
        (xl_ref, xc_ref, mod_ref, g1_ref, w_ref, cos_ref, sin_ref, qg_ref, kg_ref, ones_ref,
         za_ref, qga_ref, kvg_ref, qwa_ref, kvw_ref, kv32_ref, qg32_ref, qw32_ref) = refs
    is_lat = pl.program_id(0) < lat_tiles
    x = jnp.where(is_lat, xl_ref[...], xc_ref[...])
    mod = mod_ref[...]
    gain = g1_ref[...] * (1.0 + mod[:, D_MODEL:2 * D_MODEL])
    h = x * lax.rsqrt(jnp.mean(x * x, axis=-1, keepdims=True) + EPS) * gain + mod[:, 0:D_MODEL]

    mix_width = Q_WIDTH + 2 * KV_WIDTH
    col_g, col_w = 5 * A_WIDTH, 5 * A_WIDTH + mix_width

    def project(c0, c1, pieces):
        z = _dot(pieces[0], w_ref[:, c0:c1])
        if len(pieces) == 2:
            z = z + _dot(pieces[1], w_ref[:, c0:c1]) + _dot(pieces[0], wlo_ref[:, c0:c1])
        return z

    def tile(three_pass):
        pieces = _split2(h) if three_pass else (h.astype(BF16),)
        z_g = project(col_g, col_w, pieces)
        z_w = project(col_w, IN_WIDTH, pieces)
        za_ref[...] = project(0, col_g, pieces)

        cos = cos_ref[...]
        sin = sin_ref[...]
        ones_blk = ones_ref[...]
        lane = lax.broadcasted_iota(jnp.int32, (x.shape[0], LANES), 1)
        first16 = (lane % 32) < 16
        lane_half = lane // HEAD_DIM
        scale = LOG2E * HEAD_DIM ** -0.5

        qg = [z_g[:, LANES * b:LANES * (b + 1)] for b in range(3)]
        kg = z_g[:, Q_WIDTH:Q_WIDTH + KV_WIDTH]
        vg = z_g[:, Q_WIDTH + KV_WIDTH:]
        qw = [z_w[:, LANES * b:LANES * (b + 1)] for b in range(3)]
        kw = z_w[:, Q_WIDTH:Q_WIDTH + KV_WIDTH]
        vw = z_w[:, Q_WIDTH + KV_WIDTH:]

        qg = [_rope(_head_rms(b, qg_ref[...], ones_blk), cos, sin, first16) * scale for b in qg]
        kg = _rope(_head_rms(kg, kg_ref[...], ones_blk), cos, sin, first16)
        qw = [_rope(b, cos, sin, first16) * scale for b in qw]
        kw = _rope(kw, cos, sin, first16)

        qg_att = _q_attention_layout(qg, lane_half)
        qw_att = _q_attention_layout(qw, lane_half)
        qga_ref[...] = qg_att.astype(BF16)
        qwa_ref[...] = qw_att.astype(BF16)
        kvg_ref[...] = jnp.concatenate([kg, vg], axis=1).astype(BF16)
        kvw_ref[...] = jnp.concatenate([kw, vw], axis=1).astype(BF16)
        kv32_ref[...] = jnp.concatenate([kg, vg, kw, vw], axis=1)
        qg32_ref[...] = qg_att
        qw32_ref[...] = qw_att

    if precise_ctx:
        pl.when(is_lat)(functools.partial(tile, False))
        pl.when(jnp.logical_not(is_lat))(functools.partial(tile, True))
    else:
        tile(False)


def _token_specs(x_pair, tm, lat_tiles):
    off = lat_tiles if x_pair[1] is x_pair[0] else 0
    lat = pl.BlockSpec((tm, D_MODEL), lambda i, *_: (jnp.minimum(i, lat_tiles - 1), 0))
    ctx = pl.BlockSpec((tm, D_MODEL), lambda i, *_: (jnp.maximum(i, lat_tiles) - lat_tiles + off, 0))
    return [lat, ctx]


def _in_projection(x_pair, mod, g1, w_hi, w_lo, cos_t, sin_t, qn_g, kn_g, ones_blk, dims, layer=0):
    n = dims["n_lat"] + dims["n_ctx"]
    tm = TOKEN_TILE
    n_tiles = n // tm
    lat_tiles = dims["n_lat"] // tm
    ctx_tiles = n_tiles - lat_tiles
    tiles_per_seq = dims["t_lat"] // tm
    ctx_cond = dims["ctx_cond"]
    precise_ctx = w_lo is not None

    def cond_idx(i):
        return jnp.where(i < lat_tiles, i // tiles_per_seq, ctx_cond)

    def rope_idx(i):
        return jnp.where(i < lat_tiles, 1 + i % tiles_per_seq, 0)

    def ctx_idx(i):
        return jnp.where(i < lat_tiles, ctx_tiles, i - lat_tiles)

    row = lambda w: pl.BlockSpec((tm, w), lambda i: (i, 0))
    ctx_row = lambda w: pl.BlockSpec((tm, w), lambda i: (ctx_idx(i), 0))
    const = lambda a: pl.BlockSpec(a.shape, lambda i: (0,) * a.ndim, pipeline_mode=pl.Buffered(1))
    weight = lambda w: const(w) if w.ndim == 2 else pl.BlockSpec(
        (None,) + w.shape[1:], lambda i: (layer, 0, 0), pipeline_mode=pl.Buffered(1))
    weights = [w_hi, w_lo] if precise_ctx else [w_hi]
    ctx_rows = (ctx_tiles + 1) * tm
    return pl.pallas_call(
        functools.partial(_inproj_kernel, precise_ctx=precise_ctx, lat_tiles=lat_tiles),
        grid=(n_tiles,),
        in_specs=_token_specs(x_pair, tm, lat_tiles) + [
            pl.BlockSpec((None, 1, 6 * D_MODEL), lambda i: (cond_idx(i), 0, 0)),
            const(g1)] + [weight(w) for w in weights] + [
            pl.BlockSpec((tm, LANES), lambda i: (rope_idx(i), 0)),
            pl.BlockSpec((tm, LANES), lambda i: (rope_idx(i), 0)),
            const(qn_g), const(kn_g), const(ones_blk),
        ],
        out_specs=[
            row(5 * A_WIDTH), row(2 * Q_WIDTH), row(2 * KV_WIDTH), row(2 * Q_WIDTH), row(2 * KV_WIDTH),
            ctx_row(4 * KV_WIDTH), ctx_row(2 * Q_WIDTH), ctx_row(2 * Q_WIDTH),
        ],
        out_shape=[
            jax.ShapeDtypeStruct((n, 5 * A_WIDTH), F32),
            jax.ShapeDtypeStruct((n, 2 * Q_WIDTH), BF16),
            jax.ShapeDtypeStruct((n, 2 * KV_WIDTH), BF16),
            jax.ShapeDtypeStruct((n, 2 * Q_WIDTH), BF16),
            jax.ShapeDtypeStruct((n, 2 * KV_WIDTH), BF16),
            jax.ShapeDtypeStruct((ctx_rows, 4 * KV_WIDTH), F32),
            jax.ShapeDtypeStruct((ctx_rows, 2 * Q_WIDTH), F32),
            jax.ShapeDtypeStruct((ctx_rows, 2 * Q_WIDTH), F32),
        ],
        compiler_params=_cparams("arbitrary"),
        name="in_projection_precise_ctx" if precise_ctx else "in_projection",
    )(*x_pair, mod, g1, *weights, cos_t, sin_t, qn_g, kn_g, ones_blk)


def _hgrn_chunk(qa, v, fz, lb, st, consts, reverse, precise):
    cum, e_sel, x_sel, sub_mask, p_mask, v_mask, head_mask = consts
    log2e = LOG2E
    n_sub, half = CHUNK // SUB, SUB // 2
    q = _silu(qa)
    f = lb + (1.0 - lb) * _sigmoid(fz)
    logf2 = jnp.log(jnp.maximum(f, TINY)) * log2e
    kk = (1.0 - lb) * _sigmoid(-fz)

    b = _dot_sel(cum, logf2)
    b3 = b.reshape(n_sub, SUB, A_WIDTH)
    end = 0 if reverse else SUB - 1
    tot = jnp.broadcast_to(b3[:, end:end + 1, :], (n_sub, SUB, A_WIDTH)).reshape(CHUNK, A_WIDTH)
    q_in = q * jnp.exp2(b)
    k_out = kk * jnp.exp2(tot - b)
    d_sub = jnp.exp2(tot)

    c3 = (b - jnp.log(kk) * log2e).reshape(n_sub, SUB, A_WIDTH)
    b4 = b.reshape(n_sub, 2, half, A_WIDTH)
    q4 = q.reshape(n_sub, 2, half, A_WIDTH)
    r_local = lax.broadcasted_iota(jnp.int32, (n_sub, half, A_WIDTH), 1)
    zero_half = jnp.zeros((n_sub, half, A_WIDTH), F32)
    pieces = []
    for s in range(SUB):
        cs = c3[:, s:s + 1, :]
        parts = []
        for hsel in range(2):
            lo, hi = hsel * half, hsel * half + half - 1
            if (lo > s) if reverse else (hi < s):
                parts.append(zero_half)
                continue
            arg = b4[:, hsel] - cs
            if not ((hi <= s) if reverse else (lo >= s)):
                t_local = r_local + lo
                arg = jnp.where((t_local <= s) if reverse else (t_local >= s), arg, MASK_VALUE)
            parts.append(q4[:, hsel] * jnp.exp2(arg))
        piece = jnp.stack(parts, axis=1).reshape(CHUNK, A_WIDTH)
        pieces.append(piece if precise else piece.astype(BF16))
    w_cat = jnp.concatenate(pieces, axis=1)
    s_local = _mm_sel(w_cat, e_sel, precise)
    p = _mm_sel(s_local, x_sel, precise) * p_mask
    v_bd = jnp.concatenate([v] * A_HEADS, axis=0) * v_mask
    o_intra = _mm(p, v_bd, precise)

    v_op = v if precise else v.astype(BF16)
    k_op = k_out if precise else k_out.astype(BF16)
    q_op = q_in if precise else q_in.astype(BF16)
    sub_m = sub_mask
    order = range(n_sub - 1, -1, -1) if reverse else range(n_sub)
    new_st, o_inter = [], []
    for p in range(A_WIDTH // LANES):
        lanes = slice(LANES * p, LANES * (p + 1))
        k_exp = jnp.concatenate([k_op[:, lanes]] * n_sub, axis=1) * sub_m
        u_all = _mm(v_op[:, lanes], k_exp, precise, _TN)
        s = st[p]
        s_before = [None] * n_sub
        for j in order:
            s_before[j] = s
            s = d_sub[SUB * j:SUB * j + 1, lanes] * s + head_mask * u_all[:, LANES * j:LANES * (j + 1)]
        new_st.append(s)
        q_exp = jnp.concatenate([q_op[:, lanes]] * n_sub, axis=1) * sub_m
        o_inter.append(_mm(q_exp, jnp.concatenate(s_before, axis=1), precise, _NT))
    return o_intra + jnp.concatenate(o_inter, axis=1), jnp.stack(new_st, axis=0)


def _hgrn_kernel(*refs, has_s0, want_state, precise):
    refs = list(refs)
    seq_refs = [refs[6 * sq:6 * sq + 6] for sq in range(HGRN_SEQS)]
    del refs[:6 * HGRN_SEQS]
    lb_ref = refs.pop(0)
    s0_ref = refs.pop(0) if has_s0 else None
    cumf_ref, cumb_ref, e_ref, x_ref, sm_ref, pm_ref, vm_ref, hm_ref, of_ref, ob_ref = refs[:10]
    del refs[:10]
    sout_ref = refs.pop(0) if want_state else None
    st_ref, = refs
    c = pl.program_id(1)

    @pl.when(c == 0)
    def _():
        st_ref[...] = s0_ref[...] if has_s0 else jnp.zeros_like(st_ref)

    shared = (e_ref[...], x_ref[...], sm_ref[...], pm_ref[...], vm_ref[...], hm_ref[...])
    lb = lb_ref[...]
    for sq in range(HGRN_SEQS):
        qaf_ref, iaf_ref, fzf_ref, qab_ref, iab_ref, fzb_ref = seq_refs[sq]
        o_f, st_f = _hgrn_chunk(qaf_ref[...], iaf_ref[...], fzf_ref[...], lb[0:1], st_ref[sq, 0],
                                (cumf_ref[...],) + shared, False, precise)
        o_b, st_b = _hgrn_chunk(qab_ref[...], iab_ref[...], fzb_ref[...], lb[1:2], st_ref[sq, 1],
                                (cumb_ref[...],) + shared, True, precise)
        of_ref[sq] = o_f
        ob_ref[sq] = o_b
        st_ref[sq, 0] = st_f
        st_ref[sq, 1] = st_b

    if want_state:
        @pl.when(c == pl.num_programs(1) - 1)
        def _():
            sout_ref[...] = st_ref[...]


def _hgrn_consts():
    r = np.arange(CHUNK)
    same = (r[:, None] // SUB) == (r[None, :] // SUB)
    cum_f = (same & (r[None, :] <= r[:, None])).astype(np.float32)
    cum_b = (same & (r[None, :] >= r[:, None])).astype(np.float32)
    tot = same.astype(np.float32)
    e = np.zeros((SUB, A_HEADS, HEAD_DIM, A_HEADS, SUB), np.float32)
    for s in range(SUB):
        for h in range(A_HEADS):
            e[s, h, :, h, s] = 1.0
    e = e.reshape(SUB * A_WIDTH, A_HEADS * SUB)
    x = np.zeros((A_HEADS, SUB, A_HEADS, CHUNK), np.float32)
    for h in range(A_HEADS):
        for s in range(CHUNK):
            x[h, s % SUB, h, s] = 1.0
    x = x.reshape(A_HEADS * SUB, A_HEADS * CHUNK)
    p_mask = np.tile(tot, (1, A_HEADS))
    hv = np.arange(A_HEADS * CHUNK) // CHUNK
    hk = np.arange(A_WIDTH) // HEAD_DIM
    v_mask = (hv[:, None] == hk[None, :]).astype(np.float32)
    head_mask = (hk[:LANES, None] == hk[None, :LANES]).astype(np.float32)
    sub_mask = np.repeat(r[:, None] // SUB == np.arange(CHUNK // SUB)[None, :], LANES, axis=1).astype(np.float32)
    sel = [jnp.asarray(a, BF16) for a in (cum_f, cum_b, e, x, sub_mask)]
    return sel + [jnp.asarray(a, F32) for a in (p_mask, v_mask, head_mask)]


def _hgrn_scan(za, first_row, lb2, s0_t, n_seq, t, precise, want_state, name):
    nc = t // CHUNK
    groups = n_seq // HGRN_SEQS
    base = first_row // CHUNK
    za3 = za.reshape(za.shape[0] // CHUNK, CHUNK, za.shape[1])
    consts = _hgrn_consts()
    state_shape = (HGRN_SEQS, 2, A_WIDTH // LANES, LANES, LANES)
    const = lambda a: pl.BlockSpec(a.shape, lambda p, c: (0,) * a.ndim)

    def chunk_spec(sq, col, backward):
        def index(p, c):
            return (base + (p * HGRN_SEQS + sq) * nc + (nc - 1 - c if backward else c), 0, col)
        return pl.BlockSpec((None, CHUNK, A_WIDTH), index)

    fwd = lambda col: pl.BlockSpec((None, HGRN_SEQS, CHUNK, A_WIDTH), lambda p, c: (p, 0, c, col))
    bwd = lambda col: pl.BlockSpec((None, HGRN_SEQS, CHUNK, A_WIDTH), lambda p, c: (p, 0, nc - 1 - c, col))
    state_spec = pl.BlockSpec((None,) + state_shape, lambda p, c: (p, 0, 0, 0, 0, 0))
    operands = [za3] * (6 * HGRN_SEQS) + [lb2]
    in_specs = [chunk_spec(sq, col, backward) for sq in range(HGRN_SEQS)
                for col, backward in ((0, False), (1, False), (2, False), (0, True), (1, True), (3, True))]
    in_specs.append(const(lb2))
    if s0_t is not None:
        operands.append(s0_t.reshape((groups,) + state_shape))
        in_specs.append(state_spec)
    o_shape = jax.ShapeDtypeStruct((groups, HGRN_SEQS, t, A_WIDTH), F32)
    out_specs, out_shape = [fwd(0), bwd(0)], [o_shape, o_shape]
    if want_state:
        out_specs.append(state_spec)
        out_shape.append(jax.ShapeDtypeStruct((groups,) + state_shape, F32))
    outs = pl.pallas_call(
        functools.partial(_hgrn_kernel, has_s0=s0_t is not None, want_state=want_state, precise=precise),
        grid=(groups, nc),
        in_specs=in_specs + [const(c) for c in consts],
        out_specs=out_specs,
        out_shape=out_shape,
        scratch_shapes=[pltpu.VMEM(state_shape, F32)],
        compiler_params=_cparams("parallel", "arbitrary"),
        name=name,
    )(*operands, *consts)
    o_f, o_b = (o.reshape(n_seq * t, A_WIDTH) for o in outs[:2])
    s_fin = outs[2].reshape((n_seq,) + state_shape[1:]) if want_state else None
    return o_f, o_b, s_fin


def _attend(q3, segments, sink_row, precise=False):
    scores = []
    for k, _, valid in segments:
        s = _mm(k, q3, precise, _NT)
        if valid is not None:
            s = jnp.where(valid, s, MASK_VALUE)
        scores.append(s)
    m = scores[0].max(axis=0, keepdims=True)
    for s in scores[1:]:
        m = jnp.maximum(m, s.max(axis=0, keepdims=True))
    if sink_row is not None:
        m = jnp.maximum(m, sink_row)
    denom = jnp.exp2(sink_row - m) if sink_row is not None else 0.0
    acc = 0.0
    for s, (_, v, _) in zip(scores, segments):
        p = jnp.exp2(s - m)
        denom = denom + p.sum(axis=0, keepdims=True)
        acc = acc + _mm(v, p, precise, _TN)
    return (acc / denom).T


def _stack_heads(q, kvh):
    base = kvh * GQA_GROUP
    return jnp.concatenate([q[:, LANES * (base + g):LANES * (base + g + 1)] for g in range(GQA_GROUP)], axis=0)


def _merge_heads(res, tq):
    lane = lax.broadcasted_iota(jnp.int32, (tq, LANES), 1)
    heads = []
    for j in range(B_Q_HEADS):
        kvh, g = divmod(j, GQA_GROUP)
        o = res[kvh][g * tq:(g + 1) * tq]
        if kvh != j % 2:
            o = pltpu.roll(o, HEAD_DIM, 1)
        heads.append(o)
    blocks = [jnp.where(lane < HEAD_DIM, heads[2 * b], heads[2 * b + 1]) for b in range(B_Q_HEADS // 2)]
    return jnp.concatenate(blocks, axis=1)


def _sink_row(sink_ref, kvh, tq):
    return jnp.concatenate(
        [jnp.full((1, tq), sink_ref[kvh * GQA_GROUP + g] * LOG2E, F32) for g in range(GQA_GROUP)], axis=1)


def _global_attn_kernel(q_ref, kv_ref, ck_ref, cv_ref, o_ref, *, tq):
    q = q_ref[...]
    kv = kv_ref[...]
    k, v = kv[:, :KV_WIDTH], kv[:, KV_WIDTH:]
    ck, cv = ck_ref[...], cv_ref[...]
    segments = [(k[c:c + KEY_CHUNK], v[c:c + KEY_CHUNK], None) for c in range(0, k.shape[0], KEY_CHUNK)]
    segments.append((ck, cv, None))
    res = [_attend(_stack_heads(q, kvh), segments, None) for kvh in range(B_KV_HEADS)]
    o_ref[...] = _merge_heads(res, tq).astype(BF16)


def _global_attention(q_att, kv, ck, cv, dims):
    t, n_seq = dims["t_lat"], dims["n_lat_seq"]
    tq = 256
    nq = t // tq
    return pl.pallas_call(
        functools.partial(_global_attn_kernel, tq=tq),
        grid=(n_seq, nq),
        in_specs=[
            pl.BlockSpec((tq, 2 * Q_WIDTH), lambda b, i: (b * nq + i, 0)),
            pl.BlockSpec((t, 2 * KV_WIDTH), lambda b, i: (b, 0)),
            pl.BlockSpec((None,) + ck.shape[1:], lambda b, i: (b, 0, 0)),
            pl.BlockSpec((None,) + cv.shape[1:], lambda b, i: (b, 0, 0)),
        ],
        out_specs=pl.BlockSpec((tq, Q_WIDTH), lambda b, i: (b * nq + i, 0)),
        out_shape=jax.ShapeDtypeStruct((n_seq * t, Q_WIDTH), BF16),
        compiler_params=_cparams("parallel", "arbitrary"),
        name="global_attention",
    )(q_att, kv, ck, cv)


def _window_attn_kernel(sink_ref, q_ref, kv_ref, ck_ref, cv_ref, o_ref, *, tq, t):
    j = pl.program_id(1)
    span = tq + 2 * WINDOW
    start = pl.multiple_of(jnp.clip(j * tq - WINDOW, 0, t - span), WINDOW)
    q = q_ref[...]
    kv = kv_ref[pl.ds(start, span), :]
    k, v = kv[:, :KV_WIDTH], kv[:, KV_WIDTH:]
    key_pos = start + lax.broadcasted_iota(jnp.int32, (span, GQA_GROUP * tq), 0)
    query_pos = j * tq + lax.broadcasted_iota(jnp.int32, (span, GQA_GROUP * tq), 1) % tq
    valid = jnp.abs(query_pos - key_pos) <= WINDOW
    segments = [(k, v, valid), (ck_ref[...], cv_ref[...], None)]
    res = [_attend(_stack_heads(q, kvh), segments, _sink_row(sink_ref, kvh, tq)) for kvh in range(C_KV_HEADS)]
    o_ref[...] = _merge_heads(res, tq).astype(BF16)


def _window_attention(sink, q_att, kv, ck, cv, dims):
    t, n_seq = dims["t_lat"], dims["n_lat_seq"]
    tq = 2 * WINDOW
    nq = t // tq
    return pl.pallas_call(
        functools.partial(_window_attn_kernel, tq=tq, t=t),
        grid=(n_seq, nq),
        in_specs=[
            pl.BlockSpec(memory_space=pltpu.SMEM),
            pl.BlockSpec((tq, 2 * Q_WIDTH), lambda b, i: (b * nq + i, 0)),
            pl.BlockSpec((t, 2 * KV_WIDTH), lambda b, i: (b, 0)),
            pl.BlockSpec((None,) + ck.shape[1:], lambda b, i: (b, 0, 0)),
            pl.BlockSpec((None,) + cv.shape[1:], lambda b, i: (b, 0, 0)),
        ],
        out_specs=pl.BlockSpec((tq, Q_WIDTH), lambda b, i: (b * nq + i, 0)),
        out_shape=jax.ShapeDtypeStruct((n_seq * t, Q_WIDTH), BF16),
        compiler_params=_cparams("parallel", "arbitrary"),
        name="window_attention",
    )(sink, q_att, kv, ck, cv)


def _context_attn_kernel(sink_ref, qg_ref, qw_ref, kv_ref, og_ref, ow_ref, *, t, precise):
    kv = kv_ref[...]
    for q_ref, o_ref, off, use_sink in ((qg_ref, og_ref, 0, False), (qw_ref, ow_ref, 2 * KV_WIDTH, True)):
        q = q_ref[...]
        k, v = kv[:, off:off + KV_WIDTH], kv[:, off + KV_WIDTH:off + 2 * KV_WIDTH]
        res = [_attend(_stack_heads(q, kvh), [(k, v, None)],
                       _sink_row(sink_ref, kvh, t) if use_sink else None, precise)
               for kvh in range(B_KV_HEADS)]
        o_ref[...] = _merge_heads(res, t)


def _context_attention(sink, qg32, qw32, kv32, dims, precise):
    t, n_seq = dims["t_ctx"], dims["n_ctx_seq"]
    seq = lambda w: pl.BlockSpec((t, w), lambda b: (b, 0))
    return pl.pallas_call(
        functools.partial(_context_attn_kernel, t=t, precise=precise),
        grid=(n_seq,),
        in_specs=[pl.BlockSpec(memory_space=pltpu.SMEM), seq(2 * Q_WIDTH), seq(2 * Q_WIDTH), seq(4 * KV_WIDTH)],
        out_specs=[seq(Q_WIDTH), seq(Q_WIDTH)],
        out_shape=[jax.ShapeDtypeStruct((n_seq * t, Q_WIDTH), F32)] * 2,
        compiler_params=_cparams("parallel"),
        name="context_attention_precise" if precise else "context_attention",
    )(sink, qg32, qw32, kv32)


def _route(scores_t, sel_t):
    s = [scores_t[e:e + 1, :] for e in range(N_EXPERTS)]
    z = [sel_t[e:e + 1, :] for e in range(N_EXPERTS)]
    gs = []
    for g in range(N_GROUPS):
        m = z[g * EXPERTS_PER_GROUP:(g + 1) * EXPERTS_PER_GROUP]
        best = None
        for a in range(EXPERTS_PER_GROUP):
            for b in range(a + 1, EXPERTS_PER_GROUP):
                pair = m[a] + m[b]
                best = pair if best is None else jnp.maximum(best, pair)
        gs.append(best)
    combine, groups = [], []
    for g in range(N_GROUPS):
        chosen_g = None
        for g2 in range(N_GROUPS):
            if g2 == g:
                continue
            c = (gs[g] > gs[g2]) if g2 < g else (gs[g] >= gs[g2])
            chosen_g = c if chosen_g is None else jnp.logical_and(chosen_g, c)
        base = g * EXPERTS_PER_GROUP
        picked = []
        for a in range(EXPERTS_PER_GROUP):
            rank = 0.0
            for b in range(EXPERTS_PER_GROUP):
                if b == a:
                    continue
                ahead = (z[base + b] >= z[base + a]) if b < a else (z[base + b] > z[base + a])
                rank = rank + ahead.astype(F32)
            picked.append(jnp.where(jnp.logical_and(chosen_g, rank < 2.0), s[base + a], 0.0))
        denom = picked[0] + picked[1] + picked[2] + picked[3]
        denom = jnp.where(chosen_g, denom, 1.0)
        combine.extend(pk / denom for pk in picked)
        groups.append(jnp.where(chosen_g, 1.0, 0.0))
    return jnp.concatenate(combine, axis=0), jnp.concatenate(groups, axis=0)


def _outproj_kernel(*refs, lat_tiles, precise_ctx):
    if precise_ctx:
        (ofl_ref, ofc_ref, obl_ref, obc_ref, ga_ref,
         ogl_ref, ogc_ref, owl_ref, owc_ref, xl_ref, xc_ref, mod_ref, ag_ref,
         ones_ref, w_ref, wlo_ref, g2_ref, rw_ref, rb_ref, before_ref,
         x1_ref, h2_ref, comb_ref, pos_ref, cnt_ref, mix_ref) = refs
    else:
        (ofl_ref, ofc_ref, obl_ref, obc_ref, ga_ref,
         ogl_ref, ogc_ref, owl_ref, owc_ref, xl_ref, xc_ref, mod_ref, ag_ref,
         ones_ref, w_ref, g2_ref, rw_ref, rb_ref, before_ref,
         x1_ref, h2_ref, comb_ref, pos_ref, cnt_ref, mix_ref) = refs
        wlo_ref = None
    is_lat = pl.program_id(0) < lat_tiles
    o = jnp.where(is_lat, ofl_ref[...] + obl_ref[...], ofc_ref[...] + obc_ref[...])
    oa = _head_rms(o, ag_ref[...], ones_ref[...]) * _sigmoid(ga_ref[...])
    def mix(operands, precise):
        if precise:
            ah, al = _split2(jnp.concatenate(operands, axis=1))
            mix_ref[...] = _dot(ah, w_ref[...]) + _dot(al, w_ref[...]) + _dot(ah, wlo_ref[...])
        else:
            mix_ref[...] = _dot(jnp.concatenate([a.astype(BF16) for a in operands], axis=1), w_ref[...])

    pl.when(is_lat)(lambda: mix((oa, ogl_ref[...], owl_ref[...]), False))
    pl.when(jnp.logical_not(is_lat))(lambda: mix((oa, ogc_ref[...], owc_ref[...]), precise_ctx))

    mod = mod_ref[...]
    gate1 = mod[:, 2 * D_MODEL:3 * D_MODEL]
    sh2 = mod[:, 3 * D_MODEL:4 * D_MODEL]
    sc2 = mod[:, 4 * D_MODEL:5 * D_MODEL]
    x1 = jnp.where(is_lat, xl_ref[...], xc_ref[...]) + gate1 * mix_ref[...]
    x1_ref[...] = x1
    gain = g2_ref[...] * (1.0 + sc2)
    h2 = x1 * lax.rsqrt(jnp.mean(x1 * x1, axis=-1, keepdims=True) + EPS) * gain + sh2
    h2_ref[...] = h2.astype(BF16)
    scores_t = _sigmoid(_mm(rw_ref[...], h2, True, _NT))
    comb_t, group_t = _route(scores_t, scores_t + rb_ref[...])
    comb_ref[...] = comb_t
    rank = _dot(group_t.astype(BF16), before_ref[...])
    counts = group_t.sum(axis=1, keepdims=True)
    pos, offset = 0.0, 0.0
    for g in range(N_GROUPS):
        pos = pos + group_t[g:g + 1] * (rank[g:g + 1] + offset)
        offset = offset + counts[g:g + 1]
    pos_ref[...] = pos
    pad = jnp.zeros((8 - N_GROUPS, 1), F32)
    cnt_ref[...] = jnp.broadcast_to(jnp.concatenate([counts, pad], axis=0), (8, LANES)).astype(jnp.int32)


def _out_projection(o_f, o_b, za, og_lat, og_ctx, ow_lat, ow_ctx, x_pair, mod, an_g, ones_blk, w_hi, w_lo, g2,
                    rw_t, rb, dims, layer=0):
    n = dims["n_lat"] + dims["n_ctx"]
    tm = TOKEN_TILE
    n_tiles = n // tm
    lat_tiles = dims["n_lat"] // tm
    tiles_per_seq = dims["t_lat"] // tm
    ctx_cond = dims["ctx_cond"]
    precise_ctx = w_lo is not None
    cond_idx = lambda i: jnp.where(i < lat_tiles, i // tiles_per_seq, ctx_cond)
    row = lambda w: pl.BlockSpec((tm, w), lambda i: (i, 0))
    lat = lambda w: pl.BlockSpec((tm, w), lambda i: (jnp.minimum(i, lat_tiles - 1), 0))
    ctx = lambda w: pl.BlockSpec((tm, w), lambda i: (jnp.maximum(i - lat_tiles, 0), 0))
    const = lambda a: pl.BlockSpec(a.shape, lambda i: (0,) * a.ndim, pipeline_mode=pl.Buffered(1))
    weight = lambda w: const(w) if w.ndim == 2 else pl.BlockSpec(
        (None,) + w.shape[1:], lambda i: (layer, 0, 0), pipeline_mode=pl.Buffered(1))
    weights = [w_hi, w_lo] if precise_ctx else [w_hi]
    t_idx = np.arange(tm)
    before = jnp.asarray((t_idx[:, None] < t_idx[None, :]).astype(np.float32), BF16)
    return pl.pallas_call(
        functools.partial(_outproj_kernel, lat_tiles=lat_tiles, precise_ctx=precise_ctx),
        grid=(n_tiles,),
        in_specs=[
            lat(A_WIDTH), ctx(A_WIDTH), lat(A_WIDTH), ctx(A_WIDTH),
            pl.BlockSpec((tm, A_WIDTH), lambda i: (i, 4)),
            lat(Q_WIDTH), ctx(Q_WIDTH), lat(Q_WIDTH), ctx(Q_WIDTH)] + _token_specs(x_pair, tm, lat_tiles) + [
            pl.BlockSpec((None, 1, 6 * D_MODEL), lambda i: (cond_idx(i), 0, 0)),
            const(an_g), const(ones_blk)] + [weight(w) for w in weights] + [
            const(g2), const(rw_t), const(rb), const(before),
        ],
        out_specs=[row(D_MODEL), row(D_MODEL), pl.BlockSpec((N_EXPERTS, tm), lambda i: (0, i)),
                   pl.BlockSpec((1, tm), lambda i: (0, i)), pl.BlockSpec((8, LANES), lambda i: (i, 0))],
        out_shape=[
            jax.ShapeDtypeStruct((n, D_MODEL), F32),
            jax.ShapeDtypeStruct((n, D_MODEL), BF16),
            jax.ShapeDtypeStruct((N_EXPERTS, n), F32),
            jax.ShapeDtypeStruct((1, n), F32),
            jax.ShapeDtypeStruct((n_tiles * 8, LANES), jnp.int32),
        ],
        scratch_shapes=[pltpu.VMEM((tm, D_MODEL), F32)],
        compiler_params=_cparams("arbitrary"),
        name="out_projection_precise_ctx" if precise_ctx else "out_projection",
    )(*o_f, *o_b, za, og_lat, og_ctx, ow_lat, ow_ctx, *x_pair, mod, an_g, ones_blk, *weights, g2, rw_t, rb, before)


def _moe_kernel(off_ref, h_ref, comb_ref, pos_ref, x1_ref, mod_ref, wgu_ref, wd_ref, fg_ref, *rest,
                final, lat_tiles):
    *outs, hs_ref, cs_ref, acc_ref = rest
    i = pl.program_id(0)
    tm = h_ref.shape[0]
    row = lax.broadcasted_iota(jnp.int32, (tm, tm), 0).astype(F32)
    perm = jnp.where(row == pos_ref[...], 1.0, 0.0).astype(BF16)
    hs_ref[...] = _dot(perm, h_ref[...]).astype(BF16)
    cs_ref[...] = sum(lax.dot_general(perm, c, _NT, preferred_element_type=F32)
                      for c in _split3(comb_ref[...]))
    acc_ref[...] = jnp.zeros_like(acc_ref)
    lane = lax.broadcasted_iota(jnp.int32, (MOE_BLOCK, N_EXPERTS), 1)

    for r in range(tm // MOE_BLOCK):
        rows = slice(r * MOE_BLOCK, (r + 1) * MOE_BLOCK)

        def group_body(g, carry, rows=rows, r=r):
            has_tokens = jnp.logical_and(off_ref[i, g] < (r + 1) * MOE_BLOCK, off_ref[i, g + 1] > r * MOE_BLOCK)

            @pl.when(has_tokens)
            def _():
                h = hs_ref[rows, :]
                cs = cs_ref[rows, :]
                acts = []
                for a in range(EXPERTS_PER_GROUP):
                    e = g * EXPERTS_PER_GROUP + a
                    gu = _dot(h, wgu_ref[e])
                    gate, up = gu[:, :D_EXPERT], gu[:, D_EXPERT:]
                    ce = jnp.sum(jnp.where(lane == e, cs, 0.0), axis=1, keepdims=True)
                    acts.append((_silu(gate) * up * ce).astype(BF16))
                acc_ref[rows, :] += _dot(jnp.concatenate(acts, axis=1), wd_ref[g])

            return carry

        lax.fori_loop(0, N_GROUPS, group_body, 0)

    y_hi, y_lo = _split2(acc_ref[...])
    y = (lax.dot_general(perm, y_hi, _TN, preferred_element_type=F32)
         + lax.dot_general(perm, y_lo, _TN, preferred_element_type=F32))
    x2 = x1_ref[...] + mod_ref[...][:, 5 * D_MODEL:] * y
    if not final:
        outs[0][...] = x2
    else:
        yn = x2 * lax.rsqrt(jnp.mean(x2 * x2, axis=-1, keepdims=True) + EPS) * fg_ref[...]
        is_lat = i < lat_tiles

        @pl.when(is_lat)
        def _():
            outs[0][...] = yn

        @pl.when(jnp.logical_not(is_lat))
        def _():
            outs[1][...] = yn


def _moe(h2, comb_t, pos, counts, x1, mod, w_gu, w_d, layer, final_g, dims, final):
    n = x1.shape[0]
    tm = TOKEN_TILE
    lat_tiles = dims["n_lat"] // tm
    tiles_per_seq = dims["t_lat"] // tm
    ctx_cond = dims["ctx_cond"]
    cond_idx = lambda i: jnp.where(i < lat_tiles, i // tiles_per_seq, ctx_cond)
    offsets = jnp.concatenate([jnp.zeros((n // tm, 1), jnp.int32), jnp.cumsum(counts, axis=1)], axis=1)
    if final:
        out_specs = [pl.BlockSpec((tm, D_MODEL), lambda i, *_: (jnp.minimum(i, lat_tiles - 1), 0)),
                     pl.BlockSpec((tm, D_MODEL), lambda i, *_: (jnp.maximum(i - lat_tiles, 0), 0))]
        out_shape = [jax.ShapeDtypeStruct((dims["n_lat"], D_MODEL), F32),
                     jax.ShapeDtypeStruct((dims["n_ctx"], D_MODEL), F32)]
    else:
        out_specs = pl.BlockSpec((tm, D_MODEL), lambda i, *_: (i, 0))
        out_shape = jax.ShapeDtypeStruct((n, D_MODEL), F32)
    resident = lambda a: pl.BlockSpec((None,) + a.shape[1:], lambda i, *_: (layer,) + (0,) * (a.ndim - 1),
                                      pipeline_mode=pl.Buffered(1))
    grid_spec = pltpu.PrefetchScalarGridSpec(
        num_scalar_prefetch=1,
        grid=(n // tm,),
        in_specs=[
            pl.BlockSpec((tm, D_MODEL), lambda i, *_: (i, 0)),
            pl.BlockSpec((N_EXPERTS, tm), lambda i, *_: (0, i)),
            pl.BlockSpec((1, tm), lambda i, *_: (0, i)),
            pl.BlockSpec((tm, D_MODEL), lambda i, *_: (i, 0)),
            pl.BlockSpec((None, 1, 6 * D_MODEL), lambda i, *_: (cond_idx(i), 0, 0)),
            resident(w_gu), resident(w_d),
            pl.BlockSpec((1, D_MODEL), lambda i, *_: (0, 0)),
        ],
        out_specs=out_specs,
        scratch_shapes=[pltpu.VMEM((tm, D_MODEL), BF16), pltpu.VMEM((tm, N_EXPERTS), F32),
                        pltpu.VMEM((tm, D_MODEL), F32)],
    )
    return pl.pallas_call(
        functools.partial(_moe_kernel, final=final, lat_tiles=lat_tiles),
        grid_spec=grid_spec,
        out_shape=out_shape,
        compiler_params=_cparams("arbitrary"),
        name="moe_final" if final else "moe",
    )(offsets, h2, comb_t, pos, x1, mod, w_gu, w_d, final_g)


def _rope_tables(t, tile):
    n_freq = HEAD_DIM // 4
    pos = np.arange(t)
    freqs = ROPE_THETA ** (-np.arange(n_freq, dtype=np.float64) / n_freq)
    ang_row = (pos // GRID_W)[:, None] * freqs
    ang_col = (pos % GRID_W)[:, None] * freqs
    cos_h = np.concatenate([np.cos(ang_row)] * 2 + [np.cos(ang_col)] * 2, axis=1)
    sin_h = np.concatenate([-np.sin(ang_row), np.sin(ang_row), -np.sin(ang_col), np.sin(ang_col)], axis=1)
    cos_t = np.concatenate([np.ones((tile, LANES)), np.concatenate([cos_h, cos_h], axis=1)], axis=0)
    sin_t = np.concatenate([np.zeros((tile, LANES)), np.concatenate([sin_h, sin_h], axis=1)], axis=0)
    return jnp.asarray(cos_t, F32), jnp.asarray(sin_t, F32)


def _head_ones(width):
    r = np.arange(width) // HEAD_DIM
    return jnp.asarray((r[:, None] == r[None, :]).astype(np.float32), BF16)


def _split_kernel(w_ref, hi_ref, lo_ref):
    hi, lo = _split2(w_ref[...])
    hi_ref[...] = hi
    lo_ref[...] = lo


def _hi_lo(w_all, layer):
    _, rows, width = w_all.shape
    tr = 256
    spec = pl.BlockSpec((tr, width), lambda i: (i, 0))
    return pl.pallas_call(
        _split_kernel,
        grid=(rows // tr,),
        in_specs=[pl.BlockSpec((None, tr, width), lambda i: (layer, i, 0))],
        out_specs=[spec, spec],
        out_shape=[jax.ShapeDtypeStruct((rows, width), BF16)] * 2,
        compiler_params=_cparams("parallel"),
        name="split_weight",
    )(w_all)


def _state_to_block_diag_t(s):
    per_pair = LANES // HEAD_DIM
    st = jnp.swapaxes(s, -1, -2).reshape(s.shape[:2] + (A_HEADS // per_pair, per_pair, HEAD_DIM, HEAD_DIM))
    eye = jnp.eye(per_pair, dtype=s.dtype)
    out = jnp.einsum("bdphvk,hg->bdphvgk", st, eye)
    return out.reshape(s.shape[:2] + (A_HEADS // per_pair, LANES, LANES))


def _block_diag_t_to_state(st):
    per_pair = LANES // HEAD_DIM
    blocks = [st[..., HEAD_DIM * h:HEAD_DIM * (h + 1), HEAD_DIM * h:HEAD_DIM * (h + 1)] for h in range(per_pair)]
    diag = jnp.stack(blocks, axis=3)
    return jnp.swapaxes(diag, -1, -2).reshape(st.shape[:2] + (A_HEADS, HEAD_DIM, HEAD_DIM))


def kernel(x_prompt, x_sample, cache_glob_k, cache_glob_v, cache_win_k, cache_win_v, state_hgrn, c, c_ctx,
           mod_w, mod_b, norm1_g, norm2_g, w_in, w_out, hgrn_lb, hgrn_norm_g, q_norm_g, k_norm_g, win_sink,
           router_w, router_b, w_gate_up, w_down, final_g):
    n_ctx_seq, t_ctx, d = x_prompt.shape
    n_lat_seq, t_lat, _ = x_sample.shape
    depth = mod_w.shape[0]
    past = cache_glob_k.shape[2]
    n_lat, n_ctx = n_lat_seq * t_lat, n_ctx_seq * t_ctx
    dims = dict(n_ctx_seq=n_ctx_seq, t_ctx=t_ctx, n_lat_seq=n_lat_seq, t_lat=t_lat,
                n_lat=n_lat, n_ctx=n_ctx, ctx_cond=n_lat_seq)
    assert d == D_MODEL and n_lat_seq < COND_ROWS
    assert n_ctx % TOKEN_TILE == 0 and t_lat % TOKEN_TILE == 0
    assert t_ctx % CHUNK == 0 and t_lat >= 4 * WINDOW

    x_pair = (x_sample.reshape(n_lat, d), x_prompt.reshape(n_ctx, d))
    cond = jnp.concatenate([c, c_ctx[None, :], jnp.zeros((COND_ROWS - n_lat_seq - 1, d), F32)], axis=0)
    mod = _modulation(cond, mod_w, mod_b).reshape(depth, COND_ROWS, 1, 6 * d)

    p_lb = jax.nn.softmax(hgrn_lb.astype(F32), axis=1)
    lbs = jnp.cumsum(p_lb, axis=1) - p_lb[:, :1]

    cos_t, sin_t = _rope_tables(t_lat, TOKEN_TILE)
    ones128, ones256 = _head_ones(LANES), _head_ones(A_WIDTH)
    tile2 = lambda g, reps: jnp.tile(g, reps)[None, :]
    rw_t = router_w.T
    rb = router_b[:, None]
    fg = final_g[None, :]

    w_gu_all = w_gate_up.astype(BF16)
    w_d_all = w_down.astype(BF16).reshape(depth, N_GROUPS, EXPERTS_PER_GROUP * D_EXPERT, d)
    w_in_last, w_out_last = w_in[depth - 1:].astype(BF16), w_out[depth - 1:].astype(BF16)

    new_kv, new_state = [], []
    for l in range(depth):
        precise = l < depth - 1
        win_hi, win_lo = _hi_lo(w_in, l) if precise else (w_in_last, None)
        za, qg_att, kvg, qw_att, kvw, kv32, qg32, qw32 = _in_projection(
            x_pair, mod[l], norm1_g[l][None, :], win_hi, win_lo, cos_t, sin_t,
            tile2(q_norm_g[l], 2), tile2(k_norm_g[l], 2), ones128, dims)

        s0 = state_hgrn[:, l].astype(F32)
        of_lat, ob_lat, _ = _hgrn_scan(za, 0, lbs[:, l], _state_to_block_diag_t(s0), n_lat_seq, t_lat,
                                       False, False, "hgrn_latent")
        of_ctx, ob_ctx, s_fin = _hgrn_scan(za, n_lat, lbs[:, l], None, n_ctx_seq, t_ctx, precise, True,
                                           "hgrn_context_precise" if precise else "hgrn_context")

        cast_cache = lambda a: a[:, l].reshape(n_lat_seq, past, KV_WIDTH).astype(BF16)
        og_lat = _global_attention(qg_att, kvg, cast_cache(cache_glob_k), cast_cache(cache_glob_v), dims)
        ow_lat = _window_attention(win_sink[l], qw_att, kvw, cast_cache(cache_win_k),
                                   cast_cache(cache_win_v), dims)
        og_ctx, ow_ctx = _context_attention(win_sink[l], qg32, qw32, kv32, dims, precise)

        wout_hi, wout_lo = _hi_lo(w_out, l) if precise else (w_out_last, None)
        x1, h2, comb_t, pos, cnt = _out_projection(
            (of_lat, of_ctx), (ob_lat, ob_ctx), za, og_lat, og_ctx, ow_lat, ow_ctx, x_pair, mod[l],
            tile2(hgrn_norm_g[l], A_HEADS), ones256, wout_hi, wout_lo, norm2_g[l][None, :], rw_t, rb, dims)

        counts = cnt.reshape(-1, 8, LANES)[:, :N_GROUPS, 0]
        x = _moe(h2, comb_t, pos, counts, x1, mod[l], w_gu_all, w_d_all, l, fg, dims, final=(l == depth - 1))
        x_pair = (x, x)

        new_kv.append(kv32[:n_ctx].reshape(n_ctx_seq, t_ctx, 4, B_KV_HEADS, HEAD_DIM))
        new_state.append(_block_diag_t_to_state(s_fin))

    y_sample = x[0].reshape(n_lat_seq, t_lat, d)
    y_prompt = x[1].reshape(n_ctx_seq, t_ctx, d)
    kv = jnp.stack(new_kv, axis=1)
    return (y_prompt, y_sample, kv[:, :, :, 0], kv[:, :, :, 1], kv[:, :, :, 2], kv[:, :, :, 3],
            jnp.stack(new_state, axis=1).astype(x_prompt.dtype))
```

```python
import functools

import numpy as np
import jax
import jax.numpy as jnp
from jax import lax
from jax.experimental import pallas as pl
from jax.experimental.pallas import tpu as pltpu

F32 = jnp.float32
BF16 = jnp.bfloat16

D_MODEL = 1024
HEAD_DIM = 64
GRID_W = 64
A_HEADS = 4
A_WIDTH = A_HEADS * HEAD_DIM
B_Q_HEADS = 6
B_KV_HEADS = 2
C_Q_HEADS = 6
C_KV_HEADS = 2
Q_WIDTH = B_Q_HEADS * HEAD_DIM
KV_WIDTH = B_KV_HEADS * HEAD_DIM
GQA_GROUP = B_Q_HEADS // B_KV_HEADS
WINDOW = 128
ROPE_THETA = 10000.0
IN_WIDTH = 5 * A_WIDTH + 2 * (Q_WIDTH + 2 * KV_WIDTH)
N_EXPERTS = 16
N_GROUPS = 4
EXPERTS_PER_GROUP = N_EXPERTS // N_GROUPS
D_EXPERT = 256
EPS = 1e-6
MASK_VALUE = -1e30
TINY = 1e-30
LOG2E = 1.4426950408889634

LANES = 128
SUB = 16
CHUNK = 128
HGRN_SEQS = 2
TOKEN_TILE = 512
MOE_BLOCK = 128
KEY_CHUNK = 512
COND_ROWS = 16
VMEM_LIMIT = 58 * 1024 * 1024

_NT = (((1,), (1,)), ((), ()))
_TN = (((0,), (0,)), ((), ()))


def _cparams(*sem):
    return pltpu.CompilerParams(dimension_semantics=sem, vmem_limit_bytes=VMEM_LIMIT)


def _dot(a, b):
    return jnp.dot(a, b, preferred_element_type=F32)


def _split2(x):
    h = x.astype(BF16)
    return h, (x - h.astype(F32)).astype(BF16)


def _split3(x):
    h = x.astype(BF16)
    r = x - h.astype(F32)
    m = r.astype(BF16)
    l = (r - m.astype(F32)).astype(BF16)
    return h, m, l


def _dot_sel(c, x):
    h, m, l = _split3(x)
    return _dot(c, h) + _dot(c, m) + _dot(c, l)


def _mm(a, b, precise, dims=None):
    if dims is None:
        dims = (((a.ndim - 1,), (0,)), ((), ()))
    dg = lambda x, y: lax.dot_general(x, y, dims, preferred_element_type=F32)
    if not precise:
        return dg(a.astype(BF16), b.astype(BF16))
    ah, al = _split2(a)
    bh, bl = _split2(b)
    return dg(ah, bh) + dg(ah, bl) + dg(al, bh)


def _mm_sel(a, sel, precise):
    if not precise:
        return _dot(a.astype(BF16), sel)
    ah, al = _split2(a)
    return _dot(ah, sel) + _dot(al, sel)


def _sigmoid(x):
    return jax.nn.sigmoid(x)


def _silu(x):
    return x * jax.nn.sigmoid(x)


def _mod_kernel(cond_ref, w_ref, b_ref, o_ref):
    o_ref[...] = _mm(_silu(cond_ref[...]), w_ref[...], True) + b_ref[...]


def _modulation(cond, mod_w, mod_b):
    depth, d, width = mod_w.shape
    tn = 1536
    return pl.pallas_call(
        _mod_kernel,
        grid=(depth, width // tn),
        in_specs=[
            pl.BlockSpec((COND_ROWS, d), lambda l, j: (0, 0)),
            pl.BlockSpec((None, d, tn), lambda l, j: (l, 0, j)),
            pl.BlockSpec((None, 1, tn), lambda l, j: (l, 0, j)),
        ],
        out_specs=pl.BlockSpec((None, COND_ROWS, tn), lambda l, j: (l, 0, j)),
        out_shape=jax.ShapeDtypeStruct((depth, COND_ROWS, width), F32),
        compiler_params=_cparams("parallel", "parallel"),
        name="modulation",
    )(cond, mod_w, mod_b.reshape(depth, 1, width))


def _head_rms(xb, g2, ones_blk):
    ss = _mm_sel(xb * xb, ones_blk, True)
    return xb * lax.rsqrt(ss * (1.0 / HEAD_DIM) + EPS) * g2


def _rope(yb, cos, sin_signed, first16):
    partner = jnp.where(first16, pltpu.roll(yb, LANES - 16, 1), pltpu.roll(yb, 16, 1))
    return yb * cos + partner * sin_signed


def _q_attention_layout(blocks, lane_half):
    out = []
    for j in range(2 * len(blocks)):
        src = blocks[j // 2]
        dst_half = j // GQA_GROUP
        if j % 2 != dst_half:
            src = pltpu.roll(src, HEAD_DIM, 1)
        out.append(jnp.where(lane_half == dst_half, src, 0.0))
    return jnp.concatenate(out, axis=1)


def _inproj_kernel(*refs, precise_ctx, lat_tiles):
    if precise_ctx:
        (xl_ref, xc_ref, mod_ref, g1_ref, w_ref, wlo_ref, cos_ref, sin_ref, qg_ref, kg_ref, ones_ref,
         za_ref, qga_ref, kvg_ref, qwa_ref, kvw_ref, kv32_ref, qg32_ref, qw32_ref) = refs
    else:
        (xl_ref, xc_ref, mod_ref, g1_ref, w_ref, cos_ref, sin_ref, qg_ref, kg_ref, ones_ref,
         za_ref, qga_ref, kvg_ref, qwa_ref, kvw_ref, kv32_ref, qg32_ref, qw32_ref) = refs
    is_lat = pl.program_id(0) < lat_tiles
    x = jnp.where(is_lat, xl_ref[...], xc_ref[...])
    mod = mod_ref[...]
    gain = g1_ref[...] * (1.0 + mod[:, D_MODEL:2 * D_MODEL])
    h = x * lax.rsqrt(jnp.mean(x * x, axis=-1, keepdims=True) + EPS) * gain + mod[:, 0:D_MODEL]

    mix_width = Q_WIDTH + 2 * KV_WIDTH
    col_g, col_w = 5 * A_WIDTH, 5 * A_WIDTH + mix_width

    def project(c0, c1, pieces):
        z = _dot(pieces[0], w_ref[:, c0:c1])
        if len(pieces) == 2:
            z = z + _dot(pieces[1], w_ref[:, c0:c1]) + _dot(pieces[0], wlo_ref[:, c0:c1])
        return z

    def tile(three_pass):
        pieces = _split2(h) if three_pass else (h.astype(BF16),)
        z_g = project(col_g, col_w, pieces)
        z_w = project(col_w, IN_WIDTH, pieces)
        za_ref[...] = project(0, col_g, pieces)

        cos = cos_ref[...]
        sin = sin_ref[...]
        ones_blk = ones_ref[...]
        lane = lax.broadcasted_iota(jnp.int32, (x.shape[0], LANES), 1)
        first16 = (lane % 32) < 16
        lane_half = lane // HEAD_DIM
        scale = LOG2E * HEAD_DIM ** -0.5

        qg = [z_g[:, LANES * b:LANES * (b + 1)] for b in range(3)]
        kg = z_g[:, Q_WIDTH:Q_WIDTH + KV_WIDTH]
        vg = z_g[:, Q_WIDTH + KV_WIDTH:]
        qw = [z_w[:, LANES * b:LANES * (b + 1)] for b in range(3)]
        kw = z_w[:, Q_WIDTH:Q_WIDTH + KV_WIDTH]
        vw = z_w[:, Q_WIDTH + KV_WIDTH:]

        qg = [_rope(_head_rms(b, qg_ref[...], ones_blk), cos, sin, first16) * scale for b in qg]
        kg = _rope(_head_rms(kg, kg_ref[...], ones_blk), cos, sin, first16)
        qw = [_rope(b, cos, sin, first16) * scale for b in qw]
        kw = _rope(kw, cos, sin, first16)

        qg_att = _q_attention_layout(qg, lane_half)
        qw_att = _q_attention_layout(qw, lane_half)
        qga_ref[...] = qg_att.astype(BF16)
        qwa_ref[...] = qw_att.astype(BF16)
        kvg_ref[...] = jnp.concatenate([kg, vg], axis=1).astype(BF16)
        kvw_ref[...] = jnp.concatenate([kw, vw], axis=1).astype(BF16)
        kv32_ref[...] = jnp.concatenate([kg, vg, kw, vw], axis=1)
        qg32_ref[...] = qg_att
        qw32_ref[...] = qw_att

    if precise_ctx:
        pl.when(is_lat)(functools.partial(tile, False))
        pl.when(jnp.logical_not(is_lat))(functools.partial(tile, True))
    else:
        tile(False)


def _token_specs(x_pair, tm, lat_tiles):
    off = lat_tiles if x_pair[1] is x_pair[0] else 0
    lat = pl.BlockSpec((tm, D_MODEL), lambda i, *_: (jnp.minimum(i, lat_tiles - 1), 0))
    ctx = pl.BlockSpec((tm, D_MODEL), lambda i, *_: (jnp.maximum(i, lat_tiles) - lat_tiles + off, 0))
    return [lat, ctx]


def _in_projection(x_pair, mod, g1, w_hi, w_lo, cos_t, sin_t, qn_g, kn_g, ones_blk, dims, layer=0):
    n = dims["n_lat"] + dims["n_ctx"]
    tm = TOKEN_TILE
    n_tiles = n // tm
    lat_tiles = dims["n_lat"] // tm
    ctx_tiles = n_tiles - lat_tiles
    tiles_per_seq = dims["t_lat"] // tm
    ctx_cond = dims["ctx_cond"]
    precise_ctx = w_lo is not None

    def cond_idx(i):
        return jnp.where(i < lat_tiles, i // tiles_per_seq, ctx_cond)

    def rope_idx(i):
        return jnp.where(i < lat_tiles, 1 + i % tiles_per_seq, 0)

    def ctx_idx(i):
        return jnp.where(i < lat_tiles, ctx_tiles, i - lat_tiles)

    row = lambda w: pl.BlockSpec((tm, w), lambda i: (i, 0))
    ctx_row = lambda w: pl.BlockSpec((tm, w), lambda i: (ctx_idx(i), 0))
    const = lambda a: pl.BlockSpec(a.shape, lambda i: (0,) * a.ndim, pipeline_mode=pl.Buffered(1))
    weight = lambda w: const(w) if w.ndim == 2 else pl.BlockSpec(
        (None,) + w.shape[1:], lambda i: (layer, 0, 0), pipeline_mode=pl.Buffered(1))
    weights = [w_hi, w_lo] if precise_ctx else [w_hi]
    ctx_rows = (ctx_tiles + 1) * tm
    return pl.pallas_call(
        functools.partial(_inproj_kernel, precise_ctx=precise_ctx, lat_tiles=lat_tiles),
        grid=(n_tiles,),
        in_specs=_token_specs(x_pair, tm, lat_tiles) + [
            pl.BlockSpec((None, 1, 6 * D_MODEL), lambda i: (cond_idx(i), 0, 0)),
            const(g1)] + [weight(w) for w in weights] + [
            pl.BlockSpec((tm, LANES), lambda i: (rope_idx(i), 0)),
            pl.BlockSpec((tm, LANES), lambda i: (rope_idx(i), 0)),
            const(qn_g), const(kn_g), const(ones_blk),
        ],
        out_specs=[
            row(5 * A_WIDTH), row(2 * Q_WIDTH), row(2 * KV_WIDTH), row(2 * Q_WIDTH), row(2 * KV_WIDTH),
            ctx_row(4 * KV_WIDTH), ctx_row(2 * Q_WIDTH), ctx_row(2 * Q_WIDTH),
        ],
        out_shape=[
            jax.ShapeDtypeStruct((n, 5 * A_WIDTH), F32),
            jax.ShapeDtypeStruct((n, 2 * Q_WIDTH), BF16),
            jax.ShapeDtypeStruct((n, 2 * KV_WIDTH), BF16),
            jax.ShapeDtypeStruct((n, 2 * Q_WIDTH), BF16),
            jax.ShapeDtypeStruct((n, 2 * KV_WIDTH), BF16),
            jax.ShapeDtypeStruct((ctx_rows, 4 * KV_WIDTH), F32),
            jax.ShapeDtypeStruct((ctx_rows, 2 * Q_WIDTH), F32),
            jax.ShapeDtypeStruct((ctx_rows, 2 * Q_WIDTH), F32),
        ],
        compiler_params=_cparams("arbitrary"),
        name="in_projection_precise_ctx" if precise_ctx else "in_projection",
    )(*x_pair, mod, g1, *weights, cos_t, sin_t, qn_g, kn_g, ones_blk)


def _hgrn_chunk(qa, v, fz, lb, st, consts, reverse, precise):
    cum, e_sel, x_sel, sub_mask, p_mask, v_mask, head_mask = consts
    log2e = LOG2E
    n_sub, half = CHUNK // SUB, SUB // 2
    q = _silu(qa)
    f = lb + (1.0 - lb) * _sigmoid(fz)
    logf2 = jnp.log(jnp.maximum(f, TINY)) * log2e
    kk = (1.0 - lb) * _sigmoid(-fz)

    b = _dot_sel(cum, logf2)
    b3 = b.reshape(n_sub, SUB, A_WIDTH)
    end = 0 if reverse else SUB - 1
    tot = jnp.broadcast_to(b3[:, end:end + 1, :], (n_sub, SUB, A_WIDTH)).reshape(CHUNK, A_WIDTH)
    q_in = q * jnp.exp2(b)
    k_out = kk * jnp.exp2(tot - b)
    d_sub = jnp.exp2(tot)

    c3 = (b - jnp.log(kk) * log2e).reshape(n_sub, SUB, A_WIDTH)
    b4 = b.reshape(n_sub, 2, half, A_WIDTH)
    q4 = q.reshape(n_sub, 2, half, A_WIDTH)
    r_local = lax.broadcasted_iota(jnp.int32, (n_sub, half, A_WIDTH), 1)
    zero_half = jnp.zeros((n_sub, half, A_WIDTH), F32)
    pieces = []
    for s in range(SUB):
        cs = c3[:, s:s + 1, :]
        parts = []
        for hsel in range(2):
            lo, hi = hsel * half, hsel * half + half - 1
            if (lo > s) if reverse else (hi < s):
                parts.append(zero_half)
                continue
            arg = b4[:, hsel] - cs
            if not ((hi <= s) if reverse else (lo >= s)):
                t_local = r_local + lo
                arg = jnp.where((t_local <= s) if reverse else (t_local >= s), arg, MASK_VALUE)
            parts.append(q4[:, hsel] * jnp.exp2(arg))
        piece = jnp.stack(parts, axis=1).reshape(CHUNK, A_WIDTH)
        pieces.append(piece if precise else piece.astype(BF16))
    w_cat = jnp.concatenate(pieces, axis=1)
    s_local = _mm_sel(w_cat, e_sel, precise)
    p = _mm_sel(s_local, x_sel, precise) * p_mask
    v_bd = jnp.concatenate([v] * A_HEADS, axis=0) * v_mask
    o_intra = _mm(p, v_bd, precise)

    v_op = v if precise else v.astype(BF16)
    k_op = k_out if precise else k_out.astype(BF16)
    q_op = q_in if precise else q_in.astype(BF16)
    sub_m = sub_mask
    order = range(n_sub - 1, -1, -1) if reverse else range(n_sub)
    new_st, o_inter = [], []
    for p in range(A_WIDTH // LANES):
        lanes = slice(LANES * p, LANES * (p + 1))
        k_exp = jnp.concatenate([k_op[:, lanes]] * n_sub, axis=1) * sub_m
        u_all = _mm(v_op[:, lanes], k_exp, precise, _TN)
        s = st[p]
        s_before = [None] * n_sub
        for j in order:
            s_before[j] = s
            s = d_sub[SUB * j:SUB * j + 1, lanes] * s + head_mask * u_all[:, LANES * j:LANES * (j + 1)]
        new_st.append(s)
        q_exp = jnp.concatenate([q_op[:, lanes]] * n_sub, axis=1) * sub_m
        o_inter.append(_mm(q_exp, jnp.concatenate(s_before, axis=1), precise, _NT))
    return o_intra + jnp.concatenate(o_inter, axis=1), jnp.stack(new_st, axis=0)


def _hgrn_kernel(*refs, has_s0, want_state, precise):
    refs = list(refs)
    seq_refs = [refs[6 * sq:6 * sq + 6] for sq in range(HGRN_SEQS)]
    del refs[:6 * HGRN_SEQS]
    lb_ref = refs.pop(0)
    s0_ref = refs.pop(0) if has_s0 else None
    cumf_ref, cumb_ref, e_ref, x_ref, sm_ref, pm_ref, vm_ref, hm_ref, of_ref, ob_ref = refs[:10]
    del refs[:10]
    sout_ref = refs.pop(0) if want_state else None
    st_ref, = refs
    c = pl.program_id(1)

    @pl.when(c == 0)
    def _():
        st_ref[...] = s0_ref[...] if has_s0 else jnp.zeros_like(st_ref)

    shared = (e_ref[...], x_ref[...], sm_ref[...], pm_ref[...], vm_ref[...], hm_ref[...])
    lb = lb_ref[...]
    for sq in range(HGRN_SEQS):
        qaf_ref, iaf_ref, fzf_ref, qab_ref, iab_ref, fzb_ref = seq_refs[sq]
        o_f, st_f = _hgrn_chunk(qaf_ref[...], iaf_ref[...], fzf_ref[...], lb[0:1], st_ref[sq, 0],
                                (cumf_ref[...],) + shared, False, precise)
        o_b, st_b = _hgrn_chunk(qab_ref[...], iab_ref[...], fzb_ref[...], lb[1:2], st_ref[sq, 1],
                                (cumb_ref[...],) + shared, True, precise)
        of_ref[sq] = o_f
        ob_ref[sq] = o_b
        st_ref[sq, 0] = st_f
        st_ref[sq, 1] = st_b

    if want_state:
        @pl.when(c == pl.num_programs(1) - 1)
        def _():
            sout_ref[...] = st_ref[...]


def _hgrn_consts():
    r = np.arange(CHUNK)
    same = (r[:, None] // SUB) == (r[None, :] // SUB)
    cum_f = (same & (r[None, :] <= r[:, None])).astype(np.float32)
    cum_b = (same & (r[None, :] >= r[:, None])).astype(np.float32)
    tot = same.astype(np.float32)
    e = np.zeros((SUB, A_HEADS, HEAD_DIM, A_HEADS, SUB), np.float32)
    for s in range(SUB):
        for h in range(A_HEADS):
            e[s, h, :, h, s] = 1.0
    e = e.reshape(SUB * A_WIDTH, A_HEADS * SUB)
    x = np.zeros((A_HEADS, SUB, A_HEADS, CHUNK), np.float32)
    for h in range(A_HEADS):
        for s in range(CHUNK):
            x[h, s % SUB, h, s] = 1.0
    x = x.reshape(A_HEADS * SUB, A_HEADS * CHUNK)
    p_mask = np.tile(tot, (1, A_HEADS))
    hv = np.arange(A_HEADS * CHUNK) // CHUNK
    hk = np.arange(A_WIDTH) // HEAD_DIM
    v_mask = (hv[:, None] == hk[None, :]).astype(np.float32)
    head_mask = (hk[:LANES, None] == hk[None, :LANES]).astype(np.float32)
    sub_mask = np.repeat(r[:, None] // SUB == np.arange(CHUNK // SUB)[None, :], LANES, axis=1).astype(np.float32)
    sel = [jnp.asarray(a, BF16) for a in (cum_f, cum_b, e, x, sub_mask)]
    return sel + [jnp.asarray(a, F32) for a in (p_mask, v_mask, head_mask)]


def _hgrn_scan(za, first_row, lb2, s0_t, n_seq, t, precise, want_state, name):
    nc = t // CHUNK
    groups = n_seq // HGRN_SEQS
    base = first_row // CHUNK
    za3 = za.reshape(za.shape[0] // CHUNK, CHUNK, za.shape[1])
    consts = _hgrn_consts()
    state_shape = (HGRN_SEQS, 2, A_WIDTH // LANES, LANES, LANES)
    const = lambda a: pl.BlockSpec(a.shape, lambda p, c: (0,) * a.ndim)

    def chunk_spec(sq, col, backward):
        def index(p, c):
            return (base + (p * HGRN_SEQS + sq) * nc + (nc - 1 - c if backward else c), 0, col)
        return pl.BlockSpec((None, CHUNK, A_WIDTH), index)

    fwd = lambda col: pl.BlockSpec((None, HGRN_SEQS, CHUNK, A_WIDTH), lambda p, c: (p, 0, c, col))
    bwd = lambda col: pl.BlockSpec((None, HGRN_SEQS, CHUNK, A_WIDTH), lambda p, c: (p, 0, nc - 1 - c, col))
    state_spec = pl.BlockSpec((None,) + state_shape, lambda p, c: (p, 0, 0, 0, 0, 0))
    operands = [za3] * (6 * HGRN_SEQS) + [lb2]
    in_specs = [chunk_spec(sq, col, backward) for sq in range(HGRN_SEQS)
                for col, backward in ((0, False), (1, False), (2, False), (0, True), (1, True), (3, True))]
    in_specs.append(const(lb2))
    if s0_t is not None:
        operands.append(s0_t.reshape((groups,) + state_shape))
        in_specs.append(state_spec)
    o_shape = jax.ShapeDtypeStruct((groups, HGRN_SEQS, t, A_WIDTH), F32)
    out_specs, out_shape = [fwd(0), bwd(0)], [o_shape, o_shape]
    if want_state:
        out_specs.append(state_spec)
        out_shape.append(jax.ShapeDtypeStruct((groups,) + state_shape, F32))
    outs = pl.pallas_call(
        functools.partial(_hgrn_kernel, has_s0=s0_t is not None, want_state=want_state, precise=precise),
        grid=(groups, nc),
        in_specs=in_specs + [const(c) for c in consts],
        out_specs=out_specs,
        out_shape=out_shape,
        scratch_shapes=[pltpu.VMEM(state_shape, F32)],
        compiler_params=_cparams("parallel", "arbitrary"),
        name=name,
    )(*operands, *consts)
    o_f, o_b = (o.reshape(n_seq * t, A_WIDTH) for o in outs[:2])
    s_fin = outs[2].reshape((n_seq,) + state_shape[1:]) if want_state else None
    return o_f, o_b, s_fin


def _attend(q3, segments, sink_row, precise=False):
    scores = []
    for k, _, valid in segments:
        s = _mm(k, q3, precise, _NT)
        if valid is not None:
            s = jnp.where(valid, s, MASK_VALUE)
        scores.append(s)
    m = scores[0].max(axis=0, keepdims=True)
    for s in scores[1:]:
        m = jnp.maximum(m, s.max(axis=0, keepdims=True))
    if sink_row is not None:
        m = jnp.maximum(m, sink_row)
    denom = jnp.exp2(sink_row - m) if sink_row is not None else 0.0
    acc = 0.0
    for s, (_, v, _) in zip(scores, segments):
        p = jnp.exp2(s - m)
        denom = denom + p.sum(axis=0, keepdims=True)
        acc = acc + _mm(v, p, precise, _TN)
    return (acc / denom).T


def _stack_heads(q, kvh):
    base = kvh * GQA_GROUP
    return jnp.concatenate([q[:, LANES * (base + g):LANES * (base + g + 1)] for g in range(GQA_GROUP)], axis=0)


def _merge_heads(res, tq):
    lane = lax.broadcasted_iota(jnp.int32, (tq, LANES), 1)
    heads = []
    for j in range(B_Q_HEADS):
        kvh, g = divmod(j, GQA_GROUP)
        o = res[kvh][g * tq:(g + 1) * tq]
        if kvh != j % 2:
            o = pltpu.roll(o, HEAD_DIM, 1)
        heads.append(o)
    blocks = [jnp.where(lane < HEAD_DIM, heads[2 * b], heads[2 * b + 1]) for b in range(B_Q_HEADS // 2)]
    return jnp.concatenate(blocks, axis=1)


def _sink_row(sink_ref, kvh, tq):
    return jnp.concatenate(
        [jnp.full((1, tq), sink_ref[kvh * GQA_GROUP + g] * LOG2E, F32) for g in range(GQA_GROUP)], axis=1)


def _global_attn_kernel(q_ref, kv_ref, ck_ref, cv_ref, o_ref, *, tq):
    q = q_ref[...]
    kv = kv_ref[...]
    k, v = kv[:, :KV_WIDTH], kv[:, KV_WIDTH:]
    ck, cv = ck_ref[...], cv_ref[...]
    segments = [(k[c:c + KEY_CHUNK], v[c:c + KEY_CHUNK], None) for c in range(0, k.shape[0], KEY_CHUNK)]
    segments.append((ck, cv, None))
    res = [_attend(_stack_heads(q, kvh), segments, None) for kvh in range(B_KV_HEADS)]
    o_ref[...] = _merge_heads(res, tq).astype(BF16)


def _global_attention(q_att, kv, ck, cv, dims):
    t, n_seq = dims["t_lat"], dims["n_lat_seq"]
    tq = 512
    nq = t // tq
    return pl.pallas_call(
        functools.partial(_global_attn_kernel, tq=tq),
        grid=(n_seq, nq),
        in_specs=[
            pl.BlockSpec((tq, 2 * Q_WIDTH), lambda b, i: (b * nq + i, 0)),
            pl.BlockSpec((t, 2 * KV_WIDTH), lambda b, i: (b, 0)),
            pl.BlockSpec((None,) + ck.shape[1:], lambda b, i: (b, 0, 0)),
            pl.BlockSpec((None,) + cv.shape[1:], lambda b, i: (b, 0, 0)),
        ],
        out_specs=pl.BlockSpec((tq, Q_WIDTH), lambda b, i: (b * nq + i, 0)),
        out_shape=jax.ShapeDtypeStruct((n_seq * t, Q_WIDTH), BF16),
        compiler_params=_cparams("parallel", "arbitrary"),
        name="global_attention",
    )(q_att, kv, ck, cv)


def _window_attn_kernel(sink_ref, q_ref, kv_ref, ck_ref, cv_ref, o_ref, *, tq, t):
    j = pl.program_id(1)
    span = tq + 2 * WINDOW
    start = pl.multiple_of(jnp.clip(j * tq - WINDOW, 0, t - span), WINDOW)
    q = q_ref[...]
    kv = kv_ref[pl.ds(start, span), :]
    k, v = kv[:, :KV_WIDTH], kv[:, KV_WIDTH:]
    key_pos = start + lax.broadcasted_iota(jnp.int32, (span, GQA_GROUP * tq), 0)
    query_pos = j * tq + lax.broadcasted_iota(jnp.int32, (span, GQA_GROUP * tq), 1) % tq
    valid = jnp.abs(query_pos - key_pos) <= WINDOW
    segments = [(k, v, valid), (ck_ref[...], cv_ref[...], None)]
    res = [_attend(_stack_heads(q, kvh), segments, _sink_row(sink_ref, kvh, tq)) for kvh in range(C_KV_HEADS)]
    o_ref[...] = _merge_heads(res, tq).astype(BF16)


def _window_attention(sink, q_att, kv, ck, cv, dims):
    t, n_seq = dims["t_lat"], dims["n_lat_seq"]
    tq = 2 * WINDOW
    nq = t // tq
    return pl.pallas_call(
        functools.partial(_window_attn_kernel, tq=tq, t=t),
        grid=(n_seq, nq),
        in_specs=[
            pl.BlockSpec(memory_space=pltpu.SMEM),
            pl.BlockSpec((tq, 2 * Q_WIDTH), lambda b, i: (b * nq + i, 0)),
            pl.BlockSpec((t, 2 * KV_WIDTH), lambda b, i: (b, 0)),
            pl.BlockSpec((None,) + ck.shape[1:], lambda b, i: (b, 0, 0)),
            pl.BlockSpec((None,) + cv.shape[1:], lambda b, i: (b, 0, 0)),
        ],
        out_specs=pl.BlockSpec((tq, Q_WIDTH), lambda b, i: (b * nq + i, 0)),
        out_shape=jax.ShapeDtypeStruct((n_seq * t, Q_WIDTH), BF16),
        compiler_params=_cparams("parallel", "arbitrary"),
        name="window_attention",
    )(sink, q_att, kv, ck, cv)


def _context_attn_kernel(sink_ref, qg_ref, qw_ref, kv_ref, og_ref, ow_ref, *, t, precise):
    kv = kv_ref[...]
    for q_ref, o_ref, off, use_sink in ((qg_ref, og_ref, 0, False), (qw_ref, ow_ref, 2 * KV_WIDTH, True)):
        q = q_ref[...]
        k, v = kv[:, off:off + KV_WIDTH], kv[:, off + KV_WIDTH:off + 2 * KV_WIDTH]
        res = [_attend(_stack_heads(q, kvh), [(k, v, None)],
                       _sink_row(sink_ref, kvh, t) if use_sink else None, precise)
               for kvh in range(B_KV_HEADS)]
        o_ref[...] = _merge_heads(res, t)


def _context_attention(sink, qg32, qw32, kv32, dims, precise):
    t, n_seq = dims["t_ctx"], dims["n_ctx_seq"]
    seq = lambda w: pl.BlockSpec((t, w), lambda b: (b, 0))
    return pl.pallas_call(
        functools.partial(_context_attn_kernel, t=t, precise=precise),
        grid=(n_seq,),
        in_specs=[pl.BlockSpec(memory_space=pltpu.SMEM), seq(2 * Q_WIDTH), seq(2 * Q_WIDTH), seq(4 * KV_WIDTH)],
        out_specs=[seq(Q_WIDTH), seq(Q_WIDTH)],
        out_shape=[jax.ShapeDtypeStruct((n_seq * t, Q_WIDTH), F32)] * 2,
        compiler_params=_cparams("parallel"),
        name="context_attention_precise" if precise else "context_attention",
    )(sink, qg32, qw32, kv32)


def _route(scores_t, sel_t):
    s = [scores_t[e:e + 1, :] for e in range(N_EXPERTS)]
    z = [sel_t[e:e + 1, :] for e in range(N_EXPERTS)]
    gs = []
    for g in range(N_GROUPS):
        m = z[g * EXPERTS_PER_GROUP:(g + 1) * EXPERTS_PER_GROUP]
        best = None
        for a in range(EXPERTS_PER_GROUP):
            for b in range(a + 1, EXPERTS_PER_GROUP):
                pair = m[a] + m[b]
                best = pair if best is None else jnp.maximum(best, pair)
        gs.append(best)
    combine, groups = [], []
    for g in range(N_GROUPS):
        chosen_g = None
        for g2 in range(N_GROUPS):
            if g2 == g:
                continue
            c = (gs[g] > gs[g2]) if g2 < g else (gs[g] >= gs[g2])
            chosen_g = c if chosen_g is None else jnp.logical_and(chosen_g, c)
        base = g * EXPERTS_PER_GROUP
        picked = []
        for a in range(EXPERTS_PER_GROUP):
            rank = 0.0
            for b in range(EXPERTS_PER_GROUP):
                if b == a:
                    continue
                ahead = (z[base + b] >= z[base + a]) if b < a else (z[base + b] > z[base + a])
                rank = rank + ahead.astype(F32)
            picked.append(jnp.where(jnp.logical_and(chosen_g, rank < 2.0), s[base + a], 0.0))
        denom = picked[0] + picked[1] + picked[2] + picked[3]
        denom = jnp.where(chosen_g, denom, 1.0)
        combine.extend(pk / denom for pk in picked)
        groups.append(jnp.where(chosen_g, 1.0, 0.0))
    return jnp.concatenate(combine, axis=0), jnp.concatenate(groups, axis=0)


def _outproj_kernel(*refs, lat_tiles, precise_ctx):
    if precise_ctx:
        (ofl_ref, ofc_ref, obl_ref, obc_ref, ga_ref,
         ogl_ref, ogc_ref, owl_ref, owc_ref, xl_ref, xc_ref, mod_ref, ag_ref,
         ones_ref, w_ref, wlo_ref, g2_ref, rw_ref, rb_ref, before_ref,
         x1_ref, h2_ref, comb_ref, pos_ref, cnt_ref, mix_ref) = refs
    else:
        (ofl_ref, ofc_ref, obl_ref, obc_ref, ga_ref,
         ogl_ref, ogc_ref, owl_ref, owc_ref, xl_ref, xc_ref, mod_ref, ag_ref,
         ones_ref, w_ref, g2_ref, rw_ref, rb_ref, before_ref,
         x1_ref, h2_ref, comb_ref, pos_ref, cnt_ref, mix_ref) = refs
        wlo_ref = None
    is_lat = pl.program_id(0) < lat_tiles
    o = jnp.where(is_lat, ofl_ref[...] + obl_ref[...], ofc_ref[...] + obc_ref[...])
    oa = _head_rms(o, ag_ref[...], ones_ref[...]) * _sigmoid(ga_ref[...])
    def mix(operands, precise):
        if precise:
            ah, al = _split2(jnp.concatenate(operands, axis=1))
            mix_ref[...] = _dot(ah, w_ref[...]) + _dot(al, w_ref[...]) + _dot(ah, wlo_ref[...])
        else:
            mix_ref[...] = _dot(jnp.concatenate([a.astype(BF16) for a in operands], axis=1), w_ref[...])

    pl.when(is_lat)(lambda: mix((oa, ogl_ref[...], owl_ref[...]), False))
    pl.when(jnp.logical_not(is_lat))(lambda: mix((oa, ogc_ref[...], owc_ref[...]), precise_ctx))

    mod = mod_ref[...]
    gate1 = mod[:, 2 * D_MODEL:3 * D_MODEL]
    sh2 = mod[:, 3 * D_MODEL:4 * D_MODEL]
    sc2 = mod[:, 4 * D_MODEL:5 * D_MODEL]
    x1 = jnp.where(is_lat, xl_ref[...], xc_ref[...]) + gate1 * mix_ref[...]
    x1_ref[...] = x1
    gain = g2_ref[...] * (1.0 + sc2)
    h2 = x1 * lax.rsqrt(jnp.mean(x1 * x1, axis=-1, keepdims=True) + EPS) * gain + sh2
    h2_ref[...] = h2.astype(BF16)
    scores_t = _sigmoid(_mm(rw_ref[...], h2, True, _NT))
    comb_t, group_t = _route(scores_t, scores_t + rb_ref[...])
    comb_ref[...] = comb_t
    rank = _dot(group_t.astype(BF16), before_ref[...])
    counts = group_t.sum(axis=1, keepdims=True)
    pos, offset = 0.0, 0.0
    for g in range(N_GROUPS):
        pos = pos + group_t[g:g + 1] * (rank[g:g + 1] + offset)
        offset = offset + counts[g:g + 1]
    pos_ref[...] = pos
    pad = jnp.zeros((8 - N_GROUPS, 1), F32)
    cnt_ref[...] = jnp.broadcast_to(jnp.concatenate([counts, pad], axis=0), (8, LANES)).astype(jnp.int32)


def _out_projection(o_f, o_b, za, og_lat, og_ctx, ow_lat, ow_ctx, x_pair, mod, an_g, ones_blk, w_hi, w_lo, g2,
                    rw_t, rb, dims, layer=0):
    n = dims["n_lat"] + dims["n_ctx"]
    tm = TOKEN_TILE
    n_tiles = n // tm
    lat_tiles = dims["n_lat"] // tm
    tiles_per_seq = dims["t_lat"] // tm
    ctx_cond = dims["ctx_cond"]
    precise_ctx = w_lo is not None
    cond_idx = lambda i: jnp.where(i < lat_tiles, i // tiles_per_seq, ctx_cond)
    row = lambda w: pl.BlockSpec((tm, w), lambda i: (i, 0))
    lat = lambda w: pl.BlockSpec((tm, w), lambda i: (jnp.minimum(i, lat_tiles - 1), 0))
    ctx = lambda w: pl.BlockSpec((tm, w), lambda i: (jnp.maximum(i - lat_tiles, 0), 0))
    const = lambda a: pl.BlockSpec(a.shape, lambda i: (0,) * a.ndim, pipeline_mode=pl.Buffered(1))
    weight = lambda w: const(w) if w.ndim == 2 else pl.BlockSpec(
        (None,) + w.shape[1:], lambda i: (layer, 0, 0), pipeline_mode=pl.Buffered(1))
    weights = [w_hi, w_lo] if precise_ctx else [w_hi]
    t_idx = np.arange(tm)
    before = jnp.asarray((t_idx[:, None] < t_idx[None, :]).astype(np.float32), BF16)
    return pl.pallas_call(
        functools.partial(_outproj_kernel, lat_tiles=lat_tiles, precise_ctx=precise_ctx),
        grid=(n_tiles,),
        in_specs=[
            lat(A_WIDTH), ctx(A_WIDTH), lat(A_WIDTH), ctx(A_WIDTH),
            pl.BlockSpec((tm, A_WIDTH), lambda i: (i, 4)),
            lat(Q_WIDTH), ctx(Q_WIDTH), lat(Q_WIDTH), ctx(Q_WIDTH)] + _token_specs(x_pair, tm, lat_tiles) + [
            pl.BlockSpec((None, 1, 6 * D_MODEL), lambda i: (cond_idx(i), 0, 0)),
            const(an_g), const(ones_blk)] + [weight(w) for w in weights] + [
            const(g2), const(rw_t), const(rb), const(before),
        ],
        out_specs=[row(D_MODEL), row(D_MODEL), pl.BlockSpec((N_EXPERTS, tm), lambda i: (0, i)),
                   pl.BlockSpec((1, tm), lambda i: (0, i)), pl.BlockSpec((8, LANES), lambda i: (i, 0))],
        out_shape=[
            jax.ShapeDtypeStruct((n, D_MODEL), F32),
            jax.ShapeDtypeStruct((n, D_MODEL), BF16),
            jax.ShapeDtypeStruct((N_EXPERTS, n), F32),
            jax.ShapeDtypeStruct((1, n), F32),
            jax.ShapeDtypeStruct((n_tiles * 8, LANES), jnp.int32),
        ],
        scratch_shapes=[pltpu.VMEM((tm, D_MODEL), F32)],
        compiler_params=_cparams("arbitrary"),
        name="out_projection_precise_ctx" if precise_ctx else "out_projection",
    )(*o_f, *o_b, za, og_lat, og_ctx, ow_lat, ow_ctx, *x_pair, mod, an_g, ones_blk, *weights, g2, rw_t, rb, before)


def _moe_kernel(off_ref, h_ref, comb_ref, pos_ref, x1_ref, mod_ref, wgu_ref, wd_ref, fg_ref, *rest,
                final, lat_tiles):
    *outs, hs_ref, cs_ref, acc_ref = rest
    i = pl.program_id(0)
    tm = h_ref.shape[0]
    row = lax.broadcasted_iota(jnp.int32, (tm, tm), 0).astype(F32)
    perm = jnp.where(row == pos_ref[...], 1.0, 0.0).astype(BF16)
    hs_ref[...] = _dot(perm, h_ref[...]).astype(BF16)
    cs_ref[...] = sum(lax.dot_general(perm, c, _NT, preferred_element_type=F32)
                      for c in _split3(comb_ref[...]))
    acc_ref[...] = jnp.zeros_like(acc_ref)
    lane = lax.broadcasted_iota(jnp.int32, (MOE_BLOCK, N_EXPERTS), 1)

    for r in range(tm // MOE_BLOCK):
        rows = slice(r * MOE_BLOCK, (r + 1) * MOE_BLOCK)

        def group_body(g, carry, rows=rows, r=r):
            has_tokens = jnp.logical_and(off_ref[i, g] < (r + 1) * MOE_BLOCK, off_ref[i, g + 1] > r * MOE_BLOCK)

            @pl.when(has_tokens)
            def _():
                h = hs_ref[rows, :]
                cs = cs_ref[rows, :]
                acts = []
                for a in range(EXPERTS_PER_GROUP):
                    e = g * EXPERTS_PER_GROUP + a
                    gu = _dot(h, wgu_ref[e])
                    gate, up = gu[:, :D_EXPERT], gu[:, D_EXPERT:]
                    ce = jnp.sum(jnp.where(lane == e, cs, 0.0), axis=1, keepdims=True)
                    acts.append((_silu(gate) * up * ce).astype(BF16))
                acc_ref[rows, :] += _dot(jnp.concatenate(acts, axis=1), wd_ref[g])

            return carry

        lax.fori_loop(0, N_GROUPS, group_body, 0)

    y_hi, y_lo = _split2(acc_ref[...])
    y = (lax.dot_general(perm, y_hi, _TN, preferred_element_type=F32)
         + lax.dot_general(perm, y_lo, _TN, preferred_element_type=F32))
    x2 = x1_ref[...] + mod_ref[...][:, 5 * D_MODEL:] * y
    if not final:
        outs[0][...] = x2
    else:
        yn = x2 * lax.rsqrt(jnp.mean(x2 * x2, axis=-1, keepdims=True) + EPS) * fg_ref[...]
        is_lat = i < lat_tiles

        @pl.when(is_lat)
        def _():
            outs[0][...] = yn

        @pl.when(jnp.logical_not(is_lat))
        def _():
            outs[1][...] = yn


def _moe(h2, comb_t, pos, counts, x1, mod, w_gu, w_d, layer, final_g, dims, final):
    n = x1.shape[0]
    tm = TOKEN_TILE
    lat_tiles = dims["n_lat"] // tm
    tiles_per_seq = dims["t_lat"] // tm
    ctx_cond = dims["ctx_cond"]
    cond_idx = lambda i: jnp.where(i < lat_tiles, i // tiles_per_seq, ctx_cond)
    offsets = jnp.concatenate([jnp.zeros((n // tm, 1), jnp.int32), jnp.cumsum(counts, axis=1)], axis=1)
    if final:
        out_specs = [pl.BlockSpec((tm, D_MODEL), lambda i, *_: (jnp.minimum(i, lat_tiles - 1), 0)),
                     pl.BlockSpec((tm, D_MODEL), lambda i, *_: (jnp.maximum(i - lat_tiles, 0), 0))]
        out_shape = [jax.ShapeDtypeStruct((dims["n_lat"], D_MODEL), F32),
                     jax.ShapeDtypeStruct((dims["n_ctx"], D_MODEL), F32)]
    else:
        out_specs = pl.BlockSpec((tm, D_MODEL), lambda i, *_: (i, 0))
        out_shape = jax.ShapeDtypeStruct((n, D_MODEL), F32)
    resident = lambda a: pl.BlockSpec((None,) + a.shape[1:], lambda i, *_: (layer,) + (0,) * (a.ndim - 1),
                                      pipeline_mode=pl.Buffered(1))
    grid_spec = pltpu.PrefetchScalarGridSpec(
        num_scalar_prefetch=1,
        grid=(n // tm,),
        in_specs=[
            pl.BlockSpec((tm, D_MODEL), lambda i, *_: (i, 0)),
            pl.BlockSpec((N_EXPERTS, tm), lambda i, *_: (0, i)),
            pl.BlockSpec((1, tm), lambda i, *_: (0, i)),
            pl.BlockSpec((tm, D_MODEL), lambda i, *_: (i, 0)),
            pl.BlockSpec((None, 1, 6 * D_MODEL), lambda i, *_: (cond_idx(i), 0, 0)),
            resident(w_gu), resident(w_d),
            pl.BlockSpec((1, D_MODEL), lambda i, *_: (0, 0)),
        ],
        out_specs=out_specs,
        scratch_shapes=[pltpu.VMEM((tm, D_MODEL), BF16), pltpu.VMEM((tm, N_EXPERTS), F32),
                        pltpu.VMEM((tm, D_MODEL), F32)],
    )
    return pl.pallas_call(
        functools.partial(_moe_kernel, final=final, lat_tiles=lat_tiles),
        grid_spec=grid_spec,
        out_shape=out_shape,
        compiler_params=_cparams("arbitrary"),
        name="moe_final" if final else "moe",
    )(offsets, h2, comb_t, pos, x1, mod, w_gu, w_d, final_g)


def _rope_tables(t, tile):
    n_freq = HEAD_DIM // 4
    pos = np.arange(t)
    freqs = ROPE_THETA ** (-np.arange(n_freq, dtype=np.float64) / n_freq)
    ang_row = (pos // GRID_W)[:, None] * freqs
    ang_col = (pos % GRID_W)[:, None] * freqs
    cos_h = np.concatenate([np.cos(ang_row)] * 2 + [np.cos(ang_col)] * 2, axis=1)
    sin_h = np.concatenate([-np.sin(ang_row), np.sin(ang_row), -np.sin(ang_col), np.sin(ang_col)], axis=1)
    cos_t = np.concatenate([np.ones((tile, LANES)), np.concatenate([cos_h, cos_h], axis=1)], axis=0)
    sin_t = np.concatenate([np.zeros((tile, LANES)), np.concatenate([sin_h, sin_h], axis=1)], axis=0)
    return jnp.asarray(cos_t, F32), jnp.asarray(sin_t, F32)


def _head_ones(width):
    r = np.arange(width) // HEAD_DIM
    return jnp.asarray((r[:, None] == r[None, :]).astype(np.float32), BF16)


def _split_kernel(w_ref, hi_ref, lo_ref):
    hi, lo = _split2(w_ref[...])
    hi_ref[...] = hi
    lo_ref[...] = lo


def _hi_lo(w_all, layer):
    _, rows, width = w_all.shape
    tr = 256
    spec = pl.BlockSpec((tr, width), lambda i: (i, 0))
    return pl.pallas_call(
        _split_kernel,
        grid=(rows // tr,),
        in_specs=[pl.BlockSpec((None, tr, width), lambda i: (layer, i, 0))],
        out_specs=[spec, spec],
        out_shape=[jax.ShapeDtypeStruct((rows, width), BF16)] * 2,
        compiler_params=_cparams("parallel"),
        name="split_weight",
    )(w_all)


def _state_to_block_diag_t(s):
    per_pair = LANES // HEAD_DIM
    st = jnp.swapaxes(s, -1, -2).reshape(s.shape[:2] + (A_HEADS // per_pair, per_pair, HEAD_DIM, HEAD_DIM))
    eye = jnp.eye(per_pair, dtype=s.dtype)
    out = jnp.einsum("bdphvk,hg->bdphvgk", st, eye)
    return out.reshape(s.shape[:2] + (A_HEADS // per_pair, LANES, LANES))


def _block_diag_t_to_state(st):
    per_pair = LANES // HEAD_DIM
    blocks = [st[..., HEAD_DIM * h:HEAD_DIM * (h + 1), HEAD_DIM * h:HEAD_DIM * (h + 1)] for h in range(per_pair)]
    diag = jnp.stack(blocks, axis=3)
    return jnp.swapaxes(diag, -1, -2).reshape(st.shape[:2] + (A_HEADS, HEAD_DIM, HEAD_DIM))


def kernel(x_prompt, x_sample, cache_glob_k, cache_glob_v, cache_win_k, cache_win_v, state_hgrn, c, c_ctx,
           mod_w, mod_b, norm1_g, norm2_g, w_in, w_out, hgrn_lb, hgrn_norm_g, q_norm_g, k_norm_g, win_sink,
           router_w, router_b, w_gate_up, w_down, final_g):
    n_ctx_seq, t_ctx, d = x_prompt.shape
    n_lat_seq, t_lat, _ = x_sample.shape
    depth = mod_w.shape[0]
    past = cache_glob_k.shape[2]
    n_lat, n_ctx = n_lat_seq * t_lat, n_ctx_seq * t_ctx
    dims = dict(n_ctx_seq=n_ctx_seq, t_ctx=t_ctx, n_lat_seq=n_lat_seq, t_lat=t_lat,
                n_lat=n_lat, n_ctx=n_ctx, ctx_cond=n_lat_seq)
    assert d == D_MODEL and n_lat_seq < COND_ROWS
    assert n_ctx % TOKEN_TILE == 0 and t_lat % TOKEN_TILE == 0
    assert t_ctx % CHUNK == 0 and t_lat >= 4 * WINDOW

    x_pair = (x_sample.reshape(n_lat, d), x_prompt.reshape(n_ctx, d))
    cond = jnp.concatenate([c, c_ctx[None, :], jnp.zeros((COND_ROWS - n_lat_seq - 1, d), F32)], axis=0)
    mod = _modulation(cond, mod_w, mod_b).reshape(depth, COND_ROWS, 1, 6 * d)

    p_lb = jax.nn.softmax(hgrn_lb.astype(F32), axis=1)
    lbs = jnp.cumsum(p_lb, axis=1) - p_lb[:, :1]

    cos_t, sin_t = _rope_tables(t_lat, TOKEN_TILE)
    ones128, ones256 = _head_ones(LANES), _head_ones(A_WIDTH)
    tile2 = lambda g, reps: jnp.tile(g, reps)[None, :]
    rw_t = router_w.T
    rb = router_b[:, None]
    fg = final_g[None, :]

    w_gu_all = w_gate_up.astype(BF16)
    w_d_all = w_down.astype(BF16).reshape(depth, N_GROUPS, EXPERTS_PER_GROUP * D_EXPERT, d)
    w_in_last, w_out_last = w_in[depth - 1:].astype(BF16), w_out[depth - 1:].astype(BF16)

    new_kv, new_state = [], []
    for l in range(depth):
        precise = l < depth - 1
        win_hi, win_lo = _hi_lo(w_in, l) if precise else (w_in_last, None)
        za, qg_att, kvg, qw_att, kvw, kv32, qg32, qw32 = _in_projection(
            x_pair, mod[l], norm1_g[l][None, :], win_hi, win_lo, cos_t, sin_t,
            tile2(q_norm_g[l], 2), tile2(k_norm_g[l], 2), ones128, dims)

        s0 = state_hgrn[:, l].astype(F32)
        of_lat, ob_lat, _ = _hgrn_scan(za, 0, lbs[:, l], _state_to_block_diag_t(s0), n_lat_seq, t_lat,
                                       False, False, "hgrn_latent")
        of_ctx, ob_ctx, s_fin = _hgrn_scan(za, n_lat, lbs[:, l], None, n_ctx_seq, t_ctx, precise, True,
                                           "hgrn_context_precise" if precise else "hgrn_context")

        cast_cache = lambda a: a[:, l].reshape(n_lat_seq, past, KV_WIDTH).astype(BF16)
        og_lat = _global_attention(qg_att, kvg, cast_cache(cache_glob_k), cast_cache(cache_glob_v), dims)
        ow_lat = _window_attention(win_sink[l], qw_att, kvw, cast_cache(cache_win_k),
                                   cast_cache(cache_win_v), dims)
        og_ctx, ow_ctx = _context_attention(win_sink[l], qg32, qw32, kv32, dims, precise)

        wout_hi, wout_lo = _hi_lo(w_out, l) if precise else (w_out_last, None)
        x1, h2, comb_t, pos, cnt = _out_projection(
            (of_lat, of_ctx), (ob_lat, ob_ctx), za, og_lat, og_ctx, ow_lat, ow_ctx, x_pair, mod[l],
            tile2(hgrn_norm_g[l], A_HEADS), ones256, wout_hi, wout_lo, norm2_g[l][None, :], rw_t, rb, dims)

        counts = cnt.reshape(-1, 8, LANES)[:, :N_GROUPS, 0]
        x = _moe(h2, comb_t, pos, counts, x1, mod[l], w_gu_all, w_d_all, l, fg, dims, final=(l == depth - 1))
        x_pair = (x, x)

        new_kv.append(kv32[:n_ctx].reshape(n_ctx_seq, t_ctx, 4, B_KV_HEADS, HEAD_DIM))
        new_state.append(_block_diag_t_to_state(s_fin))

    y_sample = x[0].reshape(n_lat_seq, t_lat, d)
    y_prompt = x[1].reshape(n_ctx_seq, t_ctx, d)
    kv = jnp.stack(new_kv, axis=1)
    return (y_prompt, y_sample, kv[:, :, :, 0], kv[:, :, :, 1], kv[:, :, :, 2], kv[:, :, :, 3],
            jnp.stack(new_state, axis=1).astype(x_prompt.dtype))
```
